```python
import math
import jax, jax.numpy as jnp
from jax import lax
import numpy as np

D_MODEL = 1024
BATCH = 1
SEQ = 16384
DEPTH = 1

D_MIX = D_MODEL
D_SSM = D_MIX // 2
D_HGRN = D_MIX - D_SSM
SSD_HEAD_DIM = 64
SSD_HEADS = D_SSM // SSD_HEAD_DIM
SSD_GROUPS = 2
SSD_STATE = 128
SSD_CONV = 4
SSD_CHUNK = 128
SSD_NORM_GROUP = D_SSM // SSD_GROUPS
D_CONV = D_SSM + 2 * SSD_GROUPS * SSD_STATE
HGRN_HEADS = 4
HGRN_DK = 128
HGRN_DV = D_HGRN // HGRN_HEADS
HGRN_CHUNK = 64
_XBC0 = D_SSM
_DT0 = _XBC0 + D_CONV
_Q0 = _DT0 + SSD_HEADS
_F0 = _Q0 + HGRN_HEADS * HGRN_DK
_I0 = _F0 + HGRN_HEADS * HGRN_DK
_G0 = _I0 + D_HGRN
N_IN = _G0 + D_HGRN
N_EXPERTS = 256
TOP_K = 8
N_EXPERT_GROUPS = 8
TOPK_GROUPS = 4
D_EXPERT = 256
D_SHARED = 256
ROUTED_SCALE = 2.5
EXPERT_BLOCK = 128
ALPHA = (2 * DEPTH) ** 0.25
BETA = (8 * DEPTH) ** -0.25
LN_EPS = 1e-5
RMS_EPS = 1e-6

kernel_name = 'hybrid_ssd_hgrn2_moe_deepnorm'


def _layer_norm(x, g, b):
    xf = x.astype(jnp.float32)
    mu = jnp.mean(xf, -1, keepdims=True)
    var = jnp.mean(jnp.square(xf - mu), -1, keepdims=True)
    return ((xf - mu) * lax.rsqrt(var + LN_EPS) * g + b).astype(x.dtype)


def _causal_depthwise_conv(u, w, b):
    y = lax.conv_general_dilated(u, w[:, None, :].astype(u.dtype), window_strides=(1,),
                                 padding=((w.shape[0] - 1, 0),),
                                 dimension_numbers=('NWC', 'WIO', 'NWC'),
                                 feature_group_count=u.shape[-1])
    return y + b


def _segsum(a):
    n = a.shape[-1]
    cs = jnp.cumsum(a, axis=-1)
    diff = cs[..., :, None] - cs[..., None, :]
    return jnp.where(jnp.tril(jnp.ones((n, n), dtype=bool)), diff, -jnp.inf)


def _ssd_mixer(z, xbc, dt, conv_w, conv_b, dt_bias, a_log, d_skip, norm_w):
    bsz, t, _ = xbc.shape
    nc = t // SSD_CHUNK
    xbc = jax.nn.silu(_causal_depthwise_conv(xbc, conv_w, conv_b))
    xs = xbc[..., :D_SSM]
    bm = xbc[..., D_SSM:D_SSM + SSD_GROUPS * SSD_STATE]
    cm = xbc[..., D_SSM + SSD_GROUPS * SSD_STATE:]
    dt = jax.nn.softplus(dt.astype(jnp.float32) + dt_bias.astype(jnp.float32))
    a = -jnp.exp(a_log.astype(jnp.float32))
    xh = xs.astype(jnp.float32).reshape(bsz, t, SSD_HEADS, SSD_HEAD_DIM)
    hpg = SSD_HEADS // SSD_GROUPS

    def per_head(m):
        m = m.astype(jnp.float32).reshape(bsz, nc, SSD_CHUNK, SSD_GROUPS, SSD_STATE)
        return jnp.repeat(m, hpg, axis=3)

    bh, ch = per_head(bm), per_head(cm)
    xdt = (xh * dt[..., None]).reshape(bsz, nc, SSD_CHUNK, SSD_HEADS, SSD_HEAD_DIM)
    ad = (dt * a).reshape(bsz, nc, SSD_CHUNK, SSD_HEADS).transpose(0, 3, 1, 2)
    a_cum = jnp.cumsum(ad, axis=-1)
    scores = jnp.einsum('bclhn,bcshn->bhcls', ch, bh) * jnp.exp(_segsum(ad))
    y_diag = jnp.einsum('bhcls,bcshp->bclhp', scores, xdt)
    decay_states = jnp.exp(a_cum[..., -1:] - a_cum)
    states = jnp.einsum('bclhn,bhcl,bclhp->bchpn', bh, decay_states, xdt)
    states = jnp.concatenate([jnp.zeros_like(states[:, :1]), states], axis=1)
    chunk_decay = jnp.exp(_segsum(jnp.pad(a_cum[..., -1], ((0, 0), (0, 0), (1, 0)))))
    states = jnp.einsum('bhzc,bchpn->bzhpn', chunk_decay, states)[:, :-1]
    y_off = jnp.einsum('bclhn,bchpn,bhcl->bclhp', ch, states, jnp.exp(a_cum))
    y = (y_diag + y_off).reshape(bsz, t, SSD_HEADS, SSD_HEAD_DIM) + xh * d_skip.astype(jnp.float32)[:, None]
    y = y.reshape(bsz, t, D_SSM) * jax.nn.silu(z.astype(jnp.float32))
    yg = y.reshape(bsz, t, SSD_GROUPS, SSD_NORM_GROUP)
    yg = yg * lax.rsqrt(jnp.mean(jnp.square(yg), -1, keepdims=True) + RMS_EPS)
    return (yg.reshape(bsz, t, D_SSM) * norm_w).astype(z.dtype)


def _hgrn2_mixer(q, f, i, g, lower_bound, norm_w):
    bsz, t, _ = q.shape
    nc = t // HGRN_CHUNK
    fg = lower_bound + (1.0 - lower_bound) * jax.nn.sigmoid(f.astype(jnp.float32))

    def chunks(a, d):
        return a.reshape(bsz, nc, HGRN_CHUNK, HGRN_HEADS, d).transpose(1, 0, 3, 2, 4)

    qc = chunks(jax.nn.silu(q.astype(jnp.float32)), HGRN_DK)
    kc = chunks(1.0 - fg, HGRN_DK)
    lc = chunks(jnp.log(fg), HGRN_DK)
    vc = chunks(i.astype(jnp.float32), HGRN_DV)
    causal = jnp.tril(jnp.ones((HGRN_CHUNK, HGRN_CHUNK), dtype=bool))[:, :, None]

    def step(state, inp):
        qb, kb, vb, lf = inp
        cum = jnp.cumsum(lf, axis=2)
        o_inter = jnp.einsum('bhtd,bhde->bhte', qb * jnp.exp(cum), state)
        diff = cum[:, :, :, None, :] - cum[:, :, None, :, :]
        decay = jnp.exp(jnp.where(causal, diff, -jnp.inf))
        att = jnp.sum(qb[:, :, :, None, :] * kb[:, :, None, :, :] * decay, axis=-1)
        o_intra = jnp.einsum('bhts,bhse->bhte', att, vb)
        last = cum[:, :, -1:, :]
        state = jnp.exp(last[:, :, 0, :, None]) * state + jnp.einsum('bhsd,bhse->bhde', kb * jnp.exp(last - cum), vb)
        return state, o_inter + o_intra

    s0 = jnp.zeros((bsz, HGRN_HEADS, HGRN_DK, HGRN_DV), jnp.float32)
    _, o = lax.scan(step, s0, (qc, kc, vc, lc))
    o = o.transpose(1, 0, 3, 2, 4).reshape(bsz, t, HGRN_HEADS, HGRN_DV)
    o = o * lax.rsqrt(jnp.mean(jnp.square(o), -1, keepdims=True) + RMS_EPS)
    o = o.reshape(bsz, t, D_HGRN) * norm_w * jax.nn.silu(g.astype(jnp.float32))
    return o.astype(q.dtype)


def _routed_experts(xt, w_router, router_bias, w_gate, w_up, w_down):
    n_tok, d = xt.shape
    scores = jax.nn.sigmoid((xt @ w_router).astype(jnp.float32))
    biased = scores + router_bias.astype(jnp.float32)
    per_group = N_EXPERTS // N_EXPERT_GROUPS
    group_score = lax.top_k(biased.reshape(n_tok, N_EXPERT_GROUPS, per_group), 2)[0].sum(-1)
    _, group_idx = lax.top_k(group_score, TOPK_GROUPS)
    group_mask = jax.nn.one_hot(group_idx, N_EXPERT_GROUPS, dtype=jnp.float32).sum(1) > 0
    masked = jnp.where(jnp.repeat(group_mask, per_group, axis=1), biased, -jnp.inf)
    _, idx = lax.top_k(masked, TOP_K)
    gate = jnp.take_along_axis(scores, idx, axis=1)
    gate = gate / jnp.sum(gate, -1, keepdims=True) * ROUTED_SCALE
    n_assign = n_tok * TOP_K
    flat_e = idx.reshape(n_assign).astype(jnp.int32)
    flat_tok = jnp.repeat(jnp.arange(n_tok, dtype=jnp.int32), TOP_K)
    flat_gate = gate.reshape(n_assign).astype(xt.dtype)
    order = jnp.argsort(flat_e)
    sorted_e = flat_e[order]
    counts = jnp.bincount(flat_e, length=N_EXPERTS).astype(jnp.int32)
    padded = (counts + EXPERT_BLOCK - 1) // EXPERT_BLOCK * EXPERT_BLOCK
    pad_end = jnp.cumsum(padded).astype(jnp.int32)
    pad_start = pad_end - padded
    start = jnp.cumsum(counts).astype(jnp.int32) - counts
    dest = pad_start[sorted_e] + jnp.arange(n_assign, dtype=jnp.int32) - start[sorted_e]
    n_blocks = -(-(n_assign + N_EXPERTS * (EXPERT_BLOCK - 1)) // EXPERT_BLOCK)
    n_rows = n_blocks * EXPERT_BLOCK
    row_tok = jnp.full((n_rows,), n_tok, jnp.int32).at[dest].set(flat_tok[order])
    row_gate = jnp.zeros((n_rows,), xt.dtype).at[dest].set(flat_gate[order])
    block_start = jnp.arange(n_blocks, dtype=jnp.int32) * EXPERT_BLOCK
    block_e = jnp.minimum(jnp.searchsorted(pad_end, block_start, side='right'), N_EXPERTS - 1)
    x_pad = jnp.concatenate([xt, jnp.zeros((1, d), xt.dtype)], axis=0)

    def expert_block(args):
        tok, e = args
        xb = x_pad[tok]
        hb = jax.nn.silu(xb @ w_gate[e]) * (xb @ w_up[e])
        return hb @ w_down[e]

    yb = lax.map(expert_block, (row_tok.reshape(n_blocks, EXPERT_BLOCK), block_e))
    y = jnp.zeros_like(x_pad).at[row_tok].add(yb.reshape(n_rows, d) * row_gate[:, None])
    return y[:n_tok]


def _shared_expert(xt, w_gate, w_up, w_down):
    return (jax.nn.silu(xt @ w_gate) * (xt @ w_up)) @ w_down


def setup_inputs(seed: int = 0) -> dict:
    key = jax.random.key(seed)
    ks = jax.random.split(key, 24)
    f32 = jnp.float32

    def nrm(k, shape, scale):
        return jax.random.normal(k, shape, f32) * scale

    x = nrm(ks[0], (BATCH, SEQ, D_MODEL), 1.0)
    col_scale = jnp.ones((N_IN,), f32).at[_XBC0:_XBC0 + D_SSM].set(BETA).at[_I0:_G0].set(BETA)
    w_in = nrm(ks[1], (DEPTH, D_MODEL, N_IN), D_MODEL ** -0.5) * col_scale
    conv_w = jax.random.uniform(ks[2], (DEPTH, SSD_CONV, D_CONV), f32, -1.0, 1.0) * SSD_CONV ** -0.5
    conv_b = nrm(ks[3], (DEPTH, D_CONV), 0.02)
    dt0 = jnp.exp(jax.random.uniform(ks[4], (DEPTH, SSD_HEADS), f32, math.log(1e-3), math.log(1e-1)))
    dt_bias = dt0 + jnp.log(-jnp.expm1(-dt0))
    a_log = jnp.log(jax.random.uniform(ks[5], (DEPTH, SSD_HEADS), f32, 1.0, 16.0))
    d_skip = 1.0 + nrm(ks[6], (DEPTH, SSD_HEADS), 0.1)
    ssd_norm_w = 1.0 + nrm(ks[7], (DEPTH, D_SSM), 0.02)
    hgrn_lb_logits = nrm(ks[8], (DEPTH + 1, HGRN_HEADS * HGRN_DK), 0.5)
    hgrn_norm_w = 1.0 + nrm(ks[9], (DEPTH, D_HGRN), 0.02)
    w_out = nrm(ks[10], (DEPTH, D_MIX, D_MODEL), D_MIX ** -0.5 * BETA)
    ln1_g = 1.0 + nrm(ks[11], (DEPTH, D_MODEL), 0.02)
    ln1_b = nrm(ks[12], (DEPTH, D_MODEL), 0.02)
    w_router = nrm(ks[13], (DEPTH, D_MODEL, N_EXPERTS), D_MODEL ** -0.5)
    router_bias = nrm(ks[14], (DEPTH, N_EXPERTS), 0.01)
    w_gate_e = nrm(ks[15], (DEPTH, N_EXPERTS, D_MODEL, D_EXPERT), D_MODEL ** -0.5)
    w_up_e = nrm(ks[16], (DEPTH, N_EXPERTS, D_MODEL, D_EXPERT), D_MODEL ** -0.5)
    w_down_e = nrm(ks[17], (DEPTH, N_EXPERTS, D_EXPERT, D_MODEL), D_EXPERT ** -0.5 * BETA)
    w_gate_s = nrm(ks[18], (DEPTH, D_MODEL, D_SHARED), D_MODEL ** -0.5)
    w_up_s = nrm(ks[19], (DEPTH, D_MODEL, D_SHARED), D_MODEL ** -0.5)
    w_down_s = nrm(ks[20], (DEPTH, D_SHARED, D_MODEL), D_SHARED ** -0.5 * BETA)
    ln2_g = 1.0 + nrm(ks[21], (DEPTH, D_MODEL), 0.02)
    ln2_b = nrm(ks[22], (DEPTH, D_MODEL), 0.02)
    return {'x': x, 'w_in': w_in, 'conv_w': conv_w, 'conv_b': conv_b, 'dt_bias': dt_bias,
            'a_log': a_log, 'd_skip': d_skip, 'ssd_norm_w': ssd_norm_w,
            'hgrn_lb_logits': hgrn_lb_logits, 'hgrn_norm_w': hgrn_norm_w, 'w_out': w_out,
            'ln1_g': ln1_g, 'ln1_b': ln1_b, 'w_router': w_router, 'router_bias': router_bias,
            'w_gate_e': w_gate_e, 'w_up_e': w_up_e, 'w_down_e': w_down_e,
            'w_gate_s': w_gate_s, 'w_up_s': w_up_s, 'w_down_s': w_down_s,
            'ln2_g': ln2_g, 'ln2_b': ln2_b}


def reference(x, w_in, conv_w, conv_b, dt_bias, a_log, d_skip, ssd_norm_w, hgrn_lb_logits,
              hgrn_norm_w, w_out, ln1_g, ln1_b, w_router, router_bias, w_gate_e, w_up_e,
              w_down_e, w_gate_s, w_up_s, w_down_s, ln2_g, ln2_b):
    bsz, t, d = x.shape
    lower_bounds = jnp.cumsum(jax.nn.softmax(hgrn_lb_logits.astype(jnp.float32), axis=0), axis=0)
    h = x
    for l in range(DEPTH):
        p = h @ w_in[l]
        y_ssd = _ssd_mixer(p[..., :_XBC0], p[..., _XBC0:_DT0], p[..., _DT0:_Q0],
                           conv_w[l], conv_b[l], dt_bias[l], a_log[l], d_skip[l], ssd_norm_w[l])
        y_hgrn = _hgrn2_mixer(p[..., _Q0:_F0], p[..., _F0:_I0], p[..., _I0:_G0], p[..., _G0:],
                              lower_bounds[l], hgrn_norm_w[l])
        mix = jnp.concatenate([y_ssd, y_hgrn], axis=-1) @ w_out[l]
        h = _layer_norm(ALPHA * h + mix, ln1_g[l], ln1_b[l])
        ht = h.reshape(bsz * t, d)
        ffn = (_routed_experts(ht, w_router[l], router_bias[l], w_gate_e[l], w_up_e[l], w_down_e[l])
               + _shared_expert(ht, w_gate_s[l], w_up_s[l], w_down_s[l]))
        h = _layer_norm(ALPHA * h + ffn.reshape(bsz, t, d), ln2_g[l], ln2_b[l])
    return h
```

```python
import jax
import jax.numpy as jnp
from jax import lax
from jax.experimental import pallas as pl
from jax.experimental.pallas import tpu as pltpu

F32 = jnp.float32
BF16 = jnp.bfloat16
I32 = jnp.int32

D_MODEL = 1024
D_SSM = 512
D_HGRN = 512
SSD_HEADS = 8
SSD_HEAD_DIM = 64
SSD_GROUPS = 2
SSD_STATE = 128
SSD_CONV = 4
SSD_CHUNK = 128
D_CONV = D_SSM + 2 * SSD_GROUPS * SSD_STATE
HGRN_HEADS = 4
HGRN_DK = 128
HGRN_CHUNK = 64
HGRN_SUB = 8
N_EXPERTS = 256
TOP_K = 8
N_EXPERT_GROUPS = 8
TOPK_GROUPS = 4
D_EXPERT = 256
ROUTED_SCALE = 2.5
ALPHA = 2.0 ** 0.25
LN_EPS = 1e-5
RMS_EPS = 1e-6

LANES = 128
SUBLANES = 8
ROW_TILES = D_MODEL // LANES
DT_PAD = LANES
N_IN_PAD = D_SSM + D_CONV + 4 * D_HGRN + DT_PAD

TM_PROJ = 256
TM_TOK = 256
EXPERT_ROWS = 256
NEG = -1e30


def _sigmoid(x):
    return 1.0 / (1.0 + jnp.exp(-x))


def _silu(x):
    return x * _sigmoid(x)


def _split3(x):
    hi = x.astype(BF16)
    r = x - hi.astype(F32)
    mid = r.astype(BF16)
    lo = (r - mid.astype(F32)).astype(BF16)
    return hi, mid, lo


def _dot(a, b):
    return jnp.dot(a, b, preferred_element_type=F32)


def _dot_nt(a, b):
    return lax.dot_general(a, b, (((1,), (1,)), ((), ())), preferred_element_type=F32)


def _dot_tn(a, b):
    return lax.dot_general(a, b, (((0,), (0,)), ((), ())), preferred_element_type=F32)


def _sel_dot(sel, x):
    hi, mid, lo = _split3(x)
    return _dot(sel, hi) + _dot(sel, mid) + _dot(sel, lo)


def _dot_sel(x, sel):
    hi, mid, lo = _split3(x)
    return _dot(hi, sel) + _dot(mid, sel) + _dot(lo, sel)


def _layer_norm(x, g, b):
    mu = jnp.mean(x, axis=-1, keepdims=True)
    xc = x - mu
    var = jnp.mean(xc * xc, axis=-1, keepdims=True)
    return xc * lax.rsqrt(var + LN_EPS) * g + b


def _inproj_kernel(x_ref, w_ref, z_ref, xbc_ref, q_ref, f_ref, i_ref, g_ref, dt_ref):
    xb = x_ref[...].astype(BF16)
    col = 0
    for ref in (z_ref, xbc_ref, q_ref, f_ref, i_ref, g_ref, dt_ref):
        n = ref.shape[-1]
        ref[...] = _dot(xb, w_ref[:, col:col + n])
        col += n


def _inproj(x2, w_perm):
    t = x2.shape[0]
    widths = (D_SSM, D_CONV, D_HGRN, D_HGRN, D_HGRN, D_HGRN, DT_PAD)
    return pl.pallas_call(
        _inproj_kernel,
        grid=(t // TM_PROJ,),
        in_specs=[pl.BlockSpec((TM_PROJ, D_MODEL), lambda i: (i, 0)),
                  pl.BlockSpec((D_MODEL, N_IN_PAD), lambda i: (0, 0))],
        out_specs=[pl.BlockSpec((TM_PROJ, n), lambda i: (i, 0)) for n in widths],
        out_shape=[jax.ShapeDtypeStruct((t, n), F32) for n in widths],
        compiler_params=pltpu.CompilerParams(dimension_semantics=("arbitrary",),
                                             vmem_limit_bytes=48 * 1024 * 1024),
        name="inproj",
    )(x2, w_perm)


def _ssd_kernel(z_ref, xbc_ref, dt_ref, cw_ref, cb_ref, dtb_ref, alog_ref, dskip_ref, nw_ref,
                y_ref, ext_ref, st_ref):
    L = SSD_CHUNK
    halo = SUBLANES

    @pl.when(pl.program_id(0) == 0)
    def _():
        ext_ref[0:halo, :] = jnp.zeros((halo, D_CONV), F32)
        st_ref[...] = jnp.zeros(st_ref.shape, F32)

    ext_ref[halo:halo + L, :] = xbc_ref[...]
    acc = jnp.broadcast_to(cb_ref[...], (L, D_CONV))
    for k in range(SSD_CONV):
        off = halo - (SSD_CONV - 1) + k
        acc = acc + cw_ref[k:k + 1, :] * ext_ref[off:off + L, :]
    ext_ref[0:halo, :] = xbc_ref[L - halo:L, :]
    u = _silu(acc)
    xs = u[:, :D_SSM]
    bm = u[:, D_SSM:D_SSM + SSD_GROUPS * SSD_STATE]
    cm = u[:, D_SSM + SSD_GROUPS * SSD_STATE:]

    draw = dt_ref[...] + dtb_ref[...]
    dt = jnp.maximum(draw, 0.0) + jnp.log(1.0 + jnp.exp(-jnp.abs(draw)))
    ad = dt * (-jnp.exp(alog_ref[...]))
    rows = lax.broadcasted_iota(I32, (L, L), 0)
    cols = lax.broadcasted_iota(I32, (L, L), 1)
    causal = rows >= cols
    a_cum = _sel_dot(causal.astype(BF16), ad)
    a_cum_t = a_cum.T

    hrow = lax.broadcasted_iota(I32, (LANES, D_SSM), 0)
    hcol = lax.broadcasted_iota(I32, (LANES, D_SSM), 1) // SSD_HEAD_DIM
    expand = (hrow == hcol).astype(BF16)
    dt_x = _dot_sel(dt, expand)
    acx = _dot_sel(a_cum, expand)
    last = acx[L - 1:L, :]
    ea_x = jnp.exp(acx)
    dec_x = jnp.exp(last - acx)
    elast_x = jnp.exp(last)

    xdt = xs * dt_x
    gw = SSD_HEADS // SSD_GROUPS * SSD_HEAD_DIM
    lane_head = lax.broadcasted_iota(I32, (L, gw), 1) // SSD_HEAD_DIM
    ys = []
    for g in range(SSD_GROUPS):
        bg = bm[:, g * SSD_STATE:(g + 1) * SSD_STATE]
        cg = cm[:, g * SSD_STATE:(g + 1) * SSD_STATE].astype(BF16)
        bg_t = bg.T.astype(BF16)
        gmat = _dot(cg, bg_t)
        xdt_g = xdt[:, g * gw:(g + 1) * gw]
        xdt_gb = xdt_g.astype(BF16)
        r_prev = st_ref[g]
        y_g = _dot(cg, r_prev.astype(BF16)) * ea_x[:, g * gw:(g + 1) * gw]
        new_s = _dot(bg_t, (xdt_g * dec_x[:, g * gw:(g + 1) * gw]).astype(BF16))
        st_ref[g] = r_prev * elast_x[:, g * gw:(g + 1) * gw] + new_s
        for j in range(SSD_HEADS // SSD_GROUPS):
            h = g * (SSD_HEADS // SSD_GROUPS) + j
            diff = a_cum[:, h:h + 1] - a_cum_t[h:h + 1, :]
            decay = jnp.exp(jnp.where(causal, diff, NEG))
            yd = _dot((gmat * decay).astype(BF16), xdt_gb)
            y_g = y_g + jnp.where(lane_head == j, yd, 0.0)
        ys.append(y_g)
    y = jnp.concatenate(ys, axis=-1) + xs * dskip_ref[...]
    y = y * _silu(z_ref[...])
    outs = []
    ng = D_SSM // SSD_GROUPS
    for g in range(SSD_GROUPS):
        yg = y[:, g * ng:(g + 1) * ng]
        ms = jnp.mean(yg * yg, axis=-1, keepdims=True)
        outs.append(yg * lax.rsqrt(ms + RMS_EPS))
    y_ref[...] = jnp.concatenate(outs, axis=-1) * nw_ref[...]


def _ssd(z, xbc, dt, conv_w, conv_b, dt_bias, a_log, d_skip, norm_w):
    t = z.shape[0]
    L = SSD_CHUNK
    pad = lambda v: jnp.pad(v.astype(F32), (0, LANES - v.shape[0])).reshape(1, LANES)
    full = lambda shape: pl.BlockSpec(shape, lambda c: (0,) * len(shape))
    return pl.pallas_call(
        _ssd_kernel,
        grid=(t // L,),
        in_specs=[pl.BlockSpec((L, D_SSM), lambda c: (c, 0)),
                  pl.BlockSpec((L, D_CONV), lambda c: (c, 0)),
                  pl.BlockSpec((L, DT_PAD), lambda c: (c, 0)),
                  full((SSD_CONV, D_CONV)), full((1, D_CONV)), full((1, LANES)), full((1, LANES)),
                  full((1, D_SSM)), full((1, D_SSM))],
        out_specs=pl.BlockSpec((L, D_SSM), lambda c: (c, 0)),
        out_shape=jax.ShapeDtypeStruct((t, D_SSM), F32),
        scratch_shapes=[pltpu.VMEM((L + SUBLANES, D_CONV), F32),
                        pltpu.VMEM((SSD_GROUPS, SSD_STATE, D_SSM // SSD_GROUPS), F32)],
        compiler_params=pltpu.CompilerParams(dimension_semantics=("arbitrary",)),
        name="ssd",
    )(z, xbc, dt, conv_w.astype(F32), conv_b.reshape(1, D_CONV).astype(F32), pad(dt_bias), pad(a_log),
      jnp.repeat(d_skip.astype(F32), SSD_HEAD_DIM).reshape(1, D_SSM), norm_w.reshape(1, D_SSM).astype(F32))


def _tile_bcast(x, r):
    n, d = x.shape
    x3 = x.reshape(n // SUBLANES, SUBLANES, d)
    return jnp.broadcast_to(x3[:, r:r + 1, :], x3.shape).reshape(n, d)


def _hgrn_kernel(q_ref, f_ref, i_ref, g_ref, lb_ref, nw_ref, o_ref, st_ref):
    C = HGRN_CHUNK
    S = HGRN_SUB
    nsub = C // S

    @pl.when(pl.program_id(0) == 0)
    def _():
        st_ref[...] = jnp.zeros(st_ref.shape, F32)

    lb = lb_ref[...]
    fg = lb + (1.0 - lb) * _sigmoid(f_ref[...])
    kk = 1.0 - fg
    qs = _silu(q_ref[...])
    v = i_ref[...]
    rows = lax.broadcasted_iota(I32, (C, C), 0)
    cols = lax.broadcasted_iota(I32, (C, C), 1)
    cum = _sel_dot((rows >= cols).astype(BF16), jnp.log(fg))
    last = cum[C - 1:C, :]

    trow = lax.broadcasted_iota(I32, (C, D_HGRN), 0)
    tmod = trow % S
    tblk = trow // S

    xr = []
    for r in range(S):
        e = jnp.exp(jnp.where(tmod >= r, cum - _tile_bcast(cum, r), NEG))
        xr.append(qs * _tile_bcast(kk, r) * e)
    kbe = kk * jnp.exp(_tile_bcast(cum, S - 1) - cum)
    aj, bj = [], []
    for j in range(nsub - 1):
        ce = cum[j * S + S - 1:j * S + S, :]
        aj.append(qs * jnp.exp(jnp.where(tblk > j, cum - ce, NEG)))
        bj.append(jnp.where(tblk == j, kbe, 0.0))
    qd = (qs * jnp.exp(cum)).astype(BF16)
    kd = (kk * jnp.exp(last - cum)).astype(BF16)
    elast = jnp.exp(last)

    att_col = lax.broadcasted_iota(I32, (C, C), 1)
    att_rblk = lax.broadcasted_iota(I32, (C, C), 0) // S * S
    outs = []
    for h in range(HGRN_HEADS):
        blk = slice(h * HGRN_DK, (h + 1) * HGRN_DK)
        a_cat = jnp.concatenate([a[:, blk] for a in aj], axis=-1).astype(BF16)
        b_cat = jnp.concatenate([b[:, blk] for b in bj], axis=-1).astype(BF16)
        att = _dot_nt(a_cat, b_cat)
        for r in range(S):
            rs = jnp.sum(xr[r][:, blk], axis=-1, keepdims=True)
            att = jnp.where(att_col == att_rblk + r, rs, att)
        vb = v[:, blk].astype(BF16)
        s_prev = st_ref[h]
        o = _dot_nt(qd[:, blk], s_prev.astype(BF16)) + _dot(att.astype(BF16), vb)
        st_ref[h] = s_prev * elast[:, blk] + _dot_tn(vb, kd[:, blk])
        ms = jnp.mean(o * o, axis=-1, keepdims=True)
        outs.append(o * lax.rsqrt(ms + RMS_EPS))
    o_ref[...] = jnp.concatenate(outs, axis=-1) * nw_ref[...] * _silu(g_ref[...])


def _hgrn(q, f, i, g, lower_bound, norm_w):
    t = q.shape[0]
    C = HGRN_CHUNK
    tok = pl.BlockSpec((C, D_HGRN), lambda c: (c, 0))
    vec = pl.BlockSpec((1, D_HGRN), lambda c: (0, 0))
    return pl.pallas_call(
        _hgrn_kernel,
        grid=(t // C,),
        in_specs=[tok, tok, tok, tok, vec, vec],
        out_specs=tok,
        out_shape=jax.ShapeDtypeStruct((t, D_HGRN), F32),
        scratch_shapes=[pltpu.VMEM((HGRN_HEADS, D_HGRN // HGRN_HEADS, HGRN_DK), F32)],
        compiler_params=pltpu.CompilerParams(dimension_semantics=("arbitrary",)),
        name="hgrn",
    )(q, f, i, g, lower_bound.reshape(1, D_HGRN).astype(F32), norm_w.reshape(1, D_HGRN).astype(F32))


def _post_kernel(x_ref, ys_ref, yh_ref, wo_ref, g1_ref, b1_ref, wrh_ref, wrl_ref, rb_ref,
                 wgs_ref, wus_ref, wds_ref, htt_ref, base_ref, idx_ref, gate_ref):
    tm = x_ref.shape[0]
    mix = (_dot(ys_ref[...].astype(BF16), wo_ref[0:D_SSM, :])
           + _dot(yh_ref[...].astype(BF16), wo_ref[D_SSM:, :]))
    h1 = _layer_norm(ALPHA * x_ref[...] + mix, g1_ref[...], b1_ref[...])
    for k in range(ROW_TILES):
        htt_ref[:, k, :] = h1[:, k * LANES:(k + 1) * LANES]
    hb = h1.astype(BF16)
    hid = _silu(_dot(hb, wgs_ref[...])) * _dot(hb, wus_ref[...])
    base_ref[...] = ALPHA * h1 + _dot(hid.astype(BF16), wds_ref[...])

    hlo = (h1 - hb.astype(F32)).astype(BF16)
    logits = _dot_nt(wrh_ref[...], hb) + _dot_nt(wrh_ref[...], hlo) + _dot_nt(wrl_ref[...], hb)
    scores = _sigmoid(logits)
    biased = scores + rb_ref[...]
    per_group = N_EXPERTS // N_EXPERT_GROUPS
    eidx = lax.broadcasted_iota(I32, (N_EXPERTS, tm), 0)
    big = jnp.int32(1 << 20)
    gsc = []
    bidx = lax.broadcasted_iota(I32, (per_group, tm), 0)
    for gi in range(N_EXPERT_GROUPS):
        blk = biased[gi * per_group:(gi + 1) * per_group, :]
        m1 = jnp.max(blk, axis=0, keepdims=True)
        i1 = jnp.min(jnp.where(blk == m1, bidx, big), axis=0, keepdims=True)
        m2 = jnp.max(jnp.where(bidx == i1, NEG, blk), axis=0, keepdims=True)
        gsc.append(m1 + m2)
    cur = jnp.concatenate(gsc, axis=0)
    gidx = lax.broadcasted_iota(I32, (N_EXPERT_GROUPS, tm), 0)
    gsel = jnp.zeros((N_EXPERT_GROUPS, tm), F32)
    for _ in range(TOPK_GROUPS):
        m = jnp.max(cur, axis=0, keepdims=True)
        i = jnp.min(jnp.where(cur == m, gidx, big), axis=0, keepdims=True)
        hit = gidx == i
        gsel = jnp.where(hit, 1.0, gsel)
        cur = jnp.where(hit, NEG, cur)
    emask = jnp.concatenate(
        [jnp.broadcast_to(gsel[gi:gi + 1, :], (per_group, tm)) for gi in range(N_EXPERT_GROUPS)], axis=0)
    masked = jnp.where(emask > 0.0, biased, NEG)
    idx_rows, gate_rows = [], []
    for _ in range(TOP_K):
        m = jnp.max(masked, axis=0, keepdims=True)
        i = jnp.min(jnp.where(masked == m, eidx, big), axis=0, keepdims=True)
        hit = eidx == i
        idx_rows.append(i)
        gate_rows.append(jnp.sum(jnp.where(hit, scores, 0.0), axis=0, keepdims=True))
        masked = jnp.where(hit, NEG, masked)
    gates = jnp.concatenate(gate_rows, axis=0)
    gates = gates / jnp.sum(gates, axis=0, keepdims=True) * ROUTED_SCALE
    idx_ref[0] = jnp.concatenate(idx_rows, axis=0)
    gate_ref[0] = gates


def _post(x2, y_ssd, y_hgrn, wo, g1, b1, wr_hi, wr_lo, rbias, wgs, wus, wds):
    t = x2.shape[0]
    tm = TM_TOK
    nt = t // tm
    full = lambda shape: pl.BlockSpec(shape, lambda i: (0,) * len(shape))
    return pl.pallas_call(
        _post_kernel,
        grid=(nt,),
        in_specs=[pl.BlockSpec((tm, D_MODEL), lambda i: (i, 0)),
                  pl.BlockSpec((tm, D_SSM), lambda i: (i, 0)),
                  pl.BlockSpec((tm, D_HGRN), lambda i: (i, 0)),
                  full((D_MODEL, D_MODEL)), full((1, D_MODEL)), full((1, D_MODEL)),
                  full((N_EXPERTS, D_MODEL)), full((N_EXPERTS, D_MODEL)), full((N_EXPERTS, 1)),
                  full((D_MODEL, D_EXPERT)), full((D_MODEL, D_EXPERT)), full((D_EXPERT, D_MODEL))],
        out_specs=[pl.BlockSpec((tm, ROW_TILES, LANES), lambda i: (i, 0, 0)),
                   pl.BlockSpec((tm, D_MODEL), lambda i: (i, 0)),
                   pl.BlockSpec((1, TOP_K, tm), lambda i: (i, 0, 0)),
                   pl.BlockSpec((1, TOP_K, tm), lambda i: (i, 0, 0))],
        out_shape=[jax.ShapeDtypeStruct((t, ROW_TILES, LANES), F32),
                   jax.ShapeDtypeStruct((t, D_MODEL), F32),
                   jax.ShapeDtypeStruct((nt, TOP_K, tm), I32),
                   jax.ShapeDtypeStruct((nt, TOP_K, tm), F32)],
        compiler_params=pltpu.CompilerParams(dimension_semantics=("arbitrary",),
                                             vmem_limit_bytes=48 * 1024 * 1024),
        name="post",
    )(x2, y_ssd, y_hgrn, wo, g1, b1, wr_hi, wr_lo, rbias, wgs, wus, wds)


def _max_blocks(t):
    return (t * TOP_K + N_EXPERTS * (EXPERT_ROWS - 1)) // EXPERT_ROWS


def _pos_kernel(idx_ref, dest_ref, be_ref, nb_ref, cnt_ref, start_ref):
    phase = pl.program_id(0)
    i = pl.program_id(1)
    tm = idx_ref.shape[-1]
    nblk_pad = be_ref.shape[-1]
    idx = idx_ref[0]
    eidx = lax.broadcasted_iota(I32, (N_EXPERTS, tm), 0)
    sel = [eidx == idx[k:k + 1, :] for k in range(TOP_K)]
    onehot = sel[0]
    for k in range(1, TOP_K):
        onehot = jnp.logical_or(onehot, sel[k])
    mt = jnp.where(onehot, 1.0, 0.0).astype(BF16)
    tile_cnt = _dot(mt, jnp.ones((tm, LANES), BF16))

    @pl.when(jnp.logical_and(phase == 0, i == 0))
    def _():
        cnt_ref[...] = jnp.zeros(cnt_ref.shape, F32)

    @pl.when(phase == 0)
    def _():
        cnt_ref[...] += tile_cnt

    @pl.when(jnp.logical_and(phase == 1, i == 0))
    def _():
        nb = jnp.floor((cnt_ref[...] + (EXPERT_ROWS - 1)) * (1.0 / EXPERT_ROWS))
        r = lax.broadcasted_iota(I32, (N_EXPERTS, N_EXPERTS), 0)
        c = lax.broadcasted_iota(I32, (N_EXPERTS, N_EXPERTS), 1)
        end = _dot((r >= c).astype(BF16), nb.astype(BF16))
        start_ref[...] = (end - nb) * EXPERT_ROWS
        cnt_ref[...] = jnp.zeros(cnt_ref.shape, F32)
        bi = lax.broadcasted_iota(I32, (N_EXPERTS, nblk_pad), 1).astype(F32)
        owner = jnp.sum(jnp.where(end[:, 0:1] <= bi, 1.0, 0.0), axis=0, keepdims=True)
        be_ref[...] = jnp.minimum(owner, N_EXPERTS - 1.0).astype(I32)
        nb_ref[...] = end[N_EXPERTS - 1:N_EXPERTS, :].astype(I32)

    @pl.when(phase == 1)
    def _():
        r = lax.broadcasted_iota(I32, (tm, tm), 0)
        c = lax.broadcasted_iota(I32, (tm, tm), 1)
        before = _dot(mt, (r < c).astype(BF16))
        slot = start_ref[:, 0:1] + cnt_ref[:, 0:1] + before
        rows = [jnp.sum(jnp.where(sel[k], slot, 0.0), axis=0, keepdims=True) for k in range(TOP_K)]
        dest_ref[0] = jnp.concatenate(rows, axis=0).astype(I32)
        cnt_ref[...] += tile_cnt


def _positions(idx, t):
    nt, _, tm = idx.shape
    nblk_pad = -(-_max_blocks(t) // LANES) * LANES
    return pl.pallas_call(
        _pos_kernel,
        grid=(2, nt),
        in_specs=[pl.BlockSpec((1, TOP_K, tm), lambda p, i: (i, 0, 0))],
        out_specs=[pl.BlockSpec((1, TOP_K, tm), lambda p, i: (i * p, 0, 0)),
                   pl.BlockSpec((1, nblk_pad), lambda p, i: (0, 0)),
                   pl.BlockSpec((1, LANES), lambda p, i: (0, 0))],
        out_shape=[jax.ShapeDtypeStruct((nt, TOP_K, tm), I32),
                   jax.ShapeDtypeStruct((1, nblk_pad), I32),
                   jax.ShapeDtypeStruct((1, LANES), I32)],
        scratch_shapes=[pltpu.VMEM((N_EXPERTS, LANES), F32), pltpu.VMEM((N_EXPERTS, LANES), F32)],
        compiler_params=pltpu.CompilerParams(dimension_semantics=("arbitrary", "arbitrary")),
        name="positions",
    )(idx)


def _row_copy_wait(src_rows, dst_hbm, sem, n):
    pltpu.make_async_copy(src_rows.at[pl.ds(0, n)], dst_hbm.at[pl.ds(0, n)], sem).wait()


def _dispatch_kernel(dest_ref, h_ref, xs_ref, sem):
    tm = h_ref.shape[0]

    def body(t, carry):
        for k in range(TOP_K):
            d = dest_ref[0, k, t]
            pltpu.make_async_copy(h_ref.at[t], xs_ref.at[d], sem).start()
        return carry

    lax.fori_loop(0, tm, body, 0)
    for _ in range(TOP_K):
        _row_copy_wait(h_ref, xs_ref, sem, tm)


def _dispatch(dest, htt, n_rows):
    t = htt.shape[0]
    tm = dest.shape[-1]
    return pl.pallas_call(
        _dispatch_kernel,
        grid=(t // tm,),
        in_specs=[pl.BlockSpec((1, TOP_K, tm), lambda i: (i, 0, 0), memory_space=pltpu.SMEM),
                  pl.BlockSpec((tm, ROW_TILES, LANES), lambda i: (i, 0, 0))],
        out_specs=pl.BlockSpec(memory_space=pl.ANY),
        out_shape=jax.ShapeDtypeStruct((n_rows, ROW_TILES, LANES), F32),
        scratch_shapes=[pltpu.SemaphoreType.DMA(())],
        compiler_params=pltpu.CompilerParams(dimension_semantics=("arbitrary",), has_side_effects=True),
        name="dispatch",
    )(dest, htt)


def _expert_kernel(be_ref, nb_ref, xs_ref, wg_ref, wu_ref, wd_ref, ys_ref, wgb_ref, wub_ref, wdb_ref):
    i = pl.program_id(0)
    prev = be_ref[jnp.maximum(i - 1, 0)]
    fresh = jnp.logical_or(i == 0, be_ref[i] != prev)

    @pl.when(jnp.logical_and(i < nb_ref[0], fresh))
    def _():
        wgb_ref[...] = wg_ref[0].astype(BF16)
        wub_ref[...] = wu_ref[0].astype(BF16)
        wdb_ref[...] = wd_ref[0].astype(BF16)

    @pl.when(i < nb_ref[0])
    def _():
        xb = jnp.concatenate([xs_ref[:, k, :] for k in range(ROW_TILES)], axis=-1).astype(BF16)
        hid = _silu(_dot(xb, wgb_ref[...])) * _dot(xb, wub_ref[...])
        yb = _dot(hid.astype(BF16), wdb_ref[...])
        for k in range(ROW_TILES):
            ys_ref[:, k, :] = yb[:, k * LANES:(k + 1) * LANES]


def _experts(block_e, n_blocks, xs, w_gate, w_up, w_down):
    n_rows = xs.shape[0]
    nblk = n_rows // EXPERT_ROWS
    blk = lambda i, be, nb: (jnp.minimum(i, nb[0] - 1), 0, 0)
    wmap = lambda i, be, nb: (be[jnp.minimum(i, nb[0] - 1)], 0, 0)
    grid_spec = pltpu.PrefetchScalarGridSpec(
        num_scalar_prefetch=2,
        grid=(nblk,),
        in_specs=[pl.BlockSpec((EXPERT_ROWS, ROW_TILES, LANES), blk),
                  pl.BlockSpec((1, D_MODEL, D_EXPERT), wmap),
                  pl.BlockSpec((1, D_MODEL, D_EXPERT), wmap),
                  pl.BlockSpec((1, D_EXPERT, D_MODEL), wmap)],
        out_specs=pl.BlockSpec((EXPERT_ROWS, ROW_TILES, LANES), blk),
        scratch_shapes=[pltpu.VMEM((D_MODEL, D_EXPERT), BF16), pltpu.VMEM((D_MODEL, D_EXPERT), BF16),
                        pltpu.VMEM((D_EXPERT, D_MODEL), BF16)],
    )
    return pl.pallas_call(
        _expert_kernel,
        grid_spec=grid_spec,
        out_shape=jax.ShapeDtypeStruct((n_rows, ROW_TILES, LANES), F32),
        compiler_params=pltpu.CompilerParams(dimension_semantics=("arbitrary",),
                                             vmem_limit_bytes=48 * 1024 * 1024),
        name="experts",
    )(block_e, n_blocks, xs, w_gate, w_up, w_down)


def _combine_kernel(dest_ref, gate_ref, base_ref, g2_ref, b2_ref, ys_ref, o_ref, rows_ref, sem):
    tm = base_ref.shape[0]

    def body(t, carry):
        for k in range(TOP_K):
            d = dest_ref[0, k, t]
            pltpu.make_async_copy(ys_ref.at[d], rows_ref.at[k * tm + t], sem).start()
        return carry

    lax.fori_loop(0, tm, body, 0)
    gates_t = jnp.concatenate([gate_ref[0], jnp.zeros((LANES - TOP_K, tm), F32)], axis=0).T
    for k in range(TOP_K):
        pltpu.make_async_copy(ys_ref.at[pl.ds(0, tm)], rows_ref.at[pl.ds(k * tm, tm)], sem).wait()
    cols = []
    for c in range(ROW_TILES):
        acc = base_ref[:, c * LANES:(c + 1) * LANES]
        for k in range(TOP_K):
            acc = acc + gates_t[:, k:k + 1] * rows_ref[k * tm:(k + 1) * tm, c, :]
        cols.append(acc)
    o_ref[...] = _layer_norm(jnp.concatenate(cols, axis=-1), g2_ref[...], b2_ref[...])


def _combine(dest, gates, base, g2, b2, ys):
    t = base.shape[0]
    tm = dest.shape[-1]
    return pl.pallas_call(
        _combine_kernel,
        grid=(t // tm,),
        in_specs=[pl.BlockSpec((1, TOP_K, tm), lambda i: (i, 0, 0), memory_space=pltpu.SMEM),
                  pl.BlockSpec((1, TOP_K, tm), lambda i: (i, 0, 0)),
                  pl.BlockSpec((tm, D_MODEL), lambda i: (i, 0)),
                  pl.BlockSpec((1, D_MODEL), lambda i: (0, 0)),
                  pl.BlockSpec((1, D_MODEL), lambda i: (0, 0)),
                  pl.BlockSpec(memory_space=pl.ANY)],
        out_specs=pl.BlockSpec((tm, D_MODEL), lambda i: (i, 0)),
        out_shape=jax.ShapeDtypeStruct((t, D_MODEL), F32),
        scratch_shapes=[pltpu.VMEM((TOP_K * tm, ROW_TILES, LANES), F32), pltpu.SemaphoreType.DMA(())],
        compiler_params=pltpu.CompilerParams(dimension_semantics=("arbitrary",),
                                             vmem_limit_bytes=48 * 1024 * 1024),
        name="combine",
    )(dest, gates, base, g2, b2, ys)


def _layer(h2, w_in, conv_w, conv_b, dt_bias, a_log, d_skip, ssd_norm_w, lower_bound, hgrn_norm_w, w_out,
           ln1_g, ln1_b, w_router, router_bias, w_gate_e, w_up_e, w_down_e, w_gate_s, w_up_s, w_down_s,
           ln2_g, ln2_b):
    t = h2.shape[0]
    dt0 = D_SSM + D_CONV
    q0 = dt0 + SSD_HEADS
    w_perm = jnp.concatenate(
        [w_in[:, :dt0], w_in[:, q0:], w_in[:, dt0:q0], jnp.zeros((D_MODEL, DT_PAD - SSD_HEADS), w_in.dtype)],
        axis=1).astype(BF16)
    z, xbc, q, f, i, g, dt = _inproj(h2, w_perm)
    y_ssd = _ssd(z, xbc, dt, conv_w, conv_b, dt_bias, a_log, d_skip, ssd_norm_w)
    y_hgrn = _hgrn(q, f, i, g, lower_bound, hgrn_norm_w)

    wr_t = w_router.astype(F32).T
    wr_hi = wr_t.astype(BF16)
    wr_lo = (wr_t - wr_hi.astype(F32)).astype(BF16)
    row = lambda v: v.reshape(1, -1).astype(F32)
    htt, base, idx, gates = _post(
        h2, y_ssd, y_hgrn, w_out.astype(BF16), row(ln1_g), row(ln1_b), wr_hi, wr_lo,
        router_bias.reshape(N_EXPERTS, 1).astype(F32),
        w_gate_s.astype(BF16), w_up_s.astype(BF16), w_down_s.astype(BF16))

    dest, block_e, n_blocks = _positions(idx, t)
    n_rows = _max_blocks(t) * EXPERT_ROWS
    xs = _dispatch(dest, htt, n_rows)
    ys = _experts(block_e.reshape(-1), n_blocks.reshape(-1), xs, w_gate_e, w_up_e, w_down_e)
    return _combine(dest, gates, base, row(ln2_g), row(ln2_b), ys)


def kernel(x, w_in, conv_w, conv_b, dt_bias, a_log, d_skip, ssd_norm_w, hgrn_lb_logits, hgrn_norm_w, w_out,
           ln1_g, ln1_b, w_router, router_bias, w_gate_e, w_up_e, w_down_e, w_gate_s, w_up_s, w_down_s,
           ln2_g, ln2_b):
    bsz, t, d = x.shape
    assert bsz == 1 and d == D_MODEL, "the recurrent mixers carry state across the flattened token axis"
    depth = w_in.shape[0]
    lower_bounds = jnp.cumsum(jax.nn.softmax(hgrn_lb_logits.astype(F32), axis=0), axis=0)
    h = x.reshape(bsz * t, d)
    for l in range(depth):
        h = _layer(h, w_in[l], conv_w[l], conv_b[l], dt_bias[l], a_log[l], d_skip[l], ssd_norm_w[l],
                   lower_bounds[l], hgrn_norm_w[l], w_out[l], ln1_g[l], ln1_b[l], w_router[l],
                   router_bias[l], w_gate_e[l], w_up_e[l], w_down_e[l], w_gate_s[l], w_up_s[l],
                   w_down_s[l], ln2_g[l], ln2_b[l])
    return h.reshape(bsz, t, d)
```

```python
import jax
import jax.numpy as jnp
from jax import lax
from jax.experimental import pallas as pl
from jax.experimental.pallas import tpu as pltpu

F32 = jnp.float32
BF16 = jnp.bfloat16
I32 = jnp.int32

D_MODEL = 1024
D_SSM = 512
D_HGRN = 512
SSD_HEADS = 8
SSD_HEAD_DIM = 64
SSD_GROUPS = 2
SSD_STATE = 128
SSD_CONV = 4
SSD_CHUNK = 128
D_CONV = D_SSM + 2 * SSD_GROUPS * SSD_STATE
HGRN_HEADS = 4
HGRN_DK = 128
HGRN_CHUNK = 64
HGRN_SUB = 8
N_EXPERTS = 256
TOP_K = 8
N_EXPERT_GROUPS = 8
TOPK_GROUPS = 4
D_EXPERT = 256
ROUTED_SCALE = 2.5
ALPHA = 2.0 ** 0.25
LN_EPS = 1e-5
RMS_EPS = 1e-6

LANES = 128
SUBLANES = 8
ROW_TILES = D_MODEL // LANES
DT_PAD = LANES
N_IN_PAD = D_SSM + D_CONV + 4 * D_HGRN + DT_PAD

TM_PROJ = 256
TM_TOK = 256
EXPERT_ROWS = 256
NEG = -1e30


def _sigmoid(x):
    return 1.0 / (1.0 + jnp.exp(-x))


def _silu(x):
    return x * _sigmoid(x)


def _split3(x):
    hi = x.astype(BF16)
    r = x - hi.astype(F32)
    mid = r.astype(BF16)
    lo = (r - mid.astype(F32)).astype(BF16)
    return hi, mid, lo


def _dot(a, b):
    return jnp.dot(a, b, preferred_element_type=F32)


def _dot_nt(a, b):
    return lax.dot_general(a, b, (((1,), (1,)), ((), ())), preferred_element_type=F32)


def _dot_tn(a, b):
    return lax.dot_general(a, b, (((0,), (0,)), ((), ())), preferred_element_type=F32)


def _sel_dot(sel, x):
    hi, mid, lo = _split3(x)
    return _dot(sel, hi) + _dot(sel, mid) + _dot(sel, lo)


def _dot_sel(x, sel):
    hi, mid, lo = _split3(x)
    return _dot(hi, sel) + _dot(mid, sel) + _dot(lo, sel)


def _layer_norm(x, g, b):
    mu = jnp.mean(x, axis=-1, keepdims=True)
    xc = x - mu
    var = jnp.mean(xc * xc, axis=-1, keepdims=True)
    return xc * lax.rsqrt(var + LN_EPS) * g + b


def _inproj_kernel(x_ref, w_ref, z_ref, xbc_ref, q_ref, f_ref, i_ref, g_ref, dt_ref):
    xb = x_ref[...].astype(BF16)
    col = 0
    for ref in (z_ref, xbc_ref, q_ref, f_ref, i_ref, g_ref, dt_ref):
        n = ref.shape[-1]
        ref[...] = _dot(xb, w_ref[:, col:col + n])
        col += n


def _inproj(x2, w_perm):
    t = x2.shape[0]
    widths = (D_SSM, D_CONV, D_HGRN, D_HGRN, D_HGRN, D_HGRN, DT_PAD)
    return pl.pallas_call(
        _inproj_kernel,
        grid=(t // TM_PROJ,),
        in_specs=[pl.BlockSpec((TM_PROJ, D_MODEL), lambda i: (i, 0)),
                  pl.BlockSpec((D_MODEL, N_IN_PAD), lambda i: (0, 0))],
        out_specs=[pl.BlockSpec((TM_PROJ, n), lambda i: (i, 0)) for n in widths],
        out_shape=[jax.ShapeDtypeStruct((t, n), F32) for n in widths],
        compiler_params=pltpu.CompilerParams(dimension_semantics=("arbitrary",),
                                             vmem_limit_bytes=48 * 1024 * 1024),
        name="inproj",
    )(x2, w_perm)


def _ssd_kernel(z_ref, xbc_ref, dt_ref, cw_ref, cb_ref, dtb_ref, alog_ref, dskip_ref, nw_ref,
                y_ref, ext_ref, st_ref):
    L = SSD_CHUNK
    halo = SUBLANES

    @pl.when(pl.program_id(0) == 0)
    def _():
        ext_ref[0:halo, :] = jnp.zeros((halo, D_CONV), F32)
        st_ref[...] = jnp.zeros(st_ref.shape, F32)

    ext_ref[halo:halo + L, :] = xbc_ref[...]
    acc = jnp.broadcast_to(cb_ref[...], (L, D_CONV))
    for k in range(SSD_CONV):
        off = halo - (SSD_CONV - 1) + k
        acc = acc + cw_ref[k:k + 1, :] * ext_ref[off:off + L, :]
    ext_ref[0:halo, :] = xbc_ref[L - halo:L, :]
    u = _silu(acc)
    xs = u[:, :D_SSM]
    bm = u[:, D_SSM:D_SSM + SSD_GROUPS * SSD_STATE]
    cm = u[:, D_SSM + SSD_GROUPS * SSD_STATE:]

    draw = dt_ref[...] + dtb_ref[...]
    dt = jnp.maximum(draw, 0.0) + jnp.log(1.0 + jnp.exp(-jnp.abs(draw)))
    ad = dt * (-jnp.exp(alog_ref[...]))
    rows = lax.broadcasted_iota(I32, (L, L), 0)
    cols = lax.broadcasted_iota(I32, (L, L), 1)
    causal = rows >= cols
    a_cum = _sel_dot(causal.astype(BF16), ad)
    a_cum_t = a_cum.T

    hrow = lax.broadcasted_iota(I32, (LANES, D_SSM), 0)
    hcol = lax.broadcasted_iota(I32, (LANES, D_SSM), 1) // SSD_HEAD_DIM
    expand = (hrow == hcol).astype(BF16)
    dt_x = _dot_sel(dt, expand)
    acx = _dot_sel(a_cum, expand)
    last = acx[L - 1:L, :]
    ea_x = jnp.exp(acx)
    dec_x = jnp.exp(last - acx)
    elast_x = jnp.exp(last)

    xdt = xs * dt_x
    gw = SSD_HEADS // SSD_GROUPS * SSD_HEAD_DIM
    lane_head = lax.broadcasted_iota(I32, (L, gw), 1) // SSD_HEAD_DIM
    ys = []
    for g in range(SSD_GROUPS):
        bg = bm[:, g * SSD_STATE:(g + 1) * SSD_STATE]
        cg = cm[:, g * SSD_STATE:(g + 1) * SSD_STATE].astype(BF16)
        bg_t = bg.T.astype(BF16)
        gmat = _dot(cg, bg_t)
        xdt_g = xdt[:, g * gw:(g + 1) * gw]
        xdt_gb = xdt_g.astype(BF16)
        r_prev = st_ref[g]
        y_g = _dot(cg, r_prev.astype(BF16)) * ea_x[:, g * gw:(g + 1) * gw]
        new_s = _dot(bg_t, (xdt_g * dec_x[:, g * gw:(g + 1) * gw]).astype(BF16))
        st_ref[g] = r_prev * elast_x[:, g * gw:(g + 1) * gw] + new_s
        for j in range(SSD_HEADS // SSD_GROUPS):
            h = g * (SSD_HEADS // SSD_GROUPS) + j
            diff = a_cum[:, h:h + 1] - a_cum_t[h:h + 1, :]
            decay = jnp.exp(jnp.where(causal, diff, NEG))
            yd = _dot((gmat * decay).astype(BF16), xdt_gb)
            y_g = y_g + jnp.where(lane_head == j, yd, 0.0)
        ys.append(y_g)
    y = jnp.concatenate(ys, axis=-1) + xs * dskip_ref[...]
    y = y * _silu(z_ref[...])
    outs = []
    ng = D_SSM // SSD_GROUPS
    for g in range(SSD_GROUPS):
        yg = y[:, g * ng:(g + 1) * ng]
        ms = jnp.mean(yg * yg, axis=-1, keepdims=True)
        outs.append(yg * lax.rsqrt(ms + RMS_EPS))
    y_ref[...] = jnp.concatenate(outs, axis=-1) * nw_ref[...]


def _ssd(z, xbc, dt, conv_w, conv_b, dt_bias, a_log, d_skip, norm_w):
    t = z.shape[0]
    L = SSD_CHUNK
    pad = lambda v: jnp.pad(v.astype(F32), (0, LANES - v.shape[0])).reshape(1, LANES)
    full = lambda shape: pl.BlockSpec(shape, lambda c: (0,) * len(shape))
    return pl.pallas_call(
        _ssd_kernel,
        grid=(t // L,),
        in_specs=[pl.BlockSpec((L, D_SSM), lambda c: (c, 0)),
                  pl.BlockSpec((L, D_CONV), lambda c: (c, 0)),
                  pl.BlockSpec((L, DT_PAD), lambda c: (c, 0)),
                  full((SSD_CONV, D_CONV)), full((1, D_CONV)), full((1, LANES)), full((1, LANES)),
                  full((1, D_SSM)), full((1, D_SSM))],
        out_specs=pl.BlockSpec((L, D_SSM), lambda c: (c, 0)),
        out_shape=jax.ShapeDtypeStruct((t, D_SSM), F32),
        scratch_shapes=[pltpu.VMEM((L + SUBLANES, D_CONV), F32),
                        pltpu.VMEM((SSD_GROUPS, SSD_STATE, D_SSM // SSD_GROUPS), F32)],
        compiler_params=pltpu.CompilerParams(dimension_semantics=("arbitrary",)),
        name="ssd",
    )(z, xbc, dt, conv_w.astype(F32), conv_b.reshape(1, D_CONV).astype(F32), pad(dt_bias), pad(a_log),
      jnp.repeat(d_skip.astype(F32), SSD_HEAD_DIM).reshape(1, D_SSM), norm_w.reshape(1, D_SSM).astype(F32))


def _tile_bcast(x, r):
    n, d = x.shape
    x3 = x.reshape(n // SUBLANES, SUBLANES, d)
    return jnp.broadcast_to(x3[:, r:r + 1, :], x3.shape).reshape(n, d)


def _hgrn_kernel(q_ref, f_ref, i_ref, g_ref, lb_ref, nw_ref, o_ref, st_ref):
    C = HGRN_CHUNK
    S = HGRN_SUB
    nsub = C // S

    @pl.when(pl.program_id(0) == 0)
    def _():
        st_ref[...] = jnp.zeros(st_ref.shape, F32)

    lb = lb_ref[...]
    fg = lb + (1.0 - lb) * _sigmoid(f_ref[...])
    kk = 1.0 - fg
    qs = _silu(q_ref[...])
    v = i_ref[...]
    rows = lax.broadcasted_iota(I32, (C, C), 0)
    cols = lax.broadcasted_iota(I32, (C, C), 1)
    cum = _sel_dot((rows >= cols).astype(BF16), jnp.log(fg))
    last = cum[C - 1:C, :]

    trow = lax.broadcasted_iota(I32, (C, D_HGRN), 0)
    tmod = trow % S
    tblk = trow // S

    xr = []
    for r in range(S):
        e = jnp.exp(jnp.where(tmod >= r, cum - _tile_bcast(cum, r), NEG))
        xr.append(qs * _tile_bcast(kk, r) * e)
    kbe = kk * jnp.exp(_tile_bcast(cum, S - 1) - cum)
    aj, bj = [], []
    for j in range(nsub - 1):
        ce = cum[j * S + S - 1:j * S + S, :]
        aj.append(qs * jnp.exp(jnp.where(tblk > j, cum - ce, NEG)))
        bj.append(jnp.where(tblk == j, kbe, 0.0))
    qd = (qs * jnp.exp(cum)).astype(BF16)
    kd = (kk * jnp.exp(last - cum)).astype(BF16)
    elast = jnp.exp(last)

    att_col = lax.broadcasted_iota(I32, (C, C), 1)
    att_rblk = lax.broadcasted_iota(I32, (C, C), 0) // S * S
    outs = []
    for h in range(HGRN_HEADS):
        blk = slice(h * HGRN_DK, (h + 1) * HGRN_DK)
        a_cat = jnp.concatenate([a[:, blk] for a in aj], axis=-1).astype(BF16)
        b_cat = jnp.concatenate([b[:, blk] for b in bj], axis=-1).astype(BF16)
        att = _dot_nt(a_cat, b_cat)
        for r in range(S):
            rs = jnp.sum(xr[r][:, blk], axis=-1, keepdims=True)
            att = jnp.where(att_col == att_rblk + r, rs, att)
        vb = v[:, blk].astype(BF16)
        s_prev = st_ref[h]
        o = _dot_nt(qd[:, blk], s_prev.astype(BF16)) + _dot(att.astype(BF16), vb)
        st_ref[h] = s_prev * elast[:, blk] + _dot_tn(vb, kd[:, blk])
        ms = jnp.mean(o * o, axis=-1, keepdims=True)
        outs.append(o * lax.rsqrt(ms + RMS_EPS))
    o_ref[...] = jnp.concatenate(outs, axis=-1) * nw_ref[...] * _silu(g_ref[...])


def _hgrn(q, f, i, g, lower_bound, norm_w):
    t = q.shape[0]
    C = HGRN_CHUNK
    tok = pl.BlockSpec((C, D_HGRN), lambda c: (c, 0))
    vec = pl.BlockSpec((1, D_HGRN), lambda c: (0, 0))
    return pl.pallas_call(
        _hgrn_kernel,
        grid=(t // C,),
        in_specs=[tok, tok, tok, tok, vec, vec],
        out_specs=tok,
        out_shape=jax.ShapeDtypeStruct((t, D_HGRN), F32),
        scratch_shapes=[pltpu.VMEM((HGRN_HEADS, D_HGRN // HGRN_HEADS, HGRN_DK), F32)],
        compiler_params=pltpu.CompilerParams(dimension_semantics=("arbitrary",)),
        name="hgrn",
    )(q, f, i, g, lower_bound.reshape(1, D_HGRN).astype(F32), norm_w.reshape(1, D_HGRN).astype(F32))


def _post_kernel(x_ref, ys_ref, yh_ref, wo_ref, g1_ref, b1_ref, wrh_ref, wrl_ref, rb_ref,
                 wgs_ref, wus_ref, wds_ref, htt_ref, base_ref, idx_ref, gate_ref):
    tm = x_ref.shape[0]
    mix = (_dot(ys_ref[...].astype(BF16), wo_ref[0:D_SSM, :])
           + _dot(yh_ref[...].astype(BF16), wo_ref[D_SSM:, :]))
    h1 = _layer_norm(ALPHA * x_ref[...] + mix, g1_ref[...], b1_ref[...])
    for k in range(ROW_TILES):
        htt_ref[pl.ds(k, tm, stride=ROW_TILES), :] = h1[:, k * LANES:(k + 1) * LANES]
    hb = h1.astype(BF16)
    hid = _silu(_dot(hb, wgs_ref[...])) * _dot(hb, wus_ref[...])
    base_ref[...] = ALPHA * h1 + _dot(hid.astype(BF16), wds_ref[...])

    hlo = (h1 - hb.astype(F32)).astype(BF16)
    logits = _dot_nt(wrh_ref[...], hb) + _dot_nt(wrh_ref[...], hlo) + _dot_nt(wrl_ref[...], hb)
    scores = _sigmoid(logits)
    biased = scores + rb_ref[...]
    per_group = N_EXPERTS // N_EXPERT_GROUPS
    eidx = lax.broadcasted_iota(I32, (N_EXPERTS, tm), 0)
    big = jnp.int32(1 << 20)
    gsc = []
    bidx = lax.broadcasted_iota(I32, (per_group, tm), 0)
    for gi in range(N_EXPERT_GROUPS):
        blk = biased[gi * per_group:(gi + 1) * per_group, :]
        m1 = jnp.max(blk, axis=0, keepdims=True)
        i1 = jnp.min(jnp.where(blk == m1, bidx, big), axis=0, keepdims=True)
        m2 = jnp.max(jnp.where(bidx == i1, NEG, blk), axis=0, keepdims=True)
        gsc.append(m1 + m2)
    cur = jnp.concatenate(gsc, axis=0)
    gidx = lax.broadcasted_iota(I32, (N_EXPERT_GROUPS, tm), 0)
    gsel = jnp.zeros((N_EXPERT_GROUPS, tm), F32)
    for _ in range(TOPK_GROUPS):
        m = jnp.max(cur, axis=0, keepdims=True)
        i = jnp.min(jnp.where(cur == m, gidx, big), axis=0, keepdims=True)
        hit = gidx == i
        gsel = jnp.where(hit, 1.0, gsel)
        cur = jnp.where(hit, NEG, cur)
    emask = jnp.concatenate(
        [jnp.broadcast_to(gsel[gi:gi + 1, :], (per_group, tm)) for gi in range(N_EXPERT_GROUPS)], axis=0)
    masked = jnp.where(emask > 0.0, biased, NEG)
    idx_rows, gate_rows = [], []
    for _ in range(TOP_K):
        m = jnp.max(masked, axis=0, keepdims=True)
        i = jnp.min(jnp.where(masked == m, eidx, big), axis=0, keepdims=True)
        hit = eidx == i
        idx_rows.append(i)
        gate_rows.append(jnp.sum(jnp.where(hit, scores, 0.0), axis=0, keepdims=True))
        masked = jnp.where(hit, NEG, masked)
    gates = jnp.concatenate(gate_rows, axis=0)
    gates = gates / jnp.sum(gates, axis=0, keepdims=True) * ROUTED_SCALE
    idx_ref[0] = jnp.concatenate(idx_rows, axis=0)
    gate_ref[0] = gates


def _post(x2, y_ssd, y_hgrn, wo, g1, b1, wr_hi, wr_lo, rbias, wgs, wus, wds):
    t = x2.shape[0]
    tm = TM_TOK
    nt = t // tm
    full = lambda shape: pl.BlockSpec(shape, lambda i: (0,) * len(shape))
    return pl.pallas_call(
        _post_kernel,
        grid=(nt,),
        in_specs=[pl.BlockSpec((tm, D_MODEL), lambda i: (i, 0)),
                  pl.BlockSpec((tm, D_SSM), lambda i: (i, 0)),
                  pl.BlockSpec((tm, D_HGRN), lambda i: (i, 0)),
                  full((D_MODEL, D_MODEL)), full((1, D_MODEL)), full((1, D_MODEL)),
                  full((N_EXPERTS, D_MODEL)), full((N_EXPERTS, D_MODEL)), full((N_EXPERTS, 1)),
                  full((D_MODEL, D_EXPERT)), full((D_MODEL, D_EXPERT)), full((D_EXPERT, D_MODEL))],
        out_specs=[pl.BlockSpec((tm * ROW_TILES, LANES), lambda i: (i, 0)),
                   pl.BlockSpec((tm, D_MODEL), lambda i: (i, 0)),
                   pl.BlockSpec((1, TOP_K, tm), lambda i: (i, 0, 0)),
                   pl.BlockSpec((1, TOP_K, tm), lambda i: (i, 0, 0))],
        out_shape=[jax.ShapeDtypeStruct((t * ROW_TILES, LANES), F32),
                   jax.ShapeDtypeStruct((t, D_MODEL), F32),
                   jax.ShapeDtypeStruct((nt, TOP_K, tm), I32),
                   jax.ShapeDtypeStruct((nt, TOP_K, tm), F32)],
        compiler_params=pltpu.CompilerParams(dimension_semantics=("arbitrary",),
                                             vmem_limit_bytes=48 * 1024 * 1024),
        name="post",
    )(x2, y_ssd, y_hgrn, wo, g1, b1, wr_hi, wr_lo, rbias, wgs, wus, wds)


def _max_blocks(t):
    return (t * TOP_K + N_EXPERTS * (EXPERT_ROWS - 1)) // EXPERT_ROWS


def _pos_kernel(idx_ref, dest_ref, be_ref, nb_ref, cnt_ref, start_ref):
    phase = pl.program_id(0)
    i = pl.program_id(1)
    tm = idx_ref.shape[-1]
    nblk_pad = be_ref.shape[-1]
    idx = idx_ref[0]
    eidx = lax.broadcasted_iota(I32, (N_EXPERTS, tm), 0)
    sel = [eidx == idx[k:k + 1, :] for k in range(TOP_K)]
    onehot = sel[0]
    for k in range(1, TOP_K):
        onehot = jnp.logical_or(onehot, sel[k])
    mt = jnp.where(onehot, 1.0, 0.0).astype(BF16)
    tile_cnt = _dot(mt, jnp.ones((tm, LANES), BF16))

    @pl.when(jnp.logical_and(phase == 0, i == 0))
    def _():
        cnt_ref[...] = jnp.zeros(cnt_ref.shape, F32)

    @pl.when(phase == 0)
    def _():
        cnt_ref[...] += tile_cnt

    @pl.when(jnp.logical_and(phase == 1, i == 0))
    def _():
        nb = jnp.floor((cnt_ref[...] + (EXPERT_ROWS - 1)) * (1.0 / EXPERT_ROWS))
        r = lax.broadcasted_iota(I32, (N_EXPERTS, N_EXPERTS), 0)
        c = lax.broadcasted_iota(I32, (N_EXPERTS, N_EXPERTS), 1)
        end = _dot((r >= c).astype(BF16), nb.astype(BF16))
        start_ref[...] = (end - nb) * EXPERT_ROWS
        cnt_ref[...] = jnp.zeros(cnt_ref.shape, F32)
        bi = lax.broadcasted_iota(I32, (N_EXPERTS, nblk_pad), 1).astype(F32)
        owner = jnp.sum(jnp.where(end[:, 0:1] <= bi, 1.0, 0.0), axis=0, keepdims=True)
        be_ref[...] = jnp.minimum(owner, N_EXPERTS - 1.0).astype(I32)
        nb_ref[...] = end[N_EXPERTS - 1:N_EXPERTS, :].astype(I32)

    @pl.when(phase == 1)
    def _():
        r = lax.broadcasted_iota(I32, (tm, tm), 0)
        c = lax.broadcasted_iota(I32, (tm, tm), 1)
        before = _dot(mt, (r < c).astype(BF16))
        slot = start_ref[:, 0:1] + cnt_ref[:, 0:1] + before
        rows = [jnp.sum(jnp.where(sel[k], slot, 0.0), axis=0, keepdims=True) for k in range(TOP_K)]
        dest_ref[0] = jnp.concatenate(rows, axis=0).astype(I32)
        cnt_ref[...] += tile_cnt


def _positions(idx, t):
    nt, _, tm = idx.shape
    nblk_pad = -(-_max_blocks(t) // LANES) * LANES
    return pl.pallas_call(
        _pos_kernel,
        grid=(2, nt),
        in_specs=[pl.BlockSpec((1, TOP_K, tm), lambda p, i: (i, 0, 0))],
        out_specs=[pl.BlockSpec((1, TOP_K, tm), lambda p, i: (i * p, 0, 0)),
                   pl.BlockSpec((1, nblk_pad), lambda p, i: (0, 0)),
                   pl.BlockSpec((1, LANES), lambda p, i: (0, 0))],
        out_shape=[jax.ShapeDtypeStruct((nt, TOP_K, tm), I32),
                   jax.ShapeDtypeStruct((1, nblk_pad), I32),
                   jax.ShapeDtypeStruct((1, LANES), I32)],
        scratch_shapes=[pltpu.VMEM((N_EXPERTS, LANES), F32), pltpu.VMEM((N_EXPERTS, LANES), F32)],
        compiler_params=pltpu.CompilerParams(dimension_semantics=("arbitrary", "arbitrary")),
        name="positions",
    )(idx)


def _row(ref, r):
    return ref.at[pl.ds(pl.multiple_of(r * ROW_TILES, ROW_TILES), ROW_TILES)]


def _rows_wait(src, dst, sem, n):
    pltpu.make_async_copy(src.at[pl.ds(0, n * ROW_TILES)], dst.at[pl.ds(0, n * ROW_TILES)], sem).wait()


def _dispatch_kernel(dest_ref, h_ref, xs_ref, sem):
    tm = h_ref.shape[0] // ROW_TILES

    def body(t, carry):
        for k in range(TOP_K):
            d = dest_ref[0, k, t]
            pltpu.make_async_copy(_row(h_ref, t), _row(xs_ref, d), sem).start(priority=k % 2)
        return carry

    lax.fori_loop(0, tm, body, 0)
    for _ in range(TOP_K):
        _rows_wait(h_ref, xs_ref, sem, tm)


def _dispatch(dest, htt, n_rows):
    t = htt.shape[0] // ROW_TILES
    tm = dest.shape[-1]
    return pl.pallas_call(
        _dispatch_kernel,
        grid=(t // tm,),
        in_specs=[pl.BlockSpec((1, TOP_K, tm), lambda i: (i, 0, 0), memory_space=pltpu.SMEM),
                  pl.BlockSpec((tm * ROW_TILES, LANES), lambda i: (i, 0))],
        out_specs=pl.BlockSpec(memory_space=pl.ANY),
        out_shape=jax.ShapeDtypeStruct((n_rows * ROW_TILES, LANES), F32),
        scratch_shapes=[pltpu.SemaphoreType.DMA(())],
        compiler_params=pltpu.CompilerParams(dimension_semantics=("arbitrary",), has_side_effects=True),
        name="dispatch",
    )(dest, htt)


def _expert_kernel(be_ref, nb_ref, xs_ref, wg_ref, wu_ref, wd_ref, ys_ref, wgb_ref, wub_ref, wdb_ref):
    i = pl.program_id(0)
    prev = be_ref[jnp.maximum(i - 1, 0)]
    fresh = jnp.logical_or(i == 0, be_ref[i] != prev)

    @pl.when(jnp.logical_and(i < nb_ref[0], fresh))
    def _():
        wgb_ref[...] = wg_ref[0].astype(BF16)
        wub_ref[...] = wu_ref[0].astype(BF16)
        wdb_ref[...] = wd_ref[0].astype(BF16)

    @pl.when(i < nb_ref[0])
    def _():
        xb = jnp.concatenate([xs_ref[pl.ds(k, EXPERT_ROWS, stride=ROW_TILES), :] for k in range(ROW_TILES)],
                             axis=-1).astype(BF16)
        hid = _silu(_dot(xb, wgb_ref[...])) * _dot(xb, wub_ref[...])
        yb = _dot(hid.astype(BF16), wdb_ref[...])
        for k in range(ROW_TILES):
            ys_ref[pl.ds(k, EXPERT_ROWS, stride=ROW_TILES), :] = yb[:, k * LANES:(k + 1) * LANES]


def _experts(block_e, n_blocks, xs, w_gate, w_up, w_down):
    n_rows = xs.shape[0] // ROW_TILES
    nblk = n_rows // EXPERT_ROWS
    blk = lambda i, be, nb: (jnp.minimum(i, nb[0] - 1), 0)
    wmap = lambda i, be, nb: (be[jnp.minimum(i, nb[0] - 1)], 0, 0)
    grid_spec = pltpu.PrefetchScalarGridSpec(
        num_scalar_prefetch=2,
        grid=(nblk,),
        in_specs=[pl.BlockSpec((EXPERT_ROWS * ROW_TILES, LANES), blk),
                  pl.BlockSpec((1, D_MODEL, D_EXPERT), wmap),
                  pl.BlockSpec((1, D_MODEL, D_EXPERT), wmap),
                  pl.BlockSpec((1, D_EXPERT, D_MODEL), wmap)],
        out_specs=pl.BlockSpec((EXPERT_ROWS * ROW_TILES, LANES), blk),
        scratch_shapes=[pltpu.VMEM((D_MODEL, D_EXPERT), BF16), pltpu.VMEM((D_MODEL, D_EXPERT), BF16),
                        pltpu.VMEM((D_EXPERT, D_MODEL), BF16)],
    )
    return pl.pallas_call(
        _expert_kernel,
        grid_spec=grid_spec,
        out_shape=jax.ShapeDtypeStruct((n_rows * ROW_TILES, LANES), F32),
        compiler_params=pltpu.CompilerParams(dimension_semantics=("arbitrary",),
                                             vmem_limit_bytes=48 * 1024 * 1024),
        name="experts",
    )(block_e, n_blocks, xs, w_gate, w_up, w_down)


def _combine_kernel(dest_ref, gate_ref, base_ref, g2_ref, b2_ref, ys_ref, o_ref, rows_ref, sem):
    tm = base_ref.shape[0]

    def body(t, carry):
        for k in range(TOP_K):
            d = dest_ref[0, k, t]
            pltpu.make_async_copy(_row(ys_ref, d), _row(rows_ref, k * tm + t), sem).start(priority=k % 2)
        return carry

    lax.fori_loop(0, tm, body, 0)
    gates_t = jnp.concatenate([gate_ref[0], jnp.zeros((LANES - TOP_K, tm), F32)], axis=0).T
    for k in range(TOP_K):
        _rows_wait(ys_ref, rows_ref, sem, tm)
    cols = []
    for c in range(ROW_TILES):
        acc = base_ref[:, c * LANES:(c + 1) * LANES]
        for k in range(TOP_K):
            acc = acc + gates_t[:, k:k + 1] * rows_ref[pl.ds(k * tm * ROW_TILES + c, tm, stride=ROW_TILES), :]
        cols.append(acc)
    o_ref[...] = _layer_norm(jnp.concatenate(cols, axis=-1), g2_ref[...], b2_ref[...])


def _combine(dest, gates, base, g2, b2, ys):
    t = base.shape[0]
    tm = dest.shape[-1]
    return pl.pallas_call(
        _combine_kernel,
        grid=(t // tm,),
        in_specs=[pl.BlockSpec((1, TOP_K, tm), lambda i: (i, 0, 0), memory_space=pltpu.SMEM),
                  pl.BlockSpec((1, TOP_K, tm), lambda i: (i, 0, 0)),
                  pl.BlockSpec((tm, D_MODEL), lambda i: (i, 0)),
                  pl.BlockSpec((1, D_MODEL), lambda i: (0, 0)),
                  pl.BlockSpec((1, D_MODEL), lambda i: (0, 0)),
                  pl.BlockSpec(memory_space=pl.ANY)],
        out_specs=pl.BlockSpec((tm, D_MODEL), lambda i: (i, 0)),
        out_shape=jax.ShapeDtypeStruct((t, D_MODEL), F32),
        scratch_shapes=[pltpu.VMEM((TOP_K * tm * ROW_TILES, LANES), F32), pltpu.SemaphoreType.DMA(())],
        compiler_params=pltpu.CompilerParams(dimension_semantics=("arbitrary",),
                                             vmem_limit_bytes=48 * 1024 * 1024),
        name="combine",
    )(dest, gates, base, g2, b2, ys)


def _layer(h2, w_in, conv_w, conv_b, dt_bias, a_log, d_skip, ssd_norm_w, lower_bound, hgrn_norm_w, w_out,
           ln1_g, ln1_b, w_router, router_bias, w_gate_e, w_up_e, w_down_e, w_gate_s, w_up_s, w_down_s,
           ln2_g, ln2_b):
    t = h2.shape[0]
    dt0 = D_SSM + D_CONV
    q0 = dt0 + SSD_HEADS
    w_perm = jnp.concatenate(
        [w_in[:, :dt0], w_in[:, q0:], w_in[:, dt0:q0], jnp.zeros((D_MODEL, DT_PAD - SSD_HEADS), w_in.dtype)],
        axis=1).astype(BF16)
    z, xbc, q, f, i, g, dt = _inproj(h2, w_perm)
    y_ssd = _ssd(z, xbc, dt, conv_w, conv_b, dt_bias, a_log, d_skip, ssd_norm_w)
    y_hgrn = _hgrn(q, f, i, g, lower_bound, hgrn_norm_w)

    wr_t = w_router.astype(F32).T
    wr_hi = wr_t.astype(BF16)
    wr_lo = (wr_t - wr_hi.astype(F32)).astype(BF16)
    row = lambda v: v.reshape(1, -1).astype(F32)
    htt, base, idx, gates = _post(
        h2, y_ssd, y_hgrn, w_out.astype(BF16), row(ln1_g), row(ln1_b), wr_hi, wr_lo,
        router_bias.reshape(N_EXPERTS, 1).astype(F32),
        w_gate_s.astype(BF16), w_up_s.astype(BF16), w_down_s.astype(BF16))

    dest, block_e, n_blocks = _positions(idx, t)
    n_rows = _max_blocks(t) * EXPERT_ROWS
    xs = _dispatch(dest, htt, n_rows)
    ys = _experts(block_e.reshape(-1), n_blocks.reshape(-1), xs, w_gate_e, w_up_e, w_down_e)
    return _combine(dest, gates, base, row(ln2_g), row(ln2_b), ys)


def kernel(x, w_in, conv_w, conv_b, dt_bias, a_log, d_skip, ssd_norm_w, hgrn_lb_logits, hgrn_norm_w, w_out,
           ln1_g, ln1_b, w_router, router_bias, w_gate_e, w_up_e, w_down_e, w_gate_s, w_up_s, w_down_s,
           ln2_g, ln2_b):
    bsz, t, d = x.shape
    assert bsz == 1 and d == D_MODEL, "the recurrent mixers carry state across the flattened token axis"
    depth = w_in.shape[0]
    lower_bounds = jnp.cumsum(jax.nn.softmax(hgrn_lb_logits.astype(F32), axis=0), axis=0)
    h = x.reshape(bsz * t, d)
    for l in range(depth):
        h = _layer(h, w_in[l], conv_w[l], conv_b[l], dt_bias[l], a_log[l], d_skip[l], ssd_norm_w[l],
                   lower_bounds[l], hgrn_norm_w[l], w_out[l], ln1_g[l], ln1_b[l], w_router[l],
                   router_bias[l], w_gate_e[l], w_up_e[l], w_down_e[l], w_gate_s[l], w_up_s[l],
                   w_down_s[l], ln2_g[l], ln2_b[l])
    return h.reshape(bsz, t, d)
```

```python
import jax
import jax.numpy as jnp
from jax import lax
from jax.experimental import pallas as pl
from jax.experimental.pallas import tpu as pltpu

F32 = jnp.float32
BF16 = jnp.bfloat16
I32 = jnp.int32
U32 = jnp.uint32

D_MODEL = 1024
D_SSM = 512
D_HGRN = 512
SSD_HEADS = 8
SSD_HEAD_DIM = 64
SSD_GROUPS = 2
SSD_STATE = 128
SSD_CONV = 4
SSD_CHUNK = 128
D_CONV = D_SSM + 2 * SSD_GROUPS * SSD_STATE
HGRN_HEADS = 4
HGRN_DK = 128
HGRN_CHUNK = 64
HGRN_SUB = 8
N_EXPERTS = 256
TOP_K = 8
N_EXPERT_GROUPS = 8
TOPK_GROUPS = 4
D_EXPERT = 256
ROUTED_SCALE = 2.5
ALPHA = 2.0 ** 0.25
LN_EPS = 1e-5
RMS_EPS = 1e-6

LANES = 128
SUBLANES = 8
ROW_TILES = D_MODEL // LANES
PACK_TILES = ROW_TILES // 2
DT_PAD = LANES
N_IN_PAD = D_SSM + D_CONV + 4 * D_HGRN + DT_PAD

TM_PROJ = 256
TM_TOK = 256
EXPERT_ROWS = 256
NEG = -1e30


def _sigmoid(x):
    return 1.0 / (1.0 + jnp.exp(-x))


def _silu(x):
    return x * _sigmoid(x)


def _split3(x):
    hi = x.astype(BF16)
    r = x - hi.astype(F32)
    mid = r.astype(BF16)
    lo = (r - mid.astype(F32)).astype(BF16)
    return hi, mid, lo


def _dot(a, b):
    return jnp.dot(a, b, preferred_element_type=F32)


def _dot_nt(a, b):
    return lax.dot_general(a, b, (((1,), (1,)), ((), ())), preferred_element_type=F32)


def _dot_tn(a, b):
    return lax.dot_general(a, b, (((0,), (0,)), ((), ())), preferred_element_type=F32)


def _sel_dot(sel, x):
    hi, mid, lo = _split3(x)
    return _dot(sel, hi) + _dot(sel, mid) + _dot(sel, lo)


def _dot_sel(x, sel):
    hi, mid, lo = _split3(x)
    return _dot(hi, sel) + _dot(mid, sel) + _dot(lo, sel)


def _pack_pairs(lo, hi):
    lo_bits = pltpu.bitcast(lo.astype(BF16).astype(F32), U32) >> 16
    hi_bits = pltpu.bitcast(hi.astype(BF16).astype(F32), U32) & jnp.uint32(0xFFFF0000)
    return hi_bits | lo_bits


def _unpack_pairs(p):
    return pltpu.bitcast(p << 16, F32), pltpu.bitcast(p & jnp.uint32(0xFFFF0000), F32)


def _layer_norm(x, g, b):
    mu = jnp.mean(x, axis=-1, keepdims=True)
    xc = x - mu
    var = jnp.mean(xc * xc, axis=-1, keepdims=True)
    return xc * lax.rsqrt(var + LN_EPS) * g + b


def _inproj_kernel(x_ref, w_ref, z_ref, xbc_ref, q_ref, f_ref, i_ref, g_ref, dt_ref):
    xb = x_ref[...].astype(BF16)
    col = 0
    for ref in (z_ref, xbc_ref, q_ref, f_ref, i_ref, g_ref, dt_ref):
        n = ref.shape[-1]
        ref[...] = _dot(xb, w_ref[:, col:col + n])
        col += n


def _inproj(x2, w_perm):
    t = x2.shape[0]
    widths = (D_SSM, D_CONV, D_HGRN, D_HGRN, D_HGRN, D_HGRN, DT_PAD)
    return pl.pallas_call(
        _inproj_kernel,
        grid=(t // TM_PROJ,),
        in_specs=[pl.BlockSpec((TM_PROJ, D_MODEL), lambda i: (i, 0)),
                  pl.BlockSpec((D_MODEL, N_IN_PAD), lambda i: (0, 0))],
        out_specs=[pl.BlockSpec((TM_PROJ, n), lambda i: (i, 0)) for n in widths],
        out_shape=[jax.ShapeDtypeStruct((t, n), F32) for n in widths],
        compiler_params=pltpu.CompilerParams(dimension_semantics=("arbitrary",),
                                             vmem_limit_bytes=48 * 1024 * 1024),
        name="inproj",
    )(x2, w_perm)


def _ssd_kernel(z_ref, xbc_ref, dt_ref, cw_ref, cb_ref, dtb_ref, alog_ref, dskip_ref, nw_ref,
                y_ref, ext_ref, st_ref):
    L = SSD_CHUNK
    halo = SUBLANES

    @pl.when(pl.program_id(0) == 0)
    def _():
        ext_ref[0:halo, :] = jnp.zeros((halo, D_CONV), F32)
        st_ref[...] = jnp.zeros(st_ref.shape, F32)

    ext_ref[halo:halo + L, :] = xbc_ref[...]
    acc = jnp.broadcast_to(cb_ref[...], (L, D_CONV))
    for k in range(SSD_CONV):
        off = halo - (SSD_CONV - 1) + k
        acc = acc + cw_ref[k:k + 1, :] * ext_ref[off:off + L, :]
    ext_ref[0:halo, :] = xbc_ref[L - halo:L, :]
    u = _silu(acc)
    xs = u[:, :D_SSM]
    bm = u[:, D_SSM:D_SSM + SSD_GROUPS * SSD_STATE]
    cm = u[:, D_SSM + SSD_GROUPS * SSD_STATE:]

    draw = dt_ref[...] + dtb_ref[...]
    dt = jnp.maximum(draw, 0.0) + jnp.log(1.0 + jnp.exp(-jnp.abs(draw)))
    ad = dt * (-jnp.exp(alog_ref[...]))
    rows = lax.broadcasted_iota(I32, (L, L), 0)
    cols = lax.broadcasted_iota(I32, (L, L), 1)
    causal = rows >= cols
    a_cum = _sel_dot(causal.astype(BF16), ad)
    a_cum_t = a_cum.T

    hrow = lax.broadcasted_iota(I32, (LANES, D_SSM), 0)
    hcol = lax.broadcasted_iota(I32, (LANES, D_SSM), 1) // SSD_HEAD_DIM
    expand = (hrow == hcol).astype(BF16)
    dt_x = _dot_sel(dt, expand)
    acx = _dot_sel(a_cum, expand)
    last = acx[L - 1:L, :]
    ea_x = jnp.exp(acx)
    dec_x = jnp.exp(last - acx)
    elast_x = jnp.exp(last)

    xdt = xs * dt_x
    gw = SSD_HEADS // SSD_GROUPS * SSD_HEAD_DIM
    lane_head = lax.broadcasted_iota(I32, (L, gw), 1) // SSD_HEAD_DIM
    ys = []
    for g in range(SSD_GROUPS):
        bg = bm[:, g * SSD_STATE:(g + 1) * SSD_STATE]
        cg = cm[:, g * SSD_STATE:(g + 1) * SSD_STATE].astype(BF16)
        bg_t = bg.T.astype(BF16)
        gmat = _dot(cg, bg_t)
        xdt_g = xdt[:, g * gw:(g + 1) * gw]
        xdt_gb = xdt_g.astype(BF16)
        r_prev = st_ref[g]
        y_g = _dot(cg, r_prev.astype(BF16)) * ea_x[:, g * gw:(g + 1) * gw]
        new_s = _dot(bg_t, (xdt_g * dec_x[:, g * gw:(g + 1) * gw]).astype(BF16))
        st_ref[g] = r_prev * elast_x[:, g * gw:(g + 1) * gw] + new_s
        for j in range(SSD_HEADS // SSD_GROUPS):
            h = g * (SSD_HEADS // SSD_GROUPS) + j
            diff = a_cum[:, h:h + 1] - a_cum_t[h:h + 1, :]
            decay = jnp.exp(jnp.where(causal, diff, NEG))
            yd = _dot((gmat * decay).astype(BF16), xdt_gb)
            y_g = y_g + jnp.where(lane_head == j, yd, 0.0)
        ys.append(y_g)
    y = jnp.concatenate(ys, axis=-1) + xs * dskip_ref[...]
    y = y * _silu(z_ref[...])
    outs = []
    ng = D_SSM // SSD_GROUPS
    for g in range(SSD_GROUPS):
        yg = y[:, g * ng:(g + 1) * ng]
        ms = jnp.mean(yg * yg, axis=-1, keepdims=True)
        outs.append(yg * lax.rsqrt(ms + RMS_EPS))
    y_ref[...] = jnp.concatenate(outs, axis=-1) * nw_ref[...]


def _ssd(z, xbc, dt, conv_w, conv_b, dt_bias, a_log, d_skip, norm_w):
    t = z.shape[0]
    L = SSD_CHUNK
    pad = lambda v: jnp.pad(v.astype(F32), (0, LANES - v.shape[0])).reshape(1, LANES)
    full = lambda shape: pl.BlockSpec(shape, lambda c: (0,) * len(shape))
    return pl.pallas_call(
        _ssd_kernel,
        grid=(t // L,),
        in_specs=[pl.BlockSpec((L, D_SSM), lambda c: (c, 0)),
                  pl.BlockSpec((L, D_CONV), lambda c: (c, 0)),
                  pl.BlockSpec((L, DT_PAD), lambda c: (c, 0)),
                  full((SSD_CONV, D_CONV)), full((1, D_CONV)), full((1, LANES)), full((1, LANES)),
                  full((1, D_SSM)), full((1, D_SSM))],
        out_specs=pl.BlockSpec((L, D_SSM), lambda c: (c, 0)),
        out_shape=jax.ShapeDtypeStruct((t, D_SSM), F32),
        scratch_shapes=[pltpu.VMEM((L + SUBLANES, D_CONV), F32),
                        pltpu.VMEM((SSD_GROUPS, SSD_STATE, D_SSM // SSD_GROUPS), F32)],
        compiler_params=pltpu.CompilerParams(dimension_semantics=("arbitrary",)),
        name="ssd",
    )(z, xbc, dt, conv_w.astype(F32), conv_b.reshape(1, D_CONV).astype(F32), pad(dt_bias), pad(a_log),
      jnp.repeat(d_skip.astype(F32), SSD_HEAD_DIM).reshape(1, D_SSM), norm_w.reshape(1, D_SSM).astype(F32))


def _tile_bcast(x, r):
    n, d = x.shape
    x3 = x.reshape(n // SUBLANES, SUBLANES, d)
    return jnp.broadcast_to(x3[:, r:r + 1, :], x3.shape).reshape(n, d)


def _hgrn_kernel(q_ref, f_ref, i_ref, g_ref, lb_ref, nw_ref, o_ref, st_ref):
    C = HGRN_CHUNK
    S = HGRN_SUB
    nsub = C // S

    @pl.when(pl.program_id(0) == 0)
    def _():
        st_ref[...] = jnp.zeros(st_ref.shape, F32)

    lb = lb_ref[...]
    fg = lb + (1.0 - lb) * _sigmoid(f_ref[...])
    kk = 1.0 - fg
    qs = _silu(q_ref[...])
    v = i_ref[...]
    rows = lax.broadcasted_iota(I32, (C, C), 0)
    cols = lax.broadcasted_iota(I32, (C, C), 1)
    cum = _sel_dot((rows >= cols).astype(BF16), jnp.log(fg))
    last = cum[C - 1:C, :]

    trow = lax.broadcasted_iota(I32, (C, D_HGRN), 0)
    tmod = trow % S
    tblk = trow // S

    xr = []
    for r in range(S):
        e = jnp.exp(jnp.where(tmod >= r, cum - _tile_bcast(cum, r), NEG))
        xr.append(qs * _tile_bcast(kk, r) * e)
    kbe = kk * jnp.exp(_tile_bcast(cum, S - 1) - cum)
    aj, bj = [], []
    for j in range(nsub - 1):
        ce = cum[j * S + S - 1:j * S + S, :]
        aj.append(qs * jnp.exp(jnp.where(tblk > j, cum - ce, NEG)))
        bj.append(jnp.where(tblk == j, kbe, 0.0))
    qd = (qs * jnp.exp(cum)).astype(BF16)
    kd = (kk * jnp.exp(last - cum)).astype(BF16)
    elast = jnp.exp(last)

    att_col = lax.broadcasted_iota(I32, (C, C), 1)
    att_rblk = lax.broadcasted_iota(I32, (C, C), 0) // S * S
    outs = []
    for h in range(HGRN_HEADS):
        blk = slice(h * HGRN_DK, (h + 1) * HGRN_DK)
        a_cat = jnp.concatenate([a[:, blk] for a in aj], axis=-1).astype(BF16)
        b_cat = jnp.concatenate([b[:, blk] for b in bj], axis=-1).astype(BF16)
        att = _dot_nt(a_cat, b_cat)
        for r in range(S):
            rs = jnp.sum(xr[r][:, blk], axis=-1, keepdims=True)
            att = jnp.where(att_col == att_rblk + r, rs, att)
        vb = v[:, blk].astype(BF16)
        s_prev = st_ref[h]
        o = _dot_nt(qd[:, blk], s_prev.astype(BF16)) + _dot(att.astype(BF16), vb)
        st_ref[h] = s_prev * elast[:, blk] + _dot_tn(vb, kd[:, blk])
        ms = jnp.mean(o * o, axis=-1, keepdims=True)
        outs.append(o * lax.rsqrt(ms + RMS_EPS))
    o_ref[...] = jnp.concatenate(outs, axis=-1) * nw_ref[...] * _silu(g_ref[...])


def _hgrn(q, f, i, g, lower_bound, norm_w):
    t = q.shape[0]
    C = HGRN_CHUNK
    tok = pl.BlockSpec((C, D_HGRN), lambda c: (c, 0))
    vec = pl.BlockSpec((1, D_HGRN), lambda c: (0, 0))
    return pl.pallas_call(
        _hgrn_kernel,
        grid=(t // C,),
        in_specs=[tok, tok, tok, tok, vec, vec],
        out_specs=tok,
        out_shape=jax.ShapeDtypeStruct((t, D_HGRN), F32),
        scratch_shapes=[pltpu.VMEM((HGRN_HEADS, D_HGRN // HGRN_HEADS, HGRN_DK), F32)],
        compiler_params=pltpu.CompilerParams(dimension_semantics=("arbitrary",)),
        name="hgrn",
    )(q, f, i, g, lower_bound.reshape(1, D_HGRN).astype(F32), norm_w.reshape(1, D_HGRN).astype(F32))


def _post_kernel(x_ref, ys_ref, yh_ref, wo_ref, g1_ref, b1_ref, wrh_ref, wrl_ref, rb_ref,
                 wgs_ref, wus_ref, wds_ref, htt_ref, base_ref, idx_ref, gate_ref):
    tm = x_ref.shape[0]
    mix = (_dot(ys_ref[...].astype(BF16), wo_ref[0:D_SSM, :])
           + _dot(yh_ref[...].astype(BF16), wo_ref[D_SSM:, :]))
    h1 = _layer_norm(ALPHA * x_ref[...] + mix, g1_ref[...], b1_ref[...])
    packed = _pack_pairs(h1[:, :D_MODEL // 2], h1[:, D_MODEL // 2:])
    for k in range(PACK_TILES):
        htt_ref[pl.ds(k, tm, stride=PACK_TILES), :] = packed[:, k * LANES:(k + 1) * LANES]
    hb = h1.astype(BF16)
    hid = _silu(_dot(hb, wgs_ref[...])) * _dot(hb, wus_ref[...])
    base_ref[...] = ALPHA * h1 + _dot(hid.astype(BF16), wds_ref[...])

    hlo = (h1 - hb.astype(F32)).astype(BF16)
    logits = _dot_nt(wrh_ref[...], hb) + _dot_nt(wrh_ref[...], hlo) + _dot_nt(wrl_ref[...], hb)
    scores = _sigmoid(logits)
    biased = scores + rb_ref[...]
    per_group = N_EXPERTS // N_EXPERT_GROUPS
    eidx = lax.broadcasted_iota(I32, (N_EXPERTS, tm), 0)
    big = jnp.int32(1 << 20)
    gsc = []
    bidx = lax.broadcasted_iota(I32, (per_group, tm), 0)
    for gi in range(N_EXPERT_GROUPS):
        blk = biased[gi * per_group:(gi + 1) * per_group, :]
        m1 = jnp.max(blk, axis=0, keepdims=True)
        i1 = jnp.min(jnp.where(blk == m1, bidx, big), axis=0, keepdims=True)
        m2 = jnp.max(jnp.where(bidx == i1, NEG, blk), axis=0, keepdims=True)
        gsc.append(m1 + m2)
    cur = jnp.concatenate(gsc, axis=0)
    gidx = lax.broadcasted_iota(I32, (N_EXPERT_GROUPS, tm), 0)
    gsel = jnp.zeros((N_EXPERT_GROUPS, tm), F32)
    for _ in range(TOPK_GROUPS):
        m = jnp.max(cur, axis=0, keepdims=True)
        i = jnp.min(jnp.where(cur == m, gidx, big), axis=0, keepdims=True)
        hit = gidx == i
        gsel = jnp.where(hit, 1.0, gsel)
        cur = jnp.where(hit, NEG, cur)
    emask = jnp.concatenate(
        [jnp.broadcast_to(gsel[gi:gi + 1, :], (per_group, tm)) for gi in range(N_EXPERT_GROUPS)], axis=0)
    masked = jnp.where(emask > 0.0, biased, NEG)
    idx_rows, gate_rows = [], []
    for _ in range(TOP_K):
        m = jnp.max(masked, axis=0, keepdims=True)
        i = jnp.min(jnp.where(masked == m, eidx, big), axis=0, keepdims=True)
        hit = eidx == i
        idx_rows.append(i)
        gate_rows.append(jnp.sum(jnp.where(hit, scores, 0.0), axis=0, keepdims=True))
        masked = jnp.where(hit, NEG, masked)
    gates = jnp.concatenate(gate_rows, axis=0)
    gates = gates / jnp.sum(gates, axis=0, keepdims=True) * ROUTED_SCALE
    idx_ref[0] = jnp.concatenate(idx_rows, axis=0)
    gate_ref[0] = gates


def _post(x2, y_ssd, y_hgrn, wo, g1, b1, wr_hi, wr_lo, rbias, wgs, wus, wds):
    t = x2.shape[0]
    tm = TM_TOK
    nt = t // tm
    full = lambda shape: pl.BlockSpec(shape, lambda i: (0,) * len(shape))
    return pl.pallas_call(
        _post_kernel,
        grid=(nt,),
        in_specs=[pl.BlockSpec((tm, D_MODEL), lambda i: (i, 0)),
                  pl.BlockSpec((tm, D_SSM), lambda i: (i, 0)),
                  pl.BlockSpec((tm, D_HGRN), lambda i: (i, 0)),
                  full((D_MODEL, D_MODEL)), full((1, D_MODEL)), full((1, D_MODEL)),
                  full((N_EXPERTS, D_MODEL)), full((N_EXPERTS, D_MODEL)), full((N_EXPERTS, 1)),
                  full((D_MODEL, D_EXPERT)), full((D_MODEL, D_EXPERT)), full((D_EXPERT, D_MODEL))],
        out_specs=[pl.BlockSpec((tm * PACK_TILES, LANES), lambda i: (i, 0)),
                   pl.BlockSpec((tm, D_MODEL), lambda i: (i, 0)),
                   pl.BlockSpec((1, TOP_K, tm), lambda i: (i, 0, 0)),
                   pl.BlockSpec((1, TOP_K, tm), lambda i: (i, 0, 0))],
        out_shape=[jax.ShapeDtypeStruct((t * PACK_TILES, LANES), U32),
                   jax.ShapeDtypeStruct((t, D_MODEL), F32),
                   jax.ShapeDtypeStruct((nt, TOP_K, tm), I32),
                   jax.ShapeDtypeStruct((nt, TOP_K, tm), F32)],
        compiler_params=pltpu.CompilerParams(dimension_semantics=("arbitrary",),
                                             vmem_limit_bytes=48 * 1024 * 1024),
        name="post",
    )(x2, y_ssd, y_hgrn, wo, g1, b1, wr_hi, wr_lo, rbias, wgs, wus, wds)


def _max_blocks(t):
    return (t * TOP_K + N_EXPERTS * (EXPERT_ROWS - 1)) // EXPERT_ROWS


def _pos_kernel(idx_ref, dest_ref, be_ref, nb_ref, cnt_ref, start_ref):
    phase = pl.program_id(0)
    i = pl.program_id(1)
    tm = idx_ref.shape[-1]
    nblk_pad = be_ref.shape[-1]
    idx = idx_ref[0]
    eidx = lax.broadcasted_iota(I32, (N_EXPERTS, tm), 0)
    sel = [eidx == idx[k:k + 1, :] for k in range(TOP_K)]
    onehot = sel[0]
    for k in range(1, TOP_K):
        onehot = jnp.logical_or(onehot, sel[k])
    mt = jnp.where(onehot, 1.0, 0.0).astype(BF16)
    tile_cnt = _dot(mt, jnp.ones((tm, LANES), BF16))

    @pl.when(jnp.logical_and(phase == 0, i == 0))
    def _():
        cnt_ref[...] = jnp.zeros(cnt_ref.shape, F32)

    @pl.when(phase == 0)
    def _():
        cnt_ref[...] += tile_cnt

    @pl.when(jnp.logical_and(phase == 1, i == 0))
    def _():
        nb = jnp.floor((cnt_ref[...] + (EXPERT_ROWS - 1)) * (1.0 / EXPERT_ROWS))
        r = lax.broadcasted_iota(I32, (N_EXPERTS, N_EXPERTS), 0)
        c = lax.broadcasted_iota(I32, (N_EXPERTS, N_EXPERTS), 1)
        end = _dot((r >= c).astype(BF16), nb.astype(BF16))
        start_ref[...] = (end - nb) * EXPERT_ROWS
        cnt_ref[...] = jnp.zeros(cnt_ref.shape, F32)
        bi = lax.broadcasted_iota(I32, (N_EXPERTS, nblk_pad), 1).astype(F32)
        owner = jnp.sum(jnp.where(end[:, 0:1] <= bi, 1.0, 0.0), axis=0, keepdims=True)
        be_ref[...] = jnp.minimum(owner, N_EXPERTS - 1.0).astype(I32)
        nb_ref[...] = end[N_EXPERTS - 1:N_EXPERTS, :].astype(I32)

    @pl.when(phase == 1)
    def _():
        r = lax.broadcasted_iota(I32, (tm, tm), 0)
        c = lax.broadcasted_iota(I32, (tm, tm), 1)
        before = _dot(mt, (r < c).astype(BF16))
        slot = start_ref[:, 0:1] + cnt_ref[:, 0:1] + before
        rows = [jnp.sum(jnp.where(sel[k], slot, 0.0), axis=0, keepdims=True) for k in range(TOP_K)]
        dest_ref[0] = jnp.concatenate(rows, axis=0).astype(I32)
        cnt_ref[...] += tile_cnt


def _positions(idx, t):
    nt, _, tm = idx.shape
    nblk_pad = -(-_max_blocks(t) // LANES) * LANES
    return pl.pallas_call(
        _pos_kernel,
        grid=(2, nt),
        in_specs=[pl.BlockSpec((1, TOP_K, tm), lambda p, i: (i, 0, 0))],
        out_specs=[pl.BlockSpec((1, TOP_K, tm), lambda p, i: (i * p, 0, 0)),
                   pl.BlockSpec((1, nblk_pad), lambda p, i: (0, 0)),
                   pl.BlockSpec((1, LANES), lambda p, i: (0, 0))],
        out_shape=[jax.ShapeDtypeStruct((nt, TOP_K, tm), I32),
                   jax.ShapeDtypeStruct((1, nblk_pad), I32),
                   jax.ShapeDtypeStruct((1, LANES), I32)],
        scratch_shapes=[pltpu.VMEM((N_EXPERTS, LANES), F32), pltpu.VMEM((N_EXPERTS, LANES), F32)],
        compiler_params=pltpu.CompilerParams(dimension_semantics=("arbitrary", "arbitrary")),
        name="positions",
    )(idx)


def _row(ref, r, tiles):
    return ref.at[pl.ds(pl.multiple_of(r * tiles, tiles), tiles)]


def _rows_wait(src, dst, sem, n, tiles):
    pltpu.make_async_copy(src.at[pl.ds(0, n * tiles)], dst.at[pl.ds(0, n * tiles)], sem).wait()


def _dispatch_kernel(dest_ref, h_ref, xs_ref, sem):
    tm = h_ref.shape[0] // PACK_TILES

    def body(t, carry):
        for k in range(TOP_K):
            d = dest_ref[0, k, t]
            pltpu.make_async_copy(_row(h_ref, t, PACK_TILES), _row(xs_ref, d, PACK_TILES), sem).start(
                priority=k % 2)
        return carry

    lax.fori_loop(0, tm, body, 0)
    for _ in range(TOP_K):
        _rows_wait(h_ref, xs_ref, sem, tm, PACK_TILES)


def _dispatch(dest, htt, n_rows):
    t = htt.shape[0] // PACK_TILES
    tm = dest.shape[-1]
    return pl.pallas_call(
        _dispatch_kernel,
        grid=(t // tm,),
        in_specs=[pl.BlockSpec((1, TOP_K, tm), lambda i: (i, 0, 0), memory_space=pltpu.SMEM),
                  pl.BlockSpec((tm * PACK_TILES, LANES), lambda i: (i, 0))],
        out_specs=pl.BlockSpec(memory_space=pl.ANY),
        out_shape=jax.ShapeDtypeStruct((n_rows * PACK_TILES, LANES), U32),
        scratch_shapes=[pltpu.SemaphoreType.DMA(())],
        compiler_params=pltpu.CompilerParams(dimension_semantics=("arbitrary",), has_side_effects=True),
        name="dispatch",
    )(dest, htt)


def _expert_kernel(be_ref, nb_ref, xs_ref, wg_ref, wu_ref, wd_ref, ys_ref, wgb_ref, wub_ref, wdb_ref):
    i = pl.program_id(0)
    prev = be_ref[jnp.maximum(i - 1, 0)]
    fresh = jnp.logical_or(i == 0, be_ref[i] != prev)

    @pl.when(jnp.logical_and(i < nb_ref[0], fresh))
    def _():
        wgb_ref[...] = wg_ref[0].astype(BF16)
        wub_ref[...] = wu_ref[0].astype(BF16)
        wdb_ref[...] = wd_ref[0].astype(BF16)

    @pl.when(i < nb_ref[0])
    def _():
        halves = [_unpack_pairs(xs_ref[pl.ds(k, EXPERT_ROWS, stride=PACK_TILES), :]) for k in range(PACK_TILES)]
        xb = jnp.concatenate([lo for lo, _ in halves] + [hi for _, hi in halves], axis=-1).astype(BF16)
        hid = _silu(_dot(xb, wgb_ref[...])) * _dot(xb, wub_ref[...])
        yb = _dot(hid.astype(BF16), wdb_ref[...])
        for k in range(ROW_TILES):
            ys_ref[pl.ds(k, EXPERT_ROWS, stride=ROW_TILES), :] = yb[:, k * LANES:(k + 1) * LANES]


def _experts(block_e, n_blocks, xs, w_gate, w_up, w_down):
    n_rows = xs.shape[0] // PACK_TILES
    nblk = n_rows // EXPERT_ROWS
    blk = lambda i, be, nb: (jnp.minimum(i, nb[0] - 1), 0)
    wmap = lambda i, be, nb: (be[jnp.minimum(i, nb[0] - 1)], 0, 0)
    grid_spec = pltpu.PrefetchScalarGridSpec(
        num_scalar_prefetch=2,
        grid=(nblk,),
        in_specs=[pl.BlockSpec((EXPERT_ROWS * PACK_TILES, LANES), blk),
                  pl.BlockSpec((1, D_MODEL, D_EXPERT), wmap),
                  pl.BlockSpec((1, D_MODEL, D_EXPERT), wmap),
                  pl.BlockSpec((1, D_EXPERT, D_MODEL), wmap)],
        out_specs=pl.BlockSpec((EXPERT_ROWS * ROW_TILES, LANES), blk),
        scratch_shapes=[pltpu.VMEM((D_MODEL, D_EXPERT), BF16), pltpu.VMEM((D_MODEL, D_EXPERT), BF16),
                        pltpu.VMEM((D_EXPERT, D_MODEL), BF16)],
    )
    return pl.pallas_call(
        _expert_kernel,
        grid_spec=grid_spec,
        out_shape=jax.ShapeDtypeStruct((n_rows * ROW_TILES, LANES), F32),
        compiler_params=pltpu.CompilerParams(dimension_semantics=("arbitrary",),
                                             vmem_limit_bytes=48 * 1024 * 1024),
        name="experts",
    )(block_e, n_blocks, xs, w_gate, w_up, w_down)


def _combine_kernel(dest_ref, gate_ref, base_ref, g2_ref, b2_ref, ys_ref, o_ref, rows_ref, sem):
    tm = base_ref.shape[0]

    def body(t, carry):
        for k in range(TOP_K):
            d = dest_ref[0, k, t]
            pltpu.make_async_copy(_row(ys_ref, d, ROW_TILES), _row(rows_ref, k * tm + t, ROW_TILES), sem).start(
                priority=k % 2)
        return carry

    lax.fori_loop(0, tm, body, 0)
    gates_t = jnp.concatenate([gate_ref[0], jnp.zeros((LANES - TOP_K, tm), F32)], axis=0).T
    for k in range(TOP_K):
        _rows_wait(ys_ref, rows_ref, sem, tm, ROW_TILES)
    cols = []
    for c in range(ROW_TILES):
        acc = base_ref[:, c * LANES:(c + 1) * LANES]
        for k in range(TOP_K):
            acc = acc + gates_t[:, k:k + 1] * rows_ref[pl.ds(k * tm * ROW_TILES + c, tm, stride=ROW_TILES), :]
        cols.append(acc)
    o_ref[...] = _layer_norm(jnp.concatenate(cols, axis=-1), g2_ref[...], b2_ref[...])


def _combine(dest, gates, base, g2, b2, ys):
    t = base.shape[0]
    tm = dest.shape[-1]
    return pl.pallas_call(
        _combine_kernel,
        grid=(t // tm,),
        in_specs=[pl.BlockSpec((1, TOP_K, tm), lambda i: (i, 0, 0), memory_space=pltpu.SMEM),
                  pl.BlockSpec((1, TOP_K, tm), lambda i: (i, 0, 0)),
                  pl.BlockSpec((tm, D_MODEL), lambda i: (i, 0)),
                  pl.BlockSpec((1, D_MODEL), lambda i: (0, 0)),
                  pl.BlockSpec((1, D_MODEL), lambda i: (0, 0)),
                  pl.BlockSpec(memory_space=pl.ANY)],
        out_specs=pl.BlockSpec((tm, D_MODEL), lambda i: (i, 0)),
        out_shape=jax.ShapeDtypeStruct((t, D_MODEL), F32),
        scratch_shapes=[pltpu.VMEM((TOP_K * tm * ROW_TILES, LANES), F32), pltpu.SemaphoreType.DMA(())],
        compiler_params=pltpu.CompilerParams(dimension_semantics=("arbitrary",),
                                             vmem_limit_bytes=48 * 1024 * 1024),
        name="combine",
    )(dest, gates, base, g2, b2, ys)


def _layer(h2, w_in, conv_w, conv_b, dt_bias, a_log, d_skip, ssd_norm_w, lower_bound, hgrn_norm_w, w_out,
           ln1_g, ln1_b, w_router, router_bias, w_gate_e, w_up_e, w_down_e, w_gate_s, w_up_s, w_down_s,
           ln2_g, ln2_b):
    t = h2.shape[0]
    dt0 = D_SSM + D_CONV
    q0 = dt0 + SSD_HEADS
    w_perm = jnp.concatenate(
        [w_in[:, :dt0], w_in[:, q0:], w_in[:, dt0:q0], jnp.zeros((D_MODEL, DT_PAD - SSD_HEADS), w_in.dtype)],
        axis=1).astype(BF16)
    z, xbc, q, f, i, g, dt = _inproj(h2, w_perm)
    y_ssd = _ssd(z, xbc, dt, conv_w, conv_b, dt_bias, a_log, d_skip, ssd_norm_w)
    y_hgrn = _hgrn(q, f, i, g, lower_bound, hgrn_norm_w)

    wr_t = w_router.astype(F32).T
    wr_hi = wr_t.astype(BF16)
    wr_lo = (wr_t - wr_hi.astype(F32)).astype(BF16)
    row = lambda v: v.reshape(1, -1).astype(F32)
    htt, base, idx, gates = _post(
        h2, y_ssd, y_hgrn, w_out.astype(BF16), row(ln1_g), row(ln1_b), wr_hi, wr_lo,
        router_bias.reshape(N_EXPERTS, 1).astype(F32),
        w_gate_s.astype(BF16), w_up_s.astype(BF16), w_down_s.astype(BF16))

    dest, block_e, n_blocks = _positions(idx, t)
    n_rows = _max_blocks(t) * EXPERT_ROWS
    xs = _dispatch(dest, htt, n_rows)
    ys = _experts(block_e.reshape(-1), n_blocks.reshape(-1), xs, w_gate_e, w_up_e, w_down_e)
    return _combine(dest, gates, base, row(ln2_g), row(ln2_b), ys)


def kernel(x, w_in, conv_w, conv_b, dt_bias, a_log, d_skip, ssd_norm_w, hgrn_lb_logits, hgrn_norm_w, w_out,
           ln1_g, ln1_b, w_router, router_bias, w_gate_e, w_up_e, w_down_e, w_gate_s, w_up_s, w_down_s,
           ln2_g, ln2_b):
    bsz, t, d = x.shape
    assert bsz == 1 and d == D_MODEL, "the recurrent mixers carry state across the flattened token axis"
    depth = w_in.shape[0]
    lower_bounds = jnp.cumsum(jax.nn.softmax(hgrn_lb_logits.astype(F32), axis=0), axis=0)
    h = x.reshape(bsz * t, d)
    for l in range(depth):
        h = _layer(h, w_in[l], conv_w[l], conv_b[l], dt_bias[l], a_log[l], d_skip[l], ssd_norm_w[l],
                   lower_bounds[l], hgrn_norm_w[l], w_out[l], ln1_g[l], ln1_b[l], w_router[l],
                   router_bias[l], w_gate_e[l], w_up_e[l], w_down_e[l], w_gate_s[l], w_up_s[l],
                   w_down_s[l], ln2_g[l], ln2_b[l])
    return h.reshape(bsz, t, d)
```

```python
import jax
import jax.numpy as jnp
from jax import lax
from jax.experimental import pallas as pl
from jax.experimental.pallas import tpu as pltpu

F32 = jnp.float32
BF16 = jnp.bfloat16
I32 = jnp.int32
U32 = jnp.uint32

D_MODEL = 1024
D_SSM = 512
D_HGRN = 512
SSD_HEADS = 8
SSD_HEAD_DIM = 64
SSD_GROUPS = 2
SSD_STATE = 128
SSD_CONV = 4
SSD_CHUNK = 128
D_CONV = D_SSM + 2 * SSD_GROUPS * SSD_STATE
HGRN_HEADS = 4
HGRN_DK = 128
HGRN_CHUNK = 64
HGRN_SUB = 8
N_EXPERTS = 256
TOP_K = 8
N_EXPERT_GROUPS = 8
TOPK_GROUPS = 4
D_EXPERT = 256
ROUTED_SCALE = 2.5
ALPHA = 2.0 ** 0.25
LN_EPS = 1e-5
RMS_EPS = 1e-6

LANES = 128
SUBLANES = 8
ROW_TILES = D_MODEL // LANES
PACK_TILES = ROW_TILES // 2
DT_PAD = LANES
N_IN_PAD = D_SSM + D_CONV + 4 * D_HGRN + DT_PAD

TM_PROJ = 256
TM_TOK = 256
EXPERT_ROWS = 256
NEG = -1e30


def _sigmoid(x):
    return 1.0 / (1.0 + jnp.exp(-x))


def _silu(x):
    return x * _sigmoid(x)


def _split3(x):
    hi = x.astype(BF16)
    r = x - hi.astype(F32)
    mid = r.astype(BF16)
    lo = (r - mid.astype(F32)).astype(BF16)
    return hi, mid, lo


def _dot(a, b):
    return jnp.dot(a, b, preferred_element_type=F32)


def _dot_nt(a, b):
    return lax.dot_general(a, b, (((1,), (1,)), ((), ())), preferred_element_type=F32)


def _dot_tn(a, b):
    return lax.dot_general(a, b, (((0,), (0,)), ((), ())), preferred_element_type=F32)


def _sel_dot(sel, x):
    hi, mid, lo = _split3(x)
    return _dot(sel, hi) + _dot(sel, mid) + _dot(sel, lo)


def _dot_sel(x, sel):
    hi, mid, lo = _split3(x)
    return _dot(hi, sel) + _dot(mid, sel) + _dot(lo, sel)


def _pack_pairs(lo, hi):
    lo_bits = pltpu.bitcast(lo.astype(BF16).astype(F32), U32) >> 16
    hi_bits = pltpu.bitcast(hi.astype(BF16).astype(F32), U32) & jnp.uint32(0xFFFF0000)
    return hi_bits | lo_bits


def _unpack_pairs(p):
    return pltpu.bitcast(p << 16, F32), pltpu.bitcast(p & jnp.uint32(0xFFFF0000), F32)


def _layer_norm(x, g, b):
    mu = jnp.mean(x, axis=-1, keepdims=True)
    xc = x - mu
    var = jnp.mean(xc * xc, axis=-1, keepdims=True)
    return xc * lax.rsqrt(var + LN_EPS) * g + b


def _inproj_kernel(x_ref, w_ref, z_ref, xbc_ref, q_ref, f_ref, i_ref, g_ref, dt_ref):
    xb = x_ref[...].astype(BF16)
    col = 0
    for ref in (z_ref, xbc_ref, q_ref, f_ref, i_ref, g_ref, dt_ref):
        n = ref.shape[-1]
        ref[...] = _dot(xb, w_ref[:, col:col + n])
        col += n


def _inproj(x2, w_perm):
    t = x2.shape[0]
    widths = (D_SSM, D_CONV, D_HGRN, D_HGRN, D_HGRN, D_HGRN, DT_PAD)
    return pl.pallas_call(
        _inproj_kernel,
        grid=(t // TM_PROJ,),
        in_specs=[pl.BlockSpec((TM_PROJ, D_MODEL), lambda i: (i, 0)),
                  pl.BlockSpec((D_MODEL, N_IN_PAD), lambda i: (0, 0))],
        out_specs=[pl.BlockSpec((TM_PROJ, n), lambda i: (i, 0)) for n in widths],
        out_shape=[jax.ShapeDtypeStruct((t, n), F32) for n in widths],
        compiler_params=pltpu.CompilerParams(dimension_semantics=("arbitrary",),
                                             vmem_limit_bytes=48 * 1024 * 1024),
        name="inproj",
    )(x2, w_perm)


def _ssd_kernel(z_ref, xbc_ref, dt_ref, cw_ref, cb_ref, dtb_ref, alog_ref, dskip_ref, nw_ref,
                y_ref, ext_ref, st_ref):
    L = SSD_CHUNK
    halo = SUBLANES

    @pl.when(pl.program_id(0) == 0)
    def _():
        ext_ref[0:halo, :] = jnp.zeros((halo, D_CONV), F32)
        st_ref[...] = jnp.zeros(st_ref.shape, F32)

    ext_ref[halo:halo + L, :] = xbc_ref[...]
    acc = jnp.broadcast_to(cb_ref[...], (L, D_CONV))
    for k in range(SSD_CONV):
        off = halo - (SSD_CONV - 1) + k
        acc = acc + cw_ref[k:k + 1, :] * ext_ref[off:off + L, :]
    ext_ref[0:halo, :] = xbc_ref[L - halo:L, :]
    u = _silu(acc)
    xs = u[:, :D_SSM]
    bm = u[:, D_SSM:D_SSM + SSD_GROUPS * SSD_STATE]
    cm = u[:, D_SSM + SSD_GROUPS * SSD_STATE:]

    draw = dt_ref[...] + dtb_ref[...]
    dt = jnp.maximum(draw, 0.0) + jnp.log(1.0 + jnp.exp(-jnp.abs(draw)))
    ad = dt * (-jnp.exp(alog_ref[...]))
    rows = lax.broadcasted_iota(I32, (L, L), 0)
    cols = lax.broadcasted_iota(I32, (L, L), 1)
    causal = rows >= cols
    a_cum = _sel_dot(causal.astype(BF16), ad)
    a_cum_t = a_cum.T

    hrow = lax.broadcasted_iota(I32, (LANES, D_SSM), 0)
    hcol = lax.broadcasted_iota(I32, (LANES, D_SSM), 1) // SSD_HEAD_DIM
    expand = (hrow == hcol).astype(BF16)
    dt_x = _dot_sel(dt, expand)
    acx = _dot_sel(a_cum, expand)
    last = acx[L - 1:L, :]
    ea_x = jnp.exp(acx)
    dec_x = jnp.exp(last - acx)
    elast_x = jnp.exp(last)

    xdt = xs * dt_x
    gw = SSD_HEADS // SSD_GROUPS * SSD_HEAD_DIM
    lane_head = lax.broadcasted_iota(I32, (L, gw), 1) // SSD_HEAD_DIM
    ys = []
    for g in range(SSD_GROUPS):
        bg = bm[:, g * SSD_STATE:(g + 1) * SSD_STATE]
        cg = cm[:, g * SSD_STATE:(g + 1) * SSD_STATE].astype(BF16)
        bg_t = bg.T.astype(BF16)
        gmat = _dot(cg, bg_t)
        xdt_g = xdt[:, g * gw:(g + 1) * gw]
        xdt_gb = xdt_g.astype(BF16)
        r_prev = st_ref[g]
        y_g = _dot(cg, r_prev.astype(BF16)) * ea_x[:, g * gw:(g + 1) * gw]
        new_s = _dot(bg_t, (xdt_g * dec_x[:, g * gw:(g + 1) * gw]).astype(BF16))
        st_ref[g] = r_prev * elast_x[:, g * gw:(g + 1) * gw] + new_s
        for j in range(SSD_HEADS // SSD_GROUPS):
            h = g * (SSD_HEADS // SSD_GROUPS) + j
            diff = a_cum[:, h:h + 1] - a_cum_t[h:h + 1, :]
            decay = jnp.exp(jnp.where(causal, diff, NEG))
            yd = _dot((gmat * decay).astype(BF16), xdt_gb)
            y_g = y_g + jnp.where(lane_head == j, yd, 0.0)
        ys.append(y_g)
    y = jnp.concatenate(ys, axis=-1) + xs * dskip_ref[...]
    y = y * _silu(z_ref[...])
    outs = []
    ng = D_SSM // SSD_GROUPS
    for g in range(SSD_GROUPS):
        yg = y[:, g * ng:(g + 1) * ng]
        ms = jnp.mean(yg * yg, axis=-1, keepdims=True)
        outs.append(yg * lax.rsqrt(ms + RMS_EPS))
    y_ref[...] = jnp.concatenate(outs, axis=-1) * nw_ref[...]


def _ssd(z, xbc, dt, conv_w, conv_b, dt_bias, a_log, d_skip, norm_w):
    t = z.shape[0]
    L = SSD_CHUNK
    pad = lambda v: jnp.pad(v.astype(F32), (0, LANES - v.shape[0])).reshape(1, LANES)
    full = lambda shape: pl.BlockSpec(shape, lambda c: (0,) * len(shape))
    return pl.pallas_call(
        _ssd_kernel,
        grid=(t // L,),
        in_specs=[pl.BlockSpec((L, D_SSM), lambda c: (c, 0)),
                  pl.BlockSpec((L, D_CONV), lambda c: (c, 0)),
                  pl.BlockSpec((L, DT_PAD), lambda c: (c, 0)),
                  full((SSD_CONV, D_CONV)), full((1, D_CONV)), full((1, LANES)), full((1, LANES)),
                  full((1, D_SSM)), full((1, D_SSM))],
        out_specs=pl.BlockSpec((L, D_SSM), lambda c: (c, 0)),
        out_shape=jax.ShapeDtypeStruct((t, D_SSM), F32),
        scratch_shapes=[pltpu.VMEM((L + SUBLANES, D_CONV), F32),
                        pltpu.VMEM((SSD_GROUPS, SSD_STATE, D_SSM // SSD_GROUPS), F32)],
        compiler_params=pltpu.CompilerParams(dimension_semantics=("arbitrary",)),
        name="ssd",
    )(z, xbc, dt, conv_w.astype(F32), conv_b.reshape(1, D_CONV).astype(F32), pad(dt_bias), pad(a_log),
      jnp.repeat(d_skip.astype(F32), SSD_HEAD_DIM).reshape(1, D_SSM), norm_w.reshape(1, D_SSM).astype(F32))


def _tile_bcast(x, r):
    n, d = x.shape
    x3 = x.reshape(n // SUBLANES, SUBLANES, d)
    return jnp.broadcast_to(x3[:, r:r + 1, :], x3.shape).reshape(n, d)


def _hgrn_kernel(q_ref, f_ref, i_ref, g_ref, lb_ref, nw_ref, o_ref, st_ref):
    C = HGRN_CHUNK
    S = HGRN_SUB
    nsub = C // S

    @pl.when(pl.program_id(0) == 0)
    def _():
        st_ref[...] = jnp.zeros(st_ref.shape, F32)

    lb = lb_ref[...]
    fg = lb + (1.0 - lb) * _sigmoid(f_ref[...])
    kk = 1.0 - fg
    qs = _silu(q_ref[...])
    v = i_ref[...]
    rows = lax.broadcasted_iota(I32, (C, C), 0)
    cols = lax.broadcasted_iota(I32, (C, C), 1)
    cum = _sel_dot((rows >= cols).astype(BF16), jnp.log(fg))
    last = cum[C - 1:C, :]

    trow = lax.broadcasted_iota(I32, (C, D_HGRN), 0)
    tmod = trow % S
    tblk = trow // S

    xr = []
    for r in range(S):
        e = jnp.exp(jnp.where(tmod >= r, cum - _tile_bcast(cum, r), NEG))
        xr.append(qs * _tile_bcast(kk, r) * e)
    kbe = kk * jnp.exp(_tile_bcast(cum, S - 1) - cum)
    aj, bj = [], []
    for j in range(nsub - 1):
        ce = cum[j * S + S - 1:j * S + S, :]
        aj.append(qs * jnp.exp(jnp.where(tblk > j, cum - ce, NEG)))
        bj.append(jnp.where(tblk == j, kbe, 0.0))
    qd = (qs * jnp.exp(cum)).astype(BF16)
    kd = (kk * jnp.exp(last - cum)).astype(BF16)
    elast = jnp.exp(last)

    att_col = lax.broadcasted_iota(I32, (C, C), 1)
    att_rblk = lax.broadcasted_iota(I32, (C, C), 0) // S * S
    outs = []
    for h in range(HGRN_HEADS):
        blk = slice(h * HGRN_DK, (h + 1) * HGRN_DK)
        a_cat = jnp.concatenate([a[:, blk] for a in aj], axis=-1).astype(BF16)
        b_cat = jnp.concatenate([b[:, blk] for b in bj], axis=-1).astype(BF16)
        att = _dot_nt(a_cat, b_cat)
        for r in range(S):
            rs = jnp.sum(xr[r][:, blk], axis=-1, keepdims=True)
            att = jnp.where(att_col == att_rblk + r, rs, att)
        vb = v[:, blk].astype(BF16)
        s_prev = st_ref[h]
        o = _dot_nt(qd[:, blk], s_prev.astype(BF16)) + _dot(att.astype(BF16), vb)
        st_ref[h] = s_prev * elast[:, blk] + _dot_tn(vb, kd[:, blk])
        ms = jnp.mean(o * o, axis=-1, keepdims=True)
        outs.append(o * lax.rsqrt(ms + RMS_EPS))
    o_ref[...] = jnp.concatenate(outs, axis=-1) * nw_ref[...] * _silu(g_ref[...])


def _hgrn(q, f, i, g, lower_bound, norm_w):
    t = q.shape[0]
    C = HGRN_CHUNK
    tok = pl.BlockSpec((C, D_HGRN), lambda c: (c, 0))
    vec = pl.BlockSpec((1, D_HGRN), lambda c: (0, 0))
    return pl.pallas_call(
        _hgrn_kernel,
        grid=(t // C,),
        in_specs=[tok, tok, tok, tok, vec, vec],
        out_specs=tok,
        out_shape=jax.ShapeDtypeStruct((t, D_HGRN), F32),
        scratch_shapes=[pltpu.VMEM((HGRN_HEADS, D_HGRN // HGRN_HEADS, HGRN_DK), F32)],
        compiler_params=pltpu.CompilerParams(dimension_semantics=("arbitrary",)),
        name="hgrn",
    )(q, f, i, g, lower_bound.reshape(1, D_HGRN).astype(F32), norm_w.reshape(1, D_HGRN).astype(F32))


def _post_kernel(x_ref, ys_ref, yh_ref, wo_ref, g1_ref, b1_ref, wrh_ref, wrl_ref, rb_ref,
                 wgs_ref, wus_ref, wds_ref, htt_ref, base_ref, idx_ref, gate_ref):
    tm = x_ref.shape[0]
    mix = (_dot(ys_ref[...].astype(BF16), wo_ref[0:D_SSM, :])
           + _dot(yh_ref[...].astype(BF16), wo_ref[D_SSM:, :]))
    h1 = _layer_norm(ALPHA * x_ref[...] + mix, g1_ref[...], b1_ref[...])
    packed = _pack_pairs(h1[:, :D_MODEL // 2], h1[:, D_MODEL // 2:])
    for k in range(PACK_TILES):
        htt_ref[pl.ds(k, tm, stride=PACK_TILES), :] = packed[:, k * LANES:(k + 1) * LANES]
    hb = h1.astype(BF16)
    hid = _silu(_dot(hb, wgs_ref[...])) * _dot(hb, wus_ref[...])
    base_ref[...] = ALPHA * h1 + _dot(hid.astype(BF16), wds_ref[...])

    hlo = (h1 - hb.astype(F32)).astype(BF16)
    logits = _dot_nt(wrh_ref[...], hb) + _dot_nt(wrh_ref[...], hlo) + _dot_nt(wrl_ref[...], hb)
    scores = _sigmoid(logits)
    biased = scores + rb_ref[...]
    per_group = N_EXPERTS // N_EXPERT_GROUPS
    eidx = lax.broadcasted_iota(I32, (N_EXPERTS, tm), 0)
    big = jnp.int32(1 << 20)
    gsc = []
    bidx = lax.broadcasted_iota(I32, (per_group, tm), 0)
    for gi in range(N_EXPERT_GROUPS):
        blk = biased[gi * per_group:(gi + 1) * per_group, :]
        m1 = jnp.max(blk, axis=0, keepdims=True)
        i1 = jnp.min(jnp.where(blk == m1, bidx, big), axis=0, keepdims=True)
        m2 = jnp.max(jnp.where(bidx == i1, NEG, blk), axis=0, keepdims=True)
        gsc.append(m1 + m2)
    cur = jnp.concatenate(gsc, axis=0)
    gidx = lax.broadcasted_iota(I32, (N_EXPERT_GROUPS, tm), 0)
    gsel = jnp.zeros((N_EXPERT_GROUPS, tm), F32)
    for _ in range(TOPK_GROUPS):
        m = jnp.max(cur, axis=0, keepdims=True)
        i = jnp.min(jnp.where(cur == m, gidx, big), axis=0, keepdims=True)
        hit = gidx == i
        gsel = jnp.where(hit, 1.0, gsel)
        cur = jnp.where(hit, NEG, cur)
    emask = jnp.concatenate(
        [jnp.broadcast_to(gsel[gi:gi + 1, :], (per_group, tm)) for gi in range(N_EXPERT_GROUPS)], axis=0)
    masked = jnp.where(emask > 0.0, biased, NEG)
    idx_rows, gate_rows = [], []
    for _ in range(TOP_K):
        m = jnp.max(masked, axis=0, keepdims=True)
        i = jnp.min(jnp.where(masked == m, eidx, big), axis=0, keepdims=True)
        hit = eidx == i
        idx_rows.append(i)
        gate_rows.append(jnp.sum(jnp.where(hit, scores, 0.0), axis=0, keepdims=True))
        masked = jnp.where(hit, NEG, masked)
    gates = jnp.concatenate(gate_rows, axis=0)
    gates = gates / jnp.sum(gates, axis=0, keepdims=True) * ROUTED_SCALE
    idx_ref[0] = jnp.concatenate(idx_rows, axis=0)
    gate_ref[0] = gates


def _post(x2, y_ssd, y_hgrn, wo, g1, b1, wr_hi, wr_lo, rbias, wgs, wus, wds):
    t = x2.shape[0]
    tm = TM_TOK
    nt = t // tm
    full = lambda shape: pl.BlockSpec(shape, lambda i: (0,) * len(shape))
    return pl.pallas_call(
        _post_kernel,
        grid=(nt,),
        in_specs=[pl.BlockSpec((tm, D_MODEL), lambda i: (i, 0)),
                  pl.BlockSpec((tm, D_SSM), lambda i: (i, 0)),
                  pl.BlockSpec((tm, D_HGRN), lambda i: (i, 0)),
                  full((D_MODEL, D_MODEL)), full((1, D_MODEL)), full((1, D_MODEL)),
                  full((N_EXPERTS, D_MODEL)), full((N_EXPERTS, D_MODEL)), full((N_EXPERTS, 1)),
                  full((D_MODEL, D_EXPERT)), full((D_MODEL, D_EXPERT)), full((D_EXPERT, D_MODEL))],
        out_specs=[pl.BlockSpec((tm * PACK_TILES, LANES), lambda i: (i, 0)),
                   pl.BlockSpec((tm, D_MODEL), lambda i: (i, 0)),
                   pl.BlockSpec((1, TOP_K, tm), lambda i: (i, 0, 0)),
                   pl.BlockSpec((1, TOP_K, tm), lambda i: (i, 0, 0))],
        out_shape=[jax.ShapeDtypeStruct((t * PACK_TILES, LANES), U32),
                   jax.ShapeDtypeStruct((t, D_MODEL), F32),
                   jax.ShapeDtypeStruct((nt, TOP_K, tm), I32),
                   jax.ShapeDtypeStruct((nt, TOP_K, tm), F32)],
        compiler_params=pltpu.CompilerParams(dimension_semantics=("arbitrary",),
                                             vmem_limit_bytes=48 * 1024 * 1024),
        name="post",
    )(x2, y_ssd, y_hgrn, wo, g1, b1, wr_hi, wr_lo, rbias, wgs, wus, wds)


def _max_blocks(t):
    return (t * TOP_K + N_EXPERTS * (EXPERT_ROWS - 1)) // EXPERT_ROWS


def _pos_kernel(idx_ref, dest_ref, bstart_ref, nblk_ref, cnt_ref, start_ref):
    phase = pl.program_id(0)
    i = pl.program_id(1)
    tm = idx_ref.shape[-1]
    idx = idx_ref[0]
    eidx = lax.broadcasted_iota(I32, (N_EXPERTS, tm), 0)
    sel = [eidx == idx[k:k + 1, :] for k in range(TOP_K)]
    onehot = sel[0]
    for k in range(1, TOP_K):
        onehot = jnp.logical_or(onehot, sel[k])
    mt = jnp.where(onehot, 1.0, 0.0).astype(BF16)
    tile_cnt = _dot(mt, jnp.ones((tm, LANES), BF16))

    @pl.when(jnp.logical_and(phase == 0, i == 0))
    def _():
        cnt_ref[...] = jnp.zeros(cnt_ref.shape, F32)

    @pl.when(phase == 0)
    def _():
        cnt_ref[...] += tile_cnt

    @pl.when(jnp.logical_and(phase == 1, i == 0))
    def _():
        nb = jnp.floor((cnt_ref[...] + (EXPERT_ROWS - 1)) * (1.0 / EXPERT_ROWS))
        r = lax.broadcasted_iota(I32, (N_EXPERTS, N_EXPERTS), 0)
        c = lax.broadcasted_iota(I32, (N_EXPERTS, N_EXPERTS), 1)
        end = _dot((r >= c).astype(BF16), nb.astype(BF16))
        start_ref[...] = (end - nb) * EXPERT_ROWS
        cnt_ref[...] = jnp.zeros(cnt_ref.shape, F32)
        bstart_ref[...] = (end - nb).astype(I32)
        nblk_ref[...] = nb.astype(I32)

    @pl.when(phase == 1)
    def _():
        r = lax.broadcasted_iota(I32, (tm, tm), 0)
        c = lax.broadcasted_iota(I32, (tm, tm), 1)
        before = _dot(mt, (r < c).astype(BF16))
        slot = start_ref[:, 0:1] + cnt_ref[:, 0:1] + before
        rows = [jnp.sum(jnp.where(sel[k], slot, 0.0), axis=0, keepdims=True) for k in range(TOP_K)]
        dest_ref[0] = jnp.concatenate(rows, axis=0).astype(I32)
        cnt_ref[...] += tile_cnt


def _positions(idx):
    nt, _, tm = idx.shape
    return pl.pallas_call(
        _pos_kernel,
        grid=(2, nt),
        in_specs=[pl.BlockSpec((1, TOP_K, tm), lambda p, i: (i, 0, 0))],
        out_specs=[pl.BlockSpec((1, TOP_K, tm), lambda p, i: (i * p, 0, 0)),
                   pl.BlockSpec((N_EXPERTS, LANES), lambda p, i: (0, 0)),
                   pl.BlockSpec((N_EXPERTS, LANES), lambda p, i: (0, 0))],
        out_shape=[jax.ShapeDtypeStruct((nt, TOP_K, tm), I32),
                   jax.ShapeDtypeStruct((N_EXPERTS, LANES), I32),
                   jax.ShapeDtypeStruct((N_EXPERTS, LANES), I32)],
        scratch_shapes=[pltpu.VMEM((N_EXPERTS, LANES), F32), pltpu.VMEM((N_EXPERTS, LANES), F32)],
        compiler_params=pltpu.CompilerParams(dimension_semantics=("arbitrary", "arbitrary")),
        name="positions",
    )(idx)


def _row(ref, r, tiles):
    return ref.at[pl.ds(pl.multiple_of(r * tiles, tiles), tiles)]


def _rows_wait(src, dst, sem, n, tiles):
    pltpu.make_async_copy(src.at[pl.ds(0, n * tiles)], dst.at[pl.ds(0, n * tiles)], sem).wait()


def _dispatch_kernel(dest_ref, h_ref, xs_ref, sem):
    tm = h_ref.shape[0] // PACK_TILES

    def body(t, carry):
        for k in range(TOP_K):
            d = dest_ref[0, k, t]
            pltpu.make_async_copy(_row(h_ref, t, PACK_TILES), _row(xs_ref, d, PACK_TILES), sem).start(
                priority=k % 2)
        return carry

    lax.fori_loop(0, tm, body, 0)
    for _ in range(TOP_K):
        _rows_wait(h_ref, xs_ref, sem, tm, PACK_TILES)


def _dispatch(dest, htt, n_rows):
    t = htt.shape[0] // PACK_TILES
    tm = dest.shape[-1]
    return pl.pallas_call(
        _dispatch_kernel,
        grid=(t // tm,),
        in_specs=[pl.BlockSpec((1, TOP_K, tm), lambda i: (i, 0, 0), memory_space=pltpu.SMEM),
                  pl.BlockSpec((tm * PACK_TILES, LANES), lambda i: (i, 0))],
        out_specs=pl.BlockSpec(memory_space=pl.ANY),
        out_shape=jax.ShapeDtypeStruct((n_rows * PACK_TILES, LANES), U32),
        scratch_shapes=[pltpu.SemaphoreType.DMA(())],
        compiler_params=pltpu.CompilerParams(dimension_semantics=("arbitrary",), has_side_effects=True),
        name="dispatch",
    )(dest, htt)


def _expert_kernel(bstart_ref, nblk_ref, xs_hbm, wg_ref, wu_ref, wd_ref, ys_hbm,
                   xbuf, ybuf, wgb_ref, wub_ref, wdb_ref, xsem, ysem, ypend):
    e = pl.program_id(0)
    last = pl.num_programs(0) - 1
    n = nblk_ref[e]
    b0 = bstart_ref[e]
    blk_words = EXPERT_ROWS * PACK_TILES

    def hbm_block(ref, b):
        return ref.at[pl.ds(pl.multiple_of(b * blk_words, blk_words), blk_words)]

    def x_copy(b, slot):
        return pltpu.make_async_copy(hbm_block(xs_hbm, b), xbuf.at[slot], xsem.at[slot])

    def y_copy(b, slot):
        return pltpu.make_async_copy(ybuf.at[slot], hbm_block(ys_hbm, b), ysem.at[slot])

    @pl.when(e == 0)
    def _():
        ypend[0] = 0
        ypend[1] = 0

        @pl.when(n > 0)
        def _():
            x_copy(b0, 0).start()

    def block(j, slot):
        x_copy(b0 + j, slot).wait()

        @pl.when(j + 1 < n)
        def _():
            x_copy(b0 + j + 1, 1 - slot).start()

        @pl.when(ypend[slot] == 1)
        def _():
            y_copy(b0, slot).wait()

        halves = [_unpack_pairs(xbuf[slot, pl.ds(k, EXPERT_ROWS, stride=PACK_TILES), :])
                  for k in range(PACK_TILES)]
        xb = jnp.concatenate([lo for lo, _ in halves] + [hi for _, hi in halves], axis=-1).astype(BF16)
        hid = _silu(_dot(xb, wgb_ref[...])) * _dot(xb, wub_ref[...])
        yb = _dot(hid.astype(BF16), wdb_ref[...])
        packed = _pack_pairs(yb[:, :D_MODEL // 2], yb[:, D_MODEL // 2:])
        for k in range(PACK_TILES):
            ybuf[slot, pl.ds(k, EXPERT_ROWS, stride=PACK_TILES), :] = packed[:, k * LANES:(k + 1) * LANES]
        y_copy(b0 + j, slot).start()
        ypend[slot] = 1

    @pl.when(n > 0)
    def _():
        wgb_ref[...] = wg_ref[0].astype(BF16)
        wub_ref[...] = wu_ref[0].astype(BF16)
        wdb_ref[...] = wd_ref[0].astype(BF16)

        def pair(p, carry):
            block(2 * p, 0)

            @pl.when(2 * p + 1 < n)
            def _():
                block(2 * p + 1, 1)

            return carry

        lax.fori_loop(0, (n + 1) // 2, pair, 0)

    nxt = jnp.minimum(e + 1, last)

    @pl.when(jnp.logical_and(e < last, nblk_ref[nxt] > 0))
    def _():
        x_copy(bstart_ref[nxt], 0).start()

    @pl.when(e == last)
    def _():
        for slot in range(2):
            @pl.when(ypend[slot] == 1)
            def _():
                y_copy(b0, slot).wait()


def _experts(bstart, nblk, xs, w_gate, w_up, w_down):
    blk_words = EXPERT_ROWS * PACK_TILES
    wspec = lambda shape: pl.BlockSpec((1,) + shape, lambda e, bs, nb: (e, 0, 0))
    grid_spec = pltpu.PrefetchScalarGridSpec(
        num_scalar_prefetch=2,
        grid=(N_EXPERTS,),
        in_specs=[pl.BlockSpec(memory_space=pl.ANY),
                  wspec((D_MODEL, D_EXPERT)), wspec((D_MODEL, D_EXPERT)), wspec((D_EXPERT, D_MODEL))],
        out_specs=pl.BlockSpec(memory_space=pl.ANY),
        scratch_shapes=[pltpu.VMEM((2, blk_words, LANES), U32), pltpu.VMEM((2, blk_words, LANES), U32),
                        pltpu.VMEM((D_MODEL, D_EXPERT), BF16), pltpu.VMEM((D_MODEL, D_EXPERT), BF16),
                        pltpu.VMEM((D_EXPERT, D_MODEL), BF16),
                        pltpu.SemaphoreType.DMA((2,)), pltpu.SemaphoreType.DMA((2,)), pltpu.SMEM((2,), I32)],
    )
    return pl.pallas_call(
        _expert_kernel,
        grid_spec=grid_spec,
        out_shape=jax.ShapeDtypeStruct(xs.shape, U32),
        compiler_params=pltpu.CompilerParams(dimension_semantics=("arbitrary",), has_side_effects=True),
        name="experts",
    )(bstart, nblk, xs, w_gate, w_up, w_down)


def _combine_kernel(dest_ref, gate_ref, base_ref, g2_ref, b2_ref, ys_ref, o_ref, rows_ref, sem):
    tm = base_ref.shape[0]

    def body(t, carry):
        for k in range(TOP_K):
            d = dest_ref[0, k, t]
            pltpu.make_async_copy(_row(ys_ref, d, PACK_TILES), _row(rows_ref, k * tm + t, PACK_TILES), sem).start(
                priority=k % 2)
        return carry

    lax.fori_loop(0, tm, body, 0)
    gates_t = jnp.concatenate([gate_ref[0], jnp.zeros((LANES - TOP_K, tm), F32)], axis=0).T
    for k in range(TOP_K):
        _rows_wait(ys_ref, rows_ref, sem, tm, PACK_TILES)
    half = D_MODEL // 2
    lo_cols, hi_cols = [], []
    for c in range(PACK_TILES):
        lo_acc = base_ref[:, c * LANES:(c + 1) * LANES]
        hi_acc = base_ref[:, half + c * LANES:half + (c + 1) * LANES]
        for k in range(TOP_K):
            lo, hi = _unpack_pairs(rows_ref[pl.ds(k * tm * PACK_TILES + c, tm, stride=PACK_TILES), :])
            lo_acc = lo_acc + gates_t[:, k:k + 1] * lo
            hi_acc = hi_acc + gates_t[:, k:k + 1] * hi
        lo_cols.append(lo_acc)
        hi_cols.append(hi_acc)
    o_ref[...] = _layer_norm(jnp.concatenate(lo_cols + hi_cols, axis=-1), g2_ref[...], b2_ref[...])


def _combine(dest, gates, base, g2, b2, ys):
    t = base.shape[0]
    tm = dest.shape[-1]
    return pl.pallas_call(
        _combine_kernel,
        grid=(t // tm,),
        in_specs=[pl.BlockSpec((1, TOP_K, tm), lambda i: (i, 0, 0), memory_space=pltpu.SMEM),
                  pl.BlockSpec((1, TOP_K, tm), lambda i: (i, 0, 0)),
                  pl.BlockSpec((tm, D_MODEL), lambda i: (i, 0)),
                  pl.BlockSpec((1, D_MODEL), lambda i: (0, 0)),
                  pl.BlockSpec((1, D_MODEL), lambda i: (0, 0)),
                  pl.BlockSpec(memory_space=pl.ANY)],
        out_specs=pl.BlockSpec((tm, D_MODEL), lambda i: (i, 0)),
        out_shape=jax.ShapeDtypeStruct((t, D_MODEL), F32),
        scratch_shapes=[pltpu.VMEM((TOP_K * tm * PACK_TILES, LANES), U32), pltpu.SemaphoreType.DMA(())],
        compiler_params=pltpu.CompilerParams(dimension_semantics=("arbitrary",),
                                             vmem_limit_bytes=48 * 1024 * 1024),
        name="combine",
    )(dest, gates, base, g2, b2, ys)


def _layer(h2, w_in, conv_w, conv_b, dt_bias, a_log, d_skip, ssd_norm_w, lower_bound, hgrn_norm_w, w_out,
           ln1_g, ln1_b, w_router, router_bias, w_gate_e, w_up_e, w_down_e, w_gate_s, w_up_s, w_down_s,
           ln2_g, ln2_b):
    t = h2.shape[0]
    dt0 = D_SSM + D_CONV
    q0 = dt0 + SSD_HEADS
    w_perm = jnp.concatenate(
        [w_in[:, :dt0], w_in[:, q0:], w_in[:, dt0:q0], jnp.zeros((D_MODEL, DT_PAD - SSD_HEADS), w_in.dtype)],
        axis=1).astype(BF16)
    z, xbc, q, f, i, g, dt = _inproj(h2, w_perm)
    y_ssd = _ssd(z, xbc, dt, conv_w, conv_b, dt_bias, a_log, d_skip, ssd_norm_w)
    y_hgrn = _hgrn(q, f, i, g, lower_bound, hgrn_norm_w)

    wr_t = w_router.astype(F32).T
    wr_hi = wr_t.astype(BF16)
    wr_lo = (wr_t - wr_hi.astype(F32)).astype(BF16)
    row = lambda v: v.reshape(1, -1).astype(F32)
    htt, base, idx, gates = _post(
        h2, y_ssd, y_hgrn, w_out.astype(BF16), row(ln1_g), row(ln1_b), wr_hi, wr_lo,
        router_bias.reshape(N_EXPERTS, 1).astype(F32),
        w_gate_s.astype(BF16), w_up_s.astype(BF16), w_down_s.astype(BF16))

    dest, bstart, nblk = _positions(idx)
    n_rows = _max_blocks(t) * EXPERT_ROWS
    xs = _dispatch(dest, htt, n_rows)
    ys = _experts(bstart[:, 0], nblk[:, 0], xs, w_gate_e, w_up_e, w_down_e)
    return _combine(dest, gates, base, row(ln2_g), row(ln2_b), ys)


def kernel(x, w_in, conv_w, conv_b, dt_bias, a_log, d_skip, ssd_norm_w, hgrn_lb_logits, hgrn_norm_w, w_out,
           ln1_g, ln1_b, w_router, router_bias, w_gate_e, w_up_e, w_down_e, w_gate_s, w_up_s, w_down_s,
           ln2_g, ln2_b):
    bsz, t, d = x.shape
    assert bsz == 1 and d == D_MODEL, "the recurrent mixers carry state across the flattened token axis"
    depth = w_in.shape[0]
    lower_bounds = jnp.cumsum(jax.nn.softmax(hgrn_lb_logits.astype(F32), axis=0), axis=0)
    h = x.reshape(bsz * t, d)
    for l in range(depth):
        h = _layer(h, w_in[l], conv_w[l], conv_b[l], dt_bias[l], a_log[l], d_skip[l], ssd_norm_w[l],
                   lower_bounds[l], hgrn_norm_w[l], w_out[l], ln1_g[l], ln1_b[l], w_router[l],
                   router_bias[l], w_gate_e[l], w_up_e[l], w_down_e[l], w_gate_s[l], w_up_s[l],
                   w_down_s[l], ln2_g[l], ln2_b[l])
    return h.reshape(bsz, t, d)
```

```python
import jax
import jax.numpy as jnp
from jax import lax
from jax.experimental import pallas as pl
from jax.experimental.pallas import tpu as pltpu

F32 = jnp.float32
BF16 = jnp.bfloat16
I32 = jnp.int32
U32 = jnp.uint32

D_MODEL = 1024
D_SSM = 512
D_HGRN = 512
SSD_HEADS = 8
SSD_HEAD_DIM = 64
SSD_GROUPS = 2
SSD_STATE = 128
SSD_CONV = 4
SSD_CHUNK = 128
D_CONV = D_SSM + 2 * SSD_GROUPS * SSD_STATE
HGRN_HEADS = 4
HGRN_DK = 128
HGRN_CHUNK = 64
HGRN_SUB = 8
N_EXPERTS = 256
TOP_K = 8
N_EXPERT_GROUPS = 8
TOPK_GROUPS = 4
D_EXPERT = 256
ROUTED_SCALE = 2.5
ALPHA = 2.0 ** 0.25
LN_EPS = 1e-5
RMS_EPS = 1e-6

LANES = 128
SUBLANES = 8
ROW_TILES = D_MODEL // LANES
PACK_TILES = ROW_TILES // 2
DT_PAD = LANES
N_IN_PAD = D_SSM + D_CONV + 4 * D_HGRN + DT_PAD

TM_PROJ = 256
TM_TOK = 256
EXPERT_ROWS = 256
X_RING = 4
Y_RING = 2
NEG = -1e30


def _sigmoid(x):
    return 1.0 / (1.0 + jnp.exp(-x))


def _silu(x):
    return x * _sigmoid(x)


def _split3(x):
    hi = x.astype(BF16)
    r = x - hi.astype(F32)
    mid = r.astype(BF16)
    lo = (r - mid.astype(F32)).astype(BF16)
    return hi, mid, lo


def _dot(a, b):
    return jnp.dot(a, b, preferred_element_type=F32)


def _dot_nt(a, b):
    return lax.dot_general(a, b, (((1,), (1,)), ((), ())), preferred_element_type=F32)


def _dot_tn(a, b):
    return lax.dot_general(a, b, (((0,), (0,)), ((), ())), preferred_element_type=F32)


def _sel_dot(sel, x):
    hi, mid, lo = _split3(x)
    return _dot(sel, hi) + _dot(sel, mid) + _dot(sel, lo)


def _dot_sel(x, sel):
    hi, mid, lo = _split3(x)
    return _dot(hi, sel) + _dot(mid, sel) + _dot(lo, sel)


def _pack_pairs(lo, hi):
    lo_bits = pltpu.bitcast(lo.astype(BF16).astype(F32), U32) >> 16
    hi_bits = pltpu.bitcast(hi.astype(BF16).astype(F32), U32) & jnp.uint32(0xFFFF0000)
    return hi_bits | lo_bits


def _unpack_pairs(p):
    return pltpu.bitcast(p << 16, F32), pltpu.bitcast(p & jnp.uint32(0xFFFF0000), F32)


def _layer_norm(x, g, b):
    mu = jnp.mean(x, axis=-1, keepdims=True)
    xc = x - mu
    var = jnp.mean(xc * xc, axis=-1, keepdims=True)
    return xc * lax.rsqrt(var + LN_EPS) * g + b


def _inproj_kernel(x_ref, w_ref, z_ref, xbc_ref, q_ref, f_ref, i_ref, g_ref, dt_ref):
    xb = x_ref[...].astype(BF16)
    col = 0
    for ref in (z_ref, xbc_ref, q_ref, f_ref, i_ref, g_ref, dt_ref):
        n = ref.shape[-1]
        ref[...] = _dot(xb, w_ref[:, col:col + n])
        col += n


def _inproj(x2, w_perm):
    t = x2.shape[0]
    widths = (D_SSM, D_CONV, D_HGRN, D_HGRN, D_HGRN, D_HGRN, DT_PAD)
    return pl.pallas_call(
        _inproj_kernel,
        grid=(t // TM_PROJ,),
        in_specs=[pl.BlockSpec((TM_PROJ, D_MODEL), lambda i: (i, 0)),
                  pl.BlockSpec((D_MODEL, N_IN_PAD), lambda i: (0, 0))],
        out_specs=[pl.BlockSpec((TM_PROJ, n), lambda i: (i, 0)) for n in widths],
        out_shape=[jax.ShapeDtypeStruct((t, n), F32) for n in widths],
        compiler_params=pltpu.CompilerParams(dimension_semantics=("arbitrary",),
                                             vmem_limit_bytes=48 * 1024 * 1024),
        name="inproj",
    )(x2, w_perm)


def _ssd_kernel(z_ref, xbc_ref, dt_ref, cw_ref, cb_ref, dtb_ref, alog_ref, dskip_ref, nw_ref,
                y_ref, ext_ref, st_ref):
    L = SSD_CHUNK
    halo = SUBLANES

    @pl.when(pl.program_id(0) == 0)
    def _():
        ext_ref[0:halo, :] = jnp.zeros((halo, D_CONV), F32)
        st_ref[...] = jnp.zeros(st_ref.shape, F32)

    ext_ref[halo:halo + L, :] = xbc_ref[...]
    acc = jnp.broadcast_to(cb_ref[...], (L, D_CONV))
    for k in range(SSD_CONV):
        off = halo - (SSD_CONV - 1) + k
        acc = acc + cw_ref[k:k + 1, :] * ext_ref[off:off + L, :]
    ext_ref[0:halo, :] = xbc_ref[L - halo:L, :]
    u = _silu(acc)
    xs = u[:, :D_SSM]
    bm = u[:, D_SSM:D_SSM + SSD_GROUPS * SSD_STATE]
    cm = u[:, D_SSM + SSD_GROUPS * SSD_STATE:]

    draw = dt_ref[...] + dtb_ref[...]
    dt = jnp.maximum(draw, 0.0) + jnp.log(1.0 + jnp.exp(-jnp.abs(draw)))
    ad = dt * (-jnp.exp(alog_ref[...]))
    rows = lax.broadcasted_iota(I32, (L, L), 0)
    cols = lax.broadcasted_iota(I32, (L, L), 1)
    causal = rows >= cols
    a_cum = _sel_dot(causal.astype(BF16), ad)
    a_cum_t = a_cum.T

    hrow = lax.broadcasted_iota(I32, (LANES, D_SSM), 0)
    hcol = lax.broadcasted_iota(I32, (LANES, D_SSM), 1) // SSD_HEAD_DIM
    expand = (hrow == hcol).astype(BF16)
    dt_x = _dot_sel(dt, expand)
    acx = _dot_sel(a_cum, expand)
    last = acx[L - 1:L, :]
    ea_x = jnp.exp(acx)
    dec_x = jnp.exp(last - acx)
    elast_x = jnp.exp(last)

    xdt = xs * dt_x
    gw = SSD_HEADS // SSD_GROUPS * SSD_HEAD_DIM
    lane_head = lax.broadcasted_iota(I32, (L, gw), 1) // SSD_HEAD_DIM
    ys = []
    for g in range(SSD_GROUPS):
        bg = bm[:, g * SSD_STATE:(g + 1) * SSD_STATE]
        cg = cm[:, g * SSD_STATE:(g + 1) * SSD_STATE].astype(BF16)
        bg_t = bg.T.astype(BF16)
        gmat = _dot(cg, bg_t)
        xdt_g = xdt[:, g * gw:(g + 1) * gw]
        xdt_gb = xdt_g.astype(BF16)
        r_prev = st_ref[g]
        y_g = _dot(cg, r_prev.astype(BF16)) * ea_x[:, g * gw:(g + 1) * gw]
        new_s = _dot(bg_t, (xdt_g * dec_x[:, g * gw:(g + 1) * gw]).astype(BF16))
        st_ref[g] = r_prev * elast_x[:, g * gw:(g + 1) * gw] + new_s
        for j in range(SSD_HEADS // SSD_GROUPS):
            h = g * (SSD_HEADS // SSD_GROUPS) + j
            diff = a_cum[:, h:h + 1] - a_cum_t[h:h + 1, :]
            decay = jnp.exp(jnp.where(causal, diff, NEG))
            yd = _dot((gmat * decay).astype(BF16), xdt_gb)
            y_g = y_g + jnp.where(lane_head == j, yd, 0.0)
        ys.append(y_g)
    y = jnp.concatenate(ys, axis=-1) + xs * dskip_ref[...]
    y = y * _silu(z_ref[...])
    outs = []
    ng = D_SSM // SSD_GROUPS
    for g in range(SSD_GROUPS):
        yg = y[:, g * ng:(g + 1) * ng]
        ms = jnp.mean(yg * yg, axis=-1, keepdims=True)
        outs.append(yg * lax.rsqrt(ms + RMS_EPS))
    y_ref[...] = jnp.concatenate(outs, axis=-1) * nw_ref[...]


def _ssd(z, xbc, dt, conv_w, conv_b, dt_bias, a_log, d_skip, norm_w):
    t = z.shape[0]
    L = SSD_CHUNK
    pad = lambda v: jnp.pad(v.astype(F32), (0, LANES - v.shape[0])).reshape(1, LANES)
    full = lambda shape: pl.BlockSpec(shape, lambda c: (0,) * len(shape))
    return pl.pallas_call(
        _ssd_kernel,
        grid=(t // L,),
        in_specs=[pl.BlockSpec((L, D_SSM), lambda c: (c, 0)),
                  pl.BlockSpec((L, D_CONV), lambda c: (c, 0)),
                  pl.BlockSpec((L, DT_PAD), lambda c: (c, 0)),
                  full((SSD_CONV, D_CONV)), full((1, D_CONV)), full((1, LANES)), full((1, LANES)),
                  full((1, D_SSM)), full((1, D_SSM))],
        out_specs=pl.BlockSpec((L, D_SSM), lambda c: (c, 0)),
        out_shape=jax.ShapeDtypeStruct((t, D_SSM), F32),
        scratch_shapes=[pltpu.VMEM((L + SUBLANES, D_CONV), F32),
                        pltpu.VMEM((SSD_GROUPS, SSD_STATE, D_SSM // SSD_GROUPS), F32)],
        compiler_params=pltpu.CompilerParams(dimension_semantics=("arbitrary",)),
        name="ssd",
    )(z, xbc, dt, conv_w.astype(F32), conv_b.reshape(1, D_CONV).astype(F32), pad(dt_bias), pad(a_log),
      jnp.repeat(d_skip.astype(F32), SSD_HEAD_DIM).reshape(1, D_SSM), norm_w.reshape(1, D_SSM).astype(F32))


def _tile_bcast(x, r):
    n, d = x.shape
    x3 = x.reshape(n // SUBLANES, SUBLANES, d)
    return jnp.broadcast_to(x3[:, r:r + 1, :], x3.shape).reshape(n, d)


def _hgrn_kernel(q_ref, f_ref, i_ref, g_ref, lb_ref, nw_ref, o_ref, st_ref):
    C = HGRN_CHUNK
    S = HGRN_SUB
    nsub = C // S

    @pl.when(pl.program_id(0) == 0)
    def _():
        st_ref[...] = jnp.zeros(st_ref.shape, F32)

    lb = lb_ref[...]
    fg = lb + (1.0 - lb) * _sigmoid(f_ref[...])
    kk = 1.0 - fg
    qs = _silu(q_ref[...])
    v = i_ref[...]
    rows = lax.broadcasted_iota(I32, (C, C), 0)
    cols = lax.broadcasted_iota(I32, (C, C), 1)
    cum = _sel_dot((rows >= cols).astype(BF16), jnp.log(fg))
    last = cum[C - 1:C, :]

    trow = lax.broadcasted_iota(I32, (C, D_HGRN), 0)
    tmod = trow % S
    tblk = trow // S

    xr = []
    for r in range(S):
        e = jnp.exp(jnp.where(tmod >= r, cum - _tile_bcast(cum, r), NEG))
        xr.append(qs * _tile_bcast(kk, r) * e)
    kbe = kk * jnp.exp(_tile_bcast(cum, S - 1) - cum)
    aj, bj = [], []
    for j in range(nsub - 1):
        ce = cum[j * S + S - 1:j * S + S, :]
        aj.append(qs * jnp.exp(jnp.where(tblk > j, cum - ce, NEG)))
        bj.append(jnp.where(tblk == j, kbe, 0.0))
    qd = (qs * jnp.exp(cum)).astype(BF16)
    kd = (kk * jnp.exp(last - cum)).astype(BF16)
    elast = jnp.exp(last)

    att_col = lax.broadcasted_iota(I32, (C, C), 1)
    att_rblk = lax.broadcasted_iota(I32, (C, C), 0) // S * S
    outs = []
    for h in range(HGRN_HEADS):
        blk = slice(h * HGRN_DK, (h + 1) * HGRN_DK)
        a_cat = jnp.concatenate([a[:, blk] for a in aj], axis=-1).astype(BF16)
        b_cat = jnp.concatenate([b[:, blk] for b in bj], axis=-1).astype(BF16)
        att = _dot_nt(a_cat, b_cat)
        for r in range(S):
            rs = jnp.sum(xr[r][:, blk], axis=-1, keepdims=True)
            att = jnp.where(att_col == att_rblk + r, rs, att)
        vb = v[:, blk].astype(BF16)
        s_prev = st_ref[h]
        o = _dot_nt(qd[:, blk], s_prev.astype(BF16)) + _dot(att.astype(BF16), vb)
        st_ref[h] = s_prev * elast[:, blk] + _dot_tn(vb, kd[:, blk])
        ms = jnp.mean(o * o, axis=-1, keepdims=True)
        outs.append(o * lax.rsqrt(ms + RMS_EPS))
    o_ref[...] = jnp.concatenate(outs, axis=-1) * nw_ref[...] * _silu(g_ref[...])


def _hgrn(q, f, i, g, lower_bound, norm_w):
    t = q.shape[0]
    C = HGRN_CHUNK
    tok = pl.BlockSpec((C, D_HGRN), lambda c: (c, 0))
    vec = pl.BlockSpec((1, D_HGRN), lambda c: (0, 0))
    return pl.pallas_call(
        _hgrn_kernel,
        grid=(t // C,),
        in_specs=[tok, tok, tok, tok, vec, vec],
        out_specs=tok,
        out_shape=jax.ShapeDtypeStruct((t, D_HGRN), F32),
        scratch_shapes=[pltpu.VMEM((HGRN_HEADS, D_HGRN // HGRN_HEADS, HGRN_DK), F32)],
        compiler_params=pltpu.CompilerParams(dimension_semantics=("arbitrary",)),
        name="hgrn",
    )(q, f, i, g, lower_bound.reshape(1, D_HGRN).astype(F32), norm_w.reshape(1, D_HGRN).astype(F32))


def _post_kernel(x_ref, ys_ref, yh_ref, wo_ref, g1_ref, b1_ref, wrh_ref, wrl_ref, rb_ref,
                 wgs_ref, wus_ref, wds_ref, htt_ref, base_ref, idx_ref, gate_ref):
    tm = x_ref.shape[0]
    mix = (_dot(ys_ref[...].astype(BF16), wo_ref[0:D_SSM, :])
           + _dot(yh_ref[...].astype(BF16), wo_ref[D_SSM:, :]))
    h1 = _layer_norm(ALPHA * x_ref[...] + mix, g1_ref[...], b1_ref[...])
    packed = _pack_pairs(h1[:, :D_MODEL // 2], h1[:, D_MODEL // 2:])
    for k in range(PACK_TILES):
        htt_ref[pl.ds(k, tm, stride=PACK_TILES), :] = packed[:, k * LANES:(k + 1) * LANES]
    hb = h1.astype(BF16)
    hid = _silu(_dot(hb, wgs_ref[...])) * _dot(hb, wus_ref[...])
    base_ref[...] = ALPHA * h1 + _dot(hid.astype(BF16), wds_ref[...])

    hlo = (h1 - hb.astype(F32)).astype(BF16)
    logits = _dot_nt(wrh_ref[...], hb) + _dot_nt(wrh_ref[...], hlo) + _dot_nt(wrl_ref[...], hb)
    scores = _sigmoid(logits)
    biased = scores + rb_ref[...]
    per_group = N_EXPERTS // N_EXPERT_GROUPS
    eidx = lax.broadcasted_iota(I32, (N_EXPERTS, tm), 0)
    big = jnp.int32(1 << 20)
    gsc = []
    bidx = lax.broadcasted_iota(I32, (per_group, tm), 0)
    for gi in range(N_EXPERT_GROUPS):
        blk = biased[gi * per_group:(gi + 1) * per_group, :]
        m1 = jnp.max(blk, axis=0, keepdims=True)
        i1 = jnp.min(jnp.where(blk == m1, bidx, big), axis=0, keepdims=True)
        m2 = jnp.max(jnp.where(bidx == i1, NEG, blk), axis=0, keepdims=True)
        gsc.append(m1 + m2)
    cur = jnp.concatenate(gsc, axis=0)
    gidx = lax.broadcasted_iota(I32, (N_EXPERT_GROUPS, tm), 0)
    gsel = jnp.zeros((N_EXPERT_GROUPS, tm), F32)
    for _ in range(TOPK_GROUPS):
        m = jnp.max(cur, axis=0, keepdims=True)
        i = jnp.min(jnp.where(cur == m, gidx, big), axis=0, keepdims=True)
        hit = gidx == i
        gsel = jnp.where(hit, 1.0, gsel)
        cur = jnp.where(hit, NEG, cur)
    emask = jnp.concatenate(
        [jnp.broadcast_to(gsel[gi:gi + 1, :], (per_group, tm)) for gi in range(N_EXPERT_GROUPS)], axis=0)
    masked = jnp.where(emask > 0.0, biased, NEG)
    idx_rows, gate_rows = [], []
    for _ in range(TOP_K):
        m = jnp.max(masked, axis=0, keepdims=True)
        i = jnp.min(jnp.where(masked == m, eidx, big), axis=0, keepdims=True)
        hit = eidx == i
        idx_rows.append(i)
        gate_rows.append(jnp.sum(jnp.where(hit, scores, 0.0), axis=0, keepdims=True))
        masked = jnp.where(hit, NEG, masked)
    gates = jnp.concatenate(gate_rows, axis=0)
    gates = gates / jnp.sum(gates, axis=0, keepdims=True) * ROUTED_SCALE
    idx_ref[0] = jnp.concatenate(idx_rows, axis=0)
    gate_ref[0] = gates


def _post(x2, y_ssd, y_hgrn, wo, g1, b1, wr_hi, wr_lo, rbias, wgs, wus, wds):
    t = x2.shape[0]
    tm = TM_TOK
    nt = t // tm
    full = lambda shape: pl.BlockSpec(shape, lambda i: (0,) * len(shape))
    return pl.pallas_call(
        _post_kernel,
        grid=(nt,),
        in_specs=[pl.BlockSpec((tm, D_MODEL), lambda i: (i, 0)),
                  pl.BlockSpec((tm, D_SSM), lambda i: (i, 0)),
                  pl.BlockSpec((tm, D_HGRN), lambda i: (i, 0)),
                  full((D_MODEL, D_MODEL)), full((1, D_MODEL)), full((1, D_MODEL)),
                  full((N_EXPERTS, D_MODEL)), full((N_EXPERTS, D_MODEL)), full((N_EXPERTS, 1)),
                  full((D_MODEL, D_EXPERT)), full((D_MODEL, D_EXPERT)), full((D_EXPERT, D_MODEL))],
        out_specs=[pl.BlockSpec((tm * PACK_TILES, LANES), lambda i: (i, 0)),
                   pl.BlockSpec((tm, D_MODEL), lambda i: (i, 0)),
                   pl.BlockSpec((1, TOP_K, tm), lambda i: (i, 0, 0)),
                   pl.BlockSpec((1, TOP_K, tm), lambda i: (i, 0, 0))],
        out_shape=[jax.ShapeDtypeStruct((t * PACK_TILES, LANES), U32),
                   jax.ShapeDtypeStruct((t, D_MODEL), F32),
                   jax.ShapeDtypeStruct((nt, TOP_K, tm), I32),
                   jax.ShapeDtypeStruct((nt, TOP_K, tm), F32)],
        compiler_params=pltpu.CompilerParams(dimension_semantics=("arbitrary",),
                                             vmem_limit_bytes=48 * 1024 * 1024),
        name="post",
    )(x2, y_ssd, y_hgrn, wo, g1, b1, wr_hi, wr_lo, rbias, wgs, wus, wds)


def _max_blocks(t):
    return (t * TOP_K + N_EXPERTS * (EXPERT_ROWS - 1)) // EXPERT_ROWS


def _pos_kernel(idx_ref, dest_ref, bstart_ref, nblk_ref, cnt_ref, start_ref):
    phase = pl.program_id(0)
    i = pl.program_id(1)
    tm = idx_ref.shape[-1]
    idx = idx_ref[0]
    eidx = lax.broadcasted_iota(I32, (N_EXPERTS, tm), 0)
    sel = [eidx == idx[k:k + 1, :] for k in range(TOP_K)]
    onehot = sel[0]
    for k in range(1, TOP_K):
        onehot = jnp.logical_or(onehot, sel[k])
    mt = jnp.where(onehot, 1.0, 0.0).astype(BF16)
    tile_cnt = _dot(mt, jnp.ones((tm, LANES), BF16))

    @pl.when(jnp.logical_and(phase == 0, i == 0))
    def _():
        cnt_ref[...] = jnp.zeros(cnt_ref.shape, F32)

    @pl.when(phase == 0)
    def _():
        cnt_ref[...] += tile_cnt

    @pl.when(jnp.logical_and(phase == 1, i == 0))
    def _():
        nb = jnp.floor((cnt_ref[...] + (EXPERT_ROWS - 1)) * (1.0 / EXPERT_ROWS))
        r = lax.broadcasted_iota(I32, (N_EXPERTS, N_EXPERTS), 0)
        c = lax.broadcasted_iota(I32, (N_EXPERTS, N_EXPERTS), 1)
        end = _dot((r >= c).astype(BF16), nb.astype(BF16))
        start_ref[...] = (end - nb) * EXPERT_ROWS
        cnt_ref[...] = jnp.zeros(cnt_ref.shape, F32)
        bstart_ref[...] = (end - nb).astype(I32)
        nblk_ref[...] = nb.astype(I32)

    @pl.when(phase == 1)
    def _():
        r = lax.broadcasted_iota(I32, (tm, tm), 0)
        c = lax.broadcasted_iota(I32, (tm, tm), 1)
        before = _dot(mt, (r < c).astype(BF16))
        slot = start_ref[:, 0:1] + cnt_ref[:, 0:1] + before
        rows = [jnp.sum(jnp.where(sel[k], slot, 0.0), axis=0, keepdims=True) for k in range(TOP_K)]
        dest_ref[0] = jnp.concatenate(rows, axis=0).astype(I32)
        cnt_ref[...] += tile_cnt


def _positions(idx):
    nt, _, tm = idx.shape
    return pl.pallas_call(
        _pos_kernel,
        grid=(2, nt),
        in_specs=[pl.BlockSpec((1, TOP_K, tm), lambda p, i: (i, 0, 0))],
        out_specs=[pl.BlockSpec((1, TOP_K, tm), lambda p, i: (i * p, 0, 0)),
                   pl.BlockSpec((N_EXPERTS, LANES), lambda p, i: (0, 0)),
                   pl.BlockSpec((N_EXPERTS, LANES), lambda p, i: (0, 0))],
        out_shape=[jax.ShapeDtypeStruct((nt, TOP_K, tm), I32),
                   jax.ShapeDtypeStruct((N_EXPERTS, LANES), I32),
                   jax.ShapeDtypeStruct((N_EXPERTS, LANES), I32)],
        scratch_shapes=[pltpu.VMEM((N_EXPERTS, LANES), F32), pltpu.VMEM((N_EXPERTS, LANES), F32)],
        compiler_params=pltpu.CompilerParams(dimension_semantics=("arbitrary", "arbitrary")),
        name="positions",
    )(idx)


def _row(ref, r, tiles):
    return ref.at[pl.ds(pl.multiple_of(r * tiles, tiles), tiles)]


def _rows_wait(src, dst, sem, n, tiles):
    pltpu.make_async_copy(src.at[pl.ds(0, n * tiles)], dst.at[pl.ds(0, n * tiles)], sem).wait()


def _dispatch_kernel(dest_ref, h_ref, xs_ref, sem):
    tm = h_ref.shape[0] // PACK_TILES

    def body(t, carry):
        for k in range(TOP_K):
            d = dest_ref[0, k, t]
            pltpu.make_async_copy(_row(h_ref, t, PACK_TILES), _row(xs_ref, d, PACK_TILES), sem).start(
                priority=k % 2)
        return carry

    lax.fori_loop(0, tm, body, 0)
    for _ in range(TOP_K):
        _rows_wait(h_ref, xs_ref, sem, tm, PACK_TILES)


def _dispatch(dest, htt, n_rows):
    t = htt.shape[0] // PACK_TILES
    tm = dest.shape[-1]
    return pl.pallas_call(
        _dispatch_kernel,
        grid=(t // tm,),
        in_specs=[pl.BlockSpec((1, TOP_K, tm), lambda i: (i, 0, 0), memory_space=pltpu.SMEM),
                  pl.BlockSpec((tm * PACK_TILES, LANES), lambda i: (i, 0))],
        out_specs=pl.BlockSpec(memory_space=pl.ANY),
        out_shape=jax.ShapeDtypeStruct((n_rows * PACK_TILES, LANES), U32),
        scratch_shapes=[pltpu.SemaphoreType.DMA(())],
        compiler_params=pltpu.CompilerParams(dimension_semantics=("arbitrary",), has_side_effects=True),
        name="dispatch",
    )(dest, htt)


def _expert_kernel(bstart_ref, nblk_ref, xs_hbm, wg_ref, wu_ref, wd_ref, ys_hbm,
                   xbuf, ybuf, wgb_ref, wub_ref, wdb_ref, xsem, ysem, ypend):
    e = pl.program_id(0)
    last = pl.num_programs(0) - 1
    n = nblk_ref[e]
    b0 = bstart_ref[e]
    total = bstart_ref[last] + nblk_ref[last]
    blk_words = EXPERT_ROWS * PACK_TILES
    xdepth = xbuf.shape[0]
    ydepth = ybuf.shape[0]

    def hbm_block(ref, b):
        return ref.at[pl.ds(pl.multiple_of(b * blk_words, blk_words), blk_words)]

    def x_copy(b):
        slot = b % xdepth
        return pltpu.make_async_copy(hbm_block(xs_hbm, b), xbuf.at[slot], xsem.at[slot])

    def y_copy(b, slot):
        return pltpu.make_async_copy(ybuf.at[slot], hbm_block(ys_hbm, b), ysem.at[slot])

    @pl.when(e == 0)
    def _():
        for s in range(ydepth):
            ypend[s] = 0
        for b in range(xdepth - 1):
            @pl.when(b < total)
            def _():
                x_copy(b).start()

    @pl.when(n > 0)
    def _():
        wgb_ref[...] = wg_ref[0].astype(BF16)
        wub_ref[...] = wu_ref[0].astype(BF16)
        wdb_ref[...] = wd_ref[0].astype(BF16)

        def block(j, carry):
            b = b0 + j
            x_copy(b).wait()
            ahead = b + (xdepth - 1)

            @pl.when(ahead < total)
            def _():
                x_copy(ahead).start()

            xslot = b % xdepth
            yslot = b % ydepth

            @pl.when(ypend[yslot] == 1)
            def _():
                y_copy(b, yslot).wait()

            halves = [_unpack_pairs(xbuf[xslot, pl.ds(k, EXPERT_ROWS, stride=PACK_TILES), :])
                      for k in range(PACK_TILES)]
            xb = jnp.concatenate([lo for lo, _ in halves] + [hi for _, hi in halves], axis=-1).astype(BF16)
            hid = _silu(_dot(xb, wgb_ref[...])) * _dot(xb, wub_ref[...])
            yb = _dot(hid.astype(BF16), wdb_ref[...])
            packed = _pack_pairs(yb[:, :D_MODEL // 2], yb[:, D_MODEL // 2:])
            for k in range(PACK_TILES):
                ybuf[yslot, pl.ds(k, EXPERT_ROWS, stride=PACK_TILES), :] = packed[:, k * LANES:(k + 1) * LANES]
            y_copy(b, yslot).start()
            ypend[yslot] = 1
            return carry

        lax.fori_loop(0, n, block, 0)

    @pl.when(e == last)
    def _():
        for s in range(ydepth):
            @pl.when(ypend[s] == 1)
            def _():
                y_copy(b0, s).wait()


def _experts(bstart, nblk, xs, w_gate, w_up, w_down):
    blk_words = EXPERT_ROWS * PACK_TILES
    wspec = lambda shape: pl.BlockSpec((1,) + shape, lambda e, bs, nb: (e, 0, 0))
    grid_spec = pltpu.PrefetchScalarGridSpec(
        num_scalar_prefetch=2,
        grid=(N_EXPERTS,),
        in_specs=[pl.BlockSpec(memory_space=pl.ANY),
                  wspec((D_MODEL, D_EXPERT)), wspec((D_MODEL, D_EXPERT)), wspec((D_EXPERT, D_MODEL))],
        out_specs=pl.BlockSpec(memory_space=pl.ANY),
        scratch_shapes=[pltpu.VMEM((X_RING, blk_words, LANES), U32), pltpu.VMEM((Y_RING, blk_words, LANES), U32),
                        pltpu.VMEM((D_MODEL, D_EXPERT), BF16), pltpu.VMEM((D_MODEL, D_EXPERT), BF16),
                        pltpu.VMEM((D_EXPERT, D_MODEL), BF16),
                        pltpu.SemaphoreType.DMA((X_RING,)), pltpu.SemaphoreType.DMA((Y_RING,)),
                        pltpu.SMEM((Y_RING,), I32)],
    )
    return pl.pallas_call(
        _expert_kernel,
        grid_spec=grid_spec,
        out_shape=jax.ShapeDtypeStruct(xs.shape, U32),
        compiler_params=pltpu.CompilerParams(dimension_semantics=("arbitrary",), has_side_effects=True),
        name="experts",
    )(bstart, nblk, xs, w_gate, w_up, w_down)


def _combine_kernel(dest_ref, gate_ref, base_ref, g2_ref, b2_ref, ys_ref, o_ref, rows_ref, sem):
    tm = base_ref.shape[0]

    def body(t, carry):
        for k in range(TOP_K):
            d = dest_ref[0, k, t]
            pltpu.make_async_copy(_row(ys_ref, d, PACK_TILES), _row(rows_ref, k * tm + t, PACK_TILES), sem).start(
                priority=k % 2)
        return carry

    lax.fori_loop(0, tm, body, 0)
    gates_t = jnp.concatenate([gate_ref[0], jnp.zeros((LANES - TOP_K, tm), F32)], axis=0).T
    for k in range(TOP_K):
        _rows_wait(ys_ref, rows_ref, sem, tm, PACK_TILES)
    half = D_MODEL // 2
    lo_cols, hi_cols = [], []
    for c in range(PACK_TILES):
        lo_acc = base_ref[:, c * LANES:(c + 1) * LANES]
        hi_acc = base_ref[:, half + c * LANES:half + (c + 1) * LANES]
        for k in range(TOP_K):
            lo, hi = _unpack_pairs(rows_ref[pl.ds(k * tm * PACK_TILES + c, tm, stride=PACK_TILES), :])
            lo_acc = lo_acc + gates_t[:, k:k + 1] * lo
            hi_acc = hi_acc + gates_t[:, k:k + 1] * hi
        lo_cols.append(lo_acc)
        hi_cols.append(hi_acc)
    o_ref[...] = _layer_norm(jnp.concatenate(lo_cols + hi_cols, axis=-1), g2_ref[...], b2_ref[...])


def _combine(dest, gates, base, g2, b2, ys):
    t = base.shape[0]
    tm = dest.shape[-1]
    return pl.pallas_call(
        _combine_kernel,
        grid=(t // tm,),
        in_specs=[pl.BlockSpec((1, TOP_K, tm), lambda i: (i, 0, 0), memory_space=pltpu.SMEM),
                  pl.BlockSpec((1, TOP_K, tm), lambda i: (i, 0, 0)),
                  pl.BlockSpec((tm, D_MODEL), lambda i: (i, 0)),
                  pl.BlockSpec((1, D_MODEL), lambda i: (0, 0)),
                  pl.BlockSpec((1, D_MODEL), lambda i: (0, 0)),
                  pl.BlockSpec(memory_space=pl.ANY)],
        out_specs=pl.BlockSpec((tm, D_MODEL), lambda i: (i, 0)),
        out_shape=jax.ShapeDtypeStruct((t, D_MODEL), F32),
        scratch_shapes=[pltpu.VMEM((TOP_K * tm * PACK_TILES, LANES), U32), pltpu.SemaphoreType.DMA(())],
        compiler_params=pltpu.CompilerParams(dimension_semantics=("arbitrary",),
                                             vmem_limit_bytes=48 * 1024 * 1024),
        name="combine",
    )(dest, gates, base, g2, b2, ys)


def _layer(h2, w_in, conv_w, conv_b, dt_bias, a_log, d_skip, ssd_norm_w, lower_bound, hgrn_norm_w, w_out,
           ln1_g, ln1_b, w_router, router_bias, w_gate_e, w_up_e, w_down_e, w_gate_s, w_up_s, w_down_s,
           ln2_g, ln2_b):
    t = h2.shape[0]
    dt0 = D_SSM + D_CONV
    q0 = dt0 + SSD_HEADS
    w_perm = jnp.concatenate(
        [w_in[:, :dt0], w_in[:, q0:], w_in[:, dt0:q0], jnp.zeros((D_MODEL, DT_PAD - SSD_HEADS), w_in.dtype)],
        axis=1).astype(BF16)
    z, xbc, q, f, i, g, dt = _inproj(h2, w_perm)
    y_ssd = _ssd(z, xbc, dt, conv_w, conv_b, dt_bias, a_log, d_skip, ssd_norm_w)
    y_hgrn = _hgrn(q, f, i, g, lower_bound, hgrn_norm_w)

    wr_t = w_router.astype(F32).T
    wr_hi = wr_t.astype(BF16)
    wr_lo = (wr_t - wr_hi.astype(F32)).astype(BF16)
    row = lambda v: v.reshape(1, -1).astype(F32)
    htt, base, idx, gates = _post(
        h2, y_ssd, y_hgrn, w_out.astype(BF16), row(ln1_g), row(ln1_b), wr_hi, wr_lo,
        router_bias.reshape(N_EXPERTS, 1).astype(F32),
        w_gate_s.astype(BF16), w_up_s.astype(BF16), w_down_s.astype(BF16))

    dest, bstart, nblk = _positions(idx)
    n_rows = _max_blocks(t) * EXPERT_ROWS
    xs = _dispatch(dest, htt, n_rows)
    ys = _experts(bstart[:, 0], nblk[:, 0], xs, w_gate_e, w_up_e, w_down_e)
    return _combine(dest, gates, base, row(ln2_g), row(ln2_b), ys)


def kernel(x, w_in, conv_w, conv_b, dt_bias, a_log, d_skip, ssd_norm_w, hgrn_lb_logits, hgrn_norm_w, w_out,
           ln1_g, ln1_b, w_router, router_bias, w_gate_e, w_up_e, w_down_e, w_gate_s, w_up_s, w_down_s,
           ln2_g, ln2_b):
    bsz, t, d = x.shape
    assert bsz == 1 and d == D_MODEL, "the recurrent mixers carry state across the flattened token axis"
    depth = w_in.shape[0]
    lower_bounds = jnp.cumsum(jax.nn.softmax(hgrn_lb_logits.astype(F32), axis=0), axis=0)
    h = x.reshape(bsz * t, d)
    for l in range(depth):
        h = _layer(h, w_in[l], conv_w[l], conv_b[l], dt_bias[l], a_log[l], d_skip[l], ssd_norm_w[l],
                   lower_bounds[l], hgrn_norm_w[l], w_out[l], ln1_g[l], ln1_b[l], w_router[l],
                   router_bias[l], w_gate_e[l], w_up_e[l], w_down_e[l], w_gate_s[l], w_up_s[l],
                   w_down_s[l], ln2_g[l], ln2_b[l])
    return h.reshape(bsz, t, d)
```

```python
import jax
import jax.numpy as jnp
from jax import lax
from jax.experimental import pallas as pl
from jax.experimental.pallas import tpu as pltpu

F32 = jnp.float32
BF16 = jnp.bfloat16
I32 = jnp.int32
U32 = jnp.uint32

D_MODEL = 1024
D_SSM = 512
D_HGRN = 512
SSD_HEADS = 8
SSD_HEAD_DIM = 64
SSD_GROUPS = 2
SSD_STATE = 128
SSD_CONV = 4
SSD_CHUNK = 128
D_CONV = D_SSM + 2 * SSD_GROUPS * SSD_STATE
HGRN_HEADS = 4
HGRN_DK = 128
HGRN_CHUNK = 64
HGRN_SUB = 8
N_EXPERTS = 256
TOP_K = 8
N_EXPERT_GROUPS = 8
TOPK_GROUPS = 4
D_EXPERT = 256
ROUTED_SCALE = 2.5
ALPHA = 2.0 ** 0.25
LN_EPS = 1e-5
RMS_EPS = 1e-6

LANES = 128
SUBLANES = 8
ROW_TILES = D_MODEL // LANES
PACK_TILES = ROW_TILES // 2
DT_PAD = LANES
N_IN_PAD = D_SSM + D_CONV + 4 * D_HGRN + DT_PAD

TM_PROJ = 256
TM_TOK = 256
EXPERT_ROWS = 256
X_RING = 4
Y_RING = 2
NEG = -1e30


def _sigmoid(x):
    return 1.0 / (1.0 + jnp.exp(-x))


def _silu(x):
    return x * _sigmoid(x)


def _split3(x):
    hi = x.astype(BF16)
    r = x - hi.astype(F32)
    mid = r.astype(BF16)
    lo = (r - mid.astype(F32)).astype(BF16)
    return hi, mid, lo


def _dot(a, b):
    return jnp.dot(a, b, preferred_element_type=F32)


def _dot_nt(a, b):
    return lax.dot_general(a, b, (((1,), (1,)), ((), ())), preferred_element_type=F32)


def _dot_tn(a, b):
    return lax.dot_general(a, b, (((0,), (0,)), ((), ())), preferred_element_type=F32)


def _sel_dot(sel, x):
    hi, mid, lo = _split3(x)
    return _dot(sel, hi) + _dot(sel, mid) + _dot(sel, lo)


def _dot_sel(x, sel):
    hi, mid, lo = _split3(x)
    return _dot(hi, sel) + _dot(mid, sel) + _dot(lo, sel)


def _pack_pairs(lo, hi):
    lo_bits = pltpu.bitcast(lo.astype(BF16).astype(F32), U32) >> 16
    hi_bits = pltpu.bitcast(hi.astype(BF16).astype(F32), U32) & jnp.uint32(0xFFFF0000)
    return hi_bits | lo_bits


def _unpack_pairs(p):
    return pltpu.bitcast(p << 16, F32), pltpu.bitcast(p & jnp.uint32(0xFFFF0000), F32)


def _layer_norm(x, g, b):
    mu = jnp.mean(x, axis=-1, keepdims=True)
    xc = x - mu
    var = jnp.mean(xc * xc, axis=-1, keepdims=True)
    return xc * lax.rsqrt(var + LN_EPS) * g + b


def _inproj_kernel(x_ref, w_ref, z_ref, xbc_ref, q_ref, f_ref, i_ref, g_ref, dt_ref):
    xb = x_ref[...].astype(BF16)
    col = 0
    for ref in (z_ref, xbc_ref, q_ref, f_ref, i_ref, g_ref, dt_ref):
        n = ref.shape[-1]
        ref[...] = _dot(xb, w_ref[:, col:col + n])
        col += n


def _inproj(x2, w_perm):
    t = x2.shape[0]
    widths = (D_SSM, D_CONV, D_HGRN, D_HGRN, D_HGRN, D_HGRN, DT_PAD)
    return pl.pallas_call(
        _inproj_kernel,
        grid=(t // TM_PROJ,),
        in_specs=[pl.BlockSpec((TM_PROJ, D_MODEL), lambda i: (i, 0)),
                  pl.BlockSpec((D_MODEL, N_IN_PAD), lambda i: (0, 0))],
        out_specs=[pl.BlockSpec((TM_PROJ, n), lambda i: (i, 0)) for n in widths],
        out_shape=[jax.ShapeDtypeStruct((t, n), F32) for n in widths],
        compiler_params=pltpu.CompilerParams(dimension_semantics=("arbitrary",),
                                             vmem_limit_bytes=48 * 1024 * 1024),
        name="inproj",
    )(x2, w_perm)


def _ssd_kernel(z_ref, xbc_ref, dt_ref, cw_ref, cb_ref, dtb_ref, alog_ref, dskip_ref, nw_ref,
                y_ref, ext_ref, st_ref):
    L = SSD_CHUNK
    halo = SUBLANES

    @pl.when(pl.program_id(0) == 0)
    def _():
        ext_ref[0:halo, :] = jnp.zeros((halo, D_CONV), F32)
        st_ref[...] = jnp.zeros(st_ref.shape, F32)

    ext_ref[halo:halo + L, :] = xbc_ref[...]
    acc = jnp.broadcast_to(cb_ref[...], (L, D_CONV))
    for k in range(SSD_CONV):
        off = halo - (SSD_CONV - 1) + k
        acc = acc + cw_ref[k:k + 1, :] * ext_ref[off:off + L, :]
    ext_ref[0:halo, :] = xbc_ref[L - halo:L, :]
    u = _silu(acc)
    xs = u[:, :D_SSM]
    bm = u[:, D_SSM:D_SSM + SSD_GROUPS * SSD_STATE]
    cm = u[:, D_SSM + SSD_GROUPS * SSD_STATE:]

    draw = dt_ref[...] + dtb_ref[...]
    dt = jnp.maximum(draw, 0.0) + jnp.log(1.0 + jnp.exp(-jnp.abs(draw)))
    ad = dt * (-jnp.exp(alog_ref[...]))
    rows = lax.broadcasted_iota(I32, (L, L), 0)
    cols = lax.broadcasted_iota(I32, (L, L), 1)
    causal = rows >= cols
    a_cum = _sel_dot(causal.astype(BF16), ad)
    a_cum_t = a_cum.T

    hrow = lax.broadcasted_iota(I32, (LANES, D_SSM), 0)
    hcol = lax.broadcasted_iota(I32, (LANES, D_SSM), 1) // SSD_HEAD_DIM
    expand = (hrow == hcol).astype(BF16)
    dt_x = _dot_sel(dt, expand)
    acx = _dot_sel(a_cum, expand)
    last = acx[L - 1:L, :]
    ea_x = jnp.exp(acx)
    dec_x = jnp.exp(last - acx)
    elast_x = jnp.exp(last)

    xdt = xs * dt_x
    gw = SSD_HEADS // SSD_GROUPS * SSD_HEAD_DIM
    lane_head = lax.broadcasted_iota(I32, (L, gw), 1) // SSD_HEAD_DIM
    ys = []
    for g in range(SSD_GROUPS):
        bg = bm[:, g * SSD_STATE:(g + 1) * SSD_STATE]
        cg = cm[:, g * SSD_STATE:(g + 1) * SSD_STATE].astype(BF16)
        bg_t = bg.T.astype(BF16)
        gmat = _dot(cg, bg_t)
        xdt_g = xdt[:, g * gw:(g + 1) * gw]
        xdt_gb = xdt_g.astype(BF16)
        r_prev = st_ref[g]
        y_g = _dot(cg, r_prev.astype(BF16)) * ea_x[:, g * gw:(g + 1) * gw]
        new_s = _dot(bg_t, (xdt_g * dec_x[:, g * gw:(g + 1) * gw]).astype(BF16))
        st_ref[g] = r_prev * elast_x[:, g * gw:(g + 1) * gw] + new_s
        for j in range(SSD_HEADS // SSD_GROUPS):
            h = g * (SSD_HEADS // SSD_GROUPS) + j
            diff = a_cum[:, h:h + 1] - a_cum_t[h:h + 1, :]
            decay = jnp.exp(jnp.where(causal, diff, NEG))
            yd = _dot((gmat * decay).astype(BF16), xdt_gb)
            y_g = y_g + jnp.where(lane_head == j, yd, 0.0)
        ys.append(y_g)
    y = jnp.concatenate(ys, axis=-1) + xs * dskip_ref[...]
    y = y * _silu(z_ref[...])
    outs = []
    ng = D_SSM // SSD_GROUPS
    for g in range(SSD_GROUPS):
        yg = y[:, g * ng:(g + 1) * ng]
        ms = jnp.mean(yg * yg, axis=-1, keepdims=True)
        outs.append(yg * lax.rsqrt(ms + RMS_EPS))
    y_ref[...] = jnp.concatenate(outs, axis=-1) * nw_ref[...]


def _ssd(z, xbc, dt, conv_w, conv_b, dt_bias, a_log, d_skip, norm_w):
    t = z.shape[0]
    L = SSD_CHUNK
    pad = lambda v: jnp.pad(v.astype(F32), (0, LANES - v.shape[0])).reshape(1, LANES)
    full = lambda shape: pl.BlockSpec(shape, lambda c: (0,) * len(shape))
    return pl.pallas_call(
        _ssd_kernel,
        grid=(t // L,),
        in_specs=[pl.BlockSpec((L, D_SSM), lambda c: (c, 0)),
                  pl.BlockSpec((L, D_CONV), lambda c: (c, 0)),
                  pl.BlockSpec((L, DT_PAD), lambda c: (c, 0)),
                  full((SSD_CONV, D_CONV)), full((1, D_CONV)), full((1, LANES)), full((1, LANES)),
                  full((1, D_SSM)), full((1, D_SSM))],
        out_specs=pl.BlockSpec((L, D_SSM), lambda c: (c, 0)),
        out_shape=jax.ShapeDtypeStruct((t, D_SSM), F32),
        scratch_shapes=[pltpu.VMEM((L + SUBLANES, D_CONV), F32),
                        pltpu.VMEM((SSD_GROUPS, SSD_STATE, D_SSM // SSD_GROUPS), F32)],
        compiler_params=pltpu.CompilerParams(dimension_semantics=("arbitrary",)),
        name="ssd",
    )(z, xbc, dt, conv_w.astype(F32), conv_b.reshape(1, D_CONV).astype(F32), pad(dt_bias), pad(a_log),
      jnp.repeat(d_skip.astype(F32), SSD_HEAD_DIM).reshape(1, D_SSM), norm_w.reshape(1, D_SSM).astype(F32))


def _tile_bcast(x, r):
    n, d = x.shape
    x3 = x.reshape(n // SUBLANES, SUBLANES, d)
    return jnp.broadcast_to(x3[:, r:r + 1, :], x3.shape).reshape(n, d)


def _hgrn_kernel(q_ref, f_ref, i_ref, g_ref, lb_ref, nw_ref, o_ref, st_ref):
    C = HGRN_CHUNK
    S = HGRN_SUB
    nsub = C // S

    @pl.when(pl.program_id(0) == 0)
    def _():
        st_ref[...] = jnp.zeros(st_ref.shape, F32)

    lb = lb_ref[...]
    fg = lb + (1.0 - lb) * _sigmoid(f_ref[...])
    kk = 1.0 - fg
    qs = _silu(q_ref[...])
    v = i_ref[...]
    rows = lax.broadcasted_iota(I32, (C, C), 0)
    cols = lax.broadcasted_iota(I32, (C, C), 1)
    cum = _sel_dot((rows >= cols).astype(BF16), jnp.log(fg))
    last = cum[C - 1:C, :]

    trow = lax.broadcasted_iota(I32, (C, D_HGRN), 0)
    tmod = trow % S
    tblk = trow // S

    xr = []
    for r in range(S):
        e = jnp.exp(jnp.where(tmod >= r, cum - _tile_bcast(cum, r), NEG))
        xr.append(qs * _tile_bcast(kk, r) * e)
    kbe = kk * jnp.exp(_tile_bcast(cum, S - 1) - cum)
    aj, bj = [], []
    for j in range(nsub - 1):
        ce = cum[j * S + S - 1:j * S + S, :]
        aj.append(qs * jnp.exp(jnp.where(tblk > j, cum - ce, NEG)))
        bj.append(jnp.where(tblk == j, kbe, 0.0))
    qd = (qs * jnp.exp(cum)).astype(BF16)
    kd = (kk * jnp.exp(last - cum)).astype(BF16)
    elast = jnp.exp(last)

    att_col = lax.broadcasted_iota(I32, (C, C), 1)
    att_rblk = lax.broadcasted_iota(I32, (C, C), 0) // S * S
    outs = []
    for h in range(HGRN_HEADS):
        blk = slice(h * HGRN_DK, (h + 1) * HGRN_DK)
        a_cat = jnp.concatenate([a[:, blk] for a in aj], axis=-1).astype(BF16)
        b_cat = jnp.concatenate([b[:, blk] for b in bj], axis=-1).astype(BF16)
        att = _dot_nt(a_cat, b_cat)
        for r in range(S):
            rs = jnp.sum(xr[r][:, blk], axis=-1, keepdims=True)
            att = jnp.where(att_col == att_rblk + r, rs, att)
        vb = v[:, blk].astype(BF16)
        s_prev = st_ref[h]
        o = _dot_nt(qd[:, blk], s_prev.astype(BF16)) + _dot(att.astype(BF16), vb)
        st_ref[h] = s_prev * elast[:, blk] + _dot_tn(vb, kd[:, blk])
        ms = jnp.mean(o * o, axis=-1, keepdims=True)
        outs.append(o * lax.rsqrt(ms + RMS_EPS))
    o_ref[...] = jnp.concatenate(outs, axis=-1) * nw_ref[...] * _silu(g_ref[...])


def _hgrn(q, f, i, g, lower_bound, norm_w):
    t = q.shape[0]
    C = HGRN_CHUNK
    tok = pl.BlockSpec((C, D_HGRN), lambda c: (c, 0))
    vec = pl.BlockSpec((1, D_HGRN), lambda c: (0, 0))
    return pl.pallas_call(
        _hgrn_kernel,
        grid=(t // C,),
        in_specs=[tok, tok, tok, tok, vec, vec],
        out_specs=tok,
        out_shape=jax.ShapeDtypeStruct((t, D_HGRN), F32),
        scratch_shapes=[pltpu.VMEM((HGRN_HEADS, D_HGRN // HGRN_HEADS, HGRN_DK), F32)],
        compiler_params=pltpu.CompilerParams(dimension_semantics=("arbitrary",)),
        name="hgrn",
    )(q, f, i, g, lower_bound.reshape(1, D_HGRN).astype(F32), norm_w.reshape(1, D_HGRN).astype(F32))


def _post_kernel(x_ref, ys_ref, yh_ref, wo_ref, g1_ref, b1_ref, wrh_ref, wrl_ref, rb_ref,
                 wgs_ref, wus_ref, wds_ref, htt_ref, base_ref, idx_ref, gate_ref):
    tm = x_ref.shape[0]
    mix = (_dot(ys_ref[...].astype(BF16), wo_ref[0:D_SSM, :])
           + _dot(yh_ref[...].astype(BF16), wo_ref[D_SSM:, :]))
    h1 = _layer_norm(ALPHA * x_ref[...] + mix, g1_ref[...], b1_ref[...])
    packed = _pack_pairs(h1[:, :D_MODEL // 2], h1[:, D_MODEL // 2:])
    for k in range(PACK_TILES):
        htt_ref[pl.ds(k, tm, stride=PACK_TILES), :] = packed[:, k * LANES:(k + 1) * LANES]
    hb = h1.astype(BF16)
    hid = _silu(_dot(hb, wgs_ref[...])) * _dot(hb, wus_ref[...])
    base_ref[...] = ALPHA * h1 + _dot(hid.astype(BF16), wds_ref[...])

    hlo = (h1 - hb.astype(F32)).astype(BF16)
    logits = _dot_nt(wrh_ref[...], hb) + _dot_nt(wrh_ref[...], hlo) + _dot_nt(wrl_ref[...], hb)
    scores = _sigmoid(logits)
    biased = scores + rb_ref[...]
    per_group = N_EXPERTS // N_EXPERT_GROUPS
    eidx = lax.broadcasted_iota(I32, (N_EXPERTS, tm), 0)
    big = jnp.int32(1 << 20)
    gsc = []
    bidx = lax.broadcasted_iota(I32, (per_group, tm), 0)
    for gi in range(N_EXPERT_GROUPS):
        blk = biased[gi * per_group:(gi + 1) * per_group, :]
        m1 = jnp.max(blk, axis=0, keepdims=True)
        i1 = jnp.min(jnp.where(blk == m1, bidx, big), axis=0, keepdims=True)
        m2 = jnp.max(jnp.where(bidx == i1, NEG, blk), axis=0, keepdims=True)
        gsc.append(m1 + m2)
    cur = jnp.concatenate(gsc, axis=0)
    gidx = lax.broadcasted_iota(I32, (N_EXPERT_GROUPS, tm), 0)
    gsel = jnp.zeros((N_EXPERT_GROUPS, tm), F32)
    for _ in range(TOPK_GROUPS):
        m = jnp.max(cur, axis=0, keepdims=True)
        i = jnp.min(jnp.where(cur == m, gidx, big), axis=0, keepdims=True)
        hit = gidx == i
        gsel = jnp.where(hit, 1.0, gsel)
        cur = jnp.where(hit, NEG, cur)
    emask = jnp.concatenate(
        [jnp.broadcast_to(gsel[gi:gi + 1, :], (per_group, tm)) for gi in range(N_EXPERT_GROUPS)], axis=0)
    masked = jnp.where(emask > 0.0, biased, NEG)
    idx_rows, gate_rows = [], []
    for _ in range(TOP_K):
        m = jnp.max(masked, axis=0, keepdims=True)
        i = jnp.min(jnp.where(masked == m, eidx, big), axis=0, keepdims=True)
        hit = eidx == i
        idx_rows.append(i)
        gate_rows.append(jnp.sum(jnp.where(hit, scores, 0.0), axis=0, keepdims=True))
        masked = jnp.where(hit, NEG, masked)
    gates = jnp.concatenate(gate_rows, axis=0)
    gates = gates / jnp.sum(gates, axis=0, keepdims=True) * ROUTED_SCALE
    idx_ref[0] = jnp.concatenate(idx_rows, axis=0)
    gate_ref[0] = gates


def _post(x2, y_ssd, y_hgrn, wo, g1, b1, wr_hi, wr_lo, rbias, wgs, wus, wds):
    t = x2.shape[0]
    tm = TM_TOK
    nt = t // tm
    full = lambda shape: pl.BlockSpec(shape, lambda i: (0,) * len(shape))
    return pl.pallas_call(
        _post_kernel,
        grid=(nt,),
        in_specs=[pl.BlockSpec((tm, D_MODEL), lambda i: (i, 0)),
                  pl.BlockSpec((tm, D_SSM), lambda i: (i, 0)),
                  pl.BlockSpec((tm, D_HGRN), lambda i: (i, 0)),
                  full((D_MODEL, D_MODEL)), full((1, D_MODEL)), full((1, D_MODEL)),
                  full((N_EXPERTS, D_MODEL)), full((N_EXPERTS, D_MODEL)), full((N_EXPERTS, 1)),
                  full((D_MODEL, D_EXPERT)), full((D_MODEL, D_EXPERT)), full((D_EXPERT, D_MODEL))],
        out_specs=[pl.BlockSpec((tm * PACK_TILES, LANES), lambda i: (i, 0)),
                   pl.BlockSpec((tm, D_MODEL), lambda i: (i, 0)),
                   pl.BlockSpec((1, TOP_K, tm), lambda i: (i, 0, 0)),
                   pl.BlockSpec((1, TOP_K, tm), lambda i: (i, 0, 0))],
        out_shape=[jax.ShapeDtypeStruct((t * PACK_TILES, LANES), U32),
                   jax.ShapeDtypeStruct((t, D_MODEL), F32),
                   jax.ShapeDtypeStruct((nt, TOP_K, tm), I32),
                   jax.ShapeDtypeStruct((nt, TOP_K, tm), F32)],
        compiler_params=pltpu.CompilerParams(dimension_semantics=("arbitrary",),
                                             vmem_limit_bytes=48 * 1024 * 1024),
        name="post",
    )(x2, y_ssd, y_hgrn, wo, g1, b1, wr_hi, wr_lo, rbias, wgs, wus, wds)


def _max_blocks(t):
    return (t * TOP_K + N_EXPERTS * (EXPERT_ROWS - 1)) // EXPERT_ROWS


def _pos_kernel(idx_ref, dest_ref, bstart_ref, nblk_ref, cnt_ref, start_ref):
    phase = pl.program_id(0)
    i = pl.program_id(1)
    tm = idx_ref.shape[-1]
    idx = idx_ref[0]
    eidx = lax.broadcasted_iota(I32, (N_EXPERTS, tm), 0)
    sel = [eidx == idx[k:k + 1, :] for k in range(TOP_K)]
    onehot = sel[0]
    for k in range(1, TOP_K):
        onehot = jnp.logical_or(onehot, sel[k])
    mt = jnp.where(onehot, 1.0, 0.0).astype(BF16)
    tile_cnt = _dot(mt, jnp.ones((tm, LANES), BF16))

    @pl.when(jnp.logical_and(phase == 0, i == 0))
    def _():
        cnt_ref[...] = jnp.zeros(cnt_ref.shape, F32)

    @pl.when(phase == 0)
    def _():
        cnt_ref[...] += tile_cnt

    @pl.when(jnp.logical_and(phase == 1, i == 0))
    def _():
        nb = jnp.floor((cnt_ref[...] + (EXPERT_ROWS - 1)) * (1.0 / EXPERT_ROWS))
        r = lax.broadcasted_iota(I32, (N_EXPERTS, N_EXPERTS), 0)
        c = lax.broadcasted_iota(I32, (N_EXPERTS, N_EXPERTS), 1)
        end = _dot((r >= c).astype(BF16), nb.astype(BF16))
        start_ref[...] = (end - nb) * EXPERT_ROWS
        cnt_ref[...] = jnp.zeros(cnt_ref.shape, F32)
        bstart_ref[...] = (end - nb).astype(I32)
        nblk_ref[...] = nb.astype(I32)

    @pl.when(phase == 1)
    def _():
        r = lax.broadcasted_iota(I32, (tm, tm), 0)
        c = lax.broadcasted_iota(I32, (tm, tm), 1)
        before = _dot(mt, (r < c).astype(BF16))
        slot = start_ref[:, 0:1] + cnt_ref[:, 0:1] + before
        rows = [jnp.sum(jnp.where(sel[k], slot, 0.0), axis=0, keepdims=True) for k in range(TOP_K)]
        dest_ref[0] = jnp.concatenate(rows, axis=0).astype(I32)
        cnt_ref[...] += tile_cnt


def _positions(idx):
    nt, _, tm = idx.shape
    return pl.pallas_call(
        _pos_kernel,
        grid=(2, nt),
        in_specs=[pl.BlockSpec((1, TOP_K, tm), lambda p, i: (i, 0, 0))],
        out_specs=[pl.BlockSpec((1, TOP_K, tm), lambda p, i: (i * p, 0, 0)),
                   pl.BlockSpec((N_EXPERTS, LANES), lambda p, i: (0, 0)),
                   pl.BlockSpec((N_EXPERTS, LANES), lambda p, i: (0, 0))],
        out_shape=[jax.ShapeDtypeStruct((nt, TOP_K, tm), I32),
                   jax.ShapeDtypeStruct((N_EXPERTS, LANES), I32),
                   jax.ShapeDtypeStruct((N_EXPERTS, LANES), I32)],
        scratch_shapes=[pltpu.VMEM((N_EXPERTS, LANES), F32), pltpu.VMEM((N_EXPERTS, LANES), F32)],
        compiler_params=pltpu.CompilerParams(dimension_semantics=("arbitrary", "arbitrary")),
        name="positions",
    )(idx)


def _row(ref, r, tiles):
    return ref.at[pl.ds(pl.multiple_of(r * tiles, tiles), tiles)]


def _rows_wait(src, dst, sem, n, tiles):
    pltpu.make_async_copy(src.at[pl.ds(0, n * tiles)], dst.at[pl.ds(0, n * tiles)], sem).wait()


def _dispatch_kernel(dest_ref, h_ref, xs_ref, sem):
    tm = h_ref.shape[0] // PACK_TILES

    def body(t, carry):
        for k in range(TOP_K):
            d = dest_ref[0, k, t]
            pltpu.make_async_copy(_row(h_ref, t, PACK_TILES), _row(xs_ref, d, PACK_TILES), sem).start(
                priority=k % 2)
        return carry

    lax.fori_loop(0, tm, body, 0)
    for _ in range(TOP_K):
        _rows_wait(h_ref, xs_ref, sem, tm, PACK_TILES)


def _dispatch(dest, htt, n_rows):
    t = htt.shape[0] // PACK_TILES
    tm = dest.shape[-1]
    return pl.pallas_call(
        _dispatch_kernel,
        grid=(t // tm,),
        in_specs=[pl.BlockSpec((1, TOP_K, tm), lambda i: (i, 0, 0), memory_space=pltpu.SMEM),
                  pl.BlockSpec((tm * PACK_TILES, LANES), lambda i: (i, 0))],
        out_specs=pl.BlockSpec(memory_space=pl.ANY),
        out_shape=jax.ShapeDtypeStruct((n_rows * PACK_TILES, LANES), U32),
        scratch_shapes=[pltpu.SemaphoreType.DMA(())],
        compiler_params=pltpu.CompilerParams(dimension_semantics=("arbitrary",), has_side_effects=True),
        name="dispatch",
    )(dest, htt)


def _expert_kernel(bstart_ref, nblk_ref, xs_hbm, wg_ref, wu_ref, wd_ref, ys_hbm,
                   xbuf, ybuf, wgb_ref, wub_ref, wdb_ref, xsem, ysem, ypend):
    e = pl.program_id(0)
    last = pl.num_programs(0) - 1
    n = nblk_ref[e]
    b0 = bstart_ref[e]
    total = bstart_ref[last] + nblk_ref[last]
    blk_words = EXPERT_ROWS * PACK_TILES
    xdepth = xbuf.shape[0]
    ydepth = ybuf.shape[0]

    def hbm_block(ref, b):
        return ref.at[pl.ds(pl.multiple_of(b * blk_words, blk_words), blk_words)]

    def x_copy(b):
        slot = b % xdepth
        return pltpu.make_async_copy(hbm_block(xs_hbm, b), xbuf.at[slot], xsem.at[slot])

    def y_copy(b, slot):
        return pltpu.make_async_copy(ybuf.at[slot], hbm_block(ys_hbm, b), ysem.at[slot])

    @pl.when(e == 0)
    def _():
        for s in range(ydepth):
            ypend[s] = 0
        for b in range(xdepth - 1):
            @pl.when(b < total)
            def _():
                x_copy(b).start()

    @pl.when(n > 0)
    def _():
        wgb_ref[...] = wg_ref[0].astype(BF16)
        wub_ref[...] = wu_ref[0].astype(BF16)
        wdb_ref[...] = wd_ref[0].astype(BF16)

        def block(j, carry):
            b = b0 + j
            x_copy(b).wait()
            ahead = b + (xdepth - 1)

            @pl.when(ahead < total)
            def _():
                x_copy(ahead).start()

            xslot = b % xdepth
            yslot = b % ydepth

            @pl.when(ypend[yslot] == 1)
            def _():
                y_copy(b, yslot).wait()

            halves = [_unpack_pairs(xbuf[xslot, pl.ds(k, EXPERT_ROWS, stride=PACK_TILES), :])
                      for k in range(PACK_TILES)]
            xb = jnp.concatenate([lo for lo, _ in halves] + [hi for _, hi in halves], axis=-1).astype(BF16)
            hid = _silu(_dot(xb, wgb_ref[...])) * _dot(xb, wub_ref[...])
            yb = _dot(hid.astype(BF16), wdb_ref[...])
            packed = _pack_pairs(yb[:, :D_MODEL // 2], yb[:, D_MODEL // 2:])
            for k in range(PACK_TILES):
                ybuf[yslot, pl.ds(k, EXPERT_ROWS, stride=PACK_TILES), :] = packed[:, k * LANES:(k + 1) * LANES]
            y_copy(b, yslot).start()
            ypend[yslot] = 1
            return carry

        lax.fori_loop(0, n, block, 0)

    @pl.when(e == last)
    def _():
        for s in range(ydepth):
            @pl.when(ypend[s] == 1)
            def _():
                y_copy(b0, s).wait()


def _experts(bstart, nblk, xs, w_gate, w_up, w_down):
    blk_words = EXPERT_ROWS * PACK_TILES
    wspec = lambda shape: pl.BlockSpec((1,) + shape, lambda e, bs, nb: (e, 0, 0))
    grid_spec = pltpu.PrefetchScalarGridSpec(
        num_scalar_prefetch=2,
        grid=(N_EXPERTS,),
        in_specs=[pl.BlockSpec(memory_space=pl.ANY),
                  wspec((D_MODEL, D_EXPERT)), wspec((D_MODEL, D_EXPERT)), wspec((D_EXPERT, D_MODEL))],
        out_specs=pl.BlockSpec(memory_space=pl.ANY),
        scratch_shapes=[pltpu.VMEM((X_RING, blk_words, LANES), U32), pltpu.VMEM((Y_RING, blk_words, LANES), U32),
                        pltpu.VMEM((D_MODEL, D_EXPERT), BF16), pltpu.VMEM((D_MODEL, D_EXPERT), BF16),
                        pltpu.VMEM((D_EXPERT, D_MODEL), BF16),
                        pltpu.SemaphoreType.DMA((X_RING,)), pltpu.SemaphoreType.DMA((Y_RING,)),
                        pltpu.SMEM((Y_RING,), I32)],
    )
    return pl.pallas_call(
        _expert_kernel,
        grid_spec=grid_spec,
        out_shape=jax.ShapeDtypeStruct(xs.shape, U32),
        compiler_params=pltpu.CompilerParams(dimension_semantics=("arbitrary",), has_side_effects=True),
        name="experts",
    )(bstart, nblk, xs, w_gate, w_up, w_down)


def _combine_kernel(dest_ref, dnext_ref, gate_ref, base_ref, g2_ref, b2_ref, ys_ref, o_ref, rows_ref, gt_ref, sem):
    tm = base_ref.shape[0]
    i = pl.program_id(0)
    nt = pl.num_programs(0)
    cur = i % 2
    group = SUBLANES

    def request(dref, t, buf):
        for k in range(TOP_K):
            d = dref[0, k, t]
            pltpu.make_async_copy(_row(ys_ref, d, PACK_TILES), _row(rows_ref.at[buf], k * tm + t, PACK_TILES),
                                  sem.at[buf]).start(priority=k % 2)

    @pl.when(i == 0)
    def _():
        def first(t, carry):
            request(dest_ref, t, 0)
            return carry

        lax.fori_loop(0, tm, first, 0)

    for _ in range(TOP_K):
        _rows_wait(ys_ref, rows_ref.at[cur], sem.at[cur], tm, PACK_TILES)
    gt_ref[...] = jnp.concatenate([gate_ref[0], jnp.zeros((LANES - TOP_K, tm), F32)], axis=0).T
    half = D_MODEL // 2

    def reduce_group(gi, carry):
        t0 = pl.multiple_of(gi * group, group)
        for tt in range(group):
            request(dnext_ref, t0 + tt, 1 - cur)

        g8 = gt_ref[pl.ds(t0, group), :]
        lo_acc = [base_ref[pl.ds(t0, group), c * LANES:(c + 1) * LANES] for c in range(PACK_TILES)]
        hi_acc = [base_ref[pl.ds(t0, group), half + c * LANES:half + (c + 1) * LANES] for c in range(PACK_TILES)]
        for k in range(TOP_K):
            gk = g8[:, k:k + 1]
            for c in range(PACK_TILES):
                lo, hi = _unpack_pairs(
                    rows_ref[cur, pl.ds((k * tm + t0) * PACK_TILES + c, group, stride=PACK_TILES), :])
                lo_acc[c] = lo_acc[c] + gk * lo
                hi_acc[c] = hi_acc[c] + gk * hi
        o_ref[pl.ds(t0, group), :] = jnp.concatenate(lo_acc + hi_acc, axis=-1)
        return carry

    lax.fori_loop(0, tm // group, reduce_group, 0)
    o_ref[...] = _layer_norm(o_ref[...], g2_ref[...], b2_ref[...])

    @pl.when(i == nt - 1)
    def _():
        for _ in range(TOP_K):
            _rows_wait(ys_ref, rows_ref.at[1 - cur], sem.at[1 - cur], tm, PACK_TILES)


def _combine(dest, gates, base, g2, b2, ys):
    t = base.shape[0]
    tm = dest.shape[-1]
    nt = t // tm
    return pl.pallas_call(
        _combine_kernel,
        grid=(nt,),
        in_specs=[pl.BlockSpec((1, TOP_K, tm), lambda i: (i, 0, 0), memory_space=pltpu.SMEM),
                  pl.BlockSpec((1, TOP_K, tm), lambda i: (jnp.minimum(i + 1, nt - 1), 0, 0), memory_space=pltpu.SMEM),
                  pl.BlockSpec((1, TOP_K, tm), lambda i: (i, 0, 0)),
                  pl.BlockSpec((tm, D_MODEL), lambda i: (i, 0)),
                  pl.BlockSpec((1, D_MODEL), lambda i: (0, 0)),
                  pl.BlockSpec((1, D_MODEL), lambda i: (0, 0)),
                  pl.BlockSpec(memory_space=pl.ANY)],
        out_specs=pl.BlockSpec((tm, D_MODEL), lambda i: (i, 0)),
        out_shape=jax.ShapeDtypeStruct((t, D_MODEL), F32),
        scratch_shapes=[pltpu.VMEM((2, TOP_K * tm * PACK_TILES, LANES), U32), pltpu.VMEM((tm, LANES), F32),
                        pltpu.SemaphoreType.DMA((2,))],
        compiler_params=pltpu.CompilerParams(dimension_semantics=("arbitrary",),
                                             vmem_limit_bytes=48 * 1024 * 1024),
        name="combine",
    )(dest, dest, gates, base, g2, b2, ys)


def _layer(h2, w_in, conv_w, conv_b, dt_bias, a_log, d_skip, ssd_norm_w, lower_bound, hgrn_norm_w, w_out,
           ln1_g, ln1_b, w_router, router_bias, w_gate_e, w_up_e, w_down_e, w_gate_s, w_up_s, w_down_s,
           ln2_g, ln2_b):
    t = h2.shape[0]
    dt0 = D_SSM + D_CONV
    q0 = dt0 + SSD_HEADS
    w_perm = jnp.concatenate(
        [w_in[:, :dt0], w_in[:, q0:], w_in[:, dt0:q0], jnp.zeros((D_MODEL, DT_PAD - SSD_HEADS), w_in.dtype)],
        axis=1).astype(BF16)
    z, xbc, q, f, i, g, dt = _inproj(h2, w_perm)
    y_ssd = _ssd(z, xbc, dt, conv_w, conv_b, dt_bias, a_log, d_skip, ssd_norm_w)
    y_hgrn = _hgrn(q, f, i, g, lower_bound, hgrn_norm_w)

    wr_t = w_router.astype(F32).T
    wr_hi = wr_t.astype(BF16)
    wr_lo = (wr_t - wr_hi.astype(F32)).astype(BF16)
    row = lambda v: v.reshape(1, -1).astype(F32)
    htt, base, idx, gates = _post(
        h2, y_ssd, y_hgrn, w_out.astype(BF16), row(ln1_g), row(ln1_b), wr_hi, wr_lo,
        router_bias.reshape(N_EXPERTS, 1).astype(F32),
        w_gate_s.astype(BF16), w_up_s.astype(BF16), w_down_s.astype(BF16))

    dest, bstart, nblk = _positions(idx)
    n_rows = _max_blocks(t) * EXPERT_ROWS
    xs = _dispatch(dest, htt, n_rows)
    ys = _experts(bstart[:, 0], nblk[:, 0], xs, w_gate_e, w_up_e, w_down_e)
    return _combine(dest, gates, base, row(ln2_g), row(ln2_b), ys)


def kernel(x, w_in, conv_w, conv_b, dt_bias, a_log, d_skip, ssd_norm_w, hgrn_lb_logits, hgrn_norm_w, w_out,
           ln1_g, ln1_b, w_router, router_bias, w_gate_e, w_up_e, w_down_e, w_gate_s, w_up_s, w_down_s,
           ln2_g, ln2_b):
    bsz, t, d = x.shape
    assert bsz == 1 and d == D_MODEL, "the recurrent mixers carry state across the flattened token axis"
    depth = w_in.shape[0]
    lower_bounds = jnp.cumsum(jax.nn.softmax(hgrn_lb_logits.astype(F32), axis=0), axis=0)
    h = x.reshape(bsz * t, d)
    for l in range(depth):
        h = _layer(h, w_in[l], conv_w[l], conv_b[l], dt_bias[l], a_log[l], d_skip[l], ssd_norm_w[l],
                   lower_bounds[l], hgrn_norm_w[l], w_out[l], ln1_g[l], ln1_b[l], w_router[l],
                   router_bias[l], w_gate_e[l], w_up_e[l], w_down_e[l], w_gate_s[l], w_up_s[l],
                   w_down_s[l], ln2_g[l], ln2_b[l])
    return h.reshape(bsz, t, d)
```

```python
import jax
import jax.numpy as jnp
from jax import lax
from jax.experimental import pallas as pl
from jax.experimental.pallas import tpu as pltpu
from jax.experimental.pallas import tpu_sc as plsc

F32 = jnp.float32
BF16 = jnp.bfloat16
I32 = jnp.int32
U32 = jnp.uint32

D_MODEL = 1024
D_SSM = 512
D_HGRN = 512
SSD_HEADS = 8
SSD_HEAD_DIM = 64
SSD_GROUPS = 2
SSD_STATE = 128
SSD_CONV = 4
SSD_CHUNK = 128
D_CONV = D_SSM + 2 * SSD_GROUPS * SSD_STATE
HGRN_HEADS = 4
HGRN_DK = 128
HGRN_CHUNK = 64
HGRN_SUB = 8
N_EXPERTS = 256
TOP_K = 8
N_EXPERT_GROUPS = 8
TOPK_GROUPS = 4
D_EXPERT = 256
ROUTED_SCALE = 2.5
ALPHA = 2.0 ** 0.25
LN_EPS = 1e-5
RMS_EPS = 1e-6

LANES = 128
SUBLANES = 8
ROW_TILES = D_MODEL // LANES
ROW_HALVES = 2
HALF_WORDS = D_MODEL // 2 // ROW_HALVES
SC_WINDOW = 128
DT_PAD = LANES
N_IN_PAD = D_SSM + D_CONV + 4 * D_HGRN + DT_PAD

TM_PROJ = 256
TM_TOK = 256
EXPERT_ROWS = 256
X_RING = 4
Y_RING = 2
NEG = -1e30


def _sigmoid(x):
    return 1.0 / (1.0 + jnp.exp(-x))


def _silu(x):
    return x * _sigmoid(x)


def _split3(x):
    hi = x.astype(BF16)
    r = x - hi.astype(F32)
    mid = r.astype(BF16)
    lo = (r - mid.astype(F32)).astype(BF16)
    return hi, mid, lo


def _dot(a, b):
    return jnp.dot(a, b, preferred_element_type=F32)


def _dot_nt(a, b):
    return lax.dot_general(a, b, (((1,), (1,)), ((), ())), preferred_element_type=F32)


def _dot_tn(a, b):
    return lax.dot_general(a, b, (((0,), (0,)), ((), ())), preferred_element_type=F32)


def _sel_dot(sel, x):
    hi, mid, lo = _split3(x)
    return _dot(sel, hi) + _dot(sel, mid) + _dot(sel, lo)


def _dot_sel(x, sel):
    hi, mid, lo = _split3(x)
    return _dot(hi, sel) + _dot(mid, sel) + _dot(lo, sel)


def _pack_pairs(lo, hi):
    lo_bits = pltpu.bitcast(lo.astype(BF16).astype(F32), U32) >> 16
    hi_bits = pltpu.bitcast(hi.astype(BF16).astype(F32), U32) & jnp.uint32(0xFFFF0000)
    return hi_bits | lo_bits


def _unpack_pairs(p):
    return pltpu.bitcast(p << 16, F32), pltpu.bitcast(p & jnp.uint32(0xFFFF0000), F32)


def _layer_norm(x, g, b):
    mu = jnp.mean(x, axis=-1, keepdims=True)
    xc = x - mu
    var = jnp.mean(xc * xc, axis=-1, keepdims=True)
    return xc * lax.rsqrt(var + LN_EPS) * g + b


def _inproj_kernel(x_ref, w_ref, z_ref, xbc_ref, q_ref, f_ref, i_ref, g_ref, dt_ref):
    xb = x_ref[...].astype(BF16)
    col = 0
    for ref in (z_ref, xbc_ref, q_ref, f_ref, i_ref, g_ref, dt_ref):
        n = ref.shape[-1]
        ref[...] = _dot(xb, w_ref[:, col:col + n])
        col += n


def _inproj(x2, w_perm):
    t = x2.shape[0]
    widths = (D_SSM, D_CONV, D_HGRN, D_HGRN, D_HGRN, D_HGRN, DT_PAD)
    return pl.pallas_call(
        _inproj_kernel,
        grid=(t // TM_PROJ,),
        in_specs=[pl.BlockSpec((TM_PROJ, D_MODEL), lambda i: (i, 0)),
                  pl.BlockSpec((D_MODEL, N_IN_PAD), lambda i: (0, 0))],
        out_specs=[pl.BlockSpec((TM_PROJ, n), lambda i: (i, 0)) for n in widths],
        out_shape=[jax.ShapeDtypeStruct((t, n), F32) for n in widths],
        compiler_params=pltpu.CompilerParams(dimension_semantics=("arbitrary",),
                                             vmem_limit_bytes=48 * 1024 * 1024),
        name="inproj",
    )(x2, w_perm)


def _ssd_kernel(z_ref, xbc_ref, dt_ref, cw_ref, cb_ref, dtb_ref, alog_ref, dskip_ref, nw_ref,
                y_ref, ext_ref, st_ref):
    L = SSD_CHUNK
    halo = SUBLANES

    @pl.when(pl.program_id(0) == 0)
    def _():
        ext_ref[0:halo, :] = jnp.zeros((halo, D_CONV), F32)
        st_ref[...] = jnp.zeros(st_ref.shape, F32)

    ext_ref[halo:halo + L, :] = xbc_ref[...]
    acc = jnp.broadcast_to(cb_ref[...], (L, D_CONV))
    for k in range(SSD_CONV):
        off = halo - (SSD_CONV - 1) + k
        acc = acc + cw_ref[k:k + 1, :] * ext_ref[off:off + L, :]
    ext_ref[0:halo, :] = xbc_ref[L - halo:L, :]
    u = _silu(acc)
    xs = u[:, :D_SSM]
    bm = u[:, D_SSM:D_SSM + SSD_GROUPS * SSD_STATE]
    cm = u[:, D_SSM + SSD_GROUPS * SSD_STATE:]

    draw = dt_ref[...] + dtb_ref[...]
    dt = jnp.maximum(draw, 0.0) + jnp.log(1.0 + jnp.exp(-jnp.abs(draw)))
    ad = dt * (-jnp.exp(alog_ref[...]))
    rows = lax.broadcasted_iota(I32, (L, L), 0)
    cols = lax.broadcasted_iota(I32, (L, L), 1)
    causal = rows >= cols
    a_cum = _sel_dot(causal.astype(BF16), ad)
    a_cum_t = a_cum.T

    hrow = lax.broadcasted_iota(I32, (LANES, D_SSM), 0)
    hcol = lax.broadcasted_iota(I32, (LANES, D_SSM), 1) // SSD_HEAD_DIM
    expand = (hrow == hcol).astype(BF16)
    dt_x = _dot_sel(dt, expand)
    acx = _dot_sel(a_cum, expand)
    last = acx[L - 1:L, :]
    ea_x = jnp.exp(acx)
    dec_x = jnp.exp(last - acx)
    elast_x = jnp.exp(last)

    xdt = xs * dt_x
    gw = SSD_HEADS // SSD_GROUPS * SSD_HEAD_DIM
    lane_head = lax.broadcasted_iota(I32, (L, gw), 1) // SSD_HEAD_DIM
    ys = []
    for g in range(SSD_GROUPS):
        bg = bm[:, g * SSD_STATE:(g + 1) * SSD_STATE]
        cg = cm[:, g * SSD_STATE:(g + 1) * SSD_STATE].astype(BF16)
        bg_t = bg.T.astype(BF16)
        gmat = _dot(cg, bg_t)
        xdt_g = xdt[:, g * gw:(g + 1) * gw]
        xdt_gb = xdt_g.astype(BF16)
        r_prev = st_ref[g]
        y_g = _dot(cg, r_prev.astype(BF16)) * ea_x[:, g * gw:(g + 1) * gw]
        new_s = _dot(bg_t, (xdt_g * dec_x[:, g * gw:(g + 1) * gw]).astype(BF16))
        st_ref[g] = r_prev * elast_x[:, g * gw:(g + 1) * gw] + new_s
        for j in range(SSD_HEADS // SSD_GROUPS):
            h = g * (SSD_HEADS // SSD_GROUPS) + j
            diff = a_cum[:, h:h + 1] - a_cum_t[h:h + 1, :]
            decay = jnp.exp(jnp.where(causal, diff, NEG))
            yd = _dot((gmat * decay).astype(BF16), xdt_gb)
            y_g = y_g + jnp.where(lane_head == j, yd, 0.0)
        ys.append(y_g)
    y = jnp.concatenate(ys, axis=-1) + xs * dskip_ref[...]
    y = y * _silu(z_ref[...])
    outs = []
    ng = D_SSM // SSD_GROUPS
    for g in range(SSD_GROUPS):
        yg = y[:, g * ng:(g + 1) * ng]
        ms = jnp.mean(yg * yg, axis=-1, keepdims=True)
        outs.append(yg * lax.rsqrt(ms + RMS_EPS))
    y_ref[...] = jnp.concatenate(outs, axis=-1) * nw_ref[...]


def _ssd(z, xbc, dt, conv_w, conv_b, dt_bias, a_log, d_skip, norm_w):
    t = z.shape[0]
    L = SSD_CHUNK
    pad = lambda v: jnp.pad(v.astype(F32), (0, LANES - v.shape[0])).reshape(1, LANES)
    full = lambda shape: pl.BlockSpec(shape, lambda c: (0,) * len(shape))
    return pl.pallas_call(
        _ssd_kernel,
        grid=(t // L,),
        in_specs=[pl.BlockSpec((L, D_SSM), lambda c: (c, 0)),
                  pl.BlockSpec((L, D_CONV), lambda c: (c, 0)),
                  pl.BlockSpec((L, DT_PAD), lambda c: (c, 0)),
                  full((SSD_CONV, D_CONV)), full((1, D_CONV)), full((1, LANES)), full((1, LANES)),
                  full((1, D_SSM)), full((1, D_SSM))],
        out_specs=pl.BlockSpec((L, D_SSM), lambda c: (c, 0)),
        out_shape=jax.ShapeDtypeStruct((t, D_SSM), F32),
        scratch_shapes=[pltpu.VMEM((L + SUBLANES, D_CONV), F32),
                        pltpu.VMEM((SSD_GROUPS, SSD_STATE, D_SSM // SSD_GROUPS), F32)],
        compiler_params=pltpu.CompilerParams(dimension_semantics=("arbitrary",)),
        name="ssd",
    )(z, xbc, dt, conv_w.astype(F32), conv_b.reshape(1, D_CONV).astype(F32), pad(dt_bias), pad(a_log),
      jnp.repeat(d_skip.astype(F32), SSD_HEAD_DIM).reshape(1, D_SSM), norm_w.reshape(1, D_SSM).astype(F32))


def _tile_bcast(x, r):
    n, d = x.shape
    x3 = x.reshape(n // SUBLANES, SUBLANES, d)
    return jnp.broadcast_to(x3[:, r:r + 1, :], x3.shape).reshape(n, d)


def _hgrn_kernel(q_ref, f_ref, i_ref, g_ref, lb_ref, nw_ref, o_ref, st_ref):
    C = HGRN_CHUNK
    S = HGRN_SUB
    nsub = C // S

    @pl.when(pl.program_id(0) == 0)
    def _():
        st_ref[...] = jnp.zeros(st_ref.shape, F32)

    lb = lb_ref[...]
    fg = lb + (1.0 - lb) * _sigmoid(f_ref[...])
    kk = 1.0 - fg
    qs = _silu(q_ref[...])
    v = i_ref[...]
    rows = lax.broadcasted_iota(I32, (C, C), 0)
    cols = lax.broadcasted_iota(I32, (C, C), 1)
    cum = _sel_dot((rows >= cols).astype(BF16), jnp.log(fg))
    last = cum[C - 1:C, :]

    trow = lax.broadcasted_iota(I32, (C, D_HGRN), 0)
    tmod = trow % S
    tblk = trow // S

    xr = []
    for r in range(S):
        e = jnp.exp(jnp.where(tmod >= r, cum - _tile_bcast(cum, r), NEG))
        xr.append(qs * _tile_bcast(kk, r) * e)
    kbe = kk * jnp.exp(_tile_bcast(cum, S - 1) - cum)
    aj, bj = [], []
    for j in range(nsub - 1):
        ce = cum[j * S + S - 1:j * S + S, :]
        aj.append(qs * jnp.exp(jnp.where(tblk > j, cum - ce, NEG)))
        bj.append(jnp.where(tblk == j, kbe, 0.0))
    qd = (qs * jnp.exp(cum)).astype(BF16)
    kd = (kk * jnp.exp(last - cum)).astype(BF16)
    elast = jnp.exp(last)

    att_col = lax.broadcasted_iota(I32, (C, C), 1)
    att_rblk = lax.broadcasted_iota(I32, (C, C), 0) // S * S
    outs = []
    for h in range(HGRN_HEADS):
        blk = slice(h * HGRN_DK, (h + 1) * HGRN_DK)
        a_cat = jnp.concatenate([a[:, blk] for a in aj], axis=-1).astype(BF16)
        b_cat = jnp.concatenate([b[:, blk] for b in bj], axis=-1).astype(BF16)
        att = _dot_nt(a_cat, b_cat)
        for r in range(S):
            rs = jnp.sum(xr[r][:, blk], axis=-1, keepdims=True)
            att = jnp.where(att_col == att_rblk + r, rs, att)
        vb = v[:, blk].astype(BF16)
        s_prev = st_ref[h]
        o = _dot_nt(qd[:, blk], s_prev.astype(BF16)) + _dot(att.astype(BF16), vb)
        st_ref[h] = s_prev * elast[:, blk] + _dot_tn(vb, kd[:, blk])
        ms = jnp.mean(o * o, axis=-1, keepdims=True)
        outs.append(o * lax.rsqrt(ms + RMS_EPS))
    o_ref[...] = jnp.concatenate(outs, axis=-1) * nw_ref[...] * _silu(g_ref[...])


def _hgrn(q, f, i, g, lower_bound, norm_w):
    t = q.shape[0]
    C = HGRN_CHUNK
    tok = pl.BlockSpec((C, D_HGRN), lambda c: (c, 0))
    vec = pl.BlockSpec((1, D_HGRN), lambda c: (0, 0))
    return pl.pallas_call(
        _hgrn_kernel,
        grid=(t // C,),
        in_specs=[tok, tok, tok, tok, vec, vec],
        out_specs=tok,
        out_shape=jax.ShapeDtypeStruct((t, D_HGRN), F32),
        scratch_shapes=[pltpu.VMEM((HGRN_HEADS, D_HGRN // HGRN_HEADS, HGRN_DK), F32)],
        compiler_params=pltpu.CompilerParams(dimension_semantics=("arbitrary",)),
        name="hgrn",
    )(q, f, i, g, lower_bound.reshape(1, D_HGRN).astype(F32), norm_w.reshape(1, D_HGRN).astype(F32))


def _post_kernel(x_ref, ys_ref, yh_ref, wo_ref, g1_ref, b1_ref, wrh_ref, wrl_ref, rb_ref,
                 wgs_ref, wus_ref, wds_ref, htt_ref, base_ref, idx_ref, gate_ref):
    tm = x_ref.shape[0]
    mix = (_dot(ys_ref[...].astype(BF16), wo_ref[0:D_SSM, :])
           + _dot(yh_ref[...].astype(BF16), wo_ref[D_SSM:, :]))
    h1 = _layer_norm(ALPHA * x_ref[...] + mix, g1_ref[...], b1_ref[...])
    packed = _pack_pairs(h1[:, :D_MODEL // 2], h1[:, D_MODEL // 2:])
    for hh in range(ROW_HALVES):
        htt_ref[hh] = packed[:, hh * HALF_WORDS:(hh + 1) * HALF_WORDS]
    hb = h1.astype(BF16)
    hid = _silu(_dot(hb, wgs_ref[...])) * _dot(hb, wus_ref[...])
    base_ref[...] = ALPHA * h1 + _dot(hid.astype(BF16), wds_ref[...])

    hlo = (h1 - hb.astype(F32)).astype(BF16)
    logits = _dot_nt(wrh_ref[...], hb) + _dot_nt(wrh_ref[...], hlo) + _dot_nt(wrl_ref[...], hb)
    scores = _sigmoid(logits)
    biased = scores + rb_ref[...]
    per_group = N_EXPERTS // N_EXPERT_GROUPS
    eidx = lax.broadcasted_iota(I32, (N_EXPERTS, tm), 0)
    big = jnp.int32(1 << 20)
    gsc = []
    bidx = lax.broadcasted_iota(I32, (per_group, tm), 0)
    for gi in range(N_EXPERT_GROUPS):
        blk = biased[gi * per_group:(gi + 1) * per_group, :]
        m1 = jnp.max(blk, axis=0, keepdims=True)
        i1 = jnp.min(jnp.where(blk == m1, bidx, big), axis=0, keepdims=True)
        m2 = jnp.max(jnp.where(bidx == i1, NEG, blk), axis=0, keepdims=True)
        gsc.append(m1 + m2)
    cur = jnp.concatenate(gsc, axis=0)
    gidx = lax.broadcasted_iota(I32, (N_EXPERT_GROUPS, tm), 0)
    gsel = jnp.zeros((N_EXPERT_GROUPS, tm), F32)
    for _ in range(TOPK_GROUPS):
        m = jnp.max(cur, axis=0, keepdims=True)
        i = jnp.min(jnp.where(cur == m, gidx, big), axis=0, keepdims=True)
        hit = gidx == i
        gsel = jnp.where(hit, 1.0, gsel)
        cur = jnp.where(hit, NEG, cur)
    emask = jnp.concatenate(
        [jnp.broadcast_to(gsel[gi:gi + 1, :], (per_group, tm)) for gi in range(N_EXPERT_GROUPS)], axis=0)
    masked = jnp.where(emask > 0.0, biased, NEG)
    idx_rows, gate_rows = [], []
    for _ in range(TOP_K):
        m = jnp.max(masked, axis=0, keepdims=True)
        i = jnp.min(jnp.where(masked == m, eidx, big), axis=0, keepdims=True)
        hit = eidx == i
        idx_rows.append(i)
        gate_rows.append(jnp.sum(jnp.where(hit, scores, 0.0), axis=0, keepdims=True))
        masked = jnp.where(hit, NEG, masked)
    gates = jnp.concatenate(gate_rows, axis=0)
    gates = gates / jnp.sum(gates, axis=0, keepdims=True) * ROUTED_SCALE
    idx_ref[0] = jnp.concatenate(idx_rows, axis=0)
    gate_ref[0] = gates


def _post(x2, y_ssd, y_hgrn, wo, g1, b1, wr_hi, wr_lo, rbias, wgs, wus, wds):
    t = x2.shape[0]
    tm = TM_TOK
    nt = t // tm
    full = lambda shape: pl.BlockSpec(shape, lambda i: (0,) * len(shape))
    return pl.pallas_call(
        _post_kernel,
        grid=(nt,),
        in_specs=[pl.BlockSpec((tm, D_MODEL), lambda i: (i, 0)),
                  pl.BlockSpec((tm, D_SSM), lambda i: (i, 0)),
                  pl.BlockSpec((tm, D_HGRN), lambda i: (i, 0)),
                  full((D_MODEL, D_MODEL)), full((1, D_MODEL)), full((1, D_MODEL)),
                  full((N_EXPERTS, D_MODEL)), full((N_EXPERTS, D_MODEL)), full((N_EXPERTS, 1)),
                  full((D_MODEL, D_EXPERT)), full((D_MODEL, D_EXPERT)), full((D_EXPERT, D_MODEL))],
        out_specs=[pl.BlockSpec((ROW_HALVES, tm, HALF_WORDS), lambda i: (0, i, 0)),
                   pl.BlockSpec((tm, D_MODEL), lambda i: (i, 0)),
                   pl.BlockSpec((1, TOP_K, tm), lambda i: (i, 0, 0)),
                   pl.BlockSpec((1, TOP_K, tm), lambda i: (i, 0, 0))],
        out_shape=[jax.ShapeDtypeStruct((ROW_HALVES, t, HALF_WORDS), U32),
                   jax.ShapeDtypeStruct((t, D_MODEL), F32),
                   jax.ShapeDtypeStruct((nt, TOP_K, tm), I32),
                   jax.ShapeDtypeStruct((nt, TOP_K, tm), F32)],
        compiler_params=pltpu.CompilerParams(dimension_semantics=("arbitrary",),
                                             vmem_limit_bytes=48 * 1024 * 1024),
        name="post",
    )(x2, y_ssd, y_hgrn, wo, g1, b1, wr_hi, wr_lo, rbias, wgs, wus, wds)


def _max_blocks(t):
    return (t * TOP_K + N_EXPERTS * (EXPERT_ROWS - 1)) // EXPERT_ROWS


def _pos_kernel(idx_ref, dest_ref, bstart_ref, nblk_ref, cnt_ref, start_ref):
    phase = pl.program_id(0)
    i = pl.program_id(1)
    tm = idx_ref.shape[-1]
    idx = idx_ref[0]
    eidx = lax.broadcasted_iota(I32, (N_EXPERTS, tm), 0)
    sel = [eidx == idx[k:k + 1, :] for k in range(TOP_K)]
    onehot = sel[0]
    for k in range(1, TOP_K):
        onehot = jnp.logical_or(onehot, sel[k])
    mt = jnp.where(onehot, 1.0, 0.0).astype(BF16)
    tile_cnt = _dot(mt, jnp.ones((tm, LANES), BF16))

    @pl.when(jnp.logical_and(phase == 0, i == 0))
    def _():
        cnt_ref[...] = jnp.zeros(cnt_ref.shape, F32)

    @pl.when(phase == 0)
    def _():
        cnt_ref[...] += tile_cnt

    @pl.when(jnp.logical_and(phase == 1, i == 0))
    def _():
        nb = jnp.floor((cnt_ref[...] + (EXPERT_ROWS - 1)) * (1.0 / EXPERT_ROWS))
        r = lax.broadcasted_iota(I32, (N_EXPERTS, N_EXPERTS), 0)
        c = lax.broadcasted_iota(I32, (N_EXPERTS, N_EXPERTS), 1)
        end = _dot((r >= c).astype(BF16), nb.astype(BF16))
        start_ref[...] = (end - nb) * EXPERT_ROWS
        cnt_ref[...] = jnp.zeros(cnt_ref.shape, F32)
        bstart_ref[...] = (end - nb).astype(I32)
        nblk_ref[...] = nb.astype(I32)

    @pl.when(phase == 1)
    def _():
        r = lax.broadcasted_iota(I32, (tm, tm), 0)
        c = lax.broadcasted_iota(I32, (tm, tm), 1)
        before = _dot(mt, (r < c).astype(BF16))
        slot = start_ref[:, 0:1] + cnt_ref[:, 0:1] + before
        rows = [jnp.sum(jnp.where(sel[k], slot, 0.0), axis=0, keepdims=True) for k in range(TOP_K)]
        dest_ref[0] = jnp.concatenate(rows, axis=0).astype(I32)
        cnt_ref[...] += tile_cnt


def _positions(idx):
    nt, _, tm = idx.shape
    return pl.pallas_call(
        _pos_kernel,
        grid=(2, nt),
        in_specs=[pl.BlockSpec((1, TOP_K, tm), lambda p, i: (i, 0, 0))],
        out_specs=[pl.BlockSpec((1, TOP_K, tm), lambda p, i: (i * p, 0, 0)),
                   pl.BlockSpec((N_EXPERTS, LANES), lambda p, i: (0, 0)),
                   pl.BlockSpec((N_EXPERTS, LANES), lambda p, i: (0, 0))],
        out_shape=[jax.ShapeDtypeStruct((nt, TOP_K, tm), I32),
                   jax.ShapeDtypeStruct((N_EXPERTS, LANES), I32),
                   jax.ShapeDtypeStruct((N_EXPERTS, LANES), I32)],
        scratch_shapes=[pltpu.VMEM((N_EXPERTS, LANES), F32), pltpu.VMEM((N_EXPERTS, LANES), F32)],
        compiler_params=pltpu.CompilerParams(dimension_semantics=("arbitrary", "arbitrary")),
        name="positions",
    )(idx)


def _sc_mesh():
    return plsc.VectorSubcoreMesh(core_axis_name="core", subcore_axis_name="subcore")


def _sc_dispatch(src, index, n_out):
    n = index.shape[0]
    src_blocks = src.shape[0] // SC_WINDOW
    index = index.reshape(1, n)

    @pl.kernel(out_type=jax.ShapeDtypeStruct((n_out, HALF_WORDS), src.dtype), mesh=_sc_mesh(), name="sc_dispatch")
    def scatter(src_hbm, idx_hbm, out_hbm):
        def body(rows_vmem, idx_vmem):
            pltpu.sync_copy(rows_vmem, out_hbm.at[idx_vmem.at[0]])

        pltpu.emit_pipeline(
            body, grid=(n // SC_WINDOW,),
            in_specs=[pl.BlockSpec((SC_WINDOW, HALF_WORDS), index_map=lambda i: (i % src_blocks, 0)),
                      pl.BlockSpec((1, SC_WINDOW), index_map=lambda i: (0, i))],
            out_specs=[],
            core_axis_name=("core", "subcore"), dimension_semantics=(pltpu.PARALLEL,))(src_hbm, idx_hbm)

    return scatter(src, index)


def _sc_gather(src, index):
    n = index.shape[0]
    index = index.reshape(1, n)

    @pl.kernel(out_type=jax.ShapeDtypeStruct((n, HALF_WORDS), src.dtype), mesh=_sc_mesh(), name="sc_gather")
    def gather(src_hbm, idx_hbm, out_hbm):
        def body(idx_vmem, rows_vmem):
            pltpu.sync_copy(src_hbm.at[idx_vmem.at[0]], rows_vmem)

        pltpu.emit_pipeline(
            body, grid=(n // SC_WINDOW,),
            in_specs=[pl.BlockSpec((1, SC_WINDOW), index_map=lambda i: (0, i))],
            out_specs=[pl.BlockSpec((SC_WINDOW, HALF_WORDS), index_map=lambda i: (i, 0))],
            core_axis_name=("core", "subcore"), dimension_semantics=(pltpu.PARALLEL,))(idx_hbm, out_hbm)

    return gather(src, index)


def _expert_kernel(bstart_ref, nblk_ref, xs_hbm, wg_ref, wu_ref, wd_ref, ys_hbm,
                   xbuf, ybuf, wgb_ref, wub_ref, wdb_ref, xsem, ysem, ypend):
    e = pl.program_id(0)
    last = pl.num_programs(0) - 1
    n = nblk_ref[e]
    b0 = bstart_ref[e]
    total = bstart_ref[last] + nblk_ref[last]
    blk_rows = EXPERT_ROWS * ROW_HALVES
    xdepth = xbuf.shape[0]
    ydepth = ybuf.shape[0]

    def hbm_block(ref, b):
        return ref.at[pl.ds(pl.multiple_of(b * blk_rows, blk_rows), blk_rows)]

    def x_copy(b):
        slot = b % xdepth
        return pltpu.make_async_copy(hbm_block(xs_hbm, b), xbuf.at[slot], xsem.at[slot])

    def y_copy(b, slot):
        return pltpu.make_async_copy(ybuf.at[slot], hbm_block(ys_hbm, b), ysem.at[slot])

    @pl.when(e == 0)
    def _():
        for s in range(ydepth):
            ypend[s] = 0
        for b in range(xdepth - 1):
            @pl.when(b < total)
            def _():
                x_copy(b).start()

    @pl.when(n > 0)
    def _():
        wgb_ref[...] = wg_ref[0].astype(BF16)
        wub_ref[...] = wu_ref[0].astype(BF16)
        wdb_ref[...] = wd_ref[0].astype(BF16)

        def block(j, carry):
            b = b0 + j
            x_copy(b).wait()
            ahead = b + (xdepth - 1)

            @pl.when(ahead < total)
            def _():
                x_copy(ahead).start()

            xslot = b % xdepth
            yslot = b % ydepth

            @pl.when(ypend[yslot] == 1)
            def _():
                y_copy(b, yslot).wait()

            halves = [_unpack_pairs(xbuf[xslot, hh * EXPERT_ROWS:(hh + 1) * EXPERT_ROWS, :])
                      for hh in range(ROW_HALVES)]
            xb = jnp.concatenate([lo for lo, _ in halves] + [hi for _, hi in halves], axis=-1).astype(BF16)
            hid = _silu(_dot(xb, wgb_ref[...])) * _dot(xb, wub_ref[...])
            yb = _dot(hid.astype(BF16), wdb_ref[...])
            packed = _pack_pairs(yb[:, :D_MODEL // 2], yb[:, D_MODEL // 2:])
            for hh in range(ROW_HALVES):
                ybuf[yslot, hh * EXPERT_ROWS:(hh + 1) * EXPERT_ROWS, :] = (
                    packed[:, hh * HALF_WORDS:(hh + 1) * HALF_WORDS])
            y_copy(b, yslot).start()
            ypend[yslot] = 1
            return carry

        lax.fori_loop(0, n, block, 0)

    @pl.when(e == last)
    def _():
        for s in range(ydepth):
            @pl.when(ypend[s] == 1)
            def _():
                y_copy(b0, s).wait()


def _experts(bstart, nblk, xs, w_gate, w_up, w_down):
    blk_shape = (EXPERT_ROWS * ROW_HALVES, HALF_WORDS)
    wspec = lambda shape: pl.BlockSpec((1,) + shape, lambda e, bs, nb: (e, 0, 0))
    grid_spec = pltpu.PrefetchScalarGridSpec(
        num_scalar_prefetch=2,
        grid=(N_EXPERTS,),
        in_specs=[pl.BlockSpec(memory_space=pl.ANY),
                  wspec((D_MODEL, D_EXPERT)), wspec((D_MODEL, D_EXPERT)), wspec((D_EXPERT, D_MODEL))],
        out_specs=pl.BlockSpec(memory_space=pl.ANY),
        scratch_shapes=[pltpu.VMEM((X_RING,) + blk_shape, U32), pltpu.VMEM((Y_RING,) + blk_shape, U32),
                        pltpu.VMEM((D_MODEL, D_EXPERT), BF16), pltpu.VMEM((D_MODEL, D_EXPERT), BF16),
                        pltpu.VMEM((D_EXPERT, D_MODEL), BF16),
                        pltpu.SemaphoreType.DMA((X_RING,)), pltpu.SemaphoreType.DMA((Y_RING,)),
                        pltpu.SMEM((Y_RING,), I32)],
    )
    return pl.pallas_call(
        _expert_kernel,
        grid_spec=grid_spec,
        out_shape=jax.ShapeDtypeStruct(xs.shape, U32),
        compiler_params=pltpu.CompilerParams(dimension_semantics=("arbitrary",), has_side_effects=True),
        name="experts",
    )(bstart, nblk, xs, w_gate, w_up, w_down)


def _combine_kernel(gate_ref, base_ref, g2_ref, b2_ref, rows_ref, o_ref):
    tm = base_ref.shape[0]
    gates_t = jnp.concatenate([gate_ref[0], jnp.zeros((LANES - TOP_K, tm), F32)], axis=0).T
    half = D_MODEL // 2
    lo_acc = [base_ref[:, hh * HALF_WORDS:(hh + 1) * HALF_WORDS] for hh in range(ROW_HALVES)]
    hi_acc = [base_ref[:, half + hh * HALF_WORDS:half + (hh + 1) * HALF_WORDS] for hh in range(ROW_HALVES)]
    for k in range(TOP_K):
        gk = gates_t[:, k:k + 1]
        for hh in range(ROW_HALVES):
            lo, hi = _unpack_pairs(rows_ref[k, hh])
            lo_acc[hh] = lo_acc[hh] + gk * lo
            hi_acc[hh] = hi_acc[hh] + gk * hi
    o_ref[...] = _layer_norm(jnp.concatenate(lo_acc + hi_acc, axis=-1), g2_ref[...], b2_ref[...])


def _combine(gates, base, g2, b2, rows):
    t = base.shape[0]
    tm = gates.shape[-1]
    return pl.pallas_call(
        _combine_kernel,
        grid=(t // tm,),
        in_specs=[pl.BlockSpec((1, TOP_K, tm), lambda i: (i, 0, 0)),
                  pl.BlockSpec((tm, D_MODEL), lambda i: (i, 0)),
                  pl.BlockSpec((1, D_MODEL), lambda i: (0, 0)),
                  pl.BlockSpec((1, D_MODEL), lambda i: (0, 0)),
                  pl.BlockSpec((TOP_K, ROW_HALVES, tm, HALF_WORDS), lambda i: (0, 0, i, 0))],
        out_specs=pl.BlockSpec((tm, D_MODEL), lambda i: (i, 0)),
        out_shape=jax.ShapeDtypeStruct((t, D_MODEL), F32),
        compiler_params=pltpu.CompilerParams(dimension_semantics=("arbitrary",),
                                             vmem_limit_bytes=48 * 1024 * 1024),
        name="combine",
    )(gates, base, g2, b2, rows)


def _layer(h2, w_in, conv_w, conv_b, dt_bias, a_log, d_skip, ssd_norm_w, lower_bound, hgrn_norm_w, w_out,
           ln1_g, ln1_b, w_router, router_bias, w_gate_e, w_up_e, w_down_e, w_gate_s, w_up_s, w_down_s,
           ln2_g, ln2_b):
    t = h2.shape[0]
    dt0 = D_SSM + D_CONV
    q0 = dt0 + SSD_HEADS
    w_perm = jnp.concatenate(
        [w_in[:, :dt0], w_in[:, q0:], w_in[:, dt0:q0], jnp.zeros((D_MODEL, DT_PAD - SSD_HEADS), w_in.dtype)],
        axis=1).astype(BF16)
    z, xbc, q, f, i, g, dt = _inproj(h2, w_perm)
    y_ssd = _ssd(z, xbc, dt, conv_w, conv_b, dt_bias, a_log, d_skip, ssd_norm_w)
    y_hgrn = _hgrn(q, f, i, g, lower_bound, hgrn_norm_w)

    wr_t = w_router.astype(F32).T
    wr_hi = wr_t.astype(BF16)
    wr_lo = (wr_t - wr_hi.astype(F32)).astype(BF16)
    row = lambda v: v.reshape(1, -1).astype(F32)
    htt, base, idx, gates = _post(
        h2, y_ssd, y_hgrn, w_out.astype(BF16), row(ln1_g), row(ln1_b), wr_hi, wr_lo,
        router_bias.reshape(N_EXPERTS, 1).astype(F32),
        w_gate_s.astype(BF16), w_up_s.astype(BF16), w_down_s.astype(BF16))

    dest, bstart, nblk = _positions(idx)
    n_rows = _max_blocks(t) * EXPERT_ROWS
    d = jnp.transpose(dest, (1, 0, 2)).reshape(TOP_K, 1, t)
    slot = (d // EXPERT_ROWS) * (EXPERT_ROWS * ROW_HALVES) + d % EXPERT_ROWS
    slot = (slot + jnp.arange(ROW_HALVES, dtype=I32).reshape(1, ROW_HALVES, 1) * EXPERT_ROWS).reshape(-1)
    xs = _sc_dispatch(htt.reshape(ROW_HALVES * t, HALF_WORDS), slot, n_rows * ROW_HALVES)
    ys = _experts(bstart[:, 0], nblk[:, 0], xs, w_gate_e, w_up_e, w_down_e)
    rows = _sc_gather(ys, slot).reshape(TOP_K, ROW_HALVES, t, HALF_WORDS)
    return _combine(gates, base, row(ln2_g), row(ln2_b), rows)


def kernel(x, w_in, conv_w, conv_b, dt_bias, a_log, d_skip, ssd_norm_w, hgrn_lb_logits, hgrn_norm_w, w_out,
           ln1_g, ln1_b, w_router, router_bias, w_gate_e, w_up_e, w_down_e, w_gate_s, w_up_s, w_down_s,
           ln2_g, ln2_b):
    bsz, t, d = x.shape
    assert bsz == 1 and d == D_MODEL, "the recurrent mixers carry state across the flattened token axis"
    depth = w_in.shape[0]
    lower_bounds = jnp.cumsum(jax.nn.softmax(hgrn_lb_logits.astype(F32), axis=0), axis=0)
    h = x.reshape(bsz * t, d)
    for l in range(depth):
        h = _layer(h, w_in[l], conv_w[l], conv_b[l], dt_bias[l], a_log[l], d_skip[l], ssd_norm_w[l],
                   lower_bounds[l], hgrn_norm_w[l], w_out[l], ln1_g[l], ln1_b[l], w_router[l],
                   router_bias[l], w_gate_e[l], w_up_e[l], w_down_e[l], w_gate_s[l], w_up_s[l],
                   w_down_s[l], ln2_g[l], ln2_b[l])
    return h.reshape(bsz, t, d)
```

```python
import jax
import jax.numpy as jnp
from jax import lax
from jax.experimental import pallas as pl
from jax.experimental.pallas import tpu as pltpu
from jax.experimental.pallas import tpu_sc as plsc

F32 = jnp.float32
BF16 = jnp.bfloat16
I32 = jnp.int32
U32 = jnp.uint32

D_MODEL = 1024
D_SSM = 512
D_HGRN = 512
SSD_HEADS = 8
SSD_HEAD_DIM = 64
SSD_GROUPS = 2
SSD_STATE = 128
SSD_CONV = 4
SSD_CHUNK = 128
D_CONV = D_SSM + 2 * SSD_GROUPS * SSD_STATE
HGRN_HEADS = 4
HGRN_DK = 128
HGRN_CHUNK = 64
HGRN_SUB = 8
N_EXPERTS = 256
TOP_K = 8
N_EXPERT_GROUPS = 8
TOPK_GROUPS = 4
D_EXPERT = 256
ROUTED_SCALE = 2.5
ALPHA = 2.0 ** 0.25
LN_EPS = 1e-5
RMS_EPS = 1e-6

LANES = 128
SUBLANES = 8
ROW_TILES = D_MODEL // LANES
ROW_HALVES = 2
HALF_WORDS = D_MODEL // 2 // ROW_HALVES
SC_WINDOW = 128
DT_PAD = LANES
N_IN_PAD = D_SSM + D_CONV + 4 * D_HGRN + DT_PAD

TM_PROJ = 256
TM_TOK = 256
EXPERT_ROWS = 256
X_RING = 6
Y_RING = 4
NEG = -1e30


def _sigmoid(x):
    return 1.0 / (1.0 + jnp.exp(-x))


def _silu(x):
    return x * _sigmoid(x)


def _split3(x):
    hi = x.astype(BF16)
    r = x - hi.astype(F32)
    mid = r.astype(BF16)
    lo = (r - mid.astype(F32)).astype(BF16)
    return hi, mid, lo


def _dot(a, b):
    return jnp.dot(a, b, preferred_element_type=F32)


def _dot_nt(a, b):
    return lax.dot_general(a, b, (((1,), (1,)), ((), ())), preferred_element_type=F32)


def _dot_tn(a, b):
    return lax.dot_general(a, b, (((0,), (0,)), ((), ())), preferred_element_type=F32)


def _sel_dot(sel, x):
    hi, mid, lo = _split3(x)
    return _dot(sel, hi) + _dot(sel, mid) + _dot(sel, lo)


def _dot_sel(x, sel):
    hi, mid, lo = _split3(x)
    return _dot(hi, sel) + _dot(mid, sel) + _dot(lo, sel)


def _pack_pairs(lo, hi):
    lo_bits = pltpu.bitcast(lo.astype(BF16).astype(F32), U32) >> 16
    hi_bits = pltpu.bitcast(hi.astype(BF16).astype(F32), U32) & jnp.uint32(0xFFFF0000)
    return hi_bits | lo_bits


def _unpack_pairs(p):
    return pltpu.bitcast(p << 16, F32), pltpu.bitcast(p & jnp.uint32(0xFFFF0000), F32)


def _layer_norm(x, g, b):
    mu = jnp.mean(x, axis=-1, keepdims=True)
    xc = x - mu
    var = jnp.mean(xc * xc, axis=-1, keepdims=True)
    return xc * lax.rsqrt(var + LN_EPS) * g + b


def _inproj_kernel(x_ref, w_ref, z_ref, xbc_ref, q_ref, f_ref, i_ref, g_ref, dt_ref):
    xb = x_ref[...].astype(BF16)
    col = 0
    for ref in (z_ref, xbc_ref, q_ref, f_ref, i_ref, g_ref, dt_ref):
        n = ref.shape[-1]
        ref[...] = _dot(xb, w_ref[:, col:col + n])
        col += n


def _inproj(x2, w_perm):
    t = x2.shape[0]
    widths = (D_SSM, D_CONV, D_HGRN, D_HGRN, D_HGRN, D_HGRN, DT_PAD)
    return pl.pallas_call(
        _inproj_kernel,
        grid=(t // TM_PROJ,),
        in_specs=[pl.BlockSpec((TM_PROJ, D_MODEL), lambda i: (i, 0)),
                  pl.BlockSpec((D_MODEL, N_IN_PAD), lambda i: (0, 0))],
        out_specs=[pl.BlockSpec((TM_PROJ, n), lambda i: (i, 0)) for n in widths],
        out_shape=[jax.ShapeDtypeStruct((t, n), F32) for n in widths],
        compiler_params=pltpu.CompilerParams(dimension_semantics=("arbitrary",),
                                             vmem_limit_bytes=48 * 1024 * 1024),
        name="inproj",
    )(x2, w_perm)


def _ssd_kernel(z_ref, xbc_ref, dt_ref, cw_ref, cb_ref, dtb_ref, alog_ref, dskip_ref, nw_ref,
                y_ref, ext_ref, st_ref):
    L = SSD_CHUNK
    halo = SUBLANES

    @pl.when(pl.program_id(0) == 0)
    def _():
        ext_ref[0:halo, :] = jnp.zeros((halo, D_CONV), F32)
        st_ref[...] = jnp.zeros(st_ref.shape, F32)

    ext_ref[halo:halo + L, :] = xbc_ref[...]
    acc = jnp.broadcast_to(cb_ref[...], (L, D_CONV))
    for k in range(SSD_CONV):
        off = halo - (SSD_CONV - 1) + k
        acc = acc + cw_ref[k:k + 1, :] * ext_ref[off:off + L, :]
    ext_ref[0:halo, :] = xbc_ref[L - halo:L, :]
    u = _silu(acc)
    xs = u[:, :D_SSM]
    bm = u[:, D_SSM:D_SSM + SSD_GROUPS * SSD_STATE]
    cm = u[:, D_SSM + SSD_GROUPS * SSD_STATE:]

    draw = dt_ref[...] + dtb_ref[...]
    dt = jnp.maximum(draw, 0.0) + jnp.log(1.0 + jnp.exp(-jnp.abs(draw)))
    ad = dt * (-jnp.exp(alog_ref[...]))
    rows = lax.broadcasted_iota(I32, (L, L), 0)
    cols = lax.broadcasted_iota(I32, (L, L), 1)
    causal = rows >= cols
    a_cum = _sel_dot(causal.astype(BF16), ad)
    a_cum_t = a_cum.T

    hrow = lax.broadcasted_iota(I32, (LANES, D_SSM), 0)
    hcol = lax.broadcasted_iota(I32, (LANES, D_SSM), 1) // SSD_HEAD_DIM
    expand = (hrow == hcol).astype(BF16)
    dt_x = _dot_sel(dt, expand)
    acx = _dot_sel(a_cum, expand)
    last = acx[L - 1:L, :]
    ea_x = jnp.exp(acx)
    dec_x = jnp.exp(last - acx)
    elast_x = jnp.exp(last)

    xdt = xs * dt_x
    gw = SSD_HEADS // SSD_GROUPS * SSD_HEAD_DIM
    lane_head = lax.broadcasted_iota(I32, (L, gw), 1) // SSD_HEAD_DIM
    ys = []
    for g in range(SSD_GROUPS):
        bg = bm[:, g * SSD_STATE:(g + 1) * SSD_STATE]
        cg = cm[:, g * SSD_STATE:(g + 1) * SSD_STATE].astype(BF16)
        bg_t = bg.T.astype(BF16)
        gmat = _dot(cg, bg_t)
        xdt_g = xdt[:, g * gw:(g + 1) * gw]
        xdt_gb = xdt_g.astype(BF16)
        r_prev = st_ref[g]
        y_g = _dot(cg, r_prev.astype(BF16)) * ea_x[:, g * gw:(g + 1) * gw]
        new_s = _dot(bg_t, (xdt_g * dec_x[:, g * gw:(g + 1) * gw]).astype(BF16))
        st_ref[g] = r_prev * elast_x[:, g * gw:(g + 1) * gw] + new_s
        for j in range(SSD_HEADS // SSD_GROUPS):
            h = g * (SSD_HEADS // SSD_GROUPS) + j
            diff = a_cum[:, h:h + 1] - a_cum_t[h:h + 1, :]
            decay = jnp.exp(jnp.where(causal, diff, NEG))
            yd = _dot((gmat * decay).astype(BF16), xdt_gb)
            y_g = y_g + jnp.where(lane_head == j, yd, 0.0)
        ys.append(y_g)
    y = jnp.concatenate(ys, axis=-1) + xs * dskip_ref[...]
    y = y * _silu(z_ref[...])
    outs = []
    ng = D_SSM // SSD_GROUPS
    for g in range(SSD_GROUPS):
        yg = y[:, g * ng:(g + 1) * ng]
        ms = jnp.mean(yg * yg, axis=-1, keepdims=True)
        outs.append(yg * lax.rsqrt(ms + RMS_EPS))
    y_ref[...] = jnp.concatenate(outs, axis=-1) * nw_ref[...]


def _ssd(z, xbc, dt, conv_w, conv_b, dt_bias, a_log, d_skip, norm_w):
    t = z.shape[0]
    L = SSD_CHUNK
    pad = lambda v: jnp.pad(v.astype(F32), (0, LANES - v.shape[0])).reshape(1, LANES)
    full = lambda shape: pl.BlockSpec(shape, lambda c: (0,) * len(shape))
    return pl.pallas_call(
        _ssd_kernel,
        grid=(t // L,),
        in_specs=[pl.BlockSpec((L, D_SSM), lambda c: (c, 0)),
                  pl.BlockSpec((L, D_CONV), lambda c: (c, 0)),
                  pl.BlockSpec((L, DT_PAD), lambda c: (c, 0)),
                  full((SSD_CONV, D_CONV)), full((1, D_CONV)), full((1, LANES)), full((1, LANES)),
                  full((1, D_SSM)), full((1, D_SSM))],
        out_specs=pl.BlockSpec((L, D_SSM), lambda c: (c, 0)),
        out_shape=jax.ShapeDtypeStruct((t, D_SSM), F32),
        scratch_shapes=[pltpu.VMEM((L + SUBLANES, D_CONV), F32),
                        pltpu.VMEM((SSD_GROUPS, SSD_STATE, D_SSM // SSD_GROUPS), F32)],
        compiler_params=pltpu.CompilerParams(dimension_semantics=("arbitrary",)),
        name="ssd",
    )(z, xbc, dt, conv_w.astype(F32), conv_b.reshape(1, D_CONV).astype(F32), pad(dt_bias), pad(a_log),
      jnp.repeat(d_skip.astype(F32), SSD_HEAD_DIM).reshape(1, D_SSM), norm_w.reshape(1, D_SSM).astype(F32))


def _tile_bcast(x, r):
    n, d = x.shape
    x3 = x.reshape(n // SUBLANES, SUBLANES, d)
    return jnp.broadcast_to(x3[:, r:r + 1, :], x3.shape).reshape(n, d)


def _hgrn_kernel(q_ref, f_ref, i_ref, g_ref, lb_ref, nw_ref, o_ref, st_ref):
    C = HGRN_CHUNK
    S = HGRN_SUB
    nsub = C // S

    @pl.when(pl.program_id(0) == 0)
    def _():
        st_ref[...] = jnp.zeros(st_ref.shape, F32)

    lb = lb_ref[...]
    fg = lb + (1.0 - lb) * _sigmoid(f_ref[...])
    kk = 1.0 - fg
    qs = _silu(q_ref[...])
    v = i_ref[...]
    rows = lax.broadcasted_iota(I32, (C, C), 0)
    cols = lax.broadcasted_iota(I32, (C, C), 1)
    cum = _sel_dot((rows >= cols).astype(BF16), jnp.log(fg))
    last = cum[C - 1:C, :]

    trow = lax.broadcasted_iota(I32, (C, D_HGRN), 0)
    tmod = trow % S
    tblk = trow // S

    xr = []
    for r in range(S):
        e = jnp.exp(jnp.where(tmod >= r, cum - _tile_bcast(cum, r), NEG))
        xr.append(qs * _tile_bcast(kk, r) * e)
    kbe = kk * jnp.exp(_tile_bcast(cum, S - 1) - cum)
    aj, bj = [], []
    for j in range(nsub - 1):
        ce = cum[j * S + S - 1:j * S + S, :]
        aj.append(qs * jnp.exp(jnp.where(tblk > j, cum - ce, NEG)))
        bj.append(jnp.where(tblk == j, kbe, 0.0))
    qd = (qs * jnp.exp(cum)).astype(BF16)
    kd = (kk * jnp.exp(last - cum)).astype(BF16)
    elast = jnp.exp(last)

    att_col = lax.broadcasted_iota(I32, (C, C), 1)
    att_rblk = lax.broadcasted_iota(I32, (C, C), 0) // S * S
    outs = []
    for h in range(HGRN_HEADS):
        blk = slice(h * HGRN_DK, (h + 1) * HGRN_DK)
        a_cat = jnp.concatenate([a[:, blk] for a in aj], axis=-1).astype(BF16)
        b_cat = jnp.concatenate([b[:, blk] for b in bj], axis=-1).astype(BF16)
        att = _dot_nt(a_cat, b_cat)
        for r in range(S):
            rs = jnp.sum(xr[r][:, blk], axis=-1, keepdims=True)
            att = jnp.where(att_col == att_rblk + r, rs, att)
        vb = v[:, blk].astype(BF16)
        s_prev = st_ref[h]
        o = _dot_nt(qd[:, blk], s_prev.astype(BF16)) + _dot(att.astype(BF16), vb)
        st_ref[h] = s_prev * elast[:, blk] + _dot_tn(vb, kd[:, blk])
        ms = jnp.mean(o * o, axis=-1, keepdims=True)
        outs.append(o * lax.rsqrt(ms + RMS_EPS))
    o_ref[...] = jnp.concatenate(outs, axis=-1) * nw_ref[...] * _silu(g_ref[...])


def _hgrn(q, f, i, g, lower_bound, norm_w):
    t = q.shape[0]
    C = HGRN_CHUNK
    tok = pl.BlockSpec((C, D_HGRN), lambda c: (c, 0))
    vec = pl.BlockSpec((1, D_HGRN), lambda c: (0, 0))
    return pl.pallas_call(
        _hgrn_kernel,
        grid=(t // C,),
        in_specs=[tok, tok, tok, tok, vec, vec],
        out_specs=tok,
        out_shape=jax.ShapeDtypeStruct((t, D_HGRN), F32),
        scratch_shapes=[pltpu.VMEM((HGRN_HEADS, D_HGRN // HGRN_HEADS, HGRN_DK), F32)],
        compiler_params=pltpu.CompilerParams(dimension_semantics=("arbitrary",)),
        name="hgrn",
    )(q, f, i, g, lower_bound.reshape(1, D_HGRN).astype(F32), norm_w.reshape(1, D_HGRN).astype(F32))


def _post_kernel(x_ref, ys_ref, yh_ref, wo_ref, g1_ref, b1_ref, wrh_ref, wrl_ref, rb_ref,
                 wgs_ref, wus_ref, wds_ref, htt_ref, base_ref, idx_ref, gate_ref):
    tm = x_ref.shape[0]
    mix = (_dot(ys_ref[...].astype(BF16), wo_ref[0:D_SSM, :])
           + _dot(yh_ref[...].astype(BF16), wo_ref[D_SSM:, :]))
    h1 = _layer_norm(ALPHA * x_ref[...] + mix, g1_ref[...], b1_ref[...])
    packed = _pack_pairs(h1[:, :D_MODEL // 2], h1[:, D_MODEL // 2:])
    for hh in range(ROW_HALVES):
        htt_ref[hh] = packed[:, hh * HALF_WORDS:(hh + 1) * HALF_WORDS]
    hb = h1.astype(BF16)
    hid = _silu(_dot(hb, wgs_ref[...])) * _dot(hb, wus_ref[...])
    base_ref[...] = ALPHA * h1 + _dot(hid.astype(BF16), wds_ref[...])

    hlo = (h1 - hb.astype(F32)).astype(BF16)
    logits = _dot_nt(wrh_ref[...], hb) + _dot_nt(wrh_ref[...], hlo) + _dot_nt(wrl_ref[...], hb)
    scores = _sigmoid(logits)
    biased = scores + rb_ref[...]
    per_group = N_EXPERTS // N_EXPERT_GROUPS
    eidx = lax.broadcasted_iota(I32, (N_EXPERTS, tm), 0)
    big = jnp.int32(1 << 20)
    gsc = []
    bidx = lax.broadcasted_iota(I32, (per_group, tm), 0)
    for gi in range(N_EXPERT_GROUPS):
        blk = biased[gi * per_group:(gi + 1) * per_group, :]
        m1 = jnp.max(blk, axis=0, keepdims=True)
        i1 = jnp.min(jnp.where(blk == m1, bidx, big), axis=0, keepdims=True)
        m2 = jnp.max(jnp.where(bidx == i1, NEG, blk), axis=0, keepdims=True)
        gsc.append(m1 + m2)
    cur = jnp.concatenate(gsc, axis=0)
    gidx = lax.broadcasted_iota(I32, (N_EXPERT_GROUPS, tm), 0)
    gsel = jnp.zeros((N_EXPERT_GROUPS, tm), F32)
    for _ in range(TOPK_GROUPS):
        m = jnp.max(cur, axis=0, keepdims=True)
        i = jnp.min(jnp.where(cur == m, gidx, big), axis=0, keepdims=True)
        hit = gidx == i
        gsel = jnp.where(hit, 1.0, gsel)
        cur = jnp.where(hit, NEG, cur)
    emask = jnp.concatenate(
        [jnp.broadcast_to(gsel[gi:gi + 1, :], (per_group, tm)) for gi in range(N_EXPERT_GROUPS)], axis=0)
    masked = jnp.where(emask > 0.0, biased, NEG)
    idx_rows, gate_rows = [], []
    for _ in range(TOP_K):
        m = jnp.max(masked, axis=0, keepdims=True)
        i = jnp.min(jnp.where(masked == m, eidx, big), axis=0, keepdims=True)
        hit = eidx == i
        idx_rows.append(i)
        gate_rows.append(jnp.sum(jnp.where(hit, scores, 0.0), axis=0, keepdims=True))
        masked = jnp.where(hit, NEG, masked)
    gates = jnp.concatenate(gate_rows, axis=0)
    gates = gates / jnp.sum(gates, axis=0, keepdims=True) * ROUTED_SCALE
    idx_ref[0] = jnp.concatenate(idx_rows, axis=0)
    gate_ref[0] = gates


def _post(x2, y_ssd, y_hgrn, wo, g1, b1, wr_hi, wr_lo, rbias, wgs, wus, wds):
    t = x2.shape[0]
    tm = TM_TOK
    nt = t // tm
    full = lambda shape: pl.BlockSpec(shape, lambda i: (0,) * len(shape))
    return pl.pallas_call(
        _post_kernel,
        grid=(nt,),
        in_specs=[pl.BlockSpec((tm, D_MODEL), lambda i: (i, 0)),
                  pl.BlockSpec((tm, D_SSM), lambda i: (i, 0)),
                  pl.BlockSpec((tm, D_HGRN), lambda i: (i, 0)),
                  full((D_MODEL, D_MODEL)), full((1, D_MODEL)), full((1, D_MODEL)),
                  full((N_EXPERTS, D_MODEL)), full((N_EXPERTS, D_MODEL)), full((N_EXPERTS, 1)),
                  full((D_MODEL, D_EXPERT)), full((D_MODEL, D_EXPERT)), full((D_EXPERT, D_MODEL))],
        out_specs=[pl.BlockSpec((ROW_HALVES, tm, HALF_WORDS), lambda i: (0, i, 0)),
                   pl.BlockSpec((tm, D_MODEL), lambda i: (i, 0)),
                   pl.BlockSpec((1, TOP_K, tm), lambda i: (i, 0, 0)),
                   pl.BlockSpec((1, TOP_K, tm), lambda i: (i, 0, 0))],
        out_shape=[jax.ShapeDtypeStruct((ROW_HALVES, t, HALF_WORDS), U32),
                   jax.ShapeDtypeStruct((t, D_MODEL), F32),
                   jax.ShapeDtypeStruct((nt, TOP_K, tm), I32),
                   jax.ShapeDtypeStruct((nt, TOP_K, tm), F32)],
        compiler_params=pltpu.CompilerParams(dimension_semantics=("arbitrary",),
                                             vmem_limit_bytes=48 * 1024 * 1024),
        name="post",
    )(x2, y_ssd, y_hgrn, wo, g1, b1, wr_hi, wr_lo, rbias, wgs, wus, wds)


def _max_blocks(t):
    return (t * TOP_K + N_EXPERTS * (EXPERT_ROWS - 1)) // EXPERT_ROWS


def _pos_kernel(idx_ref, dest_ref, bstart_ref, nblk_ref, cnt_ref, start_ref):
    phase = pl.program_id(0)
    i = pl.program_id(1)
    tm = idx_ref.shape[-1]
    idx = idx_ref[0]
    eidx = lax.broadcasted_iota(I32, (N_EXPERTS, tm), 0)
    sel = [eidx == idx[k:k + 1, :] for k in range(TOP_K)]
    onehot = sel[0]
    for k in range(1, TOP_K):
        onehot = jnp.logical_or(onehot, sel[k])
    mt = jnp.where(onehot, 1.0, 0.0).astype(BF16)
    tile_cnt = _dot(mt, jnp.ones((tm, LANES), BF16))

    @pl.when(jnp.logical_and(phase == 0, i == 0))
    def _():
        cnt_ref[...] = jnp.zeros(cnt_ref.shape, F32)

    @pl.when(phase == 0)
    def _():
        cnt_ref[...] += tile_cnt

    @pl.when(jnp.logical_and(phase == 1, i == 0))
    def _():
        nb = jnp.floor((cnt_ref[...] + (EXPERT_ROWS - 1)) * (1.0 / EXPERT_ROWS))
        r = lax.broadcasted_iota(I32, (N_EXPERTS, N_EXPERTS), 0)
        c = lax.broadcasted_iota(I32, (N_EXPERTS, N_EXPERTS), 1)
        end = _dot((r >= c).astype(BF16), nb.astype(BF16))
        start_ref[...] = (end - nb) * EXPERT_ROWS
        cnt_ref[...] = jnp.zeros(cnt_ref.shape, F32)
        bstart_ref[...] = (end - nb).astype(I32)
        nblk_ref[...] = nb.astype(I32)

    @pl.when(phase == 1)
    def _():
        r = lax.broadcasted_iota(I32, (tm, tm), 0)
        c = lax.broadcasted_iota(I32, (tm, tm), 1)
        before = _dot(mt, (r < c).astype(BF16))
        slot = start_ref[:, 0:1] + cnt_ref[:, 0:1] + before
        rows = [jnp.sum(jnp.where(sel[k], slot, 0.0), axis=0, keepdims=True) for k in range(TOP_K)]
        dest_ref[0] = jnp.concatenate(rows, axis=0).astype(I32)
        cnt_ref[...] += tile_cnt


def _positions(idx):
    nt, _, tm = idx.shape
    return pl.pallas_call(
        _pos_kernel,
        grid=(2, nt),
        in_specs=[pl.BlockSpec((1, TOP_K, tm), lambda p, i: (i, 0, 0))],
        out_specs=[pl.BlockSpec((1, TOP_K, tm), lambda p, i: (i * p, 0, 0)),
                   pl.BlockSpec((N_EXPERTS, LANES), lambda p, i: (0, 0)),
                   pl.BlockSpec((N_EXPERTS, LANES), lambda p, i: (0, 0))],
        out_shape=[jax.ShapeDtypeStruct((nt, TOP_K, tm), I32),
                   jax.ShapeDtypeStruct((N_EXPERTS, LANES), I32),
                   jax.ShapeDtypeStruct((N_EXPERTS, LANES), I32)],
        scratch_shapes=[pltpu.VMEM((N_EXPERTS, LANES), F32), pltpu.VMEM((N_EXPERTS, LANES), F32)],
        compiler_params=pltpu.CompilerParams(dimension_semantics=("arbitrary", "arbitrary")),
        name="positions",
    )(idx)


def _sc_mesh():
    return plsc.VectorSubcoreMesh(core_axis_name="core", subcore_axis_name="subcore")


def _sc_dispatch(src, index, n_out):
    n = index.shape[0]
    src_blocks = src.shape[0] // SC_WINDOW
    index = index.reshape(1, n)

    @pl.kernel(out_type=jax.ShapeDtypeStruct((n_out, HALF_WORDS), src.dtype), mesh=_sc_mesh(), name="sc_dispatch")
    def scatter(src_hbm, idx_hbm, out_hbm):
        def body(rows_vmem, idx_vmem):
            pltpu.sync_copy(rows_vmem, out_hbm.at[idx_vmem.at[0]])

        pltpu.emit_pipeline(
            body, grid=(n // SC_WINDOW,),
            in_specs=[pl.BlockSpec((SC_WINDOW, HALF_WORDS), index_map=lambda i: (i % src_blocks, 0)),
                      pl.BlockSpec((1, SC_WINDOW), index_map=lambda i: (0, i))],
            out_specs=[],
            core_axis_name=("core", "subcore"), dimension_semantics=(pltpu.PARALLEL,))(src_hbm, idx_hbm)

    return scatter(src, index)


def _sc_gather(src, index):
    n = index.shape[0]
    index = index.reshape(1, n)

    @pl.kernel(out_type=jax.ShapeDtypeStruct((n, HALF_WORDS), src.dtype), mesh=_sc_mesh(), name="sc_gather")
    def gather(src_hbm, idx_hbm, out_hbm):
        def body(idx_vmem, rows_vmem):
            pltpu.sync_copy(src_hbm.at[idx_vmem.at[0]], rows_vmem)

        pltpu.emit_pipeline(
            body, grid=(n // SC_WINDOW,),
            in_specs=[pl.BlockSpec((1, SC_WINDOW), index_map=lambda i: (0, i))],
            out_specs=[pl.BlockSpec((SC_WINDOW, HALF_WORDS), index_map=lambda i: (i, 0))],
            core_axis_name=("core", "subcore"), dimension_semantics=(pltpu.PARALLEL,))(idx_hbm, out_hbm)

    return gather(src, index)


def _expert_kernel(bstart_ref, nblk_ref, xs_hbm, wg_ref, wu_ref, wd_ref, ys_hbm,
                   xbuf, ybuf, wgb_ref, wub_ref, wdb_ref, xsem, ysem, ypend):
    e = pl.program_id(0)
    last = pl.num_programs(0) - 1
    n = nblk_ref[e]
    b0 = bstart_ref[e]
    total = bstart_ref[last] + nblk_ref[last]
    blk_rows = EXPERT_ROWS * ROW_HALVES
    xdepth = xbuf.shape[0]
    ydepth = ybuf.shape[0]

    def hbm_block(ref, b):
        return ref.at[pl.ds(pl.multiple_of(b * blk_rows, blk_rows), blk_rows)]

    def x_copy(b):
        slot = b % xdepth
        return pltpu.make_async_copy(hbm_block(xs_hbm, b), xbuf.at[slot], xsem.at[slot])

    def y_copy(b, slot):
        return pltpu.make_async_copy(ybuf.at[slot], hbm_block(ys_hbm, b), ysem.at[slot])

    @pl.when(e == 0)
    def _():
        for s in range(ydepth):
            ypend[s] = 0
        for b in range(xdepth - 1):
            @pl.when(b < total)
            def _():
                x_copy(b).start()

    @pl.when(n > 0)
    def _():
        wgb_ref[...] = wg_ref[0].astype(BF16)
        wub_ref[...] = wu_ref[0].astype(BF16)
        wdb_ref[...] = wd_ref[0].astype(BF16)

        def block(j, carry):
            b = b0 + j
            x_copy(b).wait()
            ahead = b + (xdepth - 1)

            @pl.when(ahead < total)
            def _():
                x_copy(ahead).start()

            xslot = b % xdepth
            yslot = b % ydepth

            @pl.when(ypend[yslot] == 1)
            def _():
                y_copy(b, yslot).wait()

            halves = [_unpack_pairs(xbuf[xslot, hh * EXPERT_ROWS:(hh + 1) * EXPERT_ROWS, :])
                      for hh in range(ROW_HALVES)]
            xb = jnp.concatenate([lo for lo, _ in halves] + [hi for _, hi in halves], axis=-1).astype(BF16)
            hid = _silu(_dot(xb, wgb_ref[...])) * _dot(xb, wub_ref[...])
            yb = _dot(hid.astype(BF16), wdb_ref[...])
            packed = _pack_pairs(yb[:, :D_MODEL // 2], yb[:, D_MODEL // 2:])
            for hh in range(ROW_HALVES):
                ybuf[yslot, hh * EXPERT_ROWS:(hh + 1) * EXPERT_ROWS, :] = (
                    packed[:, hh * HALF_WORDS:(hh + 1) * HALF_WORDS])
            y_copy(b, yslot).start()
            ypend[yslot] = 1
            return carry

        lax.fori_loop(0, n, block, 0)

    @pl.when(e == last)
    def _():
        for s in range(ydepth):
            @pl.when(ypend[s] == 1)
            def _():
                y_copy(b0, s).wait()


def _experts(bstart, nblk, xs, w_gate, w_up, w_down):
    blk_shape = (EXPERT_ROWS * ROW_HALVES, HALF_WORDS)
    wspec = lambda shape: pl.BlockSpec((1,) + shape, lambda e, bs, nb: (e, 0, 0))
    grid_spec = pltpu.PrefetchScalarGridSpec(
        num_scalar_prefetch=2,
        grid=(N_EXPERTS,),
        in_specs=[pl.BlockSpec(memory_space=pl.ANY),
                  wspec((D_MODEL, D_EXPERT)), wspec((D_MODEL, D_EXPERT)), wspec((D_EXPERT, D_MODEL))],
        out_specs=pl.BlockSpec(memory_space=pl.ANY),
        scratch_shapes=[pltpu.VMEM((X_RING,) + blk_shape, U32), pltpu.VMEM((Y_RING,) + blk_shape, U32),
                        pltpu.VMEM((D_MODEL, D_EXPERT), BF16), pltpu.VMEM((D_MODEL, D_EXPERT), BF16),
                        pltpu.VMEM((D_EXPERT, D_MODEL), BF16),
                        pltpu.SemaphoreType.DMA((X_RING,)), pltpu.SemaphoreType.DMA((Y_RING,)),
                        pltpu.SMEM((Y_RING,), I32)],
    )
    return pl.pallas_call(
        _expert_kernel,
        grid_spec=grid_spec,
        out_shape=jax.ShapeDtypeStruct(xs.shape, U32),
        compiler_params=pltpu.CompilerParams(dimension_semantics=("arbitrary",), has_side_effects=True),
        name="experts",
    )(bstart, nblk, xs, w_gate, w_up, w_down)


def _combine_kernel(gate_ref, base_ref, g2_ref, b2_ref, rows_ref, o_ref):
    tm = base_ref.shape[0]
    gates_t = jnp.concatenate([gate_ref[0], jnp.zeros((LANES - TOP_K, tm), F32)], axis=0).T
    half = D_MODEL // 2
    lo_acc = [base_ref[:, hh * HALF_WORDS:(hh + 1) * HALF_WORDS] for hh in range(ROW_HALVES)]
    hi_acc = [base_ref[:, half + hh * HALF_WORDS:half + (hh + 1) * HALF_WORDS] for hh in range(ROW_HALVES)]
    for k in range(TOP_K):
        gk = gates_t[:, k:k + 1]
        for hh in range(ROW_HALVES):
            lo, hi = _unpack_pairs(rows_ref[k, hh])
            lo_acc[hh] = lo_acc[hh] + gk * lo
            hi_acc[hh] = hi_acc[hh] + gk * hi
    o_ref[...] = _layer_norm(jnp.concatenate(lo_acc + hi_acc, axis=-1), g2_ref[...], b2_ref[...])


def _combine(gates, base, g2, b2, rows):
    t = base.shape[0]
    tm = gates.shape[-1]
    return pl.pallas_call(
        _combine_kernel,
        grid=(t // tm,),
        in_specs=[pl.BlockSpec((1, TOP_K, tm), lambda i: (i, 0, 0)),
                  pl.BlockSpec((tm, D_MODEL), lambda i: (i, 0)),
                  pl.BlockSpec((1, D_MODEL), lambda i: (0, 0)),
                  pl.BlockSpec((1, D_MODEL), lambda i: (0, 0)),
                  pl.BlockSpec((TOP_K, ROW_HALVES, tm, HALF_WORDS), lambda i: (0, 0, i, 0))],
        out_specs=pl.BlockSpec((tm, D_MODEL), lambda i: (i, 0)),
        out_shape=jax.ShapeDtypeStruct((t, D_MODEL), F32),
        compiler_params=pltpu.CompilerParams(dimension_semantics=("arbitrary",),
                                             vmem_limit_bytes=48 * 1024 * 1024),
        name="combine",
    )(gates, base, g2, b2, rows)


def _layer(h2, w_in, conv_w, conv_b, dt_bias, a_log, d_skip, ssd_norm_w, lower_bound, hgrn_norm_w, w_out,
           ln1_g, ln1_b, w_router, router_bias, w_gate_e, w_up_e, w_down_e, w_gate_s, w_up_s, w_down_s,
           ln2_g, ln2_b):
    t = h2.shape[0]
    dt0 = D_SSM + D_CONV
    q0 = dt0 + SSD_HEADS
    w_perm = jnp.concatenate(
        [w_in[:, :dt0], w_in[:, q0:], w_in[:, dt0:q0], jnp.zeros((D_MODEL, DT_PAD - SSD_HEADS), w_in.dtype)],
        axis=1).astype(BF16)
    z, xbc, q, f, i, g, dt = _inproj(h2, w_perm)
    y_ssd = _ssd(z, xbc, dt, conv_w, conv_b, dt_bias, a_log, d_skip, ssd_norm_w)
    y_hgrn = _hgrn(q, f, i, g, lower_bound, hgrn_norm_w)

    wr_t = w_router.astype(F32).T
    wr_hi = wr_t.astype(BF16)
    wr_lo = (wr_t - wr_hi.astype(F32)).astype(BF16)
    row = lambda v: v.reshape(1, -1).astype(F32)
    htt, base, idx, gates = _post(
        h2, y_ssd, y_hgrn, w_out.astype(BF16), row(ln1_g), row(ln1_b), wr_hi, wr_lo,
        router_bias.reshape(N_EXPERTS, 1).astype(F32),
        w_gate_s.astype(BF16), w_up_s.astype(BF16), w_down_s.astype(BF16))

    dest, bstart, nblk = _positions(idx)
    n_rows = _max_blocks(t) * EXPERT_ROWS
    d = jnp.transpose(dest, (1, 0, 2)).reshape(TOP_K, 1, t)
    slot = (d // EXPERT_ROWS) * (EXPERT_ROWS * ROW_HALVES) + d % EXPERT_ROWS
    slot = (slot + jnp.arange(ROW_HALVES, dtype=I32).reshape(1, ROW_HALVES, 1) * EXPERT_ROWS).reshape(-1)
    xs = _sc_dispatch(htt.reshape(ROW_HALVES * t, HALF_WORDS), slot, n_rows * ROW_HALVES)
    ys = _experts(bstart[:, 0], nblk[:, 0], xs, w_gate_e, w_up_e, w_down_e)
    rows = _sc_gather(ys, slot).reshape(TOP_K, ROW_HALVES, t, HALF_WORDS)
    return _combine(gates, base, row(ln2_g), row(ln2_b), rows)


def kernel(x, w_in, conv_w, conv_b, dt_bias, a_log, d_skip, ssd_norm_w, hgrn_lb_logits, hgrn_norm_w, w_out,
           ln1_g, ln1_b, w_router, router_bias, w_gate_e, w_up_e, w_down_e, w_gate_s, w_up_s, w_down_s,
           ln2_g, ln2_b):
    bsz, t, d = x.shape
    assert bsz == 1 and d == D_MODEL, "the recurrent mixers carry state across the flattened token axis"
    depth = w_in.shape[0]
    lower_bounds = jnp.cumsum(jax.nn.softmax(hgrn_lb_logits.astype(F32), axis=0), axis=0)
    h = x.reshape(bsz * t, d)
    for l in range(depth):
        h = _layer(h, w_in[l], conv_w[l], conv_b[l], dt_bias[l], a_log[l], d_skip[l], ssd_norm_w[l],
                   lower_bounds[l], hgrn_norm_w[l], w_out[l], ln1_g[l], ln1_b[l], w_router[l],
                   router_bias[l], w_gate_e[l], w_up_e[l], w_down_e[l], w_gate_s[l], w_up_s[l],
                   w_down_s[l], ln2_g[l], ln2_b[l])
    return h.reshape(bsz, t, d)
```

```python
import jax
import jax.numpy as jnp
from jax import lax
from jax.experimental import pallas as pl
from jax.experimental.pallas import tpu as pltpu
from jax.experimental.pallas import tpu_sc as plsc

F32 = jnp.float32
BF16 = jnp.bfloat16
I32 = jnp.int32
U32 = jnp.uint32

D_MODEL = 1024
D_SSM = 512
D_HGRN = 512
SSD_HEADS = 8
SSD_HEAD_DIM = 64
SSD_GROUPS = 2
SSD_STATE = 128
SSD_CONV = 4
SSD_CHUNK = 128
D_CONV = D_SSM + 2 * SSD_GROUPS * SSD_STATE
HGRN_HEADS = 4
HGRN_DK = 128
HGRN_CHUNK = 64
HGRN_SUB = 8
N_EXPERTS = 256
TOP_K = 8
N_EXPERT_GROUPS = 8
TOPK_GROUPS = 4
D_EXPERT = 256
ROUTED_SCALE = 2.5
ALPHA = 2.0 ** 0.25
LN_EPS = 1e-5
RMS_EPS = 1e-6

LANES = 128
SUBLANES = 8
ROW_TILES = D_MODEL // LANES
ROW_HALVES = 2
HALF_WORDS = D_MODEL // 2 // ROW_HALVES
SC_WINDOW = 128
DT_PAD = LANES
N_IN_PAD = D_SSM + D_CONV + 4 * D_HGRN + DT_PAD

TM_PROJ = 256
TM_TOK = 256
EXPERT_ROWS = 256
MAX_CHUNK = 3
X_RING = 8
Y_RING = 4
NEG = -1e30


def _sigmoid(x):
    return 1.0 / (1.0 + jnp.exp(-x))


def _silu(x):
    return x * _sigmoid(x)


def _split3(x):
    hi = x.astype(BF16)
    r = x - hi.astype(F32)
    mid = r.astype(BF16)
    lo = (r - mid.astype(F32)).astype(BF16)
    return hi, mid, lo


def _dot(a, b):
    return jnp.dot(a, b, preferred_element_type=F32)


def _dot_nt(a, b):
    return lax.dot_general(a, b, (((1,), (1,)), ((), ())), preferred_element_type=F32)


def _dot_tn(a, b):
    return lax.dot_general(a, b, (((0,), (0,)), ((), ())), preferred_element_type=F32)


def _sel_dot(sel, x):
    hi, mid, lo = _split3(x)
    return _dot(sel, hi) + _dot(sel, mid) + _dot(sel, lo)


def _dot_sel(x, sel):
    hi, mid, lo = _split3(x)
    return _dot(hi, sel) + _dot(mid, sel) + _dot(lo, sel)


def _pack_pairs(lo, hi):
    lo_bits = pltpu.bitcast(lo.astype(BF16).astype(F32), U32) >> 16
    hi_bits = pltpu.bitcast(hi.astype(BF16).astype(F32), U32) & jnp.uint32(0xFFFF0000)
    return hi_bits | lo_bits


def _unpack_pairs(p):
    return pltpu.bitcast(p << 16, F32), pltpu.bitcast(p & jnp.uint32(0xFFFF0000), F32)


def _layer_norm(x, g, b):
    mu = jnp.mean(x, axis=-1, keepdims=True)
    xc = x - mu
    var = jnp.mean(xc * xc, axis=-1, keepdims=True)
    return xc * lax.rsqrt(var + LN_EPS) * g + b


def _inproj_kernel(x_ref, w_ref, z_ref, xbc_ref, q_ref, f_ref, i_ref, g_ref, dt_ref):
    xb = x_ref[...].astype(BF16)
    col = 0
    for ref in (z_ref, xbc_ref, q_ref, f_ref, i_ref, g_ref, dt_ref):
        n = ref.shape[-1]
        ref[...] = _dot(xb, w_ref[:, col:col + n])
        col += n


def _inproj(x2, w_perm):
    t = x2.shape[0]
    widths = (D_SSM, D_CONV, D_HGRN, D_HGRN, D_HGRN, D_HGRN, DT_PAD)
    return pl.pallas_call(
        _inproj_kernel,
        grid=(t // TM_PROJ,),
        in_specs=[pl.BlockSpec((TM_PROJ, D_MODEL), lambda i: (i, 0)),
                  pl.BlockSpec((D_MODEL, N_IN_PAD), lambda i: (0, 0))],
        out_specs=[pl.BlockSpec((TM_PROJ, n), lambda i: (i, 0)) for n in widths],
        out_shape=[jax.ShapeDtypeStruct((t, n), F32) for n in widths],
        compiler_params=pltpu.CompilerParams(dimension_semantics=("arbitrary",),
                                             vmem_limit_bytes=48 * 1024 * 1024),
        name="inproj",
    )(x2, w_perm)


def _ssd_kernel(z_ref, xbc_ref, dt_ref, cw_ref, cb_ref, dtb_ref, alog_ref, dskip_ref, nw_ref,
                y_ref, ext_ref, st_ref):
    L = SSD_CHUNK
    halo = SUBLANES

    @pl.when(pl.program_id(0) == 0)
    def _():
        ext_ref[0:halo, :] = jnp.zeros((halo, D_CONV), F32)
        st_ref[...] = jnp.zeros(st_ref.shape, F32)

    ext_ref[halo:halo + L, :] = xbc_ref[...]
    acc = jnp.broadcast_to(cb_ref[...], (L, D_CONV))
    for k in range(SSD_CONV):
        off = halo - (SSD_CONV - 1) + k
        acc = acc + cw_ref[k:k + 1, :] * ext_ref[off:off + L, :]
    ext_ref[0:halo, :] = xbc_ref[L - halo:L, :]
    u = _silu(acc)
    xs = u[:, :D_SSM]
    bm = u[:, D_SSM:D_SSM + SSD_GROUPS * SSD_STATE]
    cm = u[:, D_SSM + SSD_GROUPS * SSD_STATE:]

    draw = dt_ref[...] + dtb_ref[...]
    dt = jnp.maximum(draw, 0.0) + jnp.log(1.0 + jnp.exp(-jnp.abs(draw)))
    ad = dt * (-jnp.exp(alog_ref[...]))
    rows = lax.broadcasted_iota(I32, (L, L), 0)
    cols = lax.broadcasted_iota(I32, (L, L), 1)
    causal = rows >= cols
    a_cum = _sel_dot(causal.astype(BF16), ad)
    a_cum_t = a_cum.T

    hrow = lax.broadcasted_iota(I32, (LANES, D_SSM), 0)
    hcol = lax.broadcasted_iota(I32, (LANES, D_SSM), 1) // SSD_HEAD_DIM
    expand = (hrow == hcol).astype(BF16)
    dt_x = _dot_sel(dt, expand)
    acx = _dot_sel(a_cum, expand)
    last = acx[L - 1:L, :]
    ea_x = jnp.exp(acx)
    dec_x = jnp.exp(last - acx)
    elast_x = jnp.exp(last)

    xdt = xs * dt_x
    gw = SSD_HEADS // SSD_GROUPS * SSD_HEAD_DIM
    lane_head = lax.broadcasted_iota(I32, (L, gw), 1) // SSD_HEAD_DIM
    ys = []
    for g in range(SSD_GROUPS):
        bg = bm[:, g * SSD_STATE:(g + 1) * SSD_STATE]
        cg = cm[:, g * SSD_STATE:(g + 1) * SSD_STATE].astype(BF16)
        bg_t = bg.T.astype(BF16)
        gmat = _dot(cg, bg_t)
        xdt_g = xdt[:, g * gw:(g + 1) * gw]
        xdt_gb = xdt_g.astype(BF16)
        r_prev = st_ref[g]
        y_g = _dot(cg, r_prev.astype(BF16)) * ea_x[:, g * gw:(g + 1) * gw]
        new_s = _dot(bg_t, (xdt_g * dec_x[:, g * gw:(g + 1) * gw]).astype(BF16))
        st_ref[g] = r_prev * elast_x[:, g * gw:(g + 1) * gw] + new_s
        for j in range(SSD_HEADS // SSD_GROUPS):
            h = g * (SSD_HEADS // SSD_GROUPS) + j
            diff = a_cum[:, h:h + 1] - a_cum_t[h:h + 1, :]
            decay = jnp.exp(jnp.where(causal, diff, NEG))
            yd = _dot((gmat * decay).astype(BF16), xdt_gb)
            y_g = y_g + jnp.where(lane_head == j, yd, 0.0)
        ys.append(y_g)
    y = jnp.concatenate(ys, axis=-1) + xs * dskip_ref[...]
    y = y * _silu(z_ref[...])
    outs = []
    ng = D_SSM // SSD_GROUPS
    for g in range(SSD_GROUPS):
        yg = y[:, g * ng:(g + 1) * ng]
        ms = jnp.mean(yg * yg, axis=-1, keepdims=True)
        outs.append(yg * lax.rsqrt(ms + RMS_EPS))
    y_ref[...] = jnp.concatenate(outs, axis=-1) * nw_ref[...]


def _ssd(z, xbc, dt, conv_w, conv_b, dt_bias, a_log, d_skip, norm_w):
    t = z.shape[0]
    L = SSD_CHUNK
    pad = lambda v: jnp.pad(v.astype(F32), (0, LANES - v.shape[0])).reshape(1, LANES)
    full = lambda shape: pl.BlockSpec(shape, lambda c: (0,) * len(shape))
    return pl.pallas_call(
        _ssd_kernel,
        grid=(t // L,),
        in_specs=[pl.BlockSpec((L, D_SSM), lambda c: (c, 0)),
                  pl.BlockSpec((L, D_CONV), lambda c: (c, 0)),
                  pl.BlockSpec((L, DT_PAD), lambda c: (c, 0)),
                  full((SSD_CONV, D_CONV)), full((1, D_CONV)), full((1, LANES)), full((1, LANES)),
                  full((1, D_SSM)), full((1, D_SSM))],
        out_specs=pl.BlockSpec((L, D_SSM), lambda c: (c, 0)),
        out_shape=jax.ShapeDtypeStruct((t, D_SSM), F32),
        scratch_shapes=[pltpu.VMEM((L + SUBLANES, D_CONV), F32),
                        pltpu.VMEM((SSD_GROUPS, SSD_STATE, D_SSM // SSD_GROUPS), F32)],
        compiler_params=pltpu.CompilerParams(dimension_semantics=("arbitrary",)),
        name="ssd",
    )(z, xbc, dt, conv_w.astype(F32), conv_b.reshape(1, D_CONV).astype(F32), pad(dt_bias), pad(a_log),
      jnp.repeat(d_skip.astype(F32), SSD_HEAD_DIM).reshape(1, D_SSM), norm_w.reshape(1, D_SSM).astype(F32))


def _tile_bcast(x, r):
    n, d = x.shape
    x3 = x.reshape(n // SUBLANES, SUBLANES, d)
    return jnp.broadcast_to(x3[:, r:r + 1, :], x3.shape).reshape(n, d)


def _hgrn_kernel(q_ref, f_ref, i_ref, g_ref, lb_ref, nw_ref, o_ref, st_ref):
    C = HGRN_CHUNK
    S = HGRN_SUB
    nsub = C // S

    @pl.when(pl.program_id(0) == 0)
    def _():
        st_ref[...] = jnp.zeros(st_ref.shape, F32)

    lb = lb_ref[...]
    fg = lb + (1.0 - lb) * _sigmoid(f_ref[...])
    kk = 1.0 - fg
    qs = _silu(q_ref[...])
    v = i_ref[...]
    rows = lax.broadcasted_iota(I32, (C, C), 0)
    cols = lax.broadcasted_iota(I32, (C, C), 1)
    cum = _sel_dot((rows >= cols).astype(BF16), jnp.log(fg))
    last = cum[C - 1:C, :]

    trow = lax.broadcasted_iota(I32, (C, D_HGRN), 0)
    tmod = trow % S
    tblk = trow // S

    xr = []
    for r in range(S):
        e = jnp.exp(jnp.where(tmod >= r, cum - _tile_bcast(cum, r), NEG))
        xr.append(qs * _tile_bcast(kk, r) * e)
    kbe = kk * jnp.exp(_tile_bcast(cum, S - 1) - cum)
    aj, bj = [], []
    for j in range(nsub - 1):
        ce = cum[j * S + S - 1:j * S + S, :]
        aj.append(qs * jnp.exp(jnp.where(tblk > j, cum - ce, NEG)))
        bj.append(jnp.where(tblk == j, kbe, 0.0))
    qd = (qs * jnp.exp(cum)).astype(BF16)
    kd = (kk * jnp.exp(last - cum)).astype(BF16)
    elast = jnp.exp(last)

    att_col = lax.broadcasted_iota(I32, (C, C), 1)
    att_rblk = lax.broadcasted_iota(I32, (C, C), 0) // S * S
    outs = []
    for h in range(HGRN_HEADS):
        blk = slice(h * HGRN_DK, (h + 1) * HGRN_DK)
        a_cat = jnp.concatenate([a[:, blk] for a in aj], axis=-1).astype(BF16)
        b_cat = jnp.concatenate([b[:, blk] for b in bj], axis=-1).astype(BF16)
        att = _dot_nt(a_cat, b_cat)
        for r in range(S):
            rs = jnp.sum(xr[r][:, blk], axis=-1, keepdims=True)
            att = jnp.where(att_col == att_rblk + r, rs, att)
        vb = v[:, blk].astype(BF16)
        s_prev = st_ref[h]
        o = _dot_nt(qd[:, blk], s_prev.astype(BF16)) + _dot(att.astype(BF16), vb)
        st_ref[h] = s_prev * elast[:, blk] + _dot_tn(vb, kd[:, blk])
        ms = jnp.mean(o * o, axis=-1, keepdims=True)
        outs.append(o * lax.rsqrt(ms + RMS_EPS))
    o_ref[...] = jnp.concatenate(outs, axis=-1) * nw_ref[...] * _silu(g_ref[...])


def _hgrn(q, f, i, g, lower_bound, norm_w):
    t = q.shape[0]
    C = HGRN_CHUNK
    tok = pl.BlockSpec((C, D_HGRN), lambda c: (c, 0))
    vec = pl.BlockSpec((1, D_HGRN), lambda c: (0, 0))
    return pl.pallas_call(
        _hgrn_kernel,
        grid=(t // C,),
        in_specs=[tok, tok, tok, tok, vec, vec],
        out_specs=tok,
        out_shape=jax.ShapeDtypeStruct((t, D_HGRN), F32),
        scratch_shapes=[pltpu.VMEM((HGRN_HEADS, D_HGRN // HGRN_HEADS, HGRN_DK), F32)],
        compiler_params=pltpu.CompilerParams(dimension_semantics=("arbitrary",)),
        name="hgrn",
    )(q, f, i, g, lower_bound.reshape(1, D_HGRN).astype(F32), norm_w.reshape(1, D_HGRN).astype(F32))


def _post_kernel(x_ref, ys_ref, yh_ref, wo_ref, g1_ref, b1_ref, wrh_ref, wrl_ref, rb_ref,
                 wgs_ref, wus_ref, wds_ref, htt_ref, base_ref, idx_ref, gate_ref):
    tm = x_ref.shape[0]
    mix = (_dot(ys_ref[...].astype(BF16), wo_ref[0:D_SSM, :])
           + _dot(yh_ref[...].astype(BF16), wo_ref[D_SSM:, :]))
    h1 = _layer_norm(ALPHA * x_ref[...] + mix, g1_ref[...], b1_ref[...])
    packed = _pack_pairs(h1[:, :D_MODEL // 2], h1[:, D_MODEL // 2:])
    for hh in range(ROW_HALVES):
        htt_ref[hh] = packed[:, hh * HALF_WORDS:(hh + 1) * HALF_WORDS]
    hb = h1.astype(BF16)
    hid = _silu(_dot(hb, wgs_ref[...])) * _dot(hb, wus_ref[...])
    base_ref[...] = ALPHA * h1 + _dot(hid.astype(BF16), wds_ref[...])

    hlo = (h1 - hb.astype(F32)).astype(BF16)
    logits = _dot_nt(wrh_ref[...], hb) + _dot_nt(wrh_ref[...], hlo) + _dot_nt(wrl_ref[...], hb)
    scores = _sigmoid(logits)
    biased = scores + rb_ref[...]
    per_group = N_EXPERTS // N_EXPERT_GROUPS
    eidx = lax.broadcasted_iota(I32, (N_EXPERTS, tm), 0)
    big = jnp.int32(1 << 20)
    gsc = []
    bidx = lax.broadcasted_iota(I32, (per_group, tm), 0)
    for gi in range(N_EXPERT_GROUPS):
        blk = biased[gi * per_group:(gi + 1) * per_group, :]
        m1 = jnp.max(blk, axis=0, keepdims=True)
        i1 = jnp.min(jnp.where(blk == m1, bidx, big), axis=0, keepdims=True)
        m2 = jnp.max(jnp.where(bidx == i1, NEG, blk), axis=0, keepdims=True)
        gsc.append(m1 + m2)
    cur = jnp.concatenate(gsc, axis=0)
    gidx = lax.broadcasted_iota(I32, (N_EXPERT_GROUPS, tm), 0)
    gsel = jnp.zeros((N_EXPERT_GROUPS, tm), F32)
    for _ in range(TOPK_GROUPS):
        m = jnp.max(cur, axis=0, keepdims=True)
        i = jnp.min(jnp.where(cur == m, gidx, big), axis=0, keepdims=True)
        hit = gidx == i
        gsel = jnp.where(hit, 1.0, gsel)
        cur = jnp.where(hit, NEG, cur)
    emask = jnp.concatenate(
        [jnp.broadcast_to(gsel[gi:gi + 1, :], (per_group, tm)) for gi in range(N_EXPERT_GROUPS)], axis=0)
    masked = jnp.where(emask > 0.0, biased, NEG)
    idx_rows, gate_rows = [], []
    for _ in range(TOP_K):
        m = jnp.max(masked, axis=0, keepdims=True)
        i = jnp.min(jnp.where(masked == m, eidx, big), axis=0, keepdims=True)
        hit = eidx == i
        idx_rows.append(i)
        gate_rows.append(jnp.sum(jnp.where(hit, scores, 0.0), axis=0, keepdims=True))
        masked = jnp.where(hit, NEG, masked)
    gates = jnp.concatenate(gate_rows, axis=0)
    gates = gates / jnp.sum(gates, axis=0, keepdims=True) * ROUTED_SCALE
    idx_ref[0] = jnp.concatenate(idx_rows, axis=0)
    gate_ref[0] = gates


def _post(x2, y_ssd, y_hgrn, wo, g1, b1, wr_hi, wr_lo, rbias, wgs, wus, wds):
    t = x2.shape[0]
    tm = TM_TOK
    nt = t // tm
    full = lambda shape: pl.BlockSpec(shape, lambda i: (0,) * len(shape))
    return pl.pallas_call(
        _post_kernel,
        grid=(nt,),
        in_specs=[pl.BlockSpec((tm, D_MODEL), lambda i: (i, 0)),
                  pl.BlockSpec((tm, D_SSM), lambda i: (i, 0)),
                  pl.BlockSpec((tm, D_HGRN), lambda i: (i, 0)),
                  full((D_MODEL, D_MODEL)), full((1, D_MODEL)), full((1, D_MODEL)),
                  full((N_EXPERTS, D_MODEL)), full((N_EXPERTS, D_MODEL)), full((N_EXPERTS, 1)),
                  full((D_MODEL, D_EXPERT)), full((D_MODEL, D_EXPERT)), full((D_EXPERT, D_MODEL))],
        out_specs=[pl.BlockSpec((ROW_HALVES, tm, HALF_WORDS), lambda i: (0, i, 0)),
                   pl.BlockSpec((tm, D_MODEL), lambda i: (i, 0)),
                   pl.BlockSpec((1, TOP_K, tm), lambda i: (i, 0, 0)),
                   pl.BlockSpec((1, TOP_K, tm), lambda i: (i, 0, 0))],
        out_shape=[jax.ShapeDtypeStruct((ROW_HALVES, t, HALF_WORDS), U32),
                   jax.ShapeDtypeStruct((t, D_MODEL), F32),
                   jax.ShapeDtypeStruct((nt, TOP_K, tm), I32),
                   jax.ShapeDtypeStruct((nt, TOP_K, tm), F32)],
        compiler_params=pltpu.CompilerParams(dimension_semantics=("arbitrary",),
                                             vmem_limit_bytes=48 * 1024 * 1024),
        name="post",
    )(x2, y_ssd, y_hgrn, wo, g1, b1, wr_hi, wr_lo, rbias, wgs, wus, wds)


def _max_blocks(t):
    return (t * TOP_K + N_EXPERTS * (EXPERT_ROWS - 1)) // EXPERT_ROWS


def _pos_kernel(idx_ref, dest_ref, bstart_ref, nblk_ref, cnt_ref, start_ref):
    phase = pl.program_id(0)
    i = pl.program_id(1)
    tm = idx_ref.shape[-1]
    idx = idx_ref[0]
    eidx = lax.broadcasted_iota(I32, (N_EXPERTS, tm), 0)
    sel = [eidx == idx[k:k + 1, :] for k in range(TOP_K)]
    onehot = sel[0]
    for k in range(1, TOP_K):
        onehot = jnp.logical_or(onehot, sel[k])
    mt = jnp.where(onehot, 1.0, 0.0).astype(BF16)
    tile_cnt = _dot(mt, jnp.ones((tm, LANES), BF16))

    @pl.when(jnp.logical_and(phase == 0, i == 0))
    def _():
        cnt_ref[...] = jnp.zeros(cnt_ref.shape, F32)

    @pl.when(phase == 0)
    def _():
        cnt_ref[...] += tile_cnt

    @pl.when(jnp.logical_and(phase == 1, i == 0))
    def _():
        nb = jnp.floor((cnt_ref[...] + (EXPERT_ROWS - 1)) * (1.0 / EXPERT_ROWS))
        r = lax.broadcasted_iota(I32, (N_EXPERTS, N_EXPERTS), 0)
        c = lax.broadcasted_iota(I32, (N_EXPERTS, N_EXPERTS), 1)
        end = _dot((r >= c).astype(BF16), nb.astype(BF16))
        start_ref[...] = (end - nb) * EXPERT_ROWS
        cnt_ref[...] = jnp.zeros(cnt_ref.shape, F32)
        bstart_ref[...] = (end - nb).astype(I32)
        nblk_ref[...] = nb.astype(I32)

    @pl.when(phase == 1)
    def _():
        r = lax.broadcasted_iota(I32, (tm, tm), 0)
        c = lax.broadcasted_iota(I32, (tm, tm), 1)
        before = _dot(mt, (r < c).astype(BF16))
        slot = start_ref[:, 0:1] + cnt_ref[:, 0:1] + before
        rows = [jnp.sum(jnp.where(sel[k], slot, 0.0), axis=0, keepdims=True) for k in range(TOP_K)]
        dest_ref[0] = jnp.concatenate(rows, axis=0).astype(I32)
        cnt_ref[...] += tile_cnt


def _positions(idx):
    nt, _, tm = idx.shape
    return pl.pallas_call(
        _pos_kernel,
        grid=(2, nt),
        in_specs=[pl.BlockSpec((1, TOP_K, tm), lambda p, i: (i, 0, 0))],
        out_specs=[pl.BlockSpec((1, TOP_K, tm), lambda p, i: (i * p, 0, 0)),
                   pl.BlockSpec((N_EXPERTS, LANES), lambda p, i: (0, 0)),
                   pl.BlockSpec((N_EXPERTS, LANES), lambda p, i: (0, 0))],
        out_shape=[jax.ShapeDtypeStruct((nt, TOP_K, tm), I32),
                   jax.ShapeDtypeStruct((N_EXPERTS, LANES), I32),
                   jax.ShapeDtypeStruct((N_EXPERTS, LANES), I32)],
        scratch_shapes=[pltpu.VMEM((N_EXPERTS, LANES), F32), pltpu.VMEM((N_EXPERTS, LANES), F32)],
        compiler_params=pltpu.CompilerParams(dimension_semantics=("arbitrary", "arbitrary")),
        name="positions",
    )(idx)


def _sc_mesh():
    return plsc.VectorSubcoreMesh(core_axis_name="core", subcore_axis_name="subcore")


def _sc_dispatch(src, index, n_out):
    n = index.shape[0]
    src_blocks = src.shape[0] // SC_WINDOW
    index = index.reshape(1, n)

    @pl.kernel(out_type=jax.ShapeDtypeStruct((n_out, HALF_WORDS), src.dtype), mesh=_sc_mesh(), name="sc_dispatch")
    def scatter(src_hbm, idx_hbm, out_hbm):
        def body(rows_vmem, idx_vmem):
            pltpu.sync_copy(rows_vmem, out_hbm.at[idx_vmem.at[0]])

        pltpu.emit_pipeline(
            body, grid=(n // SC_WINDOW,),
            in_specs=[pl.BlockSpec((SC_WINDOW, HALF_WORDS), index_map=lambda i: (i % src_blocks, 0)),
                      pl.BlockSpec((1, SC_WINDOW), index_map=lambda i: (0, i))],
            out_specs=[],
            core_axis_name=("core", "subcore"), dimension_semantics=(pltpu.PARALLEL,))(src_hbm, idx_hbm)

    return scatter(src, index)


def _sc_gather(src, index):
    n = index.shape[0]
    index = index.reshape(1, n)

    @pl.kernel(out_type=jax.ShapeDtypeStruct((n, HALF_WORDS), src.dtype), mesh=_sc_mesh(), name="sc_gather")
    def gather(src_hbm, idx_hbm, out_hbm):
        def body(idx_vmem, rows_vmem):
            pltpu.sync_copy(src_hbm.at[idx_vmem.at[0]], rows_vmem)

        pltpu.emit_pipeline(
            body, grid=(n // SC_WINDOW,),
            in_specs=[pl.BlockSpec((1, SC_WINDOW), index_map=lambda i: (0, i))],
            out_specs=[pl.BlockSpec((SC_WINDOW, HALF_WORDS), index_map=lambda i: (i, 0))],
            core_axis_name=("core", "subcore"), dimension_semantics=(pltpu.PARALLEL,))(idx_hbm, out_hbm)

    return gather(src, index)


def _expert_kernel(bstart_ref, nblk_ref, xs_hbm, wg_ref, wu_ref, wd_ref, ys_hbm,
                   xbuf, ybuf, wgb_ref, wub_ref, wdb_ref, xsem, ysem, ypend):
    e = pl.program_id(0)
    last = pl.num_programs(0) - 1
    n = nblk_ref[e]
    b0 = bstart_ref[e]
    total = bstart_ref[last] + nblk_ref[last]
    blk_rows = EXPERT_ROWS * ROW_HALVES
    xdepth = xbuf.shape[0]
    ydepth = ybuf.shape[0]
    lookahead = xdepth - MAX_CHUNK

    def hbm_block(ref, b):
        return ref.at[pl.ds(pl.multiple_of(b * blk_rows, blk_rows), blk_rows)]

    def x_copy(b):
        slot = b % xdepth
        return pltpu.make_async_copy(hbm_block(xs_hbm, b), xbuf.at[slot], xsem.at[slot])

    def y_copy(b, slot):
        return pltpu.make_async_copy(ybuf.at[slot], hbm_block(ys_hbm, b), ysem.at[slot])

    @pl.when(e == 0)
    def _():
        for s in range(ydepth):
            ypend[s] = 0
        for b in range(lookahead):
            @pl.when(b < total)
            def _():
                x_copy(b).start()

    def chunk(j, m):
        b = b0 + j
        for i in range(m):
            @pl.when(b + lookahead + i < total)
            def _():
                x_copy(b + lookahead + i).start()

        for i in range(m):
            x_copy(b + i).wait()

            @pl.when(ypend[(b + i) % ydepth] == 1)
            def _():
                y_copy(b + i, (b + i) % ydepth).wait()

        for i in range(m):
            xslot = (b + i) % xdepth
            yslot = (b + i) % ydepth
            halves = [_unpack_pairs(xbuf[xslot, hh * EXPERT_ROWS:(hh + 1) * EXPERT_ROWS, :])
                      for hh in range(ROW_HALVES)]
            xb = jnp.concatenate([lo for lo, _ in halves] + [hi for _, hi in halves], axis=-1).astype(BF16)
            hid = _silu(_dot(xb, wgb_ref[...])) * _dot(xb, wub_ref[...])
            yb = _dot(hid.astype(BF16), wdb_ref[...])
            packed = _pack_pairs(yb[:, :D_MODEL // 2], yb[:, D_MODEL // 2:])
            for hh in range(ROW_HALVES):
                ybuf[yslot, hh * EXPERT_ROWS:(hh + 1) * EXPERT_ROWS, :] = (
                    packed[:, hh * HALF_WORDS:(hh + 1) * HALF_WORDS])

        for i in range(m):
            y_copy(b + i, (b + i) % ydepth).start()
            ypend[(b + i) % ydepth] = 1

    @pl.when(n > 0)
    def _():
        wgb_ref[...] = wg_ref[0].astype(BF16)
        wub_ref[...] = wu_ref[0].astype(BF16)
        wdb_ref[...] = wd_ref[0].astype(BF16)
        full = n // MAX_CHUNK

        def full_chunk(q, carry):
            chunk(q * MAX_CHUNK, MAX_CHUNK)
            return carry

        lax.fori_loop(0, full, full_chunk, 0)
        for m in range(1, MAX_CHUNK):
            @pl.when(n - full * MAX_CHUNK == m)
            def _():
                chunk(full * MAX_CHUNK, m)

    @pl.when(e == last)
    def _():
        for s in range(ydepth):
            @pl.when(ypend[s] == 1)
            def _():
                y_copy(b0, s).wait()


def _experts(bstart, nblk, xs, w_gate, w_up, w_down):
    blk_shape = (EXPERT_ROWS * ROW_HALVES, HALF_WORDS)
    wspec = lambda shape: pl.BlockSpec((1,) + shape, lambda e, bs, nb: (e, 0, 0))
    grid_spec = pltpu.PrefetchScalarGridSpec(
        num_scalar_prefetch=2,
        grid=(N_EXPERTS,),
        in_specs=[pl.BlockSpec(memory_space=pl.ANY),
                  wspec((D_MODEL, D_EXPERT)), wspec((D_MODEL, D_EXPERT)), wspec((D_EXPERT, D_MODEL))],
        out_specs=pl.BlockSpec(memory_space=pl.ANY),
        scratch_shapes=[pltpu.VMEM((X_RING,) + blk_shape, U32), pltpu.VMEM((Y_RING,) + blk_shape, U32),
                        pltpu.VMEM((D_MODEL, D_EXPERT), BF16), pltpu.VMEM((D_MODEL, D_EXPERT), BF16),
                        pltpu.VMEM((D_EXPERT, D_MODEL), BF16),
                        pltpu.SemaphoreType.DMA((X_RING,)), pltpu.SemaphoreType.DMA((Y_RING,)),
                        pltpu.SMEM((Y_RING,), I32)],
    )
    return pl.pallas_call(
        _expert_kernel,
        grid_spec=grid_spec,
        out_shape=jax.ShapeDtypeStruct(xs.shape, U32),
        compiler_params=pltpu.CompilerParams(dimension_semantics=("arbitrary",), has_side_effects=True),
        name="experts",
    )(bstart, nblk, xs, w_gate, w_up, w_down)


def _combine_kernel(gate_ref, base_ref, g2_ref, b2_ref, rows_ref, o_ref):
    tm = base_ref.shape[0]
    gates_t = jnp.concatenate([gate_ref[0], jnp.zeros((LANES - TOP_K, tm), F32)], axis=0).T
    half = D_MODEL // 2
    lo_acc = [base_ref[:, hh * HALF_WORDS:(hh + 1) * HALF_WORDS] for hh in range(ROW_HALVES)]
    hi_acc = [base_ref[:, half + hh * HALF_WORDS:half + (hh + 1) * HALF_WORDS] for hh in range(ROW_HALVES)]
    for k in range(TOP_K):
        gk = gates_t[:, k:k + 1]
        for hh in range(ROW_HALVES):
            lo, hi = _unpack_pairs(rows_ref[k, hh])
            lo_acc[hh] = lo_acc[hh] + gk * lo
            hi_acc[hh] = hi_acc[hh] + gk * hi
    o_ref[...] = _layer_norm(jnp.concatenate(lo_acc + hi_acc, axis=-1), g2_ref[...], b2_ref[...])


def _combine(gates, base, g2, b2, rows):
    t = base.shape[0]
    tm = gates.shape[-1]
    return pl.pallas_call(
        _combine_kernel,
        grid=(t // tm,),
        in_specs=[pl.BlockSpec((1, TOP_K, tm), lambda i: (i, 0, 0)),
                  pl.BlockSpec((tm, D_MODEL), lambda i: (i, 0)),
                  pl.BlockSpec((1, D_MODEL), lambda i: (0, 0)),
                  pl.BlockSpec((1, D_MODEL), lambda i: (0, 0)),
                  pl.BlockSpec((TOP_K, ROW_HALVES, tm, HALF_WORDS), lambda i: (0, 0, i, 0))],
        out_specs=pl.BlockSpec((tm, D_MODEL), lambda i: (i, 0)),
        out_shape=jax.ShapeDtypeStruct((t, D_MODEL), F32),
        compiler_params=pltpu.CompilerParams(dimension_semantics=("arbitrary",),
                                             vmem_limit_bytes=48 * 1024 * 1024),
        name="combine",
    )(gates, base, g2, b2, rows)


def _layer(h2, w_in, conv_w, conv_b, dt_bias, a_log, d_skip, ssd_norm_w, lower_bound, hgrn_norm_w, w_out,
           ln1_g, ln1_b, w_router, router_bias, w_gate_e, w_up_e, w_down_e, w_gate_s, w_up_s, w_down_s,
           ln2_g, ln2_b):
    t = h2.shape[0]
    dt0 = D_SSM + D_CONV
    q0 = dt0 + SSD_HEADS
    w_perm = jnp.concatenate(
        [w_in[:, :dt0], w_in[:, q0:], w_in[:, dt0:q0], jnp.zeros((D_MODEL, DT_PAD - SSD_HEADS), w_in.dtype)],
        axis=1).astype(BF16)
    z, xbc, q, f, i, g, dt = _inproj(h2, w_perm)
    y_ssd = _ssd(z, xbc, dt, conv_w, conv_b, dt_bias, a_log, d_skip, ssd_norm_w)
    y_hgrn = _hgrn(q, f, i, g, lower_bound, hgrn_norm_w)

    wr_t = w_router.astype(F32).T
    wr_hi = wr_t.astype(BF16)
    wr_lo = (wr_t - wr_hi.astype(F32)).astype(BF16)
    row = lambda v: v.reshape(1, -1).astype(F32)
    htt, base, idx, gates = _post(
        h2, y_ssd, y_hgrn, w_out.astype(BF16), row(ln1_g), row(ln1_b), wr_hi, wr_lo,
        router_bias.reshape(N_EXPERTS, 1).astype(F32),
        w_gate_s.astype(BF16), w_up_s.astype(BF16), w_down_s.astype(BF16))

    dest, bstart, nblk = _positions(idx)
    n_rows = _max_blocks(t) * EXPERT_ROWS
    d = jnp.transpose(dest, (1, 0, 2)).reshape(TOP_K, 1, t)
    slot = (d // EXPERT_ROWS) * (EXPERT_ROWS * ROW_HALVES) + d % EXPERT_ROWS
    slot = (slot + jnp.arange(ROW_HALVES, dtype=I32).reshape(1, ROW_HALVES, 1) * EXPERT_ROWS).reshape(-1)
    xs = _sc_dispatch(htt.reshape(ROW_HALVES * t, HALF_WORDS), slot, n_rows * ROW_HALVES)
    ys = _experts(bstart[:, 0], nblk[:, 0], xs, w_gate_e, w_up_e, w_down_e)
    rows = _sc_gather(ys, slot).reshape(TOP_K, ROW_HALVES, t, HALF_WORDS)
    return _combine(gates, base, row(ln2_g), row(ln2_b), rows)


def kernel(x, w_in, conv_w, conv_b, dt_bias, a_log, d_skip, ssd_norm_w, hgrn_lb_logits, hgrn_norm_w, w_out,
           ln1_g, ln1_b, w_router, router_bias, w_gate_e, w_up_e, w_down_e, w_gate_s, w_up_s, w_down_s,
           ln2_g, ln2_b):
    bsz, t, d = x.shape
    assert bsz == 1 and d == D_MODEL, "the recurrent mixers carry state across the flattened token axis"
    depth = w_in.shape[0]
    lower_bounds = jnp.cumsum(jax.nn.softmax(hgrn_lb_logits.astype(F32), axis=0), axis=0)
    h = x.reshape(bsz * t, d)
    for l in range(depth):
        h = _layer(h, w_in[l], conv_w[l], conv_b[l], dt_bias[l], a_log[l], d_skip[l], ssd_norm_w[l],
                   lower_bounds[l], hgrn_norm_w[l], w_out[l], ln1_g[l], ln1_b[l], w_router[l],
                   router_bias[l], w_gate_e[l], w_up_e[l], w_down_e[l], w_gate_s[l], w_up_s[l],
                   w_down_s[l], ln2_g[l], ln2_b[l])
    return h.reshape(bsz, t, d)
```

```python
import jax
import jax.numpy as jnp
from jax import lax
from jax.experimental import pallas as pl
from jax.experimental.pallas import tpu as pltpu
from jax.experimental.pallas import tpu_sc as plsc

F32 = jnp.float32
BF16 = jnp.bfloat16
I32 = jnp.int32
U32 = jnp.uint32

D_MODEL = 1024
D_SSM = 512
D_HGRN = 512
SSD_HEADS = 8
SSD_HEAD_DIM = 64
SSD_GROUPS = 2
SSD_STATE = 128
SSD_CONV = 4
SSD_CHUNK = 128
D_CONV = D_SSM + 2 * SSD_GROUPS * SSD_STATE
HGRN_HEADS = 4
HGRN_DK = 128
HGRN_CHUNK = 64
HGRN_SUB = 8
HGRN_PER_STEP = 2
N_EXPERTS = 256
TOP_K = 8
N_EXPERT_GROUPS = 8
TOPK_GROUPS = 4
D_EXPERT = 256
ROUTED_SCALE = 2.5
ALPHA = 2.0 ** 0.25
LN_EPS = 1e-5
RMS_EPS = 1e-6

LANES = 128
SUBLANES = 8
ROW_TILES = D_MODEL // LANES
ROW_HALVES = 2
HALF_WORDS = D_MODEL // 2 // ROW_HALVES
SC_WINDOW = 128
DT_PAD = LANES
N_IN_PAD = D_SSM + D_CONV + 4 * D_HGRN + DT_PAD

TM_PROJ = 256
TM_TOK = 256
EXPERT_ROWS = 256
MAX_CHUNK = 2
X_RING = 8
Y_RING = 8
NEG = -1e30


def _sigmoid(x):
    return 1.0 / (1.0 + jnp.exp(-x))


def _silu(x):
    return x * _sigmoid(x)


def _split3(x):
    hi = x.astype(BF16)
    r = x - hi.astype(F32)
    mid = r.astype(BF16)
    lo = (r - mid.astype(F32)).astype(BF16)
    return hi, mid, lo


def _dot(a, b):
    return jnp.dot(a, b, preferred_element_type=F32)


def _dot_nt(a, b):
    return lax.dot_general(a, b, (((1,), (1,)), ((), ())), preferred_element_type=F32)


def _dot_tn(a, b):
    return lax.dot_general(a, b, (((0,), (0,)), ((), ())), preferred_element_type=F32)


def _sel_dot(sel, x):
    hi, mid, lo = _split3(x)
    return _dot(sel, hi) + _dot(sel, mid) + _dot(sel, lo)


def _dot_sel(x, sel):
    hi, mid, lo = _split3(x)
    return _dot(hi, sel) + _dot(mid, sel) + _dot(lo, sel)


def _pack_pairs(lo, hi):
    lo_bits = pltpu.bitcast(lo.astype(BF16).astype(F32), U32) >> 16
    hi_bits = pltpu.bitcast(hi.astype(BF16).astype(F32), U32) & jnp.uint32(0xFFFF0000)
    return hi_bits | lo_bits


def _unpack_pairs(p):
    return pltpu.bitcast(p << 16, F32), pltpu.bitcast(p & jnp.uint32(0xFFFF0000), F32)


def _layer_norm(x, g, b):
    mu = jnp.mean(x, axis=-1, keepdims=True)
    xc = x - mu
    var = jnp.mean(xc * xc, axis=-1, keepdims=True)
    return xc * lax.rsqrt(var + LN_EPS) * g + b


def _inproj_kernel(x_ref, w_ref, z_ref, xbc_ref, q_ref, f_ref, i_ref, g_ref, dt_ref):
    xb = x_ref[...].astype(BF16)
    col = 0
    for ref in (z_ref, xbc_ref, q_ref, f_ref, i_ref, g_ref, dt_ref):
        n = ref.shape[-1]
        ref[...] = _dot(xb, w_ref[:, col:col + n])
        col += n


def _inproj(x2, w_perm):
    t = x2.shape[0]
    widths = (D_SSM, D_CONV, D_HGRN, D_HGRN, D_HGRN, D_HGRN, DT_PAD)
    return pl.pallas_call(
        _inproj_kernel,
        grid=(t // TM_PROJ,),
        in_specs=[pl.BlockSpec((TM_PROJ, D_MODEL), lambda i: (i, 0)),
                  pl.BlockSpec((D_MODEL, N_IN_PAD), lambda i: (0, 0))],
        out_specs=[pl.BlockSpec((TM_PROJ, n), lambda i: (i, 0)) for n in widths],
        out_shape=[jax.ShapeDtypeStruct((t, n), F32) for n in widths],
        compiler_params=pltpu.CompilerParams(dimension_semantics=("arbitrary",),
                                             vmem_limit_bytes=48 * 1024 * 1024),
        name="inproj",
    )(x2, w_perm)


def _ssd_kernel(z_ref, xbc_ref, dt_ref, cw_ref, cb_ref, dtb_ref, alog_ref, dskip_ref, nw_ref,
                y_ref, ext_ref, st_ref):
    L = SSD_CHUNK
    halo = SUBLANES

    @pl.when(pl.program_id(0) == 0)
    def _():
        ext_ref[0:halo, :] = jnp.zeros((halo, D_CONV), F32)
        st_ref[...] = jnp.zeros(st_ref.shape, F32)

    ext_ref[halo:halo + L, :] = xbc_ref[...]
    acc = jnp.broadcast_to(cb_ref[...], (L, D_CONV))
    for k in range(SSD_CONV):
        off = halo - (SSD_CONV - 1) + k
        acc = acc + cw_ref[k:k + 1, :] * ext_ref[off:off + L, :]
    ext_ref[0:halo, :] = xbc_ref[L - halo:L, :]
    u = _silu(acc)
    xs = u[:, :D_SSM]
    bm = u[:, D_SSM:D_SSM + SSD_GROUPS * SSD_STATE]
    cm = u[:, D_SSM + SSD_GROUPS * SSD_STATE:]

    draw = dt_ref[...] + dtb_ref[...]
    dt = jnp.maximum(draw, 0.0) + jnp.log(1.0 + jnp.exp(-jnp.abs(draw)))
    ad = dt * (-jnp.exp(alog_ref[...]))
    rows = lax.broadcasted_iota(I32, (L, L), 0)
    cols = lax.broadcasted_iota(I32, (L, L), 1)
    causal = rows >= cols
    a_cum = _sel_dot(causal.astype(BF16), ad)
    a_cum_t = a_cum.T

    hrow = lax.broadcasted_iota(I32, (LANES, D_SSM), 0)
    hcol = lax.broadcasted_iota(I32, (LANES, D_SSM), 1) // SSD_HEAD_DIM
    expand = (hrow == hcol).astype(BF16)
    dt_x = _dot_sel(dt, expand)
    acx = _dot_sel(a_cum, expand)
    last = acx[L - 1:L, :]
    ea_x = jnp.exp(acx)
    dec_x = jnp.exp(last - acx)
    elast_x = jnp.exp(last)

    xdt = xs * dt_x
    gw = SSD_HEADS // SSD_GROUPS * SSD_HEAD_DIM
    lane_head = lax.broadcasted_iota(I32, (L, gw), 1) // SSD_HEAD_DIM
    ys = []
    for g in range(SSD_GROUPS):
        bg = bm[:, g * SSD_STATE:(g + 1) * SSD_STATE]
        cg = cm[:, g * SSD_STATE:(g + 1) * SSD_STATE].astype(BF16)
        bg_t = bg.T.astype(BF16)
        gmat = _dot(cg, bg_t)
        xdt_g = xdt[:, g * gw:(g + 1) * gw]
        xdt_gb = xdt_g.astype(BF16)
        r_prev = st_ref[g]
        y_g = _dot(cg, r_prev.astype(BF16)) * ea_x[:, g * gw:(g + 1) * gw]
        new_s = _dot(bg_t, (xdt_g * dec_x[:, g * gw:(g + 1) * gw]).astype(BF16))
        st_ref[g] = r_prev * elast_x[:, g * gw:(g + 1) * gw] + new_s
        for j in range(SSD_HEADS // SSD_GROUPS):
            h = g * (SSD_HEADS // SSD_GROUPS) + j
            diff = a_cum[:, h:h + 1] - a_cum_t[h:h + 1, :]
            decay = jnp.exp(jnp.where(causal, diff, NEG))
            yd = _dot((gmat * decay).astype(BF16), xdt_gb)
            y_g = y_g + jnp.where(lane_head == j, yd, 0.0)
        ys.append(y_g)
    y = jnp.concatenate(ys, axis=-1) + xs * dskip_ref[...]
    y = y * _silu(z_ref[...])
    outs = []
    ng = D_SSM // SSD_GROUPS
    for g in range(SSD_GROUPS):
        yg = y[:, g * ng:(g + 1) * ng]
        ms = jnp.mean(yg * yg, axis=-1, keepdims=True)
        outs.append(yg * lax.rsqrt(ms + RMS_EPS))
    y_ref[...] = jnp.concatenate(outs, axis=-1) * nw_ref[...]


def _ssd(z, xbc, dt, conv_w, conv_b, dt_bias, a_log, d_skip, norm_w):
    t = z.shape[0]
    L = SSD_CHUNK
    pad = lambda v: jnp.pad(v.astype(F32), (0, LANES - v.shape[0])).reshape(1, LANES)
    full = lambda shape: pl.BlockSpec(shape, lambda c: (0,) * len(shape))
    return pl.pallas_call(
        _ssd_kernel,
        grid=(t // L,),
        in_specs=[pl.BlockSpec((L, D_SSM), lambda c: (c, 0)),
                  pl.BlockSpec((L, D_CONV), lambda c: (c, 0)),
                  pl.BlockSpec((L, DT_PAD), lambda c: (c, 0)),
                  full((SSD_CONV, D_CONV)), full((1, D_CONV)), full((1, LANES)), full((1, LANES)),
                  full((1, D_SSM)), full((1, D_SSM))],
        out_specs=pl.BlockSpec((L, D_SSM), lambda c: (c, 0)),
        out_shape=jax.ShapeDtypeStruct((t, D_SSM), F32),
        scratch_shapes=[pltpu.VMEM((L + SUBLANES, D_CONV), F32),
                        pltpu.VMEM((SSD_GROUPS, SSD_STATE, D_SSM // SSD_GROUPS), F32)],
        compiler_params=pltpu.CompilerParams(dimension_semantics=("arbitrary",)),
        name="ssd",
    )(z, xbc, dt, conv_w.astype(F32), conv_b.reshape(1, D_CONV).astype(F32), pad(dt_bias), pad(a_log),
      jnp.repeat(d_skip.astype(F32), SSD_HEAD_DIM).reshape(1, D_SSM), norm_w.reshape(1, D_SSM).astype(F32))


def _tile_bcast(x, r):
    n, d = x.shape
    x3 = x.reshape(n // SUBLANES, SUBLANES, d)
    return jnp.broadcast_to(x3[:, r:r + 1, :], x3.shape).reshape(n, d)


def _hgrn_kernel(q_ref, f_ref, i_ref, g_ref, lb_ref, nw_ref, o_ref, st_ref):
    @pl.when(pl.program_id(0) == 0)
    def _():
        st_ref[...] = jnp.zeros(st_ref.shape, F32)

    for c in range(o_ref.shape[0] // HGRN_CHUNK):
        rows = slice(c * HGRN_CHUNK, (c + 1) * HGRN_CHUNK)
        o_ref[rows, :] = _hgrn_chunk(q_ref[rows, :], f_ref[rows, :], i_ref[rows, :], g_ref[rows, :],
                                     lb_ref[...], nw_ref[...], st_ref)


def _hgrn_chunk(q, f, v, g, lb, nw, st_ref):
    C = HGRN_CHUNK
    S = HGRN_SUB
    nsub = C // S
    fg = lb + (1.0 - lb) * _sigmoid(f)
    qs = _silu(q)
    rows = lax.broadcasted_iota(I32, (C, C), 0)
    cols = lax.broadcasted_iota(I32, (C, C), 1)
    cum = _sel_dot((rows >= cols).astype(BF16), jnp.log2(fg))
    ck_all = cum - jnp.log2(1.0 - fg)

    srow = lax.broadcasted_iota(I32, (S * HGRN_DK, C), 0) // HGRN_DK
    scol = lax.broadcasted_iota(I32, (S * HGRN_DK, C), 1) % S
    spread = (srow == scol).astype(BF16)
    diag_keep = jnp.logical_and(rows // S == cols // S, rows >= cols)
    outs = []
    for h in range(HGRN_HEADS):
        blk = slice(h * HGRN_DK, (h + 1) * HGRN_DK)
        qh, ch, ck = qs[:, blk], cum[:, blk], ck_all[:, blk]
        last = ch[C - 1:C, :]
        x_cat = jnp.concatenate(
            [(qh * jnp.exp2(jnp.minimum(ch - _tile_bcast(ck, r), 0.0))).astype(BF16) for r in range(S)], axis=-1)
        kbe = jnp.exp2(_tile_bcast(ch, S - 1) - ck).astype(BF16)
        a_pieces, b_pieces = [], []
        for j in range(nsub - 1):
            lo = (j + 1) * S
            a = (qh[lo:, :] * jnp.exp2(ch[lo:, :] - ch[lo - 1:lo, :])).astype(BF16)
            a_pieces.append(jnp.concatenate([jnp.zeros((lo, HGRN_DK), BF16), a], axis=0))
            b = [kbe[j * S:lo, :], jnp.zeros((C - lo, HGRN_DK), BF16)]
            if j > 0:
                b.insert(0, jnp.zeros((j * S, HGRN_DK), BF16))
            b_pieces.append(jnp.concatenate(b, axis=0))
        att = (_dot_nt(jnp.concatenate(a_pieces, axis=-1), jnp.concatenate(b_pieces, axis=-1))
               + jnp.where(diag_keep, _dot(x_cat, spread), 0.0))
        vb = v[:, blk].astype(BF16)
        s_prev = st_ref[h]
        o = _dot_nt((qh * jnp.exp2(ch)).astype(BF16), s_prev.astype(BF16)) + _dot(att.astype(BF16), vb)
        st_ref[h] = s_prev * jnp.exp2(last) + _dot_tn(vb, jnp.exp2(last - ck).astype(BF16))
        ms = jnp.mean(o * o, axis=-1, keepdims=True)
        outs.append(o * lax.rsqrt(ms + RMS_EPS))
    return jnp.concatenate(outs, axis=-1) * nw * _silu(g)


def _hgrn(q, f, i, g, lower_bound, norm_w):
    t = q.shape[0]
    C = HGRN_CHUNK * HGRN_PER_STEP
    tok = pl.BlockSpec((C, D_HGRN), lambda c: (c, 0))
    vec = pl.BlockSpec((1, D_HGRN), lambda c: (0, 0))
    return pl.pallas_call(
        _hgrn_kernel,
        grid=(t // C,),
        in_specs=[tok, tok, tok, tok, vec, vec],
        out_specs=tok,
        out_shape=jax.ShapeDtypeStruct((t, D_HGRN), F32),
        scratch_shapes=[pltpu.VMEM((HGRN_HEADS, D_HGRN // HGRN_HEADS, HGRN_DK), F32)],
        compiler_params=pltpu.CompilerParams(dimension_semantics=("arbitrary",)),
        name="hgrn",
    )(q, f, i, g, lower_bound.reshape(1, D_HGRN).astype(F32), norm_w.reshape(1, D_HGRN).astype(F32))


def _post_kernel(x_ref, ys_ref, yh_ref, wo_ref, g1_ref, b1_ref, wrh_ref, wrl_ref, rb_ref,
                 wgs_ref, wus_ref, wds_ref, htt_ref, base_ref, idx_ref, gate_ref):
    tm = x_ref.shape[0]
    mix = (_dot(ys_ref[...].astype(BF16), wo_ref[0:D_SSM, :])
           + _dot(yh_ref[...].astype(BF16), wo_ref[D_SSM:, :]))
    h1 = _layer_norm(ALPHA * x_ref[...] + mix, g1_ref[...], b1_ref[...])
    packed = _pack_pairs(h1[:, :D_MODEL // 2], h1[:, D_MODEL // 2:])
    for hh in range(ROW_HALVES):
        htt_ref[hh] = packed[:, hh * HALF_WORDS:(hh + 1) * HALF_WORDS]
    hb = h1.astype(BF16)
    hid = _silu(_dot(hb, wgs_ref[...])) * _dot(hb, wus_ref[...])
    base_ref[...] = ALPHA * h1 + _dot(hid.astype(BF16), wds_ref[...])

    hlo = (h1 - hb.astype(F32)).astype(BF16)
    logits = _dot_nt(wrh_ref[...], hb) + _dot_nt(wrh_ref[...], hlo) + _dot_nt(wrl_ref[...], hb)
    scores = _sigmoid(logits)
    biased = scores + rb_ref[...]
    per_group = N_EXPERTS // N_EXPERT_GROUPS
    eidx = lax.broadcasted_iota(I32, (N_EXPERTS, tm), 0)
    big = jnp.int32(1 << 20)
    gsc = []
    bidx = lax.broadcasted_iota(I32, (per_group, tm), 0)
    for gi in range(N_EXPERT_GROUPS):
        blk = biased[gi * per_group:(gi + 1) * per_group, :]
        m1 = jnp.max(blk, axis=0, keepdims=True)
        i1 = jnp.min(jnp.where(blk == m1, bidx, big), axis=0, keepdims=True)
        m2 = jnp.max(jnp.where(bidx == i1, NEG, blk), axis=0, keepdims=True)
        gsc.append(m1 + m2)
    cur = jnp.concatenate(gsc, axis=0)
    gidx = lax.broadcasted_iota(I32, (N_EXPERT_GROUPS, tm), 0)
    gsel = jnp.zeros((N_EXPERT_GROUPS, tm), F32)
    for _ in range(TOPK_GROUPS):
        m = jnp.max(cur, axis=0, keepdims=True)
        i = jnp.min(jnp.where(cur == m, gidx, big), axis=0, keepdims=True)
        hit = gidx == i
        gsel = jnp.where(hit, 1.0, gsel)
        cur = jnp.where(hit, NEG, cur)
    emask = jnp.concatenate(
        [jnp.broadcast_to(gsel[gi:gi + 1, :], (per_group, tm)) for gi in range(N_EXPERT_GROUPS)], axis=0)
    masked = jnp.where(emask > 0.0, biased, NEG)
    idx_rows, gate_rows = [], []
    for _ in range(TOP_K):
        m = jnp.max(masked, axis=0, keepdims=True)
        i = jnp.min(jnp.where(masked == m, eidx, big), axis=0, keepdims=True)
        hit = eidx == i
        idx_rows.append(i)
        gate_rows.append(jnp.sum(jnp.where(hit, scores, 0.0), axis=0, keepdims=True))
        masked = jnp.where(hit, NEG, masked)
    gates = jnp.concatenate(gate_rows, axis=0)
    gates = gates / jnp.sum(gates, axis=0, keepdims=True) * ROUTED_SCALE
    idx_ref[0] = jnp.concatenate(idx_rows, axis=0)
    gate_ref[0] = gates


def _post(x2, y_ssd, y_hgrn, wo, g1, b1, wr_hi, wr_lo, rbias, wgs, wus, wds):
    t = x2.shape[0]
    tm = TM_TOK
    nt = t // tm
    full = lambda shape: pl.BlockSpec(shape, lambda i: (0,) * len(shape))
    return pl.pallas_call(
        _post_kernel,
        grid=(nt,),
        in_specs=[pl.BlockSpec((tm, D_MODEL), lambda i: (i, 0)),
                  pl.BlockSpec((tm, D_SSM), lambda i: (i, 0)),
                  pl.BlockSpec((tm, D_HGRN), lambda i: (i, 0)),
                  full((D_MODEL, D_MODEL)), full((1, D_MODEL)), full((1, D_MODEL)),
                  full((N_EXPERTS, D_MODEL)), full((N_EXPERTS, D_MODEL)), full((N_EXPERTS, 1)),
                  full((D_MODEL, D_EXPERT)), full((D_MODEL, D_EXPERT)), full((D_EXPERT, D_MODEL))],
        out_specs=[pl.BlockSpec((ROW_HALVES, tm, HALF_WORDS), lambda i: (0, i, 0)),
                   pl.BlockSpec((tm, D_MODEL), lambda i: (i, 0)),
                   pl.BlockSpec((1, TOP_K, tm), lambda i: (i, 0, 0)),
                   pl.BlockSpec((1, TOP_K, tm), lambda i: (i, 0, 0))],
        out_shape=[jax.ShapeDtypeStruct((ROW_HALVES, t, HALF_WORDS), U32),
                   jax.ShapeDtypeStruct((t, D_MODEL), F32),
                   jax.ShapeDtypeStruct((nt, TOP_K, tm), I32),
                   jax.ShapeDtypeStruct((nt, TOP_K, tm), F32)],
        compiler_params=pltpu.CompilerParams(dimension_semantics=("arbitrary",),
                                             vmem_limit_bytes=48 * 1024 * 1024),
        name="post",
    )(x2, y_ssd, y_hgrn, wo, g1, b1, wr_hi, wr_lo, rbias, wgs, wus, wds)


def _max_blocks(t):
    return (t * TOP_K + N_EXPERTS * (EXPERT_ROWS - 1)) // EXPERT_ROWS


def _pos_kernel(idx_ref, dest_ref, bstart_ref, nblk_ref, cnt_ref, start_ref):
    phase = pl.program_id(0)
    i = pl.program_id(1)
    tm = idx_ref.shape[-1]
    idx = idx_ref[0]
    eidx = lax.broadcasted_iota(I32, (N_EXPERTS, tm), 0)
    sel = [eidx == idx[k:k + 1, :] for k in range(TOP_K)]
    onehot = sel[0]
    for k in range(1, TOP_K):
        onehot = jnp.logical_or(onehot, sel[k])
    mt = jnp.where(onehot, 1.0, 0.0).astype(BF16)
    tile_cnt = _dot(mt, jnp.ones((tm, LANES), BF16))

    @pl.when(jnp.logical_and(phase == 0, i == 0))
    def _():
        cnt_ref[...] = jnp.zeros(cnt_ref.shape, F32)

    @pl.when(phase == 0)
    def _():
        cnt_ref[...] += tile_cnt

    @pl.when(jnp.logical_and(phase == 1, i == 0))
    def _():
        nb = jnp.floor((cnt_ref[...] + (EXPERT_ROWS - 1)) * (1.0 / EXPERT_ROWS))
        r = lax.broadcasted_iota(I32, (N_EXPERTS, N_EXPERTS), 0)
        c = lax.broadcasted_iota(I32, (N_EXPERTS, N_EXPERTS), 1)
        end = _dot((r >= c).astype(BF16), nb.astype(BF16))
        start_ref[...] = (end - nb) * EXPERT_ROWS
        cnt_ref[...] = jnp.zeros(cnt_ref.shape, F32)
        bstart_ref[...] = (end - nb).astype(I32)
        nblk_ref[...] = nb.astype(I32)

    @pl.when(phase == 1)
    def _():
        r = lax.broadcasted_iota(I32, (tm, tm), 0)
        c = lax.broadcasted_iota(I32, (tm, tm), 1)
        before = _dot(mt, (r < c).astype(BF16))
        slot = start_ref[:, 0:1] + cnt_ref[:, 0:1] + before
        rows = [jnp.sum(jnp.where(sel[k], slot, 0.0), axis=0, keepdims=True) for k in range(TOP_K)]
        dest_ref[0] = jnp.concatenate(rows, axis=0).astype(I32)
        cnt_ref[...] += tile_cnt


def _positions(idx):
    nt, _, tm = idx.shape
    return pl.pallas_call(
        _pos_kernel,
        grid=(2, nt),
        in_specs=[pl.BlockSpec((1, TOP_K, tm), lambda p, i: (i, 0, 0))],
        out_specs=[pl.BlockSpec((1, TOP_K, tm), lambda p, i: (i * p, 0, 0)),
                   pl.BlockSpec((N_EXPERTS, LANES), lambda p, i: (0, 0)),
                   pl.BlockSpec((N_EXPERTS, LANES), lambda p, i: (0, 0))],
        out_shape=[jax.ShapeDtypeStruct((nt, TOP_K, tm), I32),
                   jax.ShapeDtypeStruct((N_EXPERTS, LANES), I32),
                   jax.ShapeDtypeStruct((N_EXPERTS, LANES), I32)],
        scratch_shapes=[pltpu.VMEM((N_EXPERTS, LANES), F32), pltpu.VMEM((N_EXPERTS, LANES), F32)],
        compiler_params=pltpu.CompilerParams(dimension_semantics=("arbitrary", "arbitrary")),
        name="positions",
    )(idx)


def _sc_mesh():
    return plsc.VectorSubcoreMesh(core_axis_name="core", subcore_axis_name="subcore")


def _sc_dispatch(src, index, n_out):
    n = index.shape[0]
    src_blocks = src.shape[0] // SC_WINDOW
    index = index.reshape(1, n)

    @pl.kernel(out_type=jax.ShapeDtypeStruct((n_out, HALF_WORDS), src.dtype), mesh=_sc_mesh(), name="sc_dispatch")
    def scatter(src_hbm, idx_hbm, out_hbm):
        def body(rows_vmem, idx_vmem):
            pltpu.sync_copy(rows_vmem, out_hbm.at[idx_vmem.at[0]])

        pltpu.emit_pipeline(
            body, grid=(n // SC_WINDOW,),
            in_specs=[pl.BlockSpec((SC_WINDOW, HALF_WORDS), index_map=lambda i: (i % src_blocks, 0)),
                      pl.BlockSpec((1, SC_WINDOW), index_map=lambda i: (0, i))],
            out_specs=[],
            core_axis_name=("core", "subcore"), dimension_semantics=(pltpu.PARALLEL,))(src_hbm, idx_hbm)

    return scatter(src, index)


def _sc_gather(src, index):
    n = index.shape[0]
    index = index.reshape(1, n)

    @pl.kernel(out_type=jax.ShapeDtypeStruct((n, HALF_WORDS), src.dtype), mesh=_sc_mesh(), name="sc_gather")
    def gather(src_hbm, idx_hbm, out_hbm):
        def body(idx_vmem, rows_vmem):
            pltpu.sync_copy(src_hbm.at[idx_vmem.at[0]], rows_vmem)

        pltpu.emit_pipeline(
            body, grid=(n // SC_WINDOW,),
            in_specs=[pl.BlockSpec((1, SC_WINDOW), index_map=lambda i: (0, i))],
            out_specs=[pl.BlockSpec((SC_WINDOW, HALF_WORDS), index_map=lambda i: (i, 0))],
            core_axis_name=("core", "subcore"), dimension_semantics=(pltpu.PARALLEL,))(idx_hbm, out_hbm)

    return gather(src, index)


def _expert_kernel(bstart_ref, nblk_ref, xs_hbm, wg_ref, wu_ref, wd_ref, ys_hbm,
                   xbuf, ybuf, wgb_ref, wub_ref, wdb_ref, xsem, ysem, ypend):
    e = pl.program_id(0)
    last = pl.num_programs(0) - 1
    n = nblk_ref[e]
    b0 = bstart_ref[e]
    total = bstart_ref[last] + nblk_ref[last]
    blk_rows = EXPERT_ROWS * ROW_HALVES
    xdepth = xbuf.shape[0]
    ydepth = ybuf.shape[0]
    lookahead = xdepth - MAX_CHUNK

    def hbm_block(ref, b):
        return ref.at[pl.ds(pl.multiple_of(b * blk_rows, blk_rows), blk_rows)]

    def x_copy(b):
        slot = b % xdepth
        return pltpu.make_async_copy(hbm_block(xs_hbm, b), xbuf.at[slot], xsem.at[slot])

    def y_copy(b, slot):
        return pltpu.make_async_copy(ybuf.at[slot], hbm_block(ys_hbm, b), ysem.at[slot])

    @pl.when(e == 0)
    def _():
        for s in range(ydepth):
            ypend[s] = 0
        for b in range(lookahead):
            @pl.when(b < total)
            def _():
                x_copy(b).start()

    def chunk(j, m):
        b = b0 + j
        for i in range(m):
            @pl.when(b + lookahead + i < total)
            def _():
                x_copy(b + lookahead + i).start()

        for i in range(m):
            x_copy(b + i).wait()

            @pl.when(ypend[(b + i) % ydepth] == 1)
            def _():
                y_copy(b + i, (b + i) % ydepth).wait()

        for i in range(m):
            xslot = (b + i) % xdepth
            yslot = (b + i) % ydepth
            halves = [_unpack_pairs(xbuf[xslot, hh * EXPERT_ROWS:(hh + 1) * EXPERT_ROWS, :])
                      for hh in range(ROW_HALVES)]
            xb = jnp.concatenate([lo for lo, _ in halves] + [hi for _, hi in halves], axis=-1).astype(BF16)
            hid = _silu(_dot(xb, wgb_ref[...])) * _dot(xb, wub_ref[...])
            yb = _dot(hid.astype(BF16), wdb_ref[...])
            packed = _pack_pairs(yb[:, :D_MODEL // 2], yb[:, D_MODEL // 2:])
            for hh in range(ROW_HALVES):
                ybuf[yslot, hh * EXPERT_ROWS:(hh + 1) * EXPERT_ROWS, :] = (
                    packed[:, hh * HALF_WORDS:(hh + 1) * HALF_WORDS])

        for i in range(m):
            y_copy(b + i, (b + i) % ydepth).start()
            ypend[(b + i) % ydepth] = 1

    @pl.when(n > 0)
    def _():
        wgb_ref[...] = wg_ref[0].astype(BF16)
        wub_ref[...] = wu_ref[0].astype(BF16)
        wdb_ref[...] = wd_ref[0].astype(BF16)
        full = n // MAX_CHUNK

        def full_chunk(q, carry):
            chunk(q * MAX_CHUNK, MAX_CHUNK)
            return carry

        lax.fori_loop(0, full, full_chunk, 0)
        for m in range(1, MAX_CHUNK):
            @pl.when(n - full * MAX_CHUNK == m)
            def _():
                chunk(full * MAX_CHUNK, m)

    @pl.when(e == last)
    def _():
        for s in range(ydepth):
            @pl.when(ypend[s] == 1)
            def _():
                y_copy(b0, s).wait()


def _experts(bstart, nblk, xs, w_gate, w_up, w_down):
    blk_shape = (EXPERT_ROWS * ROW_HALVES, HALF_WORDS)
    wspec = lambda shape: pl.BlockSpec((1,) + shape, lambda e, bs, nb: (e, 0, 0))
    grid_spec = pltpu.PrefetchScalarGridSpec(
        num_scalar_prefetch=2,
        grid=(N_EXPERTS,),
        in_specs=[pl.BlockSpec(memory_space=pl.ANY),
                  wspec((D_MODEL, D_EXPERT)), wspec((D_MODEL, D_EXPERT)), wspec((D_EXPERT, D_MODEL))],
        out_specs=pl.BlockSpec(memory_space=pl.ANY),
        scratch_shapes=[pltpu.VMEM((X_RING,) + blk_shape, U32), pltpu.VMEM((Y_RING,) + blk_shape, U32),
                        pltpu.VMEM((D_MODEL, D_EXPERT), BF16), pltpu.VMEM((D_MODEL, D_EXPERT), BF16),
                        pltpu.VMEM((D_EXPERT, D_MODEL), BF16),
                        pltpu.SemaphoreType.DMA((X_RING,)), pltpu.SemaphoreType.DMA((Y_RING,)),
                        pltpu.SMEM((Y_RING,), I32)],
    )
    return pl.pallas_call(
        _expert_kernel,
        grid_spec=grid_spec,
        out_shape=jax.ShapeDtypeStruct(xs.shape, U32),
        compiler_params=pltpu.CompilerParams(dimension_semantics=("arbitrary",), has_side_effects=True),
        name="experts",
    )(bstart, nblk, xs, w_gate, w_up, w_down)


def _combine_kernel(gate_ref, base_ref, g2_ref, b2_ref, rows_ref, o_ref):
    tm = base_ref.shape[0]
    gates_t = jnp.concatenate([gate_ref[0], jnp.zeros((LANES - TOP_K, tm), F32)], axis=0).T
    half = D_MODEL // 2
    lo_acc = [base_ref[:, hh * HALF_WORDS:(hh + 1) * HALF_WORDS] for hh in range(ROW_HALVES)]
    hi_acc = [base_ref[:, half + hh * HALF_WORDS:half + (hh + 1) * HALF_WORDS] for hh in range(ROW_HALVES)]
    for k in range(TOP_K):
        gk = gates_t[:, k:k + 1]
        for hh in range(ROW_HALVES):
            lo, hi = _unpack_pairs(rows_ref[k, hh])
            lo_acc[hh] = lo_acc[hh] + gk * lo
            hi_acc[hh] = hi_acc[hh] + gk * hi
    o_ref[...] = _layer_norm(jnp.concatenate(lo_acc + hi_acc, axis=-1), g2_ref[...], b2_ref[...])


def _combine(gates, base, g2, b2, rows):
    t = base.shape[0]
    tm = gates.shape[-1]
    return pl.pallas_call(
        _combine_kernel,
        grid=(t // tm,),
        in_specs=[pl.BlockSpec((1, TOP_K, tm), lambda i: (i, 0, 0)),
                  pl.BlockSpec((tm, D_MODEL), lambda i: (i, 0)),
                  pl.BlockSpec((1, D_MODEL), lambda i: (0, 0)),
                  pl.BlockSpec((1, D_MODEL), lambda i: (0, 0)),
                  pl.BlockSpec((TOP_K, ROW_HALVES, tm, HALF_WORDS), lambda i: (0, 0, i, 0))],
        out_specs=pl.BlockSpec((tm, D_MODEL), lambda i: (i, 0)),
        out_shape=jax.ShapeDtypeStruct((t, D_MODEL), F32),
        compiler_params=pltpu.CompilerParams(dimension_semantics=("arbitrary",),
                                             vmem_limit_bytes=48 * 1024 * 1024),
        name="combine",
    )(gates, base, g2, b2, rows)


def _layer(h2, w_in, conv_w, conv_b, dt_bias, a_log, d_skip, ssd_norm_w, lower_bound, hgrn_norm_w, w_out,
           ln1_g, ln1_b, w_router, router_bias, w_gate_e, w_up_e, w_down_e, w_gate_s, w_up_s, w_down_s,
           ln2_g, ln2_b):
    t = h2.shape[0]
    dt0 = D_SSM + D_CONV
    q0 = dt0 + SSD_HEADS
    w_perm = jnp.concatenate(
        [w_in[:, :dt0], w_in[:, q0:], w_in[:, dt0:q0], jnp.zeros((D_MODEL, DT_PAD - SSD_HEADS), w_in.dtype)],
        axis=1).astype(BF16)
    z, xbc, q, f, i, g, dt = _inproj(h2, w_perm)
    y_ssd = _ssd(z, xbc, dt, conv_w, conv_b, dt_bias, a_log, d_skip, ssd_norm_w)
    y_hgrn = _hgrn(q, f, i, g, lower_bound, hgrn_norm_w)

    wr_t = w_router.astype(F32).T
    wr_hi = wr_t.astype(BF16)
    wr_lo = (wr_t - wr_hi.astype(F32)).astype(BF16)
    row = lambda v: v.reshape(1, -1).astype(F32)
    htt, base, idx, gates = _post(
        h2, y_ssd, y_hgrn, w_out.astype(BF16), row(ln1_g), row(ln1_b), wr_hi, wr_lo,
        router_bias.reshape(N_EXPERTS, 1).astype(F32),
        w_gate_s.astype(BF16), w_up_s.astype(BF16), w_down_s.astype(BF16))

    dest, bstart, nblk = _positions(idx)
    n_rows = _max_blocks(t) * EXPERT_ROWS
    d = jnp.transpose(dest, (1, 0, 2)).reshape(TOP_K, 1, t)
    slot = (d // EXPERT_ROWS) * (EXPERT_ROWS * ROW_HALVES) + d % EXPERT_ROWS
    slot = (slot + jnp.arange(ROW_HALVES, dtype=I32).reshape(1, ROW_HALVES, 1) * EXPERT_ROWS).reshape(-1)
    xs = _sc_dispatch(htt.reshape(ROW_HALVES * t, HALF_WORDS), slot, n_rows * ROW_HALVES)
    ys = _experts(bstart[:, 0], nblk[:, 0], xs, w_gate_e, w_up_e, w_down_e)
    rows = _sc_gather(ys, slot).reshape(TOP_K, ROW_HALVES, t, HALF_WORDS)
    return _combine(gates, base, row(ln2_g), row(ln2_b), rows)


def kernel(x, w_in, conv_w, conv_b, dt_bias, a_log, d_skip, ssd_norm_w, hgrn_lb_logits, hgrn_norm_w, w_out,
           ln1_g, ln1_b, w_router, router_bias, w_gate_e, w_up_e, w_down_e, w_gate_s, w_up_s, w_down_s,
           ln2_g, ln2_b):
    bsz, t, d = x.shape
    assert bsz == 1 and d == D_MODEL, "the recurrent mixers carry state across the flattened token axis"
    depth = w_in.shape[0]
    lower_bounds = jnp.cumsum(jax.nn.softmax(hgrn_lb_logits.astype(F32), axis=0), axis=0)
    h = x.reshape(bsz * t, d)
    for l in range(depth):
        h = _layer(h, w_in[l], conv_w[l], conv_b[l], dt_bias[l], a_log[l], d_skip[l], ssd_norm_w[l],
                   lower_bounds[l], hgrn_norm_w[l], w_out[l], ln1_g[l], ln1_b[l], w_router[l],
                   router_bias[l], w_gate_e[l], w_up_e[l], w_down_e[l], w_gate_s[l], w_up_s[l],
                   w_down_s[l], ln2_g[l], ln2_b[l])
    return h.reshape(bsz, t, d)
```

```python
import jax
import jax.numpy as jnp
from jax import lax
from jax.experimental import pallas as pl
from jax.experimental.pallas import tpu as pltpu
from jax.experimental.pallas import tpu_sc as plsc

F32 = jnp.float32
BF16 = jnp.bfloat16
I32 = jnp.int32
U32 = jnp.uint32

D_MODEL = 1024
D_SSM = 512
D_HGRN = 512
SSD_HEADS = 8
SSD_HEAD_DIM = 64
SSD_GROUPS = 2
SSD_STATE = 128
SSD_CONV = 4
SSD_CHUNK = 128
SSD_PER_STEP = 2
D_CONV = D_SSM + 2 * SSD_GROUPS * SSD_STATE
HGRN_HEADS = 4
HGRN_DK = 128
HGRN_CHUNK = 64
HGRN_SUB = 8
HGRN_PER_STEP = 2
N_EXPERTS = 256
TOP_K = 8
N_EXPERT_GROUPS = 8
TOPK_GROUPS = 4
D_EXPERT = 256
ROUTED_SCALE = 2.5
ALPHA = 2.0 ** 0.25
LN_EPS = 1e-5
RMS_EPS = 1e-6

LANES = 128
SUBLANES = 8
ROW_TILES = D_MODEL // LANES
ROW_HALVES = 2
HALF_WORDS = D_MODEL // 2 // ROW_HALVES
SC_WINDOW = 128
DT_PAD = LANES
N_IN_PAD = D_SSM + D_CONV + 4 * D_HGRN + DT_PAD

TM_PROJ = 256
TM_TOK = 256
EXPERT_ROWS = 256
MAX_CHUNK = 2
X_RING = 8
Y_RING = 8
NEG = -1e30


def _sigmoid(x):
    return 1.0 / (1.0 + jnp.exp(-x))


def _silu(x):
    return x * _sigmoid(x)


def _split3(x):
    hi = x.astype(BF16)
    r = x - hi.astype(F32)
    mid = r.astype(BF16)
    lo = (r - mid.astype(F32)).astype(BF16)
    return hi, mid, lo


def _dot(a, b):
    return jnp.dot(a, b, preferred_element_type=F32)


def _dot_nt(a, b):
    return lax.dot_general(a, b, (((1,), (1,)), ((), ())), preferred_element_type=F32)


def _dot_tn(a, b):
    return lax.dot_general(a, b, (((0,), (0,)), ((), ())), preferred_element_type=F32)


def _sel_dot(sel, x):
    hi, mid, lo = _split3(x)
    return _dot(sel, hi) + _dot(sel, mid) + _dot(sel, lo)


def _dot_sel(x, sel):
    hi, mid, lo = _split3(x)
    return _dot(hi, sel) + _dot(mid, sel) + _dot(lo, sel)


def _pack_pairs(lo, hi):
    lo_bits = pltpu.bitcast(lo.astype(BF16).astype(F32), U32) >> 16
    hi_bits = pltpu.bitcast(hi.astype(BF16).astype(F32), U32) & jnp.uint32(0xFFFF0000)
    return hi_bits | lo_bits


def _unpack_pairs(p):
    return pltpu.bitcast(p << 16, F32), pltpu.bitcast(p & jnp.uint32(0xFFFF0000), F32)


def _layer_norm(x, g, b):
    mu = jnp.mean(x, axis=-1, keepdims=True)
    xc = x - mu
    var = jnp.mean(xc * xc, axis=-1, keepdims=True)
    return xc * lax.rsqrt(var + LN_EPS) * g + b


def _inproj_kernel(x_ref, w_ref, z_ref, xbc_ref, q_ref, f_ref, i_ref, g_ref, dt_ref):
    xb = x_ref[...].astype(BF16)
    col = 0
    for ref in (z_ref, xbc_ref, q_ref, f_ref, i_ref, g_ref, dt_ref):
        n = ref.shape[-1]
        ref[...] = _dot(xb, w_ref[:, col:col + n])
        col += n


def _inproj(x2, w_perm):
    t = x2.shape[0]
    widths = (D_SSM, D_CONV, D_HGRN, D_HGRN, D_HGRN, D_HGRN, DT_PAD)
    return pl.pallas_call(
        _inproj_kernel,
        grid=(t // TM_PROJ,),
        in_specs=[pl.BlockSpec((TM_PROJ, D_MODEL), lambda i: (i, 0)),
                  pl.BlockSpec((D_MODEL, N_IN_PAD), lambda i: (0, 0))],
        out_specs=[pl.BlockSpec((TM_PROJ, n), lambda i: (i, 0)) for n in widths],
        out_shape=[jax.ShapeDtypeStruct((t, n), F32) for n in widths],
        compiler_params=pltpu.CompilerParams(dimension_semantics=("arbitrary",),
                                             vmem_limit_bytes=48 * 1024 * 1024),
        name="inproj",
    )(x2, w_perm)


def _ssd_kernel(z_ref, xbc_ref, dt_ref, cw_ref, cb_ref, dtb_ref, alog_ref, dskip_ref, nw_ref,
                y_ref, ext_ref, st_ref):
    n = xbc_ref.shape[0]
    halo = SUBLANES

    @pl.when(pl.program_id(0) == 0)
    def _():
        ext_ref[0:halo, :] = jnp.zeros((halo, D_CONV), F32)
        st_ref[...] = jnp.zeros(st_ref.shape, F32)

    ext_ref[halo:halo + n, :] = xbc_ref[...]
    acc = jnp.broadcast_to(cb_ref[...], (n, D_CONV))
    for k in range(SSD_CONV):
        off = halo - (SSD_CONV - 1) + k
        acc = acc + cw_ref[k:k + 1, :] * ext_ref[off:off + n, :]
    ext_ref[0:halo, :] = xbc_ref[n - halo:n, :]
    u = _silu(acc)
    for c in range(n // SSD_CHUNK):
        rows = slice(c * SSD_CHUNK, (c + 1) * SSD_CHUNK)
        y_ref[rows, :] = _ssd_chunk(u[rows, :], z_ref[rows, :], dt_ref[rows, :] + dtb_ref[...],
                                    -jnp.exp(alog_ref[...]), dskip_ref[...], nw_ref[...], st_ref)


def _ssd_chunk(u, z, draw, a, dskip, nw, st_ref):
    L = SSD_CHUNK
    xs = u[:, :D_SSM]
    bm = u[:, D_SSM:D_SSM + SSD_GROUPS * SSD_STATE]
    cm = u[:, D_SSM + SSD_GROUPS * SSD_STATE:]

    dt = jnp.maximum(draw, 0.0) + jnp.log(1.0 + jnp.exp(-jnp.abs(draw)))
    ad = dt * a
    rows = lax.broadcasted_iota(I32, (L, L), 0)
    cols = lax.broadcasted_iota(I32, (L, L), 1)
    causal = rows >= cols
    a_cum = _sel_dot(causal.astype(BF16), ad)
    a_cum_t = a_cum.T

    hrow = lax.broadcasted_iota(I32, (LANES, D_SSM), 0)
    hcol = lax.broadcasted_iota(I32, (LANES, D_SSM), 1) // SSD_HEAD_DIM
    expand = (hrow == hcol).astype(BF16)
    dt_x = _dot_sel(dt, expand)
    acx = _dot_sel(a_cum, expand)
    last = acx[L - 1:L, :]
    ea_x = jnp.exp(acx)
    dec_x = jnp.exp(last - acx)
    elast_x = jnp.exp(last)

    xdt = xs * dt_x
    gw = SSD_HEADS // SSD_GROUPS * SSD_HEAD_DIM
    lane_head = lax.broadcasted_iota(I32, (L, gw), 1) // SSD_HEAD_DIM
    ys = []
    for g in range(SSD_GROUPS):
        bg = bm[:, g * SSD_STATE:(g + 1) * SSD_STATE]
        cg = cm[:, g * SSD_STATE:(g + 1) * SSD_STATE].astype(BF16)
        bg_t = bg.T.astype(BF16)
        gmat = _dot(cg, bg_t)
        xdt_g = xdt[:, g * gw:(g + 1) * gw]
        xdt_gb = xdt_g.astype(BF16)
        r_prev = st_ref[g]
        y_g = _dot(cg, r_prev.astype(BF16)) * ea_x[:, g * gw:(g + 1) * gw]
        new_s = _dot(bg_t, (xdt_g * dec_x[:, g * gw:(g + 1) * gw]).astype(BF16))
        st_ref[g] = r_prev * elast_x[:, g * gw:(g + 1) * gw] + new_s
        for j in range(SSD_HEADS // SSD_GROUPS):
            h = g * (SSD_HEADS // SSD_GROUPS) + j
            diff = a_cum[:, h:h + 1] - a_cum_t[h:h + 1, :]
            decay = jnp.exp(jnp.where(causal, diff, NEG))
            yd = _dot((gmat * decay).astype(BF16), xdt_gb)
            y_g = y_g + jnp.where(lane_head == j, yd, 0.0)
        ys.append(y_g)
    y = jnp.concatenate(ys, axis=-1) + xs * dskip
    y = y * _silu(z)
    outs = []
    ng = D_SSM // SSD_GROUPS
    for g in range(SSD_GROUPS):
        yg = y[:, g * ng:(g + 1) * ng]
        ms = jnp.mean(yg * yg, axis=-1, keepdims=True)
        outs.append(yg * lax.rsqrt(ms + RMS_EPS))
    return jnp.concatenate(outs, axis=-1) * nw


def _ssd(z, xbc, dt, conv_w, conv_b, dt_bias, a_log, d_skip, norm_w):
    t = z.shape[0]
    L = SSD_CHUNK * SSD_PER_STEP
    pad = lambda v: jnp.pad(v.astype(F32), (0, LANES - v.shape[0])).reshape(1, LANES)
    full = lambda shape: pl.BlockSpec(shape, lambda c: (0,) * len(shape))
    return pl.pallas_call(
        _ssd_kernel,
        grid=(t // L,),
        in_specs=[pl.BlockSpec((L, D_SSM), lambda c: (c, 0)),
                  pl.BlockSpec((L, D_CONV), lambda c: (c, 0)),
                  pl.BlockSpec((L, DT_PAD), lambda c: (c, 0)),
                  full((SSD_CONV, D_CONV)), full((1, D_CONV)), full((1, LANES)), full((1, LANES)),
                  full((1, D_SSM)), full((1, D_SSM))],
        out_specs=pl.BlockSpec((L, D_SSM), lambda c: (c, 0)),
        out_shape=jax.ShapeDtypeStruct((t, D_SSM), F32),
        scratch_shapes=[pltpu.VMEM((L + SUBLANES, D_CONV), F32),
                        pltpu.VMEM((SSD_GROUPS, SSD_STATE, D_SSM // SSD_GROUPS), F32)],
        compiler_params=pltpu.CompilerParams(dimension_semantics=("arbitrary",)),
        name="ssd",
    )(z, xbc, dt, conv_w.astype(F32), conv_b.reshape(1, D_CONV).astype(F32), pad(dt_bias), pad(a_log),
      jnp.repeat(d_skip.astype(F32), SSD_HEAD_DIM).reshape(1, D_SSM), norm_w.reshape(1, D_SSM).astype(F32))


def _tile_bcast(x, r):
    n, d = x.shape
    x3 = x.reshape(n // SUBLANES, SUBLANES, d)
    return jnp.broadcast_to(x3[:, r:r + 1, :], x3.shape).reshape(n, d)


def _hgrn_kernel(q_ref, f_ref, i_ref, g_ref, lb_ref, nw_ref, o_ref, st_ref):
    @pl.when(pl.program_id(0) == 0)
    def _():
        st_ref[...] = jnp.zeros(st_ref.shape, F32)

    for c in range(o_ref.shape[0] // HGRN_CHUNK):
        rows = slice(c * HGRN_CHUNK, (c + 1) * HGRN_CHUNK)
        o_ref[rows, :] = _hgrn_chunk(q_ref[rows, :], f_ref[rows, :], i_ref[rows, :], g_ref[rows, :],
                                     lb_ref[...], nw_ref[...], st_ref)


def _hgrn_chunk(q, f, v, g, lb, nw, st_ref):
    C = HGRN_CHUNK
    S = HGRN_SUB
    nsub = C // S
    fg = lb + (1.0 - lb) * _sigmoid(f)
    qs = _silu(q)
    rows = lax.broadcasted_iota(I32, (C, C), 0)
    cols = lax.broadcasted_iota(I32, (C, C), 1)
    cum = _sel_dot((rows >= cols).astype(BF16), jnp.log2(fg))
    ck_all = cum - jnp.log2(1.0 - fg)

    srow = lax.broadcasted_iota(I32, (S * HGRN_DK, C), 0) // HGRN_DK
    scol = lax.broadcasted_iota(I32, (S * HGRN_DK, C), 1) % S
    spread = (srow == scol).astype(BF16)
    diag_keep = jnp.logical_and(rows // S == cols // S, rows >= cols)
    outs = []
    for h in range(HGRN_HEADS):
        blk = slice(h * HGRN_DK, (h + 1) * HGRN_DK)
        qh, ch, ck = qs[:, blk], cum[:, blk], ck_all[:, blk]
        last = ch[C - 1:C, :]
        x_cat = jnp.concatenate(
            [(qh * jnp.exp2(jnp.minimum(ch - _tile_bcast(ck, r), 0.0))).astype(BF16) for r in range(S)], axis=-1)
        kbe = jnp.exp2(_tile_bcast(ch, S - 1) - ck).astype(BF16)
        a_pieces, b_pieces = [], []
        for j in range(nsub - 1):
            lo = (j + 1) * S
            a = (qh[lo:, :] * jnp.exp2(ch[lo:, :] - ch[lo - 1:lo, :])).astype(BF16)
            a_pieces.append(jnp.concatenate([jnp.zeros((lo, HGRN_DK), BF16), a], axis=0))
            b = [kbe[j * S:lo, :], jnp.zeros((C - lo, HGRN_DK), BF16)]
            if j > 0:
                b.insert(0, jnp.zeros((j * S, HGRN_DK), BF16))
            b_pieces.append(jnp.concatenate(b, axis=0))
        att = (_dot_nt(jnp.concatenate(a_pieces, axis=-1), jnp.concatenate(b_pieces, axis=-1))
               + jnp.where(diag_keep, _dot(x_cat, spread), 0.0))
        vb = v[:, blk].astype(BF16)
        s_prev = st_ref[h]
        o = _dot_nt((qh * jnp.exp2(ch)).astype(BF16), s_prev.astype(BF16)) + _dot(att.astype(BF16), vb)
        st_ref[h] = s_prev * jnp.exp2(last) + _dot_tn(vb, jnp.exp2(last - ck).astype(BF16))
        ms = jnp.mean(o * o, axis=-1, keepdims=True)
        outs.append(o * lax.rsqrt(ms + RMS_EPS))
    return jnp.concatenate(outs, axis=-1) * nw * _silu(g)


def _hgrn(q, f, i, g, lower_bound, norm_w):
    t = q.shape[0]
    C = HGRN_CHUNK * HGRN_PER_STEP
    tok = pl.BlockSpec((C, D_HGRN), lambda c: (c, 0))
    vec = pl.BlockSpec((1, D_HGRN), lambda c: (0, 0))
    return pl.pallas_call(
        _hgrn_kernel,
        grid=(t // C,),
        in_specs=[tok, tok, tok, tok, vec, vec],
        out_specs=tok,
        out_shape=jax.ShapeDtypeStruct((t, D_HGRN), F32),
        scratch_shapes=[pltpu.VMEM((HGRN_HEADS, D_HGRN // HGRN_HEADS, HGRN_DK), F32)],
        compiler_params=pltpu.CompilerParams(dimension_semantics=("arbitrary",)),
        name="hgrn",
    )(q, f, i, g, lower_bound.reshape(1, D_HGRN).astype(F32), norm_w.reshape(1, D_HGRN).astype(F32))


def _post_kernel(x_ref, ys_ref, yh_ref, wo_ref, g1_ref, b1_ref, wrh_ref, wrl_ref, rb_ref,
                 wgs_ref, wus_ref, wds_ref, htt_ref, base_ref, idx_ref, gate_ref, cnt_ref):
    tm = x_ref.shape[0]
    mix = (_dot(ys_ref[...].astype(BF16), wo_ref[0:D_SSM, :])
           + _dot(yh_ref[...].astype(BF16), wo_ref[D_SSM:, :]))
    h1 = _layer_norm(ALPHA * x_ref[...] + mix, g1_ref[...], b1_ref[...])
    packed = _pack_pairs(h1[:, :D_MODEL // 2], h1[:, D_MODEL // 2:])
    for hh in range(ROW_HALVES):
        htt_ref[hh] = packed[:, hh * HALF_WORDS:(hh + 1) * HALF_WORDS]
    hb = h1.astype(BF16)
    hid = _silu(_dot(hb, wgs_ref[...])) * _dot(hb, wus_ref[...])
    base_ref[...] = ALPHA * h1 + _dot(hid.astype(BF16), wds_ref[...])

    hlo = (h1 - hb.astype(F32)).astype(BF16)
    logits = _dot_nt(wrh_ref[...], hb) + _dot_nt(wrh_ref[...], hlo) + _dot_nt(wrl_ref[...], hb)
    scores = _sigmoid(logits)
    biased = scores + rb_ref[...]
    per_group = N_EXPERTS // N_EXPERT_GROUPS
    eidx = lax.broadcasted_iota(I32, (N_EXPERTS, tm), 0)
    big = jnp.int32(1 << 20)
    gsc = []
    bidx = lax.broadcasted_iota(I32, (per_group, tm), 0)
    for gi in range(N_EXPERT_GROUPS):
        blk = biased[gi * per_group:(gi + 1) * per_group, :]
        m1 = jnp.max(blk, axis=0, keepdims=True)
        i1 = jnp.min(jnp.where(blk == m1, bidx, big), axis=0, keepdims=True)
        m2 = jnp.max(jnp.where(bidx == i1, NEG, blk), axis=0, keepdims=True)
        gsc.append(m1 + m2)
    cur = jnp.concatenate(gsc, axis=0)
    gidx = lax.broadcasted_iota(I32, (N_EXPERT_GROUPS, tm), 0)
    gsel = jnp.zeros((N_EXPERT_GROUPS, tm), F32)
    for _ in range(TOPK_GROUPS):
        m = jnp.max(cur, axis=0, keepdims=True)
        i = jnp.min(jnp.where(cur == m, gidx, big), axis=0, keepdims=True)
        hit = gidx == i
        gsel = jnp.where(hit, 1.0, gsel)
        cur = jnp.where(hit, NEG, cur)
    emask = jnp.concatenate(
        [jnp.broadcast_to(gsel[gi:gi + 1, :], (per_group, tm)) for gi in range(N_EXPERT_GROUPS)], axis=0)
    masked = jnp.where(emask > 0.0, biased, NEG)
    idx_rows, gate_rows = [], []
    picked = jnp.zeros((N_EXPERTS, tm), F32)
    for _ in range(TOP_K):
        m = jnp.max(masked, axis=0, keepdims=True)
        i = jnp.min(jnp.where(masked == m, eidx, big), axis=0, keepdims=True)
        hit = eidx == i
        idx_rows.append(i)
        gate_rows.append(jnp.sum(jnp.where(hit, scores, 0.0), axis=0, keepdims=True))
        masked = jnp.where(hit, NEG, masked)
        picked = jnp.where(hit, 1.0, picked)
    gates = jnp.concatenate(gate_rows, axis=0)
    gates = gates / jnp.sum(gates, axis=0, keepdims=True) * ROUTED_SCALE
    idx_ref[0] = jnp.concatenate(idx_rows, axis=0)
    gate_ref[0] = gates

    @pl.when(pl.program_id(0) == 0)
    def _():
        cnt_ref[...] = jnp.zeros(cnt_ref.shape, F32)

    cnt_ref[...] += _dot(picked.astype(BF16), jnp.ones((tm, LANES), BF16))


def _post(x2, y_ssd, y_hgrn, wo, g1, b1, wr_hi, wr_lo, rbias, wgs, wus, wds):
    t = x2.shape[0]
    tm = TM_TOK
    nt = t // tm
    full = lambda shape: pl.BlockSpec(shape, lambda i: (0,) * len(shape))
    return pl.pallas_call(
        _post_kernel,
        grid=(nt,),
        in_specs=[pl.BlockSpec((tm, D_MODEL), lambda i: (i, 0)),
                  pl.BlockSpec((tm, D_SSM), lambda i: (i, 0)),
                  pl.BlockSpec((tm, D_HGRN), lambda i: (i, 0)),
                  full((D_MODEL, D_MODEL)), full((1, D_MODEL)), full((1, D_MODEL)),
                  full((N_EXPERTS, D_MODEL)), full((N_EXPERTS, D_MODEL)), full((N_EXPERTS, 1)),
                  full((D_MODEL, D_EXPERT)), full((D_MODEL, D_EXPERT)), full((D_EXPERT, D_MODEL))],
        out_specs=[pl.BlockSpec((ROW_HALVES, tm, HALF_WORDS), lambda i: (0, i, 0)),
                   pl.BlockSpec((tm, D_MODEL), lambda i: (i, 0)),
                   pl.BlockSpec((1, TOP_K, tm), lambda i: (i, 0, 0)),
                   pl.BlockSpec((1, TOP_K, tm), lambda i: (i, 0, 0)),
                   pl.BlockSpec((N_EXPERTS, LANES), lambda i: (0, 0))],
        out_shape=[jax.ShapeDtypeStruct((ROW_HALVES, t, HALF_WORDS), U32),
                   jax.ShapeDtypeStruct((t, D_MODEL), F32),
                   jax.ShapeDtypeStruct((nt, TOP_K, tm), I32),
                   jax.ShapeDtypeStruct((nt, TOP_K, tm), F32),
                   jax.ShapeDtypeStruct((N_EXPERTS, LANES), F32)],
        compiler_params=pltpu.CompilerParams(dimension_semantics=("arbitrary",),
                                             vmem_limit_bytes=48 * 1024 * 1024),
        name="post",
    )(x2, y_ssd, y_hgrn, wo, g1, b1, wr_hi, wr_lo, rbias, wgs, wus, wds)


def _max_blocks(t):
    return (t * TOP_K + N_EXPERTS * (EXPERT_ROWS - 1)) // EXPERT_ROWS


def _pos_kernel(idx_ref, total_ref, dest_ref, bstart_ref, nblk_ref, cnt_ref, start_ref):
    i = pl.program_id(0)
    tm = idx_ref.shape[-1]
    idx = idx_ref[0]
    eidx = lax.broadcasted_iota(I32, (N_EXPERTS, tm), 0)
    sel = [eidx == idx[k:k + 1, :] for k in range(TOP_K)]
    onehot = sel[0]
    for k in range(1, TOP_K):
        onehot = jnp.logical_or(onehot, sel[k])
    mt = jnp.where(onehot, 1.0, 0.0).astype(BF16)

    @pl.when(i == 0)
    def _():
        nb = jnp.floor((total_ref[...] + (EXPERT_ROWS - 1)) * (1.0 / EXPERT_ROWS))
        r = lax.broadcasted_iota(I32, (N_EXPERTS, N_EXPERTS), 0)
        c = lax.broadcasted_iota(I32, (N_EXPERTS, N_EXPERTS), 1)
        end = _dot((r >= c).astype(BF16), nb.astype(BF16))
        start_ref[...] = (end - nb) * EXPERT_ROWS
        cnt_ref[...] = jnp.zeros(cnt_ref.shape, F32)
        bstart_ref[...] = (end - nb).astype(I32)
        nblk_ref[...] = nb.astype(I32)

    r = lax.broadcasted_iota(I32, (tm, tm), 0)
    c = lax.broadcasted_iota(I32, (tm, tm), 1)
    before = _dot(mt, (r < c).astype(BF16))
    slot = start_ref[:, 0:1] + cnt_ref[:, 0:1] + before
    rows = [jnp.sum(jnp.where(sel[k], slot, 0.0), axis=0, keepdims=True) for k in range(TOP_K)]
    dest_ref[0] = jnp.concatenate(rows, axis=0).astype(I32)
    cnt_ref[...] += _dot(mt, jnp.ones((tm, LANES), BF16))


def _positions(idx, totals):
    nt, _, tm = idx.shape
    return pl.pallas_call(
        _pos_kernel,
        grid=(nt,),
        in_specs=[pl.BlockSpec((1, TOP_K, tm), lambda i: (i, 0, 0)),
                  pl.BlockSpec((N_EXPERTS, LANES), lambda i: (0, 0))],
        out_specs=[pl.BlockSpec((1, TOP_K, tm), lambda i: (i, 0, 0)),
                   pl.BlockSpec((N_EXPERTS, LANES), lambda i: (0, 0)),
                   pl.BlockSpec((N_EXPERTS, LANES), lambda i: (0, 0))],
        out_shape=[jax.ShapeDtypeStruct((nt, TOP_K, tm), I32),
                   jax.ShapeDtypeStruct((N_EXPERTS, LANES), I32),
                   jax.ShapeDtypeStruct((N_EXPERTS, LANES), I32)],
        scratch_shapes=[pltpu.VMEM((N_EXPERTS, LANES), F32), pltpu.VMEM((N_EXPERTS, LANES), F32)],
        compiler_params=pltpu.CompilerParams(dimension_semantics=("arbitrary",)),
        name="positions",
    )(idx, totals)


def _sc_mesh():
    return plsc.VectorSubcoreMesh(core_axis_name="core", subcore_axis_name="subcore")


def _sc_dispatch(src, index, n_out):
    n = index.shape[0]
    src_blocks = src.shape[0] // SC_WINDOW
    index = index.reshape(1, n)

    @pl.kernel(out_type=jax.ShapeDtypeStruct((n_out, HALF_WORDS), src.dtype), mesh=_sc_mesh(), name="sc_dispatch")
    def scatter(src_hbm, idx_hbm, out_hbm):
        def body(rows_vmem, idx_vmem):
            pltpu.sync_copy(rows_vmem, out_hbm.at[idx_vmem.at[0]])

        pltpu.emit_pipeline(
            body, grid=(n // SC_WINDOW,),
            in_specs=[pl.BlockSpec((SC_WINDOW, HALF_WORDS), index_map=lambda i: (i % src_blocks, 0)),
                      pl.BlockSpec((1, SC_WINDOW), index_map=lambda i: (0, i))],
            out_specs=[],
            core_axis_name=("core", "subcore"), dimension_semantics=(pltpu.PARALLEL,))(src_hbm, idx_hbm)

    return scatter(src, index)


def _sc_gather(src, index):
    n = index.shape[0]
    index = index.reshape(1, n)

    @pl.kernel(out_type=jax.ShapeDtypeStruct((n, HALF_WORDS), src.dtype), mesh=_sc_mesh(), name="sc_gather")
    def gather(src_hbm, idx_hbm, out_hbm):
        def body(idx_vmem, rows_vmem):
            pltpu.sync_copy(src_hbm.at[idx_vmem.at[0]], rows_vmem)

        pltpu.emit_pipeline(
            body, grid=(n // SC_WINDOW,),
            in_specs=[pl.BlockSpec((1, SC_WINDOW), index_map=lambda i: (0, i))],
            out_specs=[pl.BlockSpec((SC_WINDOW, HALF_WORDS), index_map=lambda i: (i, 0))],
            core_axis_name=("core", "subcore"), dimension_semantics=(pltpu.PARALLEL,))(idx_hbm, out_hbm)

    return gather(src, index)


def _expert_kernel(bstart_ref, nblk_ref, xs_hbm, wg_ref, wu_ref, wd_ref, ys_hbm,
                   xbuf, ybuf, wgb_ref, wub_ref, wdb_ref, xsem, ysem, ypend):
    e = pl.program_id(0)
    last = pl.num_programs(0) - 1
    n = nblk_ref[e]
    b0 = bstart_ref[e]
    total = bstart_ref[last] + nblk_ref[last]
    blk_rows = EXPERT_ROWS * ROW_HALVES
    xdepth = xbuf.shape[0]
    ydepth = ybuf.shape[0]
    lookahead = xdepth - MAX_CHUNK

    def hbm_block(ref, b):
        return ref.at[pl.ds(pl.multiple_of(b * blk_rows, blk_rows), blk_rows)]

    def x_copy(b):
        slot = b % xdepth
        return pltpu.make_async_copy(hbm_block(xs_hbm, b), xbuf.at[slot], xsem.at[slot])

    def y_copy(b, slot):
        return pltpu.make_async_copy(ybuf.at[slot], hbm_block(ys_hbm, b), ysem.at[slot])

    @pl.when(e == 0)
    def _():
        for s in range(ydepth):
            ypend[s] = 0
        for b in range(lookahead):
            @pl.when(b < total)
            def _():
                x_copy(b).start()

    def chunk(j, m):
        b = b0 + j
        for i in range(m):
            @pl.when(b + lookahead + i < total)
            def _():
                x_copy(b + lookahead + i).start()

        for i in range(m):
            x_copy(b + i).wait()

            @pl.when(ypend[(b + i) % ydepth] == 1)
            def _():
                y_copy(b + i, (b + i) % ydepth).wait()

        for i in range(m):
            xslot = (b + i) % xdepth
            yslot = (b + i) % ydepth
            halves = [_unpack_pairs(xbuf[xslot, hh * EXPERT_ROWS:(hh + 1) * EXPERT_ROWS, :])
                      for hh in range(ROW_HALVES)]
            xb = jnp.concatenate([lo for lo, _ in halves] + [hi for _, hi in halves], axis=-1).astype(BF16)
            hid = _silu(_dot(xb, wgb_ref[...])) * _dot(xb, wub_ref[...])
            yb = _dot(hid.astype(BF16), wdb_ref[...])
            packed = _pack_pairs(yb[:, :D_MODEL // 2], yb[:, D_MODEL // 2:])
            for hh in range(ROW_HALVES):
                ybuf[yslot, hh * EXPERT_ROWS:(hh + 1) * EXPERT_ROWS, :] = (
                    packed[:, hh * HALF_WORDS:(hh + 1) * HALF_WORDS])

        for i in range(m):
            y_copy(b + i, (b + i) % ydepth).start()
            ypend[(b + i) % ydepth] = 1

    @pl.when(n > 0)
    def _():
        wgb_ref[...] = wg_ref[0].astype(BF16)
        wub_ref[...] = wu_ref[0].astype(BF16)
        wdb_ref[...] = wd_ref[0].astype(BF16)
        full = n // MAX_CHUNK

        def full_chunk(q, carry):
            chunk(q * MAX_CHUNK, MAX_CHUNK)
            return carry

        lax.fori_loop(0, full, full_chunk, 0)
        for m in range(1, MAX_CHUNK):
            @pl.when(n - full * MAX_CHUNK == m)
            def _():
                chunk(full * MAX_CHUNK, m)

    @pl.when(e == last)
    def _():
        for s in range(ydepth):
            @pl.when(ypend[s] == 1)
            def _():
                y_copy(b0, s).wait()


def _experts(bstart, nblk, xs, w_gate, w_up, w_down):
    blk_shape = (EXPERT_ROWS * ROW_HALVES, HALF_WORDS)
    wspec = lambda shape: pl.BlockSpec((1,) + shape, lambda e, bs, nb: (e, 0, 0))
    grid_spec = pltpu.PrefetchScalarGridSpec(
        num_scalar_prefetch=2,
        grid=(N_EXPERTS,),
        in_specs=[pl.BlockSpec(memory_space=pl.ANY),
                  wspec((D_MODEL, D_EXPERT)), wspec((D_MODEL, D_EXPERT)), wspec((D_EXPERT, D_MODEL))],
        out_specs=pl.BlockSpec(memory_space=pl.ANY),
        scratch_shapes=[pltpu.VMEM((X_RING,) + blk_shape, U32), pltpu.VMEM((Y_RING,) + blk_shape, U32),
                        pltpu.VMEM((D_MODEL, D_EXPERT), BF16), pltpu.VMEM((D_MODEL, D_EXPERT), BF16),
                        pltpu.VMEM((D_EXPERT, D_MODEL), BF16),
                        pltpu.SemaphoreType.DMA((X_RING,)), pltpu.SemaphoreType.DMA((Y_RING,)),
                        pltpu.SMEM((Y_RING,), I32)],
    )
    return pl.pallas_call(
        _expert_kernel,
        grid_spec=grid_spec,
        out_shape=jax.ShapeDtypeStruct(xs.shape, U32),
        compiler_params=pltpu.CompilerParams(dimension_semantics=("arbitrary",), has_side_effects=True),
        name="experts",
    )(bstart, nblk, xs, w_gate, w_up, w_down)


def _combine_kernel(gate_ref, base_ref, g2_ref, b2_ref, rows_ref, o_ref):
    tm = base_ref.shape[0]
    gates_t = jnp.concatenate([gate_ref[0], jnp.zeros((LANES - TOP_K, tm), F32)], axis=0).T
    half = D_MODEL // 2
    lo_acc = [base_ref[:, hh * HALF_WORDS:(hh + 1) * HALF_WORDS] for hh in range(ROW_HALVES)]
    hi_acc = [base_ref[:, half + hh * HALF_WORDS:half + (hh + 1) * HALF_WORDS] for hh in range(ROW_HALVES)]
    for k in range(TOP_K):
        gk = gates_t[:, k:k + 1]
        for hh in range(ROW_HALVES):
            lo, hi = _unpack_pairs(rows_ref[k, hh])
            lo_acc[hh] = lo_acc[hh] + gk * lo
            hi_acc[hh] = hi_acc[hh] + gk * hi
    o_ref[...] = _layer_norm(jnp.concatenate(lo_acc + hi_acc, axis=-1), g2_ref[...], b2_ref[...])


def _combine(gates, base, g2, b2, rows):
    t = base.shape[0]
    tm = gates.shape[-1]
    return pl.pallas_call(
        _combine_kernel,
        grid=(t // tm,),
        in_specs=[pl.BlockSpec((1, TOP_K, tm), lambda i: (i, 0, 0)),
                  pl.BlockSpec((tm, D_MODEL), lambda i: (i, 0)),
                  pl.BlockSpec((1, D_MODEL), lambda i: (0, 0)),
                  pl.BlockSpec((1, D_MODEL), lambda i: (0, 0)),
                  pl.BlockSpec((TOP_K, ROW_HALVES, tm, HALF_WORDS), lambda i: (0, 0, i, 0))],
        out_specs=pl.BlockSpec((tm, D_MODEL), lambda i: (i, 0)),
        out_shape=jax.ShapeDtypeStruct((t, D_MODEL), F32),
        compiler_params=pltpu.CompilerParams(dimension_semantics=("arbitrary",),
                                             vmem_limit_bytes=48 * 1024 * 1024),
        name="combine",
    )(gates, base, g2, b2, rows)


def _layer(h2, w_in, conv_w, conv_b, dt_bias, a_log, d_skip, ssd_norm_w, lower_bound, hgrn_norm_w, w_out,
           ln1_g, ln1_b, w_router, router_bias, w_gate_e, w_up_e, w_down_e, w_gate_s, w_up_s, w_down_s,
           ln2_g, ln2_b):
    t = h2.shape[0]
    dt0 = D_SSM + D_CONV
    q0 = dt0 + SSD_HEADS
    w_perm = jnp.concatenate(
        [w_in[:, :dt0], w_in[:, q0:], w_in[:, dt0:q0], jnp.zeros((D_MODEL, DT_PAD - SSD_HEADS), w_in.dtype)],
        axis=1).astype(BF16)
    z, xbc, q, f, i, g, dt = _inproj(h2, w_perm)
    y_ssd = _ssd(z, xbc, dt, conv_w, conv_b, dt_bias, a_log, d_skip, ssd_norm_w)
    y_hgrn = _hgrn(q, f, i, g, lower_bound, hgrn_norm_w)

    wr_t = w_router.astype(F32).T
    wr_hi = wr_t.astype(BF16)
    wr_lo = (wr_t - wr_hi.astype(F32)).astype(BF16)
    row = lambda v: v.reshape(1, -1).astype(F32)
    htt, base, idx, gates, totals = _post(
        h2, y_ssd, y_hgrn, w_out.astype(BF16), row(ln1_g), row(ln1_b), wr_hi, wr_lo,
        router_bias.reshape(N_EXPERTS, 1).astype(F32),
        w_gate_s.astype(BF16), w_up_s.astype(BF16), w_down_s.astype(BF16))

    dest, bstart, nblk = _positions(idx, totals)
    n_rows = _max_blocks(t) * EXPERT_ROWS
    d = jnp.transpose(dest, (1, 0, 2)).reshape(TOP_K, 1, t)
    slot = (d // EXPERT_ROWS) * (EXPERT_ROWS * ROW_HALVES) + d % EXPERT_ROWS
    slot = (slot + jnp.arange(ROW_HALVES, dtype=I32).reshape(1, ROW_HALVES, 1) * EXPERT_ROWS).reshape(-1)
    xs = _sc_dispatch(htt.reshape(ROW_HALVES * t, HALF_WORDS), slot, n_rows * ROW_HALVES)
    ys = _experts(bstart[:, 0], nblk[:, 0], xs, w_gate_e, w_up_e, w_down_e)
    rows = _sc_gather(ys, slot).reshape(TOP_K, ROW_HALVES, t, HALF_WORDS)
    return _combine(gates, base, row(ln2_g), row(ln2_b), rows)


def kernel(x, w_in, conv_w, conv_b, dt_bias, a_log, d_skip, ssd_norm_w, hgrn_lb_logits, hgrn_norm_w, w_out,
           ln1_g, ln1_b, w_router, router_bias, w_gate_e, w_up_e, w_down_e, w_gate_s, w_up_s, w_down_s,
           ln2_g, ln2_b):
    bsz, t, d = x.shape
    assert bsz == 1 and d == D_MODEL, "the recurrent mixers carry state across the flattened token axis"
    depth = w_in.shape[0]
    lower_bounds = jnp.cumsum(jax.nn.softmax(hgrn_lb_logits.astype(F32), axis=0), axis=0)
    h = x.reshape(bsz * t, d)
    for l in range(depth):
        h = _layer(h, w_in[l], conv_w[l], conv_b[l], dt_bias[l], a_log[l], d_skip[l], ssd_norm_w[l],
                   lower_bounds[l], hgrn_norm_w[l], w_out[l], ln1_g[l], ln1_b[l], w_router[l],
                   router_bias[l], w_gate_e[l], w_up_e[l], w_down_e[l], w_gate_s[l], w_up_s[l],
                   w_down_s[l], ln2_g[l], ln2_b[l])
    return h.reshape(bsz, t, d)
```

```python
import jax
import jax.numpy as jnp
from jax import lax
from jax.experimental import pallas as pl
from jax.experimental.pallas import tpu as pltpu
from jax.experimental.pallas import tpu_sc as plsc

F32 = jnp.float32
BF16 = jnp.bfloat16
I32 = jnp.int32
U32 = jnp.uint32

D_MODEL = 1024
D_SSM = 512
D_HGRN = 512
SSD_HEADS = 8
SSD_HEAD_DIM = 64
SSD_GROUPS = 2
SSD_STATE = 128
SSD_CONV = 4
SSD_CHUNK = 128
SSD_PER_STEP = 2
D_CONV = D_SSM + 2 * SSD_GROUPS * SSD_STATE
HGRN_HEADS = 4
HGRN_DK = 128
HGRN_CHUNK = 64
HGRN_SUB = 8
HGRN_PER_STEP = 2
N_EXPERTS = 256
TOP_K = 8
N_EXPERT_GROUPS = 8
TOPK_GROUPS = 4
D_EXPERT = 256
ROUTED_SCALE = 2.5
ALPHA = 2.0 ** 0.25
LN_EPS = 1e-5
RMS_EPS = 1e-6

LANES = 128
SUBLANES = 8
ROW_TILES = D_MODEL // LANES
ROW_HALVES = 2
HALF_WORDS = D_MODEL // 2 // ROW_HALVES
SC_WINDOW = 128
GATHER_PIECES = 4
DT_PAD = LANES
N_IN_PAD = D_SSM + D_CONV + 4 * D_HGRN + DT_PAD

TM_PROJ = 256
TM_TOK = 256
EXPERT_ROWS = 256
MAX_CHUNK = 2
X_RING = 8
Y_RING = 8
NEG = -1e30


def _sigmoid(x):
    return 1.0 / (1.0 + jnp.exp(-x))


def _silu(x):
    return x * _sigmoid(x)


def _split3(x):
    hi = x.astype(BF16)
    r = x - hi.astype(F32)
    mid = r.astype(BF16)
    lo = (r - mid.astype(F32)).astype(BF16)
    return hi, mid, lo


def _dot(a, b):
    return jnp.dot(a, b, preferred_element_type=F32)


def _dot_nt(a, b):
    return lax.dot_general(a, b, (((1,), (1,)), ((), ())), preferred_element_type=F32)


def _dot_tn(a, b):
    return lax.dot_general(a, b, (((0,), (0,)), ((), ())), preferred_element_type=F32)


def _sel_dot(sel, x):
    hi, mid, lo = _split3(x)
    return _dot(sel, hi) + _dot(sel, mid) + _dot(sel, lo)


def _dot_sel(x, sel):
    hi, mid, lo = _split3(x)
    return _dot(hi, sel) + _dot(mid, sel) + _dot(lo, sel)


def _pack_pairs(lo, hi):
    lo_bits = pltpu.bitcast(lo.astype(BF16).astype(F32), U32) >> 16
    hi_bits = pltpu.bitcast(hi.astype(BF16).astype(F32), U32) & jnp.uint32(0xFFFF0000)
    return hi_bits | lo_bits


def _unpack_pairs(p):
    return pltpu.bitcast(p << 16, F32), pltpu.bitcast(p & jnp.uint32(0xFFFF0000), F32)


def _layer_norm(x, g, b):
    mu = jnp.mean(x, axis=-1, keepdims=True)
    xc = x - mu
    var = jnp.mean(xc * xc, axis=-1, keepdims=True)
    return xc * lax.rsqrt(var + LN_EPS) * g + b


def _inproj_kernel(x_ref, w_ref, z_ref, xbc_ref, q_ref, f_ref, i_ref, g_ref, dt_ref):
    xb = x_ref[...].astype(BF16)
    col = 0
    for ref in (z_ref, xbc_ref, q_ref, f_ref, i_ref, g_ref, dt_ref):
        n = ref.shape[-1]
        ref[...] = _dot(xb, w_ref[:, col:col + n])
        col += n


def _inproj(x2, w_perm):
    t = x2.shape[0]
    widths = (D_SSM, D_CONV, D_HGRN, D_HGRN, D_HGRN, D_HGRN, DT_PAD)
    return pl.pallas_call(
        _inproj_kernel,
        grid=(t // TM_PROJ,),
        in_specs=[pl.BlockSpec((TM_PROJ, D_MODEL), lambda i: (i, 0)),
                  pl.BlockSpec((D_MODEL, N_IN_PAD), lambda i: (0, 0))],
        out_specs=[pl.BlockSpec((TM_PROJ, n), lambda i: (i, 0)) for n in widths],
        out_shape=[jax.ShapeDtypeStruct((t, n), F32) for n in widths],
        compiler_params=pltpu.CompilerParams(dimension_semantics=("arbitrary",),
                                             vmem_limit_bytes=48 * 1024 * 1024),
        name="inproj",
    )(x2, w_perm)


def _ssd_kernel(z_ref, xbc_ref, dt_ref, cw_ref, cb_ref, dtb_ref, alog_ref, dskip_ref, nw_ref,
                y_ref, ext_ref, st_ref):
    n = xbc_ref.shape[0]
    halo = SUBLANES

    @pl.when(pl.program_id(0) == 0)
    def _():
        ext_ref[0:halo, :] = jnp.zeros((halo, D_CONV), F32)
        st_ref[...] = jnp.zeros(st_ref.shape, F32)

    ext_ref[halo:halo + n, :] = xbc_ref[...]
    acc = jnp.broadcast_to(cb_ref[...], (n, D_CONV))
    for k in range(SSD_CONV):
        off = halo - (SSD_CONV - 1) + k
        acc = acc + cw_ref[k:k + 1, :] * ext_ref[off:off + n, :]
    ext_ref[0:halo, :] = xbc_ref[n - halo:n, :]
    u = _silu(acc)
    for c in range(n // SSD_CHUNK):
        rows = slice(c * SSD_CHUNK, (c + 1) * SSD_CHUNK)
        y_ref[rows, :] = _ssd_chunk(u[rows, :], z_ref[rows, :], dt_ref[rows, :] + dtb_ref[...],
                                    -jnp.exp(alog_ref[...]), dskip_ref[...], nw_ref[...], st_ref)


def _ssd_chunk(u, z, draw, a, dskip, nw, st_ref):
    L = SSD_CHUNK
    xs = u[:, :D_SSM]
    bm = u[:, D_SSM:D_SSM + SSD_GROUPS * SSD_STATE]
    cm = u[:, D_SSM + SSD_GROUPS * SSD_STATE:]

    dt = jnp.maximum(draw, 0.0) + jnp.log(1.0 + jnp.exp(-jnp.abs(draw)))
    ad = dt * a
    rows = lax.broadcasted_iota(I32, (L, L), 0)
    cols = lax.broadcasted_iota(I32, (L, L), 1)
    causal = rows >= cols
    a_cum = _sel_dot(causal.astype(BF16), ad)
    a_cum_t = a_cum.T

    hrow = lax.broadcasted_iota(I32, (LANES, D_SSM), 0)
    hcol = lax.broadcasted_iota(I32, (LANES, D_SSM), 1) // SSD_HEAD_DIM
    expand = (hrow == hcol).astype(BF16)
    dt_x = _dot_sel(dt, expand)
    acx = _dot_sel(a_cum, expand)
    last = acx[L - 1:L, :]
    ea_x = jnp.exp(acx)
    dec_x = jnp.exp(last - acx)
    elast_x = jnp.exp(last)

    xdt = xs * dt_x
    gw = SSD_HEADS // SSD_GROUPS * SSD_HEAD_DIM
    lane_head = lax.broadcasted_iota(I32, (L, gw), 1) // SSD_HEAD_DIM
    ys = []
    for g in range(SSD_GROUPS):
        bg = bm[:, g * SSD_STATE:(g + 1) * SSD_STATE]
        cg = cm[:, g * SSD_STATE:(g + 1) * SSD_STATE].astype(BF16)
        bg_t = bg.T.astype(BF16)
        gmat = _dot(cg, bg_t)
        xdt_g = xdt[:, g * gw:(g + 1) * gw]
        xdt_gb = xdt_g.astype(BF16)
        r_prev = st_ref[g]
        y_g = _dot(cg, r_prev.astype(BF16)) * ea_x[:, g * gw:(g + 1) * gw]
        new_s = _dot(bg_t, (xdt_g * dec_x[:, g * gw:(g + 1) * gw]).astype(BF16))
        st_ref[g] = r_prev * elast_x[:, g * gw:(g + 1) * gw] + new_s
        for j in range(SSD_HEADS // SSD_GROUPS):
            h = g * (SSD_HEADS // SSD_GROUPS) + j
            diff = a_cum[:, h:h + 1] - a_cum_t[h:h + 1, :]
            decay = jnp.exp(jnp.where(causal, diff, NEG))
            yd = _dot((gmat * decay).astype(BF16), xdt_gb)
            y_g = y_g + jnp.where(lane_head == j, yd, 0.0)
        ys.append(y_g)
    y = jnp.concatenate(ys, axis=-1) + xs * dskip
    y = y * _silu(z)
    outs = []
    ng = D_SSM // SSD_GROUPS
    for g in range(SSD_GROUPS):
        yg = y[:, g * ng:(g + 1) * ng]
        ms = jnp.mean(yg * yg, axis=-1, keepdims=True)
        outs.append(yg * lax.rsqrt(ms + RMS_EPS))
    return jnp.concatenate(outs, axis=-1) * nw


def _ssd(z, xbc, dt, conv_w, conv_b, dt_bias, a_log, d_skip, norm_w):
    t = z.shape[0]
    L = SSD_CHUNK * SSD_PER_STEP
    pad = lambda v: jnp.pad(v.astype(F32), (0, LANES - v.shape[0])).reshape(1, LANES)
    full = lambda shape: pl.BlockSpec(shape, lambda c: (0,) * len(shape))
    return pl.pallas_call(
        _ssd_kernel,
        grid=(t // L,),
        in_specs=[pl.BlockSpec((L, D_SSM), lambda c: (c, 0)),
                  pl.BlockSpec((L, D_CONV), lambda c: (c, 0)),
                  pl.BlockSpec((L, DT_PAD), lambda c: (c, 0)),
                  full((SSD_CONV, D_CONV)), full((1, D_CONV)), full((1, LANES)), full((1, LANES)),
                  full((1, D_SSM)), full((1, D_SSM))],
        out_specs=pl.BlockSpec((L, D_SSM), lambda c: (c, 0)),
        out_shape=jax.ShapeDtypeStruct((t, D_SSM), F32),
        scratch_shapes=[pltpu.VMEM((L + SUBLANES, D_CONV), F32),
                        pltpu.VMEM((SSD_GROUPS, SSD_STATE, D_SSM // SSD_GROUPS), F32)],
        compiler_params=pltpu.CompilerParams(dimension_semantics=("arbitrary",)),
        name="ssd",
    )(z, xbc, dt, conv_w.astype(F32), conv_b.reshape(1, D_CONV).astype(F32), pad(dt_bias), pad(a_log),
      jnp.repeat(d_skip.astype(F32), SSD_HEAD_DIM).reshape(1, D_SSM), norm_w.reshape(1, D_SSM).astype(F32))


def _tile_bcast(x, r):
    n, d = x.shape
    x3 = x.reshape(n // SUBLANES, SUBLANES, d)
    return jnp.broadcast_to(x3[:, r:r + 1, :], x3.shape).reshape(n, d)


def _hgrn_kernel(q_ref, f_ref, i_ref, g_ref, lb_ref, nw_ref, o_ref, st_ref):
    @pl.when(pl.program_id(0) == 0)
    def _():
        st_ref[...] = jnp.zeros(st_ref.shape, F32)

    for c in range(o_ref.shape[0] // HGRN_CHUNK):
        rows = slice(c * HGRN_CHUNK, (c + 1) * HGRN_CHUNK)
        o_ref[rows, :] = _hgrn_chunk(q_ref[rows, :], f_ref[rows, :], i_ref[rows, :], g_ref[rows, :],
                                     lb_ref[...], nw_ref[...], st_ref)


def _hgrn_chunk(q, f, v, g, lb, nw, st_ref):
    C = HGRN_CHUNK
    S = HGRN_SUB
    nsub = C // S
    fg = lb + (1.0 - lb) * _sigmoid(f)
    qs = _silu(q)
    rows = lax.broadcasted_iota(I32, (C, C), 0)
    cols = lax.broadcasted_iota(I32, (C, C), 1)
    cum = _sel_dot((rows >= cols).astype(BF16), jnp.log2(fg))
    ck_all = cum - jnp.log2(1.0 - fg)

    srow = lax.broadcasted_iota(I32, (S * HGRN_DK, C), 0) // HGRN_DK
    scol = lax.broadcasted_iota(I32, (S * HGRN_DK, C), 1) % S
    spread = (srow == scol).astype(BF16)
    diag_keep = jnp.logical_and(rows // S == cols // S, rows >= cols)
    outs = []
    for h in range(HGRN_HEADS):
        blk = slice(h * HGRN_DK, (h + 1) * HGRN_DK)
        qh, ch, ck = qs[:, blk], cum[:, blk], ck_all[:, blk]
        last = ch[C - 1:C, :]
        x_cat = jnp.concatenate(
            [(qh * jnp.exp2(jnp.minimum(ch - _tile_bcast(ck, r), 0.0))).astype(BF16) for r in range(S)], axis=-1)
        kbe = jnp.exp2(_tile_bcast(ch, S - 1) - ck).astype(BF16)
        a_pieces, b_pieces = [], []
        for j in range(nsub - 1):
            lo = (j + 1) * S
            a = (qh[lo:, :] * jnp.exp2(ch[lo:, :] - ch[lo - 1:lo, :])).astype(BF16)
            a_pieces.append(jnp.concatenate([jnp.zeros((lo, HGRN_DK), BF16), a], axis=0))
            b = [kbe[j * S:lo, :], jnp.zeros((C - lo, HGRN_DK), BF16)]
            if j > 0:
                b.insert(0, jnp.zeros((j * S, HGRN_DK), BF16))
            b_pieces.append(jnp.concatenate(b, axis=0))
        att = (_dot_nt(jnp.concatenate(a_pieces, axis=-1), jnp.concatenate(b_pieces, axis=-1))
               + jnp.where(diag_keep, _dot(x_cat, spread), 0.0))
        vb = v[:, blk].astype(BF16)
        s_prev = st_ref[h]
        o = _dot_nt((qh * jnp.exp2(ch)).astype(BF16), s_prev.astype(BF16)) + _dot(att.astype(BF16), vb)
        st_ref[h] = s_prev * jnp.exp2(last) + _dot_tn(vb, jnp.exp2(last - ck).astype(BF16))
        ms = jnp.mean(o * o, axis=-1, keepdims=True)
        outs.append(o * lax.rsqrt(ms + RMS_EPS))
    return jnp.concatenate(outs, axis=-1) * nw * _silu(g)


def _hgrn(q, f, i, g, lower_bound, norm_w):
    t = q.shape[0]
    C = HGRN_CHUNK * HGRN_PER_STEP
    tok = pl.BlockSpec((C, D_HGRN), lambda c: (c, 0))
    vec = pl.BlockSpec((1, D_HGRN), lambda c: (0, 0))
    return pl.pallas_call(
        _hgrn_kernel,
        grid=(t // C,),
        in_specs=[tok, tok, tok, tok, vec, vec],
        out_specs=tok,
        out_shape=jax.ShapeDtypeStruct((t, D_HGRN), F32),
        scratch_shapes=[pltpu.VMEM((HGRN_HEADS, D_HGRN // HGRN_HEADS, HGRN_DK), F32)],
        compiler_params=pltpu.CompilerParams(dimension_semantics=("arbitrary",)),
        name="hgrn",
    )(q, f, i, g, lower_bound.reshape(1, D_HGRN).astype(F32), norm_w.reshape(1, D_HGRN).astype(F32))


def _post_kernel(x_ref, ys_ref, yh_ref, wo_ref, g1_ref, b1_ref, wrh_ref, wrl_ref, rb_ref,
                 wgs_ref, wus_ref, wds_ref, htt_ref, base_ref, idx_ref, gate_ref, cnt_ref):
    tm = x_ref.shape[0]
    mix = (_dot(ys_ref[...].astype(BF16), wo_ref[0:D_SSM, :])
           + _dot(yh_ref[...].astype(BF16), wo_ref[D_SSM:, :]))
    h1 = _layer_norm(ALPHA * x_ref[...] + mix, g1_ref[...], b1_ref[...])
    packed = _pack_pairs(h1[:, :D_MODEL // 2], h1[:, D_MODEL // 2:])
    for hh in range(ROW_HALVES):
        htt_ref[hh] = packed[:, hh * HALF_WORDS:(hh + 1) * HALF_WORDS]
    hb = h1.astype(BF16)
    hid = _silu(_dot(hb, wgs_ref[...])) * _dot(hb, wus_ref[...])
    base_ref[...] = ALPHA * h1 + _dot(hid.astype(BF16), wds_ref[...])

    hlo = (h1 - hb.astype(F32)).astype(BF16)
    logits = _dot_nt(wrh_ref[...], hb) + _dot_nt(wrh_ref[...], hlo) + _dot_nt(wrl_ref[...], hb)
    scores = _sigmoid(logits)
    biased = scores + rb_ref[...]
    per_group = N_EXPERTS // N_EXPERT_GROUPS
    eidx = lax.broadcasted_iota(I32, (N_EXPERTS, tm), 0)
    big = jnp.int32(1 << 20)
    gsc = []
    bidx = lax.broadcasted_iota(I32, (per_group, tm), 0)
    for gi in range(N_EXPERT_GROUPS):
        blk = biased[gi * per_group:(gi + 1) * per_group, :]
        m1 = jnp.max(blk, axis=0, keepdims=True)
        i1 = jnp.min(jnp.where(blk == m1, bidx, big), axis=0, keepdims=True)
        m2 = jnp.max(jnp.where(bidx == i1, NEG, blk), axis=0, keepdims=True)
        gsc.append(m1 + m2)
    cur = jnp.concatenate(gsc, axis=0)
    gidx = lax.broadcasted_iota(I32, (N_EXPERT_GROUPS, tm), 0)
    gsel = jnp.zeros((N_EXPERT_GROUPS, tm), F32)
    for _ in range(TOPK_GROUPS):
        m = jnp.max(cur, axis=0, keepdims=True)
        i = jnp.min(jnp.where(cur == m, gidx, big), axis=0, keepdims=True)
        hit = gidx == i
        gsel = jnp.where(hit, 1.0, gsel)
        cur = jnp.where(hit, NEG, cur)
    emask = jnp.concatenate(
        [jnp.broadcast_to(gsel[gi:gi + 1, :], (per_group, tm)) for gi in range(N_EXPERT_GROUPS)], axis=0)
    masked = jnp.where(emask > 0.0, biased, NEG)
    idx_rows, gate_rows = [], []
    for _ in range(TOP_K):
        m = jnp.max(masked, axis=0, keepdims=True)
        i = jnp.min(jnp.where(masked == m, eidx, big), axis=0, keepdims=True)
        hit = eidx == i
        idx_rows.append(i)
        gate_rows.append(jnp.sum(jnp.where(hit, scores, 0.0), axis=0, keepdims=True))
        masked = jnp.where(hit, NEG, masked)
    picked = jnp.where(jnp.logical_and(masked == NEG, emask > 0.0), 1.0, 0.0)
    gates = jnp.concatenate(gate_rows, axis=0)
    gates = gates / jnp.sum(gates, axis=0, keepdims=True) * ROUTED_SCALE
    idx_ref[0] = jnp.concatenate(idx_rows, axis=0)
    gate_ref[0] = gates

    @pl.when(pl.program_id(0) == 0)
    def _():
        cnt_ref[...] = jnp.zeros(cnt_ref.shape, F32)

    cnt_ref[...] += _dot(picked.astype(BF16), jnp.ones((tm, LANES), BF16))


def _post(x2, y_ssd, y_hgrn, wo, g1, b1, wr_hi, wr_lo, rbias, wgs, wus, wds):
    t = x2.shape[0]
    tm = TM_TOK
    nt = t // tm
    full = lambda shape: pl.BlockSpec(shape, lambda i: (0,) * len(shape))
    return pl.pallas_call(
        _post_kernel,
        grid=(nt,),
        in_specs=[pl.BlockSpec((tm, D_MODEL), lambda i: (i, 0)),
                  pl.BlockSpec((tm, D_SSM), lambda i: (i, 0)),
                  pl.BlockSpec((tm, D_HGRN), lambda i: (i, 0)),
                  full((D_MODEL, D_MODEL)), full((1, D_MODEL)), full((1, D_MODEL)),
                  full((N_EXPERTS, D_MODEL)), full((N_EXPERTS, D_MODEL)), full((N_EXPERTS, 1)),
                  full((D_MODEL, D_EXPERT)), full((D_MODEL, D_EXPERT)), full((D_EXPERT, D_MODEL))],
        out_specs=[pl.BlockSpec((ROW_HALVES, tm, HALF_WORDS), lambda i: (0, i, 0)),
                   pl.BlockSpec((tm, D_MODEL), lambda i: (i, 0)),
                   pl.BlockSpec((1, TOP_K, tm), lambda i: (i, 0, 0)),
                   pl.BlockSpec((1, TOP_K, tm), lambda i: (i, 0, 0)),
                   pl.BlockSpec((N_EXPERTS, LANES), lambda i: (0, 0))],
        out_shape=[jax.ShapeDtypeStruct((ROW_HALVES, t, HALF_WORDS), U32),
                   jax.ShapeDtypeStruct((t, D_MODEL), F32),
                   jax.ShapeDtypeStruct((nt, TOP_K, tm), I32),
                   jax.ShapeDtypeStruct((nt, TOP_K, tm), F32),
                   jax.ShapeDtypeStruct((N_EXPERTS, LANES), F32)],
        compiler_params=pltpu.CompilerParams(dimension_semantics=("arbitrary",),
                                             vmem_limit_bytes=48 * 1024 * 1024),
        name="post",
    )(x2, y_ssd, y_hgrn, wo, g1, b1, wr_hi, wr_lo, rbias, wgs, wus, wds)


def _max_blocks(t):
    return (t * TOP_K + N_EXPERTS * (EXPERT_ROWS - 1)) // EXPERT_ROWS


def _pos_kernel(idx_ref, total_ref, dest_ref, bstart_ref, nblk_ref, cnt_ref, start_ref):
    i = pl.program_id(0)
    tm = idx_ref.shape[-1]
    idx = idx_ref[0]
    eidx = lax.broadcasted_iota(I32, (N_EXPERTS, tm), 0)
    sel = [eidx == idx[k:k + 1, :] for k in range(TOP_K)]
    onehot = sel[0]
    for k in range(1, TOP_K):
        onehot = jnp.logical_or(onehot, sel[k])
    mt = jnp.where(onehot, 1.0, 0.0).astype(BF16)

    @pl.when(i == 0)
    def _():
        nb = jnp.floor((total_ref[...] + (EXPERT_ROWS - 1)) * (1.0 / EXPERT_ROWS))
        r = lax.broadcasted_iota(I32, (N_EXPERTS, N_EXPERTS), 0)
        c = lax.broadcasted_iota(I32, (N_EXPERTS, N_EXPERTS), 1)
        end = _dot((r >= c).astype(BF16), nb.astype(BF16))
        start_ref[...] = (end - nb) * EXPERT_ROWS
        cnt_ref[...] = jnp.zeros(cnt_ref.shape, F32)
        bstart_ref[...] = (end - nb).astype(I32)
        nblk_ref[...] = nb.astype(I32)

    r = lax.broadcasted_iota(I32, (tm, tm), 0)
    c = lax.broadcasted_iota(I32, (tm, tm), 1)
    before = _dot(mt, (r < c).astype(BF16))
    slot = start_ref[:, 0:1] + cnt_ref[:, 0:1] + before
    rows = [jnp.sum(jnp.where(sel[k], slot, 0.0), axis=0, keepdims=True) for k in range(TOP_K)]
    dest_ref[0] = jnp.concatenate(rows, axis=0).astype(I32)
    cnt_ref[...] += _dot(mt, jnp.ones((tm, LANES), BF16))


def _positions(idx, totals):
    nt, _, tm = idx.shape
    return pl.pallas_call(
        _pos_kernel,
        grid=(nt,),
        in_specs=[pl.BlockSpec((1, TOP_K, tm), lambda i: (i, 0, 0)),
                  pl.BlockSpec((N_EXPERTS, LANES), lambda i: (0, 0))],
        out_specs=[pl.BlockSpec((1, TOP_K, tm), lambda i: (i, 0, 0)),
                   pl.BlockSpec((N_EXPERTS, LANES), lambda i: (0, 0)),
                   pl.BlockSpec((N_EXPERTS, LANES), lambda i: (0, 0))],
        out_shape=[jax.ShapeDtypeStruct((nt, TOP_K, tm), I32),
                   jax.ShapeDtypeStruct((N_EXPERTS, LANES), I32),
                   jax.ShapeDtypeStruct((N_EXPERTS, LANES), I32)],
        scratch_shapes=[pltpu.VMEM((N_EXPERTS, LANES), F32), pltpu.VMEM((N_EXPERTS, LANES), F32)],
        compiler_params=pltpu.CompilerParams(dimension_semantics=("arbitrary",)),
        name="positions",
    )(idx, totals)


def _sc_mesh():
    return plsc.VectorSubcoreMesh(core_axis_name="core", subcore_axis_name="subcore")


def _sc_dispatch(src, index, n_out):
    n = index.shape[0]
    src_blocks = src.shape[0] // SC_WINDOW
    index = index.reshape(1, n)

    @pl.kernel(out_type=jax.ShapeDtypeStruct((n_out, HALF_WORDS), src.dtype), mesh=_sc_mesh(), name="sc_dispatch")
    def scatter(src_hbm, idx_hbm, out_hbm):
        def body(rows_vmem, idx_vmem):
            pltpu.sync_copy(rows_vmem, out_hbm.at[idx_vmem.at[0]])

        pltpu.emit_pipeline(
            body, grid=(n // SC_WINDOW,),
            in_specs=[pl.BlockSpec((SC_WINDOW, HALF_WORDS), index_map=lambda i: (i % src_blocks, 0)),
                      pl.BlockSpec((1, SC_WINDOW), index_map=lambda i: (0, i))],
            out_specs=[],
            core_axis_name=("core", "subcore"), dimension_semantics=(pltpu.PARALLEL,))(src_hbm, idx_hbm)

    return scatter(src, index)


def _sc_gather(src, index):
    n = index.shape[0]
    index = index.reshape(1, n)

    @pl.kernel(out_type=jax.ShapeDtypeStruct((n, HALF_WORDS), src.dtype), mesh=_sc_mesh(), name="sc_gather")
    def gather(src_hbm, idx_hbm, out_hbm):
        def body(idx_vmem, rows_vmem):
            pltpu.sync_copy(src_hbm.at[idx_vmem.at[0]], rows_vmem)

        pltpu.emit_pipeline(
            body, grid=(n // SC_WINDOW,),
            in_specs=[pl.BlockSpec((1, SC_WINDOW), index_map=lambda i: (0, i))],
            out_specs=[pl.BlockSpec((SC_WINDOW, HALF_WORDS), index_map=lambda i: (i, 0))],
            core_axis_name=("core", "subcore"), dimension_semantics=(pltpu.PARALLEL,))(idx_hbm, out_hbm)

    return gather(src, index)


def _expert_kernel(bstart_ref, nblk_ref, xs_hbm, wg_ref, wu_ref, wd_ref, ys_hbm,
                   xbuf, ybuf, wgb_ref, wub_ref, wdb_ref, xsem, ysem, ypend):
    e = pl.program_id(0)
    last = pl.num_programs(0) - 1
    n = nblk_ref[e]
    b0 = bstart_ref[e]
    total = bstart_ref[last] + nblk_ref[last]
    blk_rows = EXPERT_ROWS * ROW_HALVES
    xdepth = xbuf.shape[0]
    ydepth = ybuf.shape[0]
    lookahead = xdepth - MAX_CHUNK

    def hbm_block(ref, b):
        return ref.at[pl.ds(pl.multiple_of(b * blk_rows, blk_rows), blk_rows)]

    def x_copy(b):
        slot = b % xdepth
        return pltpu.make_async_copy(hbm_block(xs_hbm, b), xbuf.at[slot], xsem.at[slot])

    def y_copy(b, slot):
        return pltpu.make_async_copy(ybuf.at[slot], hbm_block(ys_hbm, b), ysem.at[slot])

    @pl.when(e == 0)
    def _():
        for s in range(ydepth):
            ypend[s] = 0
        for b in range(lookahead):
            @pl.when(b < total)
            def _():
                x_copy(b).start()

    def chunk(j, m):
        b = b0 + j
        for i in range(m):
            @pl.when(b + lookahead + i < total)
            def _():
                x_copy(b + lookahead + i).start()

        for i in range(m):
            x_copy(b + i).wait()

            @pl.when(ypend[(b + i) % ydepth] == 1)
            def _():
                y_copy(b + i, (b + i) % ydepth).wait()

        for i in range(m):
            xslot = (b + i) % xdepth
            yslot = (b + i) % ydepth
            halves = [_unpack_pairs(xbuf[xslot, hh * EXPERT_ROWS:(hh + 1) * EXPERT_ROWS, :])
                      for hh in range(ROW_HALVES)]
            xb = jnp.concatenate([lo for lo, _ in halves] + [hi for _, hi in halves], axis=-1).astype(BF16)
            hid = _silu(_dot(xb, wgb_ref[...])) * _dot(xb, wub_ref[...])
            yb = _dot(hid.astype(BF16), wdb_ref[...])
            packed = _pack_pairs(yb[:, :D_MODEL // 2], yb[:, D_MODEL // 2:])
            for hh in range(ROW_HALVES):
                ybuf[yslot, hh * EXPERT_ROWS:(hh + 1) * EXPERT_ROWS, :] = (
                    packed[:, hh * HALF_WORDS:(hh + 1) * HALF_WORDS])

        for i in range(m):
            y_copy(b + i, (b + i) % ydepth).start()
            ypend[(b + i) % ydepth] = 1

    @pl.when(n > 0)
    def _():
        wgb_ref[...] = wg_ref[0].astype(BF16)
        wub_ref[...] = wu_ref[0].astype(BF16)
        wdb_ref[...] = wd_ref[0].astype(BF16)
        full = n // MAX_CHUNK

        def full_chunk(q, carry):
            chunk(q * MAX_CHUNK, MAX_CHUNK)
            return carry

        lax.fori_loop(0, full, full_chunk, 0)
        for m in range(1, MAX_CHUNK):
            @pl.when(n - full * MAX_CHUNK == m)
            def _():
                chunk(full * MAX_CHUNK, m)

    @pl.when(e == last)
    def _():
        for s in range(ydepth):
            @pl.when(ypend[s] == 1)
            def _():
                y_copy(b0, s).wait()


def _experts(bstart, nblk, xs, w_gate, w_up, w_down):
    blk_shape = (EXPERT_ROWS * ROW_HALVES, HALF_WORDS)
    wspec = lambda shape: pl.BlockSpec((1,) + shape, lambda e, bs, nb: (e, 0, 0))
    grid_spec = pltpu.PrefetchScalarGridSpec(
        num_scalar_prefetch=2,
        grid=(N_EXPERTS,),
        in_specs=[pl.BlockSpec(memory_space=pl.ANY),
                  wspec((D_MODEL, D_EXPERT)), wspec((D_MODEL, D_EXPERT)), wspec((D_EXPERT, D_MODEL))],
        out_specs=pl.BlockSpec(memory_space=pl.ANY),
        scratch_shapes=[pltpu.VMEM((X_RING,) + blk_shape, U32), pltpu.VMEM((Y_RING,) + blk_shape, U32),
                        pltpu.VMEM((D_MODEL, D_EXPERT), BF16), pltpu.VMEM((D_MODEL, D_EXPERT), BF16),
                        pltpu.VMEM((D_EXPERT, D_MODEL), BF16),
                        pltpu.SemaphoreType.DMA((X_RING,)), pltpu.SemaphoreType.DMA((Y_RING,)),
                        pltpu.SMEM((Y_RING,), I32)],
    )
    return pl.pallas_call(
        _expert_kernel,
        grid_spec=grid_spec,
        out_shape=jax.ShapeDtypeStruct(xs.shape, U32),
        compiler_params=pltpu.CompilerParams(dimension_semantics=("arbitrary",), has_side_effects=True),
        name="experts",
    )(bstart, nblk, xs, w_gate, w_up, w_down)


def _combine_kernel(gate_ref, base_ref, g2_ref, b2_ref, rows_ref, o_ref):
    tm = base_ref.shape[0]
    gates_t = jnp.concatenate([gate_ref[0], jnp.zeros((LANES - TOP_K, tm), F32)], axis=0).T
    half = D_MODEL // 2
    lo_acc = [base_ref[:, hh * HALF_WORDS:(hh + 1) * HALF_WORDS] for hh in range(ROW_HALVES)]
    hi_acc = [base_ref[:, half + hh * HALF_WORDS:half + (hh + 1) * HALF_WORDS] for hh in range(ROW_HALVES)]
    for k in range(TOP_K):
        gk = gates_t[:, k:k + 1]
        for hh in range(ROW_HALVES):
            lo, hi = _unpack_pairs(rows_ref[k, hh])
            lo_acc[hh] = lo_acc[hh] + gk * lo
            hi_acc[hh] = hi_acc[hh] + gk * hi
    o_ref[...] = _layer_norm(jnp.concatenate(lo_acc + hi_acc, axis=-1), g2_ref[...], b2_ref[...])


def _combine_piece_kernel(gate_ref, base_ref, g2_ref, b2_ref, rows_ref, prev_ref, o_ref):
    del prev_ref
    _combine_kernel(gate_ref, base_ref, g2_ref, b2_ref, rows_ref, o_ref)


def _combine(gates, base, g2, b2, rows, first_tile, prev):
    t = base.shape[0]
    tm = gates.shape[-1]
    tiles = rows.shape[2] // tm
    in_specs = [pl.BlockSpec((1, TOP_K, tm), lambda i: (i + first_tile, 0, 0)),
                pl.BlockSpec((tm, D_MODEL), lambda i: (i + first_tile, 0)),
                pl.BlockSpec((1, D_MODEL), lambda i: (0, 0)),
                pl.BlockSpec((1, D_MODEL), lambda i: (0, 0)),
                pl.BlockSpec((TOP_K, ROW_HALVES, tm, HALF_WORDS), lambda i: (0, 0, i, 0))]
    args = [gates, base, g2, b2, rows]
    if prev is not None:
        in_specs.append(pl.BlockSpec(memory_space=pl.ANY))
        args.append(prev)
    return pl.pallas_call(
        _combine_kernel if prev is None else _combine_piece_kernel,
        grid=(tiles,),
        in_specs=in_specs,
        out_specs=pl.BlockSpec((tm, D_MODEL), lambda i: (i + first_tile, 0)),
        out_shape=jax.ShapeDtypeStruct((t, D_MODEL), F32),
        input_output_aliases={} if prev is None else {len(args) - 1: 0},
        compiler_params=pltpu.CompilerParams(dimension_semantics=("arbitrary",),
                                             vmem_limit_bytes=48 * 1024 * 1024),
        name="combine",
    )(*args)


def _layer(h2, w_in, conv_w, conv_b, dt_bias, a_log, d_skip, ssd_norm_w, lower_bound, hgrn_norm_w, w_out,
           ln1_g, ln1_b, w_router, router_bias, w_gate_e, w_up_e, w_down_e, w_gate_s, w_up_s, w_down_s,
           ln2_g, ln2_b):
    t = h2.shape[0]
    dt0 = D_SSM + D_CONV
    q0 = dt0 + SSD_HEADS
    w_perm = jnp.concatenate(
        [w_in[:, :dt0], w_in[:, q0:], w_in[:, dt0:q0], jnp.zeros((D_MODEL, DT_PAD - SSD_HEADS), w_in.dtype)],
        axis=1).astype(BF16)
    z, xbc, q, f, i, g, dt = _inproj(h2, w_perm)
    y_ssd = _ssd(z, xbc, dt, conv_w, conv_b, dt_bias, a_log, d_skip, ssd_norm_w)
    y_hgrn = _hgrn(q, f, i, g, lower_bound, hgrn_norm_w)

    wr_t = w_router.astype(F32).T
    wr_hi = wr_t.astype(BF16)
    wr_lo = (wr_t - wr_hi.astype(F32)).astype(BF16)
    row = lambda v: v.reshape(1, -1).astype(F32)
    htt, base, idx, gates, totals = _post(
        h2, y_ssd, y_hgrn, w_out.astype(BF16), row(ln1_g), row(ln1_b), wr_hi, wr_lo,
        router_bias.reshape(N_EXPERTS, 1).astype(F32),
        w_gate_s.astype(BF16), w_up_s.astype(BF16), w_down_s.astype(BF16))

    dest, bstart, nblk = _positions(idx, totals)
    n_rows = _max_blocks(t) * EXPERT_ROWS
    d = jnp.transpose(dest, (1, 0, 2)).reshape(TOP_K, 1, t)
    slot = (d // EXPERT_ROWS) * (EXPERT_ROWS * ROW_HALVES) + d % EXPERT_ROWS
    slot = (slot + jnp.arange(ROW_HALVES, dtype=I32).reshape(1, ROW_HALVES, 1) * EXPERT_ROWS).reshape(-1)
    xs = _sc_dispatch(htt.reshape(ROW_HALVES * t, HALF_WORDS), slot, n_rows * ROW_HALVES)
    ys = _experts(bstart[:, 0], nblk[:, 0], xs, w_gate_e, w_up_e, w_down_e)
    slot = slot.reshape(TOP_K, ROW_HALVES, t)
    tp = t // GATHER_PIECES
    out = None
    for p in range(GATHER_PIECES):
        rows = _sc_gather(ys, slot[:, :, p * tp:(p + 1) * tp].reshape(-1))
        out = _combine(gates, base, row(ln2_g), row(ln2_b), rows.reshape(TOP_K, ROW_HALVES, tp, HALF_WORDS),
                       p * (tp // TM_TOK), out)
    return out


def kernel(x, w_in, conv_w, conv_b, dt_bias, a_log, d_skip, ssd_norm_w, hgrn_lb_logits, hgrn_norm_w, w_out,
           ln1_g, ln1_b, w_router, router_bias, w_gate_e, w_up_e, w_down_e, w_gate_s, w_up_s, w_down_s,
           ln2_g, ln2_b):
    bsz, t, d = x.shape
    assert bsz == 1 and d == D_MODEL, "the recurrent mixers carry state across the flattened token axis"
    depth = w_in.shape[0]
    lower_bounds = jnp.cumsum(jax.nn.softmax(hgrn_lb_logits.astype(F32), axis=0), axis=0)
    h = x.reshape(bsz * t, d)
    for l in range(depth):
        h = _layer(h, w_in[l], conv_w[l], conv_b[l], dt_bias[l], a_log[l], d_skip[l], ssd_norm_w[l],
                   lower_bounds[l], hgrn_norm_w[l], w_out[l], ln1_g[l], ln1_b[l], w_router[l],
                   router_bias[l], w_gate_e[l], w_up_e[l], w_down_e[l], w_gate_s[l], w_up_s[l],
                   w_down_s[l], ln2_g[l], ln2_b[l])
    return h.reshape(bsz, t, d)
```

```python
import jax
import jax.numpy as jnp
from jax import lax
from jax.experimental import pallas as pl
from jax.experimental.pallas import tpu as pltpu
from jax.experimental.pallas import tpu_sc as plsc

F32 = jnp.float32
BF16 = jnp.bfloat16
I32 = jnp.int32
U32 = jnp.uint32

D_MODEL = 1024
D_SSM = 512
D_HGRN = 512
SSD_HEADS = 8
SSD_HEAD_DIM = 64
SSD_GROUPS = 2
SSD_STATE = 128
SSD_CONV = 4
SSD_CHUNK = 128
SSD_PER_STEP = 4
D_CONV = D_SSM + 2 * SSD_GROUPS * SSD_STATE
HGRN_HEADS = 4
HGRN_DK = 128
HGRN_CHUNK = 64
HGRN_SUB = 8
HGRN_PER_STEP = 4
N_EXPERTS = 256
TOP_K = 8
N_EXPERT_GROUPS = 8
TOPK_GROUPS = 4
D_EXPERT = 256
ROUTED_SCALE = 2.5
ALPHA = 2.0 ** 0.25
LN_EPS = 1e-5
RMS_EPS = 1e-6

LANES = 128
SUBLANES = 8
ROW_TILES = D_MODEL // LANES
ROW_HALVES = 2
HALF_WORDS = D_MODEL // 2 // ROW_HALVES
SC_WINDOW = 128
GATHER_PIECES = 4
DT_PAD = LANES
N_IN_PAD = D_SSM + D_CONV + 4 * D_HGRN + DT_PAD

TM_PROJ = 256
TM_TOK = 256
EXPERT_ROWS = 256
MAX_CHUNK = 3
X_RING = 9
Y_RING = 9
NEG = -1e30


def _sigmoid(x):
    return 1.0 / (1.0 + jnp.exp(-x))


def _silu(x):
    return x * _sigmoid(x)


def _split3(x):
    hi = x.astype(BF16)
    r = x - hi.astype(F32)
    mid = r.astype(BF16)
    lo = (r - mid.astype(F32)).astype(BF16)
    return hi, mid, lo


def _dot(a, b):
    return jnp.dot(a, b, preferred_element_type=F32)


def _dot_nt(a, b):
    return lax.dot_general(a, b, (((1,), (1,)), ((), ())), preferred_element_type=F32)


def _dot_tn(a, b):
    return lax.dot_general(a, b, (((0,), (0,)), ((), ())), preferred_element_type=F32)


def _sel_dot(sel, x):
    hi, mid, lo = _split3(x)
    return _dot(sel, hi) + _dot(sel, mid) + _dot(sel, lo)


def _dot_sel(x, sel):
    hi, mid, lo = _split3(x)
    return _dot(hi, sel) + _dot(mid, sel) + _dot(lo, sel)


def _pack_pairs(lo, hi):
    lo_bits = pltpu.bitcast(lo.astype(BF16).astype(F32), U32) >> 16
    hi_bits = pltpu.bitcast(hi.astype(BF16).astype(F32), U32) & jnp.uint32(0xFFFF0000)
    return hi_bits | lo_bits


def _unpack_pairs(p):
    return pltpu.bitcast(p << 16, F32), pltpu.bitcast(p & jnp.uint32(0xFFFF0000), F32)


def _layer_norm(x, g, b):
    mu = jnp.mean(x, axis=-1, keepdims=True)
    xc = x - mu
    var = jnp.mean(xc * xc, axis=-1, keepdims=True)
    return xc * lax.rsqrt(var + LN_EPS) * g + b


def _inproj_kernel(x_ref, w_ref, z_ref, xbc_ref, q_ref, f_ref, i_ref, g_ref, dt_ref):
    xb = x_ref[...].astype(BF16)
    col = 0
    for ref in (z_ref, xbc_ref, q_ref, f_ref, i_ref, g_ref, dt_ref):
        n = ref.shape[-1]
        ref[...] = _dot(xb, w_ref[:, col:col + n])
        col += n


def _inproj(x2, w_perm):
    t = x2.shape[0]
    widths = (D_SSM, D_CONV, D_HGRN, D_HGRN, D_HGRN, D_HGRN, DT_PAD)
    return pl.pallas_call(
        _inproj_kernel,
        grid=(t // TM_PROJ,),
        in_specs=[pl.BlockSpec((TM_PROJ, D_MODEL), lambda i: (i, 0)),
                  pl.BlockSpec((D_MODEL, N_IN_PAD), lambda i: (0, 0))],
        out_specs=[pl.BlockSpec((TM_PROJ, n), lambda i: (i, 0)) for n in widths],
        out_shape=[jax.ShapeDtypeStruct((t, n), F32) for n in widths],
        compiler_params=pltpu.CompilerParams(dimension_semantics=("arbitrary",),
                                             vmem_limit_bytes=48 * 1024 * 1024),
        name="inproj",
    )(x2, w_perm)


def _ssd_kernel(z_ref, xbc_ref, dt_ref, cw_ref, cb_ref, dtb_ref, alog_ref, dskip_ref, nw_ref,
                y_ref, ext_ref, st_ref):
    n = xbc_ref.shape[0]
    halo = SUBLANES

    @pl.when(pl.program_id(0) == 0)
    def _():
        ext_ref[0:halo, :] = jnp.zeros((halo, D_CONV), F32)
        st_ref[...] = jnp.zeros(st_ref.shape, F32)

    ext_ref[halo:halo + n, :] = xbc_ref[...]
    acc = jnp.broadcast_to(cb_ref[...], (n, D_CONV))
    for k in range(SSD_CONV):
        off = halo - (SSD_CONV - 1) + k
        acc = acc + cw_ref[k:k + 1, :] * ext_ref[off:off + n, :]
    ext_ref[0:halo, :] = xbc_ref[n - halo:n, :]
    u = _silu(acc)
    for c in range(n // SSD_CHUNK):
        rows = slice(c * SSD_CHUNK, (c + 1) * SSD_CHUNK)
        y_ref[rows, :] = _ssd_chunk(u[rows, :], z_ref[rows, :], dt_ref[rows, :] + dtb_ref[...],
                                    -jnp.exp(alog_ref[...]), dskip_ref[...], nw_ref[...], st_ref)


def _ssd_chunk(u, z, draw, a, dskip, nw, st_ref):
    L = SSD_CHUNK
    xs = u[:, :D_SSM]
    bm = u[:, D_SSM:D_SSM + SSD_GROUPS * SSD_STATE]
    cm = u[:, D_SSM + SSD_GROUPS * SSD_STATE:]

    dt = jnp.maximum(draw, 0.0) + jnp.log(1.0 + jnp.exp(-jnp.abs(draw)))
    ad = dt * a
    rows = lax.broadcasted_iota(I32, (L, L), 0)
    cols = lax.broadcasted_iota(I32, (L, L), 1)
    causal = rows >= cols
    a_cum = _sel_dot(causal.astype(BF16), ad)
    a_cum_t = a_cum.T

    hrow = lax.broadcasted_iota(I32, (LANES, D_SSM), 0)
    hcol = lax.broadcasted_iota(I32, (LANES, D_SSM), 1) // SSD_HEAD_DIM
    expand = (hrow == hcol).astype(BF16)
    dt_x = _dot_sel(dt, expand)
    acx = _dot_sel(a_cum, expand)
    last = acx[L - 1:L, :]
    ea_x = jnp.exp(acx)
    dec_x = jnp.exp(last - acx)
    elast_x = jnp.exp(last)

    xdt = xs * dt_x
    gw = SSD_HEADS // SSD_GROUPS * SSD_HEAD_DIM
    lane_head = lax.broadcasted_iota(I32, (L, gw), 1) // SSD_HEAD_DIM
    ys = []
    for g in range(SSD_GROUPS):
        bg = bm[:, g * SSD_STATE:(g + 1) * SSD_STATE]
        cg = cm[:, g * SSD_STATE:(g + 1) * SSD_STATE].astype(BF16)
        bg_t = bg.T.astype(BF16)
        gmat = _dot(cg, bg_t)
        xdt_g = xdt[:, g * gw:(g + 1) * gw]
        xdt_gb = xdt_g.astype(BF16)
        r_prev = st_ref[g]
        y_g = _dot(cg, r_prev.astype(BF16)) * ea_x[:, g * gw:(g + 1) * gw]
        new_s = _dot(bg_t, (xdt_g * dec_x[:, g * gw:(g + 1) * gw]).astype(BF16))
        st_ref[g] = r_prev * elast_x[:, g * gw:(g + 1) * gw] + new_s
        for j in range(SSD_HEADS // SSD_GROUPS):
            h = g * (SSD_HEADS // SSD_GROUPS) + j
            diff = a_cum[:, h:h + 1] - a_cum_t[h:h + 1, :]
            decay = jnp.exp(jnp.where(causal, diff, NEG))
            yd = _dot((gmat * decay).astype(BF16), xdt_gb)
            y_g = y_g + jnp.where(lane_head == j, yd, 0.0)
        ys.append(y_g)
    y = jnp.concatenate(ys, axis=-1) + xs * dskip
    y = y * _silu(z)
    outs = []
    ng = D_SSM // SSD_GROUPS
    for g in range(SSD_GROUPS):
        yg = y[:, g * ng:(g + 1) * ng]
        ms = jnp.mean(yg * yg, axis=-1, keepdims=True)
        outs.append(yg * lax.rsqrt(ms + RMS_EPS))
    return jnp.concatenate(outs, axis=-1) * nw


def _ssd(z, xbc, dt, conv_w, conv_b, dt_bias, a_log, d_skip, norm_w):
    t = z.shape[0]
    L = SSD_CHUNK * SSD_PER_STEP
    pad = lambda v: jnp.pad(v.astype(F32), (0, LANES - v.shape[0])).reshape(1, LANES)
    full = lambda shape: pl.BlockSpec(shape, lambda c: (0,) * len(shape))
    return pl.pallas_call(
        _ssd_kernel,
        grid=(t // L,),
        in_specs=[pl.BlockSpec((L, D_SSM), lambda c: (c, 0)),
                  pl.BlockSpec((L, D_CONV), lambda c: (c, 0)),
                  pl.BlockSpec((L, DT_PAD), lambda c: (c, 0)),
                  full((SSD_CONV, D_CONV)), full((1, D_CONV)), full((1, LANES)), full((1, LANES)),
                  full((1, D_SSM)), full((1, D_SSM))],
        out_specs=pl.BlockSpec((L, D_SSM), lambda c: (c, 0)),
        out_shape=jax.ShapeDtypeStruct((t, D_SSM), F32),
        scratch_shapes=[pltpu.VMEM((L + SUBLANES, D_CONV), F32),
                        pltpu.VMEM((SSD_GROUPS, SSD_STATE, D_SSM // SSD_GROUPS), F32)],
        compiler_params=pltpu.CompilerParams(dimension_semantics=("arbitrary",)),
        name="ssd",
    )(z, xbc, dt, conv_w.astype(F32), conv_b.reshape(1, D_CONV).astype(F32), pad(dt_bias), pad(a_log),
      jnp.repeat(d_skip.astype(F32), SSD_HEAD_DIM).reshape(1, D_SSM), norm_w.reshape(1, D_SSM).astype(F32))


def _tile_bcast(x, r):
    n, d = x.shape
    x3 = x.reshape(n // SUBLANES, SUBLANES, d)
    return jnp.broadcast_to(x3[:, r:r + 1, :], x3.shape).reshape(n, d)


def _hgrn_kernel(q_ref, f_ref, i_ref, g_ref, lb_ref, nw_ref, o_ref, st_ref):
    @pl.when(pl.program_id(0) == 0)
    def _():
        st_ref[...] = jnp.zeros(st_ref.shape, F32)

    for c in range(o_ref.shape[0] // HGRN_CHUNK):
        rows = slice(c * HGRN_CHUNK, (c + 1) * HGRN_CHUNK)
        o_ref[rows, :] = _hgrn_chunk(q_ref[rows, :], f_ref[rows, :], i_ref[rows, :], g_ref[rows, :],
                                     lb_ref[...], nw_ref[...], st_ref)


def _hgrn_chunk(q, f, v, g, lb, nw, st_ref):
    C = HGRN_CHUNK
    S = HGRN_SUB
    nsub = C // S
    fg = lb + (1.0 - lb) * _sigmoid(f)
    qs = _silu(q)
    rows = lax.broadcasted_iota(I32, (C, C), 0)
    cols = lax.broadcasted_iota(I32, (C, C), 1)
    cum = _sel_dot((rows >= cols).astype(BF16), jnp.log2(fg))
    ck_all = cum - jnp.log2(1.0 - fg)

    srow = lax.broadcasted_iota(I32, (S * HGRN_DK, C), 0) // HGRN_DK
    scol = lax.broadcasted_iota(I32, (S * HGRN_DK, C), 1) % S
    spread = (srow == scol).astype(BF16)
    diag_keep = jnp.logical_and(rows // S == cols // S, rows >= cols)
    outs = []
    for h in range(HGRN_HEADS):
        blk = slice(h * HGRN_DK, (h + 1) * HGRN_DK)
        qh, ch, ck = qs[:, blk], cum[:, blk], ck_all[:, blk]
        last = ch[C - 1:C, :]
        x_cat = jnp.concatenate(
            [(qh * jnp.exp2(jnp.minimum(ch - _tile_bcast(ck, r), 0.0))).astype(BF16) for r in range(S)], axis=-1)
        kbe = jnp.exp2(_tile_bcast(ch, S - 1) - ck).astype(BF16)
        a_pieces, b_pieces = [], []
        for j in range(nsub - 1):
            lo = (j + 1) * S
            a = (qh[lo:, :] * jnp.exp2(ch[lo:, :] - ch[lo - 1:lo, :])).astype(BF16)
            a_pieces.append(jnp.concatenate([jnp.zeros((lo, HGRN_DK), BF16), a], axis=0))
            b = [kbe[j * S:lo, :], jnp.zeros((C - lo, HGRN_DK), BF16)]
            if j > 0:
                b.insert(0, jnp.zeros((j * S, HGRN_DK), BF16))
            b_pieces.append(jnp.concatenate(b, axis=0))
        att = (_dot_nt(jnp.concatenate(a_pieces, axis=-1), jnp.concatenate(b_pieces, axis=-1))
               + jnp.where(diag_keep, _dot(x_cat, spread), 0.0))
        vb = v[:, blk].astype(BF16)
        s_prev = st_ref[h]
        o = _dot_nt((qh * jnp.exp2(ch)).astype(BF16), s_prev.astype(BF16)) + _dot(att.astype(BF16), vb)
        st_ref[h] = s_prev * jnp.exp2(last) + _dot_tn(vb, jnp.exp2(last - ck).astype(BF16))
        ms = jnp.mean(o * o, axis=-1, keepdims=True)
        outs.append(o * lax.rsqrt(ms + RMS_EPS))
    return jnp.concatenate(outs, axis=-1) * nw * _silu(g)


def _hgrn(q, f, i, g, lower_bound, norm_w):
    t = q.shape[0]
    C = HGRN_CHUNK * HGRN_PER_STEP
    tok = pl.BlockSpec((C, D_HGRN), lambda c: (c, 0))
    vec = pl.BlockSpec((1, D_HGRN), lambda c: (0, 0))
    return pl.pallas_call(
        _hgrn_kernel,
        grid=(t // C,),
        in_specs=[tok, tok, tok, tok, vec, vec],
        out_specs=tok,
        out_shape=jax.ShapeDtypeStruct((t, D_HGRN), F32),
        scratch_shapes=[pltpu.VMEM((HGRN_HEADS, D_HGRN // HGRN_HEADS, HGRN_DK), F32)],
        compiler_params=pltpu.CompilerParams(dimension_semantics=("arbitrary",)),
        name="hgrn",
    )(q, f, i, g, lower_bound.reshape(1, D_HGRN).astype(F32), norm_w.reshape(1, D_HGRN).astype(F32))


def _post_kernel(x_ref, ys_ref, yh_ref, wo_ref, g1_ref, b1_ref, wrh_ref, wrl_ref, rb_ref,
                 wgs_ref, wus_ref, wds_ref, htt_ref, base_ref, idx_ref, gate_ref, cnt_ref):
    tm = x_ref.shape[0]
    mix = (_dot(ys_ref[...].astype(BF16), wo_ref[0:D_SSM, :])
           + _dot(yh_ref[...].astype(BF16), wo_ref[D_SSM:, :]))
    h1 = _layer_norm(ALPHA * x_ref[...] + mix, g1_ref[...], b1_ref[...])
    packed = _pack_pairs(h1[:, :D_MODEL // 2], h1[:, D_MODEL // 2:])
    for hh in range(ROW_HALVES):
        htt_ref[hh] = packed[:, hh * HALF_WORDS:(hh + 1) * HALF_WORDS]
    hb = h1.astype(BF16)
    hid = _silu(_dot(hb, wgs_ref[...])) * _dot(hb, wus_ref[...])
    base_ref[...] = ALPHA * h1 + _dot(hid.astype(BF16), wds_ref[...])

    hlo = (h1 - hb.astype(F32)).astype(BF16)
    logits = _dot_nt(wrh_ref[...], hb) + _dot_nt(wrh_ref[...], hlo) + _dot_nt(wrl_ref[...], hb)
    scores = _sigmoid(logits)
    biased = scores + rb_ref[...]
    per_group = N_EXPERTS // N_EXPERT_GROUPS
    eidx = lax.broadcasted_iota(I32, (N_EXPERTS, tm), 0)
    big = jnp.int32(1 << 20)
    gsc = []
    bidx = lax.broadcasted_iota(I32, (per_group, tm), 0)
    for gi in range(N_EXPERT_GROUPS):
        blk = biased[gi * per_group:(gi + 1) * per_group, :]
        m1 = jnp.max(blk, axis=0, keepdims=True)
        i1 = jnp.min(jnp.where(blk == m1, bidx, big), axis=0, keepdims=True)
        m2 = jnp.max(jnp.where(bidx == i1, NEG, blk), axis=0, keepdims=True)
        gsc.append(m1 + m2)
    cur = jnp.concatenate(gsc, axis=0)
    gidx = lax.broadcasted_iota(I32, (N_EXPERT_GROUPS, tm), 0)
    gsel = jnp.zeros((N_EXPERT_GROUPS, tm), F32)
    for _ in range(TOPK_GROUPS):
        m = jnp.max(cur, axis=0, keepdims=True)
        i = jnp.min(jnp.where(cur == m, gidx, big), axis=0, keepdims=True)
        hit = gidx == i
        gsel = jnp.where(hit, 1.0, gsel)
        cur = jnp.where(hit, NEG, cur)
    emask = jnp.concatenate(
        [jnp.broadcast_to(gsel[gi:gi + 1, :], (per_group, tm)) for gi in range(N_EXPERT_GROUPS)], axis=0)
    masked = jnp.where(emask > 0.0, biased, NEG)
    idx_rows, gate_rows = [], []
    for _ in range(TOP_K):
        m = jnp.max(masked, axis=0, keepdims=True)
        i = jnp.min(jnp.where(masked == m, eidx, big), axis=0, keepdims=True)
        hit = eidx == i
        idx_rows.append(i)
        gate_rows.append(jnp.sum(jnp.where(hit, scores, 0.0), axis=0, keepdims=True))
        masked = jnp.where(hit, NEG, masked)
    picked = jnp.where(jnp.logical_and(masked == NEG, emask > 0.0), 1.0, 0.0)
    gates = jnp.concatenate(gate_rows, axis=0)
    gates = gates / jnp.sum(gates, axis=0, keepdims=True) * ROUTED_SCALE
    idx_ref[0] = jnp.concatenate(idx_rows, axis=0)
    gate_ref[0] = gates

    @pl.when(pl.program_id(0) == 0)
    def _():
        cnt_ref[...] = jnp.zeros(cnt_ref.shape, F32)

    cnt_ref[...] += _dot(picked.astype(BF16), jnp.ones((tm, LANES), BF16))


def _post(x2, y_ssd, y_hgrn, wo, g1, b1, wr_hi, wr_lo, rbias, wgs, wus, wds):
    t = x2.shape[0]
    tm = TM_TOK
    nt = t // tm
    full = lambda shape: pl.BlockSpec(shape, lambda i: (0,) * len(shape))
    return pl.pallas_call(
        _post_kernel,
        grid=(nt,),
        in_specs=[pl.BlockSpec((tm, D_MODEL), lambda i: (i, 0)),
                  pl.BlockSpec((tm, D_SSM), lambda i: (i, 0)),
                  pl.BlockSpec((tm, D_HGRN), lambda i: (i, 0)),
                  full((D_MODEL, D_MODEL)), full((1, D_MODEL)), full((1, D_MODEL)),
                  full((N_EXPERTS, D_MODEL)), full((N_EXPERTS, D_MODEL)), full((N_EXPERTS, 1)),
                  full((D_MODEL, D_EXPERT)), full((D_MODEL, D_EXPERT)), full((D_EXPERT, D_MODEL))],
        out_specs=[pl.BlockSpec((ROW_HALVES, tm, HALF_WORDS), lambda i: (0, i, 0)),
                   pl.BlockSpec((tm, D_MODEL), lambda i: (i, 0)),
                   pl.BlockSpec((1, TOP_K, tm), lambda i: (i, 0, 0)),
                   pl.BlockSpec((1, TOP_K, tm), lambda i: (i, 0, 0)),
                   pl.BlockSpec((N_EXPERTS, LANES), lambda i: (0, 0))],
        out_shape=[jax.ShapeDtypeStruct((ROW_HALVES, t, HALF_WORDS), U32),
                   jax.ShapeDtypeStruct((t, D_MODEL), F32),
                   jax.ShapeDtypeStruct((nt, TOP_K, tm), I32),
                   jax.ShapeDtypeStruct((nt, TOP_K, tm), F32),
                   jax.ShapeDtypeStruct((N_EXPERTS, LANES), F32)],
        compiler_params=pltpu.CompilerParams(dimension_semantics=("arbitrary",),
                                             vmem_limit_bytes=48 * 1024 * 1024),
        name="post",
    )(x2, y_ssd, y_hgrn, wo, g1, b1, wr_hi, wr_lo, rbias, wgs, wus, wds)


def _max_blocks(t):
    return (t * TOP_K + N_EXPERTS * (EXPERT_ROWS - 1)) // EXPERT_ROWS


def _pos_kernel(idx_ref, total_ref, dest_ref, bstart_ref, nblk_ref, cnt_ref, start_ref):
    i = pl.program_id(0)
    tm = idx_ref.shape[-1]
    idx = idx_ref[0]
    eidx = lax.broadcasted_iota(I32, (N_EXPERTS, tm), 0)
    sel = [eidx == idx[k:k + 1, :] for k in range(TOP_K)]
    onehot = sel[0]
    for k in range(1, TOP_K):
        onehot = jnp.logical_or(onehot, sel[k])
    mt = jnp.where(onehot, 1.0, 0.0).astype(BF16)

    @pl.when(i == 0)
    def _():
        nb = jnp.floor((total_ref[...] + (EXPERT_ROWS - 1)) * (1.0 / EXPERT_ROWS))
        r = lax.broadcasted_iota(I32, (N_EXPERTS, N_EXPERTS), 0)
        c = lax.broadcasted_iota(I32, (N_EXPERTS, N_EXPERTS), 1)
        end = _dot((r >= c).astype(BF16), nb.astype(BF16))
        start_ref[...] = (end - nb) * EXPERT_ROWS
        cnt_ref[...] = jnp.zeros(cnt_ref.shape, F32)
        bstart_ref[...] = (end - nb).astype(I32)
        nblk_ref[...] = nb.astype(I32)

    r = lax.broadcasted_iota(I32, (tm, tm), 0)
    c = lax.broadcasted_iota(I32, (tm, tm), 1)
    before = _dot(mt, (r < c).astype(BF16))
    slot = start_ref[:, 0:1] + cnt_ref[:, 0:1] + before
    rows = [jnp.sum(jnp.where(sel[k], slot, 0.0), axis=0, keepdims=True) for k in range(TOP_K)]
    dest_ref[0] = jnp.concatenate(rows, axis=0).astype(I32)
    cnt_ref[...] += _dot(mt, jnp.ones((tm, LANES), BF16))


def _positions(idx, totals):
    nt, _, tm = idx.shape
    return pl.pallas_call(
        _pos_kernel,
        grid=(nt,),
        in_specs=[pl.BlockSpec((1, TOP_K, tm), lambda i: (i, 0, 0)),
                  pl.BlockSpec((N_EXPERTS, LANES), lambda i: (0, 0))],
        out_specs=[pl.BlockSpec((1, TOP_K, tm), lambda i: (i, 0, 0)),
                   pl.BlockSpec((N_EXPERTS, LANES), lambda i: (0, 0)),
                   pl.BlockSpec((N_EXPERTS, LANES), lambda i: (0, 0))],
        out_shape=[jax.ShapeDtypeStruct((nt, TOP_K, tm), I32),
                   jax.ShapeDtypeStruct((N_EXPERTS, LANES), I32),
                   jax.ShapeDtypeStruct((N_EXPERTS, LANES), I32)],
        scratch_shapes=[pltpu.VMEM((N_EXPERTS, LANES), F32), pltpu.VMEM((N_EXPERTS, LANES), F32)],
        compiler_params=pltpu.CompilerParams(dimension_semantics=("arbitrary",)),
        name="positions",
    )(idx, totals)


def _sc_mesh():
    return plsc.VectorSubcoreMesh(core_axis_name="core", subcore_axis_name="subcore")


def _sc_dispatch(src, index, n_out):
    n = index.shape[0]
    src_blocks = src.shape[0] // SC_WINDOW
    index = index.reshape(1, n)

    @pl.kernel(out_type=jax.ShapeDtypeStruct((n_out, HALF_WORDS), src.dtype), mesh=_sc_mesh(), name="sc_dispatch")
    def scatter(src_hbm, idx_hbm, out_hbm):
        def body(rows_vmem, idx_vmem):
            pltpu.sync_copy(rows_vmem, out_hbm.at[idx_vmem.at[0]])

        pltpu.emit_pipeline(
            body, grid=(n // SC_WINDOW,),
            in_specs=[pl.BlockSpec((SC_WINDOW, HALF_WORDS), index_map=lambda i: (i % src_blocks, 0)),
                      pl.BlockSpec((1, SC_WINDOW), index_map=lambda i: (0, i))],
            out_specs=[],
            core_axis_name=("core", "subcore"), dimension_semantics=(pltpu.PARALLEL,))(src_hbm, idx_hbm)

    return scatter(src, index)


def _sc_gather(src, index):
    n = index.shape[0]
    index = index.reshape(1, n)

    @pl.kernel(out_type=jax.ShapeDtypeStruct((n, HALF_WORDS), src.dtype), mesh=_sc_mesh(), name="sc_gather")
    def gather(src_hbm, idx_hbm, out_hbm):
        def body(idx_vmem, rows_vmem):
            pltpu.sync_copy(src_hbm.at[idx_vmem.at[0]], rows_vmem)

        pltpu.emit_pipeline(
            body, grid=(n // SC_WINDOW,),
            in_specs=[pl.BlockSpec((1, SC_WINDOW), index_map=lambda i: (0, i))],
            out_specs=[pl.BlockSpec((SC_WINDOW, HALF_WORDS), index_map=lambda i: (i, 0))],
            core_axis_name=("core", "subcore"), dimension_semantics=(pltpu.PARALLEL,))(idx_hbm, out_hbm)

    return gather(src, index)


def _expert_kernel(bstart_ref, nblk_ref, xs_hbm, wg_ref, wu_ref, wd_ref, ys_hbm,
                   xbuf, ybuf, wgb_ref, wub_ref, wdb_ref, xsem, ysem, ypend):
    e = pl.program_id(0)
    last = pl.num_programs(0) - 1
    n = nblk_ref[e]
    b0 = bstart_ref[e]
    total = bstart_ref[last] + nblk_ref[last]
    blk_rows = EXPERT_ROWS * ROW_HALVES
    xdepth = xbuf.shape[0]
    ydepth = ybuf.shape[0]
    lookahead = xdepth - MAX_CHUNK

    def hbm_block(ref, b):
        return ref.at[pl.ds(pl.multiple_of(b * blk_rows, blk_rows), blk_rows)]

    def x_copy(b):
        slot = b % xdepth
        return pltpu.make_async_copy(hbm_block(xs_hbm, b), xbuf.at[slot], xsem.at[slot])

    def y_copy(b, slot):
        return pltpu.make_async_copy(ybuf.at[slot], hbm_block(ys_hbm, b), ysem.at[slot])

    @pl.when(e == 0)
    def _():
        for s in range(ydepth):
            ypend[s] = 0
        for b in range(lookahead):
            @pl.when(b < total)
            def _():
                x_copy(b).start()

    def chunk(j, m):
        b = b0 + j
        for i in range(m):
            @pl.when(b + lookahead + i < total)
            def _():
                x_copy(b + lookahead + i).start()

        for i in range(m):
            x_copy(b + i).wait()

            @pl.when(ypend[(b + i) % ydepth] == 1)
            def _():
                y_copy(b + i, (b + i) % ydepth).wait()

        for i in range(m):
            xslot = (b + i) % xdepth
            yslot = (b + i) % ydepth
            halves = [_unpack_pairs(xbuf[xslot, hh * EXPERT_ROWS:(hh + 1) * EXPERT_ROWS, :])
                      for hh in range(ROW_HALVES)]
            xb = jnp.concatenate([lo for lo, _ in halves] + [hi for _, hi in halves], axis=-1).astype(BF16)
            hid = _silu(_dot(xb, wgb_ref[...])) * _dot(xb, wub_ref[...])
            yb = _dot(hid.astype(BF16), wdb_ref[...])
            packed = _pack_pairs(yb[:, :D_MODEL // 2], yb[:, D_MODEL // 2:])
            for hh in range(ROW_HALVES):
                ybuf[yslot, hh * EXPERT_ROWS:(hh + 1) * EXPERT_ROWS, :] = (
                    packed[:, hh * HALF_WORDS:(hh + 1) * HALF_WORDS])

        for i in range(m):
            y_copy(b + i, (b + i) % ydepth).start()
            ypend[(b + i) % ydepth] = 1

    @pl.when(n > 0)
    def _():
        wgb_ref[...] = wg_ref[0].astype(BF16)
        wub_ref[...] = wu_ref[0].astype(BF16)
        wdb_ref[...] = wd_ref[0].astype(BF16)
        full = n // MAX_CHUNK

        def full_chunk(q, carry):
            chunk(q * MAX_CHUNK, MAX_CHUNK)
            return carry

        lax.fori_loop(0, full, full_chunk, 0)
        for m in range(1, MAX_CHUNK):
            @pl.when(n - full * MAX_CHUNK == m)
            def _():
                chunk(full * MAX_CHUNK, m)

    @pl.when(e == last)
    def _():
        for s in range(ydepth):
            @pl.when(ypend[s] == 1)
            def _():
                y_copy(b0, s).wait()


def _experts(bstart, nblk, xs, w_gate, w_up, w_down):
    blk_shape = (EXPERT_ROWS * ROW_HALVES, HALF_WORDS)
    wspec = lambda shape: pl.BlockSpec((1,) + shape, lambda e, bs, nb: (e, 0, 0))
    grid_spec = pltpu.PrefetchScalarGridSpec(
        num_scalar_prefetch=2,
        grid=(N_EXPERTS,),
        in_specs=[pl.BlockSpec(memory_space=pl.ANY),
                  wspec((D_MODEL, D_EXPERT)), wspec((D_MODEL, D_EXPERT)), wspec((D_EXPERT, D_MODEL))],
        out_specs=pl.BlockSpec(memory_space=pl.ANY),
        scratch_shapes=[pltpu.VMEM((X_RING,) + blk_shape, U32), pltpu.VMEM((Y_RING,) + blk_shape, U32),
                        pltpu.VMEM((D_MODEL, D_EXPERT), BF16), pltpu.VMEM((D_MODEL, D_EXPERT), BF16),
                        pltpu.VMEM((D_EXPERT, D_MODEL), BF16),
                        pltpu.SemaphoreType.DMA((X_RING,)), pltpu.SemaphoreType.DMA((Y_RING,)),
                        pltpu.SMEM((Y_RING,), I32)],
    )
    return pl.pallas_call(
        _expert_kernel,
        grid_spec=grid_spec,
        out_shape=jax.ShapeDtypeStruct(xs.shape, U32),
        compiler_params=pltpu.CompilerParams(dimension_semantics=("arbitrary",), has_side_effects=True),
        name="experts",
    )(bstart, nblk, xs, w_gate, w_up, w_down)


def _combine_kernel(gate_ref, base_ref, g2_ref, b2_ref, rows_ref, o_ref):
    tm = base_ref.shape[0]
    gates_t = jnp.concatenate([gate_ref[0], jnp.zeros((LANES - TOP_K, tm), F32)], axis=0).T
    half = D_MODEL // 2
    lo_acc = [base_ref[:, hh * HALF_WORDS:(hh + 1) * HALF_WORDS] for hh in range(ROW_HALVES)]
    hi_acc = [base_ref[:, half + hh * HALF_WORDS:half + (hh + 1) * HALF_WORDS] for hh in range(ROW_HALVES)]
    for k in range(TOP_K):
        gk = gates_t[:, k:k + 1]
        for hh in range(ROW_HALVES):
            lo, hi = _unpack_pairs(rows_ref[k, hh])
            lo_acc[hh] = lo_acc[hh] + gk * lo
            hi_acc[hh] = hi_acc[hh] + gk * hi
    o_ref[...] = _layer_norm(jnp.concatenate(lo_acc + hi_acc, axis=-1), g2_ref[...], b2_ref[...])


def _combine_piece_kernel(gate_ref, base_ref, g2_ref, b2_ref, rows_ref, prev_ref, o_ref):
    del prev_ref
    _combine_kernel(gate_ref, base_ref, g2_ref, b2_ref, rows_ref, o_ref)


def _combine(gates, base, g2, b2, rows, first_tile, prev):
    t = base.shape[0]
    tm = gates.shape[-1]
    tiles = rows.shape[2] // tm
    in_specs = [pl.BlockSpec((1, TOP_K, tm), lambda i: (i + first_tile, 0, 0)),
                pl.BlockSpec((tm, D_MODEL), lambda i: (i + first_tile, 0)),
                pl.BlockSpec((1, D_MODEL), lambda i: (0, 0)),
                pl.BlockSpec((1, D_MODEL), lambda i: (0, 0)),
                pl.BlockSpec((TOP_K, ROW_HALVES, tm, HALF_WORDS), lambda i: (0, 0, i, 0))]
    args = [gates, base, g2, b2, rows]
    if prev is not None:
        in_specs.append(pl.BlockSpec(memory_space=pl.ANY))
        args.append(prev)
    return pl.pallas_call(
        _combine_kernel if prev is None else _combine_piece_kernel,
        grid=(tiles,),
        in_specs=in_specs,
        out_specs=pl.BlockSpec((tm, D_MODEL), lambda i: (i + first_tile, 0)),
        out_shape=jax.ShapeDtypeStruct((t, D_MODEL), F32),
        input_output_aliases={} if prev is None else {len(args) - 1: 0},
        compiler_params=pltpu.CompilerParams(dimension_semantics=("arbitrary",),
                                             vmem_limit_bytes=48 * 1024 * 1024),
        name="combine",
    )(*args)


def _layer(h2, w_in, conv_w, conv_b, dt_bias, a_log, d_skip, ssd_norm_w, lower_bound, hgrn_norm_w, w_out,
           ln1_g, ln1_b, w_router, router_bias, w_gate_e, w_up_e, w_down_e, w_gate_s, w_up_s, w_down_s,
           ln2_g, ln2_b):
    t = h2.shape[0]
    dt0 = D_SSM + D_CONV
    q0 = dt0 + SSD_HEADS
    w_perm = jnp.concatenate(
        [w_in[:, :dt0], w_in[:, q0:], w_in[:, dt0:q0], jnp.zeros((D_MODEL, DT_PAD - SSD_HEADS), w_in.dtype)],
        axis=1).astype(BF16)
    z, xbc, q, f, i, g, dt = _inproj(h2, w_perm)
    y_ssd = _ssd(z, xbc, dt, conv_w, conv_b, dt_bias, a_log, d_skip, ssd_norm_w)
    y_hgrn = _hgrn(q, f, i, g, lower_bound, hgrn_norm_w)

    wr_t = w_router.astype(F32).T
    wr_hi = wr_t.astype(BF16)
    wr_lo = (wr_t - wr_hi.astype(F32)).astype(BF16)
    row = lambda v: v.reshape(1, -1).astype(F32)
    htt, base, idx, gates, totals = _post(
        h2, y_ssd, y_hgrn, w_out.astype(BF16), row(ln1_g), row(ln1_b), wr_hi, wr_lo,
        router_bias.reshape(N_EXPERTS, 1).astype(F32),
        w_gate_s.astype(BF16), w_up_s.astype(BF16), w_down_s.astype(BF16))

    dest, bstart, nblk = _positions(idx, totals)
    n_rows = _max_blocks(t) * EXPERT_ROWS
    d = jnp.transpose(dest, (1, 0, 2)).reshape(TOP_K, 1, t)
    slot = (d // EXPERT_ROWS) * (EXPERT_ROWS * ROW_HALVES) + d % EXPERT_ROWS
    slot = (slot + jnp.arange(ROW_HALVES, dtype=I32).reshape(1, ROW_HALVES, 1) * EXPERT_ROWS).reshape(-1)
    xs = _sc_dispatch(htt.reshape(ROW_HALVES * t, HALF_WORDS), slot, n_rows * ROW_HALVES)
    ys = _experts(bstart[:, 0], nblk[:, 0], xs, w_gate_e, w_up_e, w_down_e)
    slot = slot.reshape(TOP_K, ROW_HALVES, t)
    tp = t // GATHER_PIECES
    out = None
    for p in range(GATHER_PIECES):
        rows = _sc_gather(ys, slot[:, :, p * tp:(p + 1) * tp].reshape(-1))
        out = _combine(gates, base, row(ln2_g), row(ln2_b), rows.reshape(TOP_K, ROW_HALVES, tp, HALF_WORDS),
                       p * (tp // TM_TOK), out)
    return out


def kernel(x, w_in, conv_w, conv_b, dt_bias, a_log, d_skip, ssd_norm_w, hgrn_lb_logits, hgrn_norm_w, w_out,
           ln1_g, ln1_b, w_router, router_bias, w_gate_e, w_up_e, w_down_e, w_gate_s, w_up_s, w_down_s,
           ln2_g, ln2_b):
    bsz, t, d = x.shape
    assert bsz == 1 and d == D_MODEL, "the recurrent mixers carry state across the flattened token axis"
    depth = w_in.shape[0]
    lower_bounds = jnp.cumsum(jax.nn.softmax(hgrn_lb_logits.astype(F32), axis=0), axis=0)
    h = x.reshape(bsz * t, d)
    for l in range(depth):
        h = _layer(h, w_in[l], conv_w[l], conv_b[l], dt_bias[l], a_log[l], d_skip[l], ssd_norm_w[l],
                   lower_bounds[l], hgrn_norm_w[l], w_out[l], ln1_g[l], ln1_b[l], w_router[l],
                   router_bias[l], w_gate_e[l], w_up_e[l], w_down_e[l], w_gate_s[l], w_up_s[l],
                   w_down_s[l], ln2_g[l], ln2_b[l])
    return h.reshape(bsz, t, d)
```

```python
import jax
import jax.numpy as jnp
from jax import lax
from jax.experimental import pallas as pl
from jax.experimental.pallas import tpu as pltpu
from jax.experimental.pallas import tpu_sc as plsc

F32 = jnp.float32
BF16 = jnp.bfloat16
I32 = jnp.int32
U32 = jnp.uint32

D_MODEL = 1024
D_SSM = 512
D_HGRN = 512
SSD_HEADS = 8
SSD_HEAD_DIM = 64
SSD_GROUPS = 2
SSD_STATE = 128
SSD_CONV = 4
SSD_CHUNK = 128
SSD_PER_STEP = 4
D_CONV = D_SSM + 2 * SSD_GROUPS * SSD_STATE
HGRN_HEADS = 4
HGRN_DK = 128
HGRN_CHUNK = 64
HGRN_SUB = 8
HGRN_PER_STEP = 4
N_EXPERTS = 256
TOP_K = 8
N_EXPERT_GROUPS = 8
TOPK_GROUPS = 4
D_EXPERT = 256
ROUTED_SCALE = 2.5
ALPHA = 2.0 ** 0.25
LN_EPS = 1e-5
RMS_EPS = 1e-6

LANES = 128
SUBLANES = 8
ROW_TILES = D_MODEL // LANES
ROW_HALVES = 2
HALF_WORDS = D_MODEL // 2 // ROW_HALVES
SC_WINDOW = 128
GATHER_PIECES = 4
DT_PAD = LANES
N_IN_PAD = D_SSM + D_CONV + 4 * D_HGRN + DT_PAD

TM_PROJ = 256
TM_TOK = 256
TM_SHARED = 512
EXPERT_ROWS = 256
MAX_CHUNK = 2
X_RING = 8
Y_RING = 8
NEG = -1e30


def _sigmoid(x):
    return 1.0 / (1.0 + jnp.exp(-x))


def _silu(x):
    return x * _sigmoid(x)


def _split3(x):
    hi = x.astype(BF16)
    r = x - hi.astype(F32)
    mid = r.astype(BF16)
    lo = (r - mid.astype(F32)).astype(BF16)
    return hi, mid, lo


def _dot(a, b):
    return jnp.dot(a, b, preferred_element_type=F32)


def _dot_nt(a, b):
    return lax.dot_general(a, b, (((1,), (1,)), ((), ())), preferred_element_type=F32)


def _dot_tn(a, b):
    return lax.dot_general(a, b, (((0,), (0,)), ((), ())), preferred_element_type=F32)


def _sel_dot(sel, x):
    hi, mid, lo = _split3(x)
    return _dot(sel, hi) + _dot(sel, mid) + _dot(sel, lo)


def _dot_sel(x, sel):
    hi, mid, lo = _split3(x)
    return _dot(hi, sel) + _dot(mid, sel) + _dot(lo, sel)


def _pack_pairs(lo, hi):
    lo_bits = pltpu.bitcast(lo.astype(BF16).astype(F32), U32) >> 16
    hi_bits = pltpu.bitcast(hi.astype(BF16).astype(F32), U32) & jnp.uint32(0xFFFF0000)
    return hi_bits | lo_bits


def _unpack_pairs(p):
    return pltpu.bitcast(p << 16, F32), pltpu.bitcast(p & jnp.uint32(0xFFFF0000), F32)


def _layer_norm(x, g, b):
    mu = jnp.mean(x, axis=-1, keepdims=True)
    xc = x - mu
    var = jnp.mean(xc * xc, axis=-1, keepdims=True)
    return xc * lax.rsqrt(var + LN_EPS) * g + b


def _inproj_kernel(x_ref, w_ref, z_ref, xbc_ref, q_ref, f_ref, i_ref, g_ref, dt_ref):
    xb = x_ref[...].astype(BF16)
    col = 0
    for ref in (z_ref, xbc_ref, q_ref, f_ref, i_ref, g_ref, dt_ref):
        n = ref.shape[-1]
        ref[...] = _dot(xb, w_ref[:, col:col + n])
        col += n


def _inproj(x2, w_perm):
    t = x2.shape[0]
    widths = (D_SSM, D_CONV, D_HGRN, D_HGRN, D_HGRN, D_HGRN, DT_PAD)
    return pl.pallas_call(
        _inproj_kernel,
        grid=(t // TM_PROJ,),
        in_specs=[pl.BlockSpec((TM_PROJ, D_MODEL), lambda i: (i, 0)),
                  pl.BlockSpec((D_MODEL, N_IN_PAD), lambda i: (0, 0))],
        out_specs=[pl.BlockSpec((TM_PROJ, n), lambda i: (i, 0)) for n in widths],
        out_shape=[jax.ShapeDtypeStruct((t, n), F32) for n in widths],
        compiler_params=pltpu.CompilerParams(dimension_semantics=("arbitrary",),
                                             vmem_limit_bytes=48 * 1024 * 1024),
        name="inproj",
    )(x2, w_perm)


def _ssd_kernel(z_ref, xbc_ref, dt_ref, cw_ref, cb_ref, dtb_ref, alog_ref, dskip_ref, nw_ref,
                y_ref, ext_ref, st_ref):
    n = xbc_ref.shape[0]
    halo = SUBLANES

    @pl.when(pl.program_id(0) == 0)
    def _():
        ext_ref[0:halo, :] = jnp.zeros((halo, D_CONV), F32)
        st_ref[...] = jnp.zeros(st_ref.shape, F32)

    ext_ref[halo:halo + n, :] = xbc_ref[...]
    acc = jnp.broadcast_to(cb_ref[...], (n, D_CONV))
    for k in range(SSD_CONV):
        off = halo - (SSD_CONV - 1) + k
        acc = acc + cw_ref[k:k + 1, :] * ext_ref[off:off + n, :]
    ext_ref[0:halo, :] = xbc_ref[n - halo:n, :]
    u = _silu(acc)
    for c in range(n // SSD_CHUNK):
        rows = slice(c * SSD_CHUNK, (c + 1) * SSD_CHUNK)
        y_ref[rows, :] = _ssd_chunk(u[rows, :], z_ref[rows, :], dt_ref[rows, :] + dtb_ref[...],
                                    -jnp.exp(alog_ref[...]), dskip_ref[...], nw_ref[...], st_ref)


def _ssd_chunk(u, z, draw, a, dskip, nw, st_ref):
    L = SSD_CHUNK
    xs = u[:, :D_SSM]
    bm = u[:, D_SSM:D_SSM + SSD_GROUPS * SSD_STATE]
    cm = u[:, D_SSM + SSD_GROUPS * SSD_STATE:]

    dt = jnp.maximum(draw, 0.0) + jnp.log(1.0 + jnp.exp(-jnp.abs(draw)))
    ad = dt * a
    rows = lax.broadcasted_iota(I32, (L, L), 0)
    cols = lax.broadcasted_iota(I32, (L, L), 1)
    causal = rows >= cols
    a_cum = _sel_dot(causal.astype(BF16), ad)
    a_cum_t = a_cum.T

    hrow = lax.broadcasted_iota(I32, (LANES, D_SSM), 0)
    hcol = lax.broadcasted_iota(I32, (LANES, D_SSM), 1) // SSD_HEAD_DIM
    expand = (hrow == hcol).astype(BF16)
    dt_x = _dot_sel(dt, expand)
    acx = _dot_sel(a_cum, expand)
    last = acx[L - 1:L, :]
    ea_x = jnp.exp(acx)
    dec_x = jnp.exp(last - acx)
    elast_x = jnp.exp(last)

    xdt = xs * dt_x
    gw = SSD_HEADS // SSD_GROUPS * SSD_HEAD_DIM
    lane_head = lax.broadcasted_iota(I32, (L, gw), 1) // SSD_HEAD_DIM
    ys = []
    for g in range(SSD_GROUPS):
        bg = bm[:, g * SSD_STATE:(g + 1) * SSD_STATE]
        cg = cm[:, g * SSD_STATE:(g + 1) * SSD_STATE].astype(BF16)
        bg_t = bg.T.astype(BF16)
        gmat = _dot(cg, bg_t)
        xdt_g = xdt[:, g * gw:(g + 1) * gw]
        xdt_gb = xdt_g.astype(BF16)
        r_prev = st_ref[g]
        y_g = _dot(cg, r_prev.astype(BF16)) * ea_x[:, g * gw:(g + 1) * gw]
        new_s = _dot(bg_t, (xdt_g * dec_x[:, g * gw:(g + 1) * gw]).astype(BF16))
        st_ref[g] = r_prev * elast_x[:, g * gw:(g + 1) * gw] + new_s
        for j in range(SSD_HEADS // SSD_GROUPS):
            h = g * (SSD_HEADS // SSD_GROUPS) + j
            diff = a_cum[:, h:h + 1] - a_cum_t[h:h + 1, :]
            decay = jnp.exp(jnp.where(causal, diff, NEG))
            yd = _dot((gmat * decay).astype(BF16), xdt_gb)
            y_g = y_g + jnp.where(lane_head == j, yd, 0.0)
        ys.append(y_g)
    y = jnp.concatenate(ys, axis=-1) + xs * dskip
    y = y * _silu(z)
    outs = []
    ng = D_SSM // SSD_GROUPS
    for g in range(SSD_GROUPS):
        yg = y[:, g * ng:(g + 1) * ng]
        ms = jnp.mean(yg * yg, axis=-1, keepdims=True)
        outs.append(yg * lax.rsqrt(ms + RMS_EPS))
    return jnp.concatenate(outs, axis=-1) * nw


def _ssd(z, xbc, dt, conv_w, conv_b, dt_bias, a_log, d_skip, norm_w):
    t = z.shape[0]
    L = SSD_CHUNK * SSD_PER_STEP
    pad = lambda v: jnp.pad(v.astype(F32), (0, LANES - v.shape[0])).reshape(1, LANES)
    full = lambda shape: pl.BlockSpec(shape, lambda c: (0,) * len(shape))
    return pl.pallas_call(
        _ssd_kernel,
        grid=(t // L,),
        in_specs=[pl.BlockSpec((L, D_SSM), lambda c: (c, 0)),
                  pl.BlockSpec((L, D_CONV), lambda c: (c, 0)),
                  pl.BlockSpec((L, DT_PAD), lambda c: (c, 0)),
                  full((SSD_CONV, D_CONV)), full((1, D_CONV)), full((1, LANES)), full((1, LANES)),
                  full((1, D_SSM)), full((1, D_SSM))],
        out_specs=pl.BlockSpec((L, D_SSM), lambda c: (c, 0)),
        out_shape=jax.ShapeDtypeStruct((t, D_SSM), F32),
        scratch_shapes=[pltpu.VMEM((L + SUBLANES, D_CONV), F32),
                        pltpu.VMEM((SSD_GROUPS, SSD_STATE, D_SSM // SSD_GROUPS), F32)],
        compiler_params=pltpu.CompilerParams(dimension_semantics=("arbitrary",)),
        name="ssd",
    )(z, xbc, dt, conv_w.astype(F32), conv_b.reshape(1, D_CONV).astype(F32), pad(dt_bias), pad(a_log),
      jnp.repeat(d_skip.astype(F32), SSD_HEAD_DIM).reshape(1, D_SSM), norm_w.reshape(1, D_SSM).astype(F32))


def _tile_bcast(x, r):
    n, d = x.shape
    x3 = x.reshape(n // SUBLANES, SUBLANES, d)
    return jnp.broadcast_to(x3[:, r:r + 1, :], x3.shape).reshape(n, d)


def _hgrn_kernel(q_ref, f_ref, i_ref, g_ref, lb_ref, nw_ref, o_ref, st_ref):
    @pl.when(pl.program_id(0) == 0)
    def _():
        st_ref[...] = jnp.zeros(st_ref.shape, F32)

    for c in range(o_ref.shape[0] // HGRN_CHUNK):
        rows = slice(c * HGRN_CHUNK, (c + 1) * HGRN_CHUNK)
        o_ref[rows, :] = _hgrn_chunk(q_ref[rows, :], f_ref[rows, :], i_ref[rows, :], g_ref[rows, :],
                                     lb_ref[...], nw_ref[...], st_ref)


def _hgrn_chunk(q, f, v, g, lb, nw, st_ref):
    C = HGRN_CHUNK
    S = HGRN_SUB
    nsub = C // S
    fg = lb + (1.0 - lb) * _sigmoid(f)
    qs = _silu(q)
    rows = lax.broadcasted_iota(I32, (C, C), 0)
    cols = lax.broadcasted_iota(I32, (C, C), 1)
    cum = _sel_dot((rows >= cols).astype(BF16), jnp.log2(fg))
    ck_all = cum - jnp.log2(1.0 - fg)

    srow = lax.broadcasted_iota(I32, (S * HGRN_DK, C), 0) // HGRN_DK
    scol = lax.broadcasted_iota(I32, (S * HGRN_DK, C), 1) % S
    spread = (srow == scol).astype(BF16)
    diag_keep = jnp.logical_and(rows // S == cols // S, rows >= cols)
    outs = []
    for h in range(HGRN_HEADS):
        blk = slice(h * HGRN_DK, (h + 1) * HGRN_DK)
        qh, ch, ck = qs[:, blk], cum[:, blk], ck_all[:, blk]
        last = ch[C - 1:C, :]
        x_cat = jnp.concatenate(
            [(qh * jnp.exp2(jnp.minimum(ch - _tile_bcast(ck, r), 0.0))).astype(BF16) for r in range(S)], axis=-1)
        kbe = jnp.exp2(_tile_bcast(ch, S - 1) - ck).astype(BF16)
        a_pieces, b_pieces = [], []
        for j in range(nsub - 1):
            lo = (j + 1) * S
            a = (qh[lo:, :] * jnp.exp2(ch[lo:, :] - ch[lo - 1:lo, :])).astype(BF16)
            a_pieces.append(jnp.concatenate([jnp.zeros((lo, HGRN_DK), BF16), a], axis=0))
            b = [kbe[j * S:lo, :], jnp.zeros((C - lo, HGRN_DK), BF16)]
            if j > 0:
                b.insert(0, jnp.zeros((j * S, HGRN_DK), BF16))
            b_pieces.append(jnp.concatenate(b, axis=0))
        att = (_dot_nt(jnp.concatenate(a_pieces, axis=-1), jnp.concatenate(b_pieces, axis=-1))
               + jnp.where(diag_keep, _dot(x_cat, spread), 0.0))
        vb = v[:, blk].astype(BF16)
        s_prev = st_ref[h]
        o = _dot_nt((qh * jnp.exp2(ch)).astype(BF16), s_prev.astype(BF16)) + _dot(att.astype(BF16), vb)
        st_ref[h] = s_prev * jnp.exp2(last) + _dot_tn(vb, jnp.exp2(last - ck).astype(BF16))
        ms = jnp.mean(o * o, axis=-1, keepdims=True)
        outs.append(o * lax.rsqrt(ms + RMS_EPS))
    return jnp.concatenate(outs, axis=-1) * nw * _silu(g)


def _hgrn(q, f, i, g, lower_bound, norm_w):
    t = q.shape[0]
    C = HGRN_CHUNK * HGRN_PER_STEP
    tok = pl.BlockSpec((C, D_HGRN), lambda c: (c, 0))
    vec = pl.BlockSpec((1, D_HGRN), lambda c: (0, 0))
    return pl.pallas_call(
        _hgrn_kernel,
        grid=(t // C,),
        in_specs=[tok, tok, tok, tok, vec, vec],
        out_specs=tok,
        out_shape=jax.ShapeDtypeStruct((t, D_HGRN), F32),
        scratch_shapes=[pltpu.VMEM((HGRN_HEADS, D_HGRN // HGRN_HEADS, HGRN_DK), F32)],
        compiler_params=pltpu.CompilerParams(dimension_semantics=("arbitrary",)),
        name="hgrn",
    )(q, f, i, g, lower_bound.reshape(1, D_HGRN).astype(F32), norm_w.reshape(1, D_HGRN).astype(F32))


def _post_kernel(x_ref, ys_ref, yh_ref, wo_ref, g1_ref, b1_ref, wrh_ref, wrl_ref, rb_ref,
                 htt_ref, base_ref, idx_ref, gate_ref, cnt_ref):
    tm = x_ref.shape[0]
    mix = (_dot(ys_ref[...].astype(BF16), wo_ref[0:D_SSM, :])
           + _dot(yh_ref[...].astype(BF16), wo_ref[D_SSM:, :]))
    h1 = _layer_norm(ALPHA * x_ref[...] + mix, g1_ref[...], b1_ref[...])
    packed = _pack_pairs(h1[:, :D_MODEL // 2], h1[:, D_MODEL // 2:])
    for hh in range(ROW_HALVES):
        htt_ref[hh] = packed[:, hh * HALF_WORDS:(hh + 1) * HALF_WORDS]
    hb = h1.astype(BF16)
    base_ref[...] = ALPHA * h1

    hlo = (h1 - hb.astype(F32)).astype(BF16)
    logits = _dot_nt(wrh_ref[...], hb) + _dot_nt(wrh_ref[...], hlo) + _dot_nt(wrl_ref[...], hb)
    scores = _sigmoid(logits)
    biased = scores + rb_ref[...]
    per_group = N_EXPERTS // N_EXPERT_GROUPS
    eidx = lax.broadcasted_iota(I32, (N_EXPERTS, tm), 0)
    big = jnp.int32(1 << 20)
    gsc = []
    bidx = lax.broadcasted_iota(I32, (per_group, tm), 0)
    for gi in range(N_EXPERT_GROUPS):
        blk = biased[gi * per_group:(gi + 1) * per_group, :]
        m1 = jnp.max(blk, axis=0, keepdims=True)
        i1 = jnp.min(jnp.where(blk == m1, bidx, big), axis=0, keepdims=True)
        m2 = jnp.max(jnp.where(bidx == i1, NEG, blk), axis=0, keepdims=True)
        gsc.append(m1 + m2)
    cur = jnp.concatenate(gsc, axis=0)
    gidx = lax.broadcasted_iota(I32, (N_EXPERT_GROUPS, tm), 0)
    gsel = jnp.zeros((N_EXPERT_GROUPS, tm), F32)
    for _ in range(TOPK_GROUPS):
        m = jnp.max(cur, axis=0, keepdims=True)
        i = jnp.min(jnp.where(cur == m, gidx, big), axis=0, keepdims=True)
        hit = gidx == i
        gsel = jnp.where(hit, 1.0, gsel)
        cur = jnp.where(hit, NEG, cur)
    emask = jnp.concatenate(
        [jnp.broadcast_to(gsel[gi:gi + 1, :], (per_group, tm)) for gi in range(N_EXPERT_GROUPS)], axis=0)
    masked = jnp.where(emask > 0.0, biased, NEG)
    idx_rows, gate_rows = [], []
    for _ in range(TOP_K):
        m = jnp.max(masked, axis=0, keepdims=True)
        i = jnp.min(jnp.where(masked == m, eidx, big), axis=0, keepdims=True)
        hit = eidx == i
        idx_rows.append(i)
        gate_rows.append(jnp.sum(jnp.where(hit, scores, 0.0), axis=0, keepdims=True))
        masked = jnp.where(hit, NEG, masked)
    picked = jnp.where(jnp.logical_and(masked == NEG, emask > 0.0), 1.0, 0.0)
    gates = jnp.concatenate(gate_rows, axis=0)
    gates = gates / jnp.sum(gates, axis=0, keepdims=True) * ROUTED_SCALE
    idx_ref[0] = jnp.concatenate(idx_rows, axis=0)
    gate_ref[0] = gates

    @pl.when(pl.program_id(0) == 0)
    def _():
        cnt_ref[...] = jnp.zeros(cnt_ref.shape, F32)

    cnt_ref[...] += _dot(picked.astype(BF16), jnp.ones((tm, LANES), BF16))


def _post(x2, y_ssd, y_hgrn, wo, g1, b1, wr_hi, wr_lo, rbias):
    t = x2.shape[0]
    tm = TM_TOK
    nt = t // tm
    full = lambda shape: pl.BlockSpec(shape, lambda i: (0,) * len(shape))
    return pl.pallas_call(
        _post_kernel,
        grid=(nt,),
        in_specs=[pl.BlockSpec((tm, D_MODEL), lambda i: (i, 0)),
                  pl.BlockSpec((tm, D_SSM), lambda i: (i, 0)),
                  pl.BlockSpec((tm, D_HGRN), lambda i: (i, 0)),
                  full((D_MODEL, D_MODEL)), full((1, D_MODEL)), full((1, D_MODEL)),
                  full((N_EXPERTS, D_MODEL)), full((N_EXPERTS, D_MODEL)), full((N_EXPERTS, 1))],
        out_specs=[pl.BlockSpec((ROW_HALVES, tm, HALF_WORDS), lambda i: (0, i, 0)),
                   pl.BlockSpec((tm, D_MODEL), lambda i: (i, 0)),
                   pl.BlockSpec((1, TOP_K, tm), lambda i: (i, 0, 0)),
                   pl.BlockSpec((1, TOP_K, tm), lambda i: (i, 0, 0)),
                   pl.BlockSpec((N_EXPERTS, LANES), lambda i: (0, 0))],
        out_shape=[jax.ShapeDtypeStruct((ROW_HALVES, t, HALF_WORDS), U32),
                   jax.ShapeDtypeStruct((t, D_MODEL), F32),
                   jax.ShapeDtypeStruct((nt, TOP_K, tm), I32),
                   jax.ShapeDtypeStruct((nt, TOP_K, tm), F32),
                   jax.ShapeDtypeStruct((N_EXPERTS, LANES), F32)],
        compiler_params=pltpu.CompilerParams(dimension_semantics=("arbitrary",),
                                             vmem_limit_bytes=48 * 1024 * 1024),
        name="post",
    )(x2, y_ssd, y_hgrn, wo, g1, b1, wr_hi, wr_lo, rbias)


def _shared_kernel(htt_ref, base_ref, wg_ref, wu_ref, wd_ref, o_ref):
    halves = [_unpack_pairs(htt_ref[hh]) for hh in range(ROW_HALVES)]
    hb = jnp.concatenate([lo for lo, _ in halves] + [hi for _, hi in halves], axis=-1).astype(BF16)
    hid = _silu(_dot(hb, wg_ref[...])) * _dot(hb, wu_ref[...])
    o_ref[...] = base_ref[...] + _dot(hid.astype(BF16), wd_ref[...])


def _shared(htt, base, wgs, wus, wds):
    t = base.shape[0]
    tm = TM_SHARED
    full = lambda shape: pl.BlockSpec(shape, lambda i: (0,) * len(shape))
    return pl.pallas_call(
        _shared_kernel,
        grid=(t // tm,),
        in_specs=[pl.BlockSpec((ROW_HALVES, tm, HALF_WORDS), lambda i: (0, i, 0)),
                  pl.BlockSpec((tm, D_MODEL), lambda i: (i, 0)),
                  full((D_MODEL, D_EXPERT)), full((D_MODEL, D_EXPERT)), full((D_EXPERT, D_MODEL))],
        out_specs=pl.BlockSpec((tm, D_MODEL), lambda i: (i, 0)),
        out_shape=jax.ShapeDtypeStruct((t, D_MODEL), F32),
        input_output_aliases={1: 0},
        compiler_params=pltpu.CompilerParams(dimension_semantics=("arbitrary",)),
        name="shared",
    )(htt, base, wgs, wus, wds)


def _max_blocks(t):
    return (t * TOP_K + N_EXPERTS * (EXPERT_ROWS - 1)) // EXPERT_ROWS


def _pos_kernel(idx_ref, total_ref, dest_ref, bstart_ref, nblk_ref, cnt_ref, start_ref):
    i = pl.program_id(0)
    tm = idx_ref.shape[-1]
    idx = idx_ref[0]
    eidx = lax.broadcasted_iota(I32, (N_EXPERTS, tm), 0)
    sel = [eidx == idx[k:k + 1, :] for k in range(TOP_K)]
    onehot = sel[0]
    for k in range(1, TOP_K):
        onehot = jnp.logical_or(onehot, sel[k])
    mt = jnp.where(onehot, 1.0, 0.0).astype(BF16)

    @pl.when(i == 0)
    def _():
        nb = jnp.floor((total_ref[...] + (EXPERT_ROWS - 1)) * (1.0 / EXPERT_ROWS))
        r = lax.broadcasted_iota(I32, (N_EXPERTS, N_EXPERTS), 0)
        c = lax.broadcasted_iota(I32, (N_EXPERTS, N_EXPERTS), 1)
        end = _dot((r >= c).astype(BF16), nb.astype(BF16))
        start_ref[...] = (end - nb) * EXPERT_ROWS
        cnt_ref[...] = jnp.zeros(cnt_ref.shape, F32)
        bstart_ref[...] = (end - nb).astype(I32)
        nblk_ref[...] = nb.astype(I32)

    r = lax.broadcasted_iota(I32, (tm, tm), 0)
    c = lax.broadcasted_iota(I32, (tm, tm), 1)
    before = _dot(mt, (r < c).astype(BF16))
    slot = start_ref[:, 0:1] + cnt_ref[:, 0:1] + before
    rows = [jnp.sum(jnp.where(sel[k], slot, 0.0), axis=0, keepdims=True) for k in range(TOP_K)]
    dest_ref[0] = jnp.concatenate(rows, axis=0).astype(I32)
    cnt_ref[...] += _dot(mt, jnp.ones((tm, LANES), BF16))


def _positions(idx, totals):
    nt, _, tm = idx.shape
    return pl.pallas_call(
        _pos_kernel,
        grid=(nt,),
        in_specs=[pl.BlockSpec((1, TOP_K, tm), lambda i: (i, 0, 0)),
                  pl.BlockSpec((N_EXPERTS, LANES), lambda i: (0, 0))],
        out_specs=[pl.BlockSpec((1, TOP_K, tm), lambda i: (i, 0, 0)),
                   pl.BlockSpec((N_EXPERTS, LANES), lambda i: (0, 0)),
                   pl.BlockSpec((N_EXPERTS, LANES), lambda i: (0, 0))],
        out_shape=[jax.ShapeDtypeStruct((nt, TOP_K, tm), I32),
                   jax.ShapeDtypeStruct((N_EXPERTS, LANES), I32),
                   jax.ShapeDtypeStruct((N_EXPERTS, LANES), I32)],
        scratch_shapes=[pltpu.VMEM((N_EXPERTS, LANES), F32), pltpu.VMEM((N_EXPERTS, LANES), F32)],
        compiler_params=pltpu.CompilerParams(dimension_semantics=("arbitrary",)),
        name="positions",
    )(idx, totals)


def _sc_mesh():
    return plsc.VectorSubcoreMesh(core_axis_name="core", subcore_axis_name="subcore")


def _sc_dispatch(src, index, n_out):
    n = index.shape[0]
    src_blocks = src.shape[0] // SC_WINDOW
    index = index.reshape(1, n)

    @pl.kernel(out_type=jax.ShapeDtypeStruct((n_out, HALF_WORDS), src.dtype), mesh=_sc_mesh(), name="sc_dispatch")
    def scatter(src_hbm, idx_hbm, out_hbm):
        def body(rows_vmem, idx_vmem):
            pltpu.sync_copy(rows_vmem, out_hbm.at[idx_vmem.at[0]])

        pltpu.emit_pipeline(
            body, grid=(n // SC_WINDOW,),
            in_specs=[pl.BlockSpec((SC_WINDOW, HALF_WORDS), index_map=lambda i: (i % src_blocks, 0)),
                      pl.BlockSpec((1, SC_WINDOW), index_map=lambda i: (0, i))],
            out_specs=[],
            core_axis_name=("core", "subcore"), dimension_semantics=(pltpu.PARALLEL,))(src_hbm, idx_hbm)

    return scatter(src, index)


def _sc_gather(src, index):
    n = index.shape[0]
    index = index.reshape(1, n)

    @pl.kernel(out_type=jax.ShapeDtypeStruct((n, HALF_WORDS), src.dtype), mesh=_sc_mesh(), name="sc_gather")
    def gather(src_hbm, idx_hbm, out_hbm):
        def body(idx_vmem, rows_vmem):
            pltpu.sync_copy(src_hbm.at[idx_vmem.at[0]], rows_vmem)

        pltpu.emit_pipeline(
            body, grid=(n // SC_WINDOW,),
            in_specs=[pl.BlockSpec((1, SC_WINDOW), index_map=lambda i: (0, i))],
            out_specs=[pl.BlockSpec((SC_WINDOW, HALF_WORDS), index_map=lambda i: (i, 0))],
            core_axis_name=("core", "subcore"), dimension_semantics=(pltpu.PARALLEL,))(idx_hbm, out_hbm)

    return gather(src, index)


def _expert_kernel(bstart_ref, nblk_ref, xs_hbm, wg_ref, wu_ref, wd_ref, ys_hbm,
                   xbuf, ybuf, wgb_ref, wub_ref, wdb_ref, xsem, ysem, ypend):
    e = pl.program_id(0)
    last = pl.num_programs(0) - 1
    n = nblk_ref[e]
    b0 = bstart_ref[e]
    total = bstart_ref[last] + nblk_ref[last]
    blk_rows = EXPERT_ROWS * ROW_HALVES
    xdepth = xbuf.shape[0]
    ydepth = ybuf.shape[0]
    lookahead = xdepth - MAX_CHUNK

    def hbm_block(ref, b):
        return ref.at[pl.ds(pl.multiple_of(b * blk_rows, blk_rows), blk_rows)]

    def x_copy(b):
        slot = b % xdepth
        return pltpu.make_async_copy(hbm_block(xs_hbm, b), xbuf.at[slot], xsem.at[slot])

    def y_copy(b, slot):
        return pltpu.make_async_copy(ybuf.at[slot], hbm_block(ys_hbm, b), ysem.at[slot])

    @pl.when(e == 0)
    def _():
        for s in range(ydepth):
            ypend[s] = 0
        for b in range(lookahead):
            @pl.when(b < total)
            def _():
                x_copy(b).start()

    def chunk(j, m):
        b = b0 + j
        for i in range(m):
            @pl.when(b + lookahead + i < total)
            def _():
                x_copy(b + lookahead + i).start()

        for i in range(m):
            x_copy(b + i).wait()

            @pl.when(ypend[(b + i) % ydepth] == 1)
            def _():
                y_copy(b + i, (b + i) % ydepth).wait()

        for i in range(m):
            xslot = (b + i) % xdepth
            yslot = (b + i) % ydepth
            halves = [_unpack_pairs(xbuf[xslot, hh * EXPERT_ROWS:(hh + 1) * EXPERT_ROWS, :])
                      for hh in range(ROW_HALVES)]
            xb = jnp.concatenate([lo for lo, _ in halves] + [hi for _, hi in halves], axis=-1).astype(BF16)
            hid = _silu(_dot(xb, wgb_ref[...])) * _dot(xb, wub_ref[...])
            yb = _dot(hid.astype(BF16), wdb_ref[...])
            packed = _pack_pairs(yb[:, :D_MODEL // 2], yb[:, D_MODEL // 2:])
            for hh in range(ROW_HALVES):
                ybuf[yslot, hh * EXPERT_ROWS:(hh + 1) * EXPERT_ROWS, :] = (
                    packed[:, hh * HALF_WORDS:(hh + 1) * HALF_WORDS])

        for i in range(m):
            y_copy(b + i, (b + i) % ydepth).start()
            ypend[(b + i) % ydepth] = 1

    @pl.when(n > 0)
    def _():
        wgb_ref[...] = wg_ref[0].astype(BF16)
        wub_ref[...] = wu_ref[0].astype(BF16)
        wdb_ref[...] = wd_ref[0].astype(BF16)
        full = n // MAX_CHUNK

        def full_chunk(q, carry):
            chunk(q * MAX_CHUNK, MAX_CHUNK)
            return carry

        lax.fori_loop(0, full, full_chunk, 0)
        for m in range(1, MAX_CHUNK):
            @pl.when(n - full * MAX_CHUNK == m)
            def _():
                chunk(full * MAX_CHUNK, m)

    @pl.when(e == last)
    def _():
        for s in range(ydepth):
            @pl.when(ypend[s] == 1)
            def _():
                y_copy(b0, s).wait()


def _experts(bstart, nblk, xs, w_gate, w_up, w_down):
    blk_shape = (EXPERT_ROWS * ROW_HALVES, HALF_WORDS)
    wspec = lambda shape: pl.BlockSpec((1,) + shape, lambda e, bs, nb: (e, 0, 0))
    grid_spec = pltpu.PrefetchScalarGridSpec(
        num_scalar_prefetch=2,
        grid=(N_EXPERTS,),
        in_specs=[pl.BlockSpec(memory_space=pl.ANY),
                  wspec((D_MODEL, D_EXPERT)), wspec((D_MODEL, D_EXPERT)), wspec((D_EXPERT, D_MODEL))],
        out_specs=pl.BlockSpec(memory_space=pl.ANY),
        scratch_shapes=[pltpu.VMEM((X_RING,) + blk_shape, U32), pltpu.VMEM((Y_RING,) + blk_shape, U32),
                        pltpu.VMEM((D_MODEL, D_EXPERT), BF16), pltpu.VMEM((D_MODEL, D_EXPERT), BF16),
                        pltpu.VMEM((D_EXPERT, D_MODEL), BF16),
                        pltpu.SemaphoreType.DMA((X_RING,)), pltpu.SemaphoreType.DMA((Y_RING,)),
                        pltpu.SMEM((Y_RING,), I32)],
    )
    return pl.pallas_call(
        _expert_kernel,
        grid_spec=grid_spec,
        out_shape=jax.ShapeDtypeStruct(xs.shape, U32),
        compiler_params=pltpu.CompilerParams(dimension_semantics=("arbitrary",), has_side_effects=True),
        name="experts",
    )(bstart, nblk, xs, w_gate, w_up, w_down)


def _combine_kernel(gate_ref, base_ref, g2_ref, b2_ref, rows_ref, o_ref):
    tm = base_ref.shape[0]
    gates_t = jnp.concatenate([gate_ref[0], jnp.zeros((LANES - TOP_K, tm), F32)], axis=0).T
    half = D_MODEL // 2
    lo_acc = [base_ref[:, hh * HALF_WORDS:(hh + 1) * HALF_WORDS] for hh in range(ROW_HALVES)]
    hi_acc = [base_ref[:, half + hh * HALF_WORDS:half + (hh + 1) * HALF_WORDS] for hh in range(ROW_HALVES)]
    for k in range(TOP_K):
        gk = gates_t[:, k:k + 1]
        for hh in range(ROW_HALVES):
            lo, hi = _unpack_pairs(rows_ref[k, hh])
            lo_acc[hh] = lo_acc[hh] + gk * lo
            hi_acc[hh] = hi_acc[hh] + gk * hi
    o_ref[...] = _layer_norm(jnp.concatenate(lo_acc + hi_acc, axis=-1), g2_ref[...], b2_ref[...])


def _combine_piece_kernel(gate_ref, base_ref, g2_ref, b2_ref, rows_ref, prev_ref, o_ref):
    del prev_ref
    _combine_kernel(gate_ref, base_ref, g2_ref, b2_ref, rows_ref, o_ref)


def _combine(gates, base, g2, b2, rows, first_tile, prev):
    t = base.shape[0]
    tm = gates.shape[-1]
    tiles = rows.shape[2] // tm
    in_specs = [pl.BlockSpec((1, TOP_K, tm), lambda i: (i + first_tile, 0, 0)),
                pl.BlockSpec((tm, D_MODEL), lambda i: (i + first_tile, 0)),
                pl.BlockSpec((1, D_MODEL), lambda i: (0, 0)),
                pl.BlockSpec((1, D_MODEL), lambda i: (0, 0)),
                pl.BlockSpec((TOP_K, ROW_HALVES, tm, HALF_WORDS), lambda i: (0, 0, i, 0))]
    args = [gates, base, g2, b2, rows]
    if prev is not None:
        in_specs.append(pl.BlockSpec(memory_space=pl.ANY))
        args.append(prev)
    return pl.pallas_call(
        _combine_kernel if prev is None else _combine_piece_kernel,
        grid=(tiles,),
        in_specs=in_specs,
        out_specs=pl.BlockSpec((tm, D_MODEL), lambda i: (i + first_tile, 0)),
        out_shape=jax.ShapeDtypeStruct((t, D_MODEL), F32),
        input_output_aliases={} if prev is None else {len(args) - 1: 0},
        compiler_params=pltpu.CompilerParams(dimension_semantics=("arbitrary",),
                                             vmem_limit_bytes=48 * 1024 * 1024),
        name="combine",
    )(*args)


def _layer(h2, w_in, conv_w, conv_b, dt_bias, a_log, d_skip, ssd_norm_w, lower_bound, hgrn_norm_w, w_out,
           ln1_g, ln1_b, w_router, router_bias, w_gate_e, w_up_e, w_down_e, w_gate_s, w_up_s, w_down_s,
           ln2_g, ln2_b):
    t = h2.shape[0]
    dt0 = D_SSM + D_CONV
    q0 = dt0 + SSD_HEADS
    w_perm = jnp.concatenate(
        [w_in[:, :dt0], w_in[:, q0:], w_in[:, dt0:q0], jnp.zeros((D_MODEL, DT_PAD - SSD_HEADS), w_in.dtype)],
        axis=1).astype(BF16)
    z, xbc, q, f, i, g, dt = _inproj(h2, w_perm)
    y_ssd = _ssd(z, xbc, dt, conv_w, conv_b, dt_bias, a_log, d_skip, ssd_norm_w)
    y_hgrn = _hgrn(q, f, i, g, lower_bound, hgrn_norm_w)

    wr_t = w_router.astype(F32).T
    wr_hi = wr_t.astype(BF16)
    wr_lo = (wr_t - wr_hi.astype(F32)).astype(BF16)
    row = lambda v: v.reshape(1, -1).astype(F32)
    htt, base, idx, gates, totals = _post(
        h2, y_ssd, y_hgrn, w_out.astype(BF16), row(ln1_g), row(ln1_b), wr_hi, wr_lo,
        router_bias.reshape(N_EXPERTS, 1).astype(F32))

    dest, bstart, nblk = _positions(idx, totals)
    n_rows = _max_blocks(t) * EXPERT_ROWS
    d = jnp.transpose(dest, (1, 0, 2)).reshape(TOP_K, 1, t)
    slot = (d // EXPERT_ROWS) * (EXPERT_ROWS * ROW_HALVES) + d % EXPERT_ROWS
    slot = (slot + jnp.arange(ROW_HALVES, dtype=I32).reshape(1, ROW_HALVES, 1) * EXPERT_ROWS).reshape(-1)
    xs = _sc_dispatch(htt.reshape(ROW_HALVES * t, HALF_WORDS), slot, n_rows * ROW_HALVES)
    base = _shared(htt, base, w_gate_s.astype(BF16), w_up_s.astype(BF16), w_down_s.astype(BF16))
    ys = _experts(bstart[:, 0], nblk[:, 0], xs, w_gate_e, w_up_e, w_down_e)
    slot = slot.reshape(TOP_K, ROW_HALVES, t)
    tp = t // GATHER_PIECES
    out = None
    for p in range(GATHER_PIECES):
        rows = _sc_gather(ys, slot[:, :, p * tp:(p + 1) * tp].reshape(-1))
        out = _combine(gates, base, row(ln2_g), row(ln2_b), rows.reshape(TOP_K, ROW_HALVES, tp, HALF_WORDS),
                       p * (tp // TM_TOK), out)
    return out


def kernel(x, w_in, conv_w, conv_b, dt_bias, a_log, d_skip, ssd_norm_w, hgrn_lb_logits, hgrn_norm_w, w_out,
           ln1_g, ln1_b, w_router, router_bias, w_gate_e, w_up_e, w_down_e, w_gate_s, w_up_s, w_down_s,
           ln2_g, ln2_b):
    bsz, t, d = x.shape
    assert bsz == 1 and d == D_MODEL, "the recurrent mixers carry state across the flattened token axis"
    depth = w_in.shape[0]
    lower_bounds = jnp.cumsum(jax.nn.softmax(hgrn_lb_logits.astype(F32), axis=0), axis=0)
    h = x.reshape(bsz * t, d)
    for l in range(depth):
        h = _layer(h, w_in[l], conv_w[l], conv_b[l], dt_bias[l], a_log[l], d_skip[l], ssd_norm_w[l],
                   lower_bounds[l], hgrn_norm_w[l], w_out[l], ln1_g[l], ln1_b[l], w_router[l],
                   router_bias[l], w_gate_e[l], w_up_e[l], w_down_e[l], w_gate_s[l], w_up_s[l],
                   w_down_s[l], ln2_g[l], ln2_b[l])
    return h.reshape(bsz, t, d)
```

```python
import jax
import jax.numpy as jnp
from jax import lax
from jax.experimental import pallas as pl
from jax.experimental.pallas import tpu as pltpu
from jax.experimental.pallas import tpu_sc as plsc

F32 = jnp.float32
BF16 = jnp.bfloat16
I32 = jnp.int32
U32 = jnp.uint32

D_MODEL = 1024
D_SSM = 512
D_HGRN = 512
SSD_HEADS = 8
SSD_HEAD_DIM = 64
SSD_GROUPS = 2
SSD_STATE = 128
SSD_CONV = 4
SSD_CHUNK = 128
SSD_PER_STEP = 4
D_CONV = D_SSM + 2 * SSD_GROUPS * SSD_STATE
HGRN_HEADS = 4
HGRN_DK = 128
HGRN_CHUNK = 64
HGRN_SUB = 8
HGRN_PER_STEP = 4
N_EXPERTS = 256
TOP_K = 8
N_EXPERT_GROUPS = 8
TOPK_GROUPS = 4
D_EXPERT = 256
ROUTED_SCALE = 2.5
ALPHA = 2.0 ** 0.25
LN_EPS = 1e-5
RMS_EPS = 1e-6

LANES = 128
SUBLANES = 8
ROW_TILES = D_MODEL // LANES
ROW_HALVES = 2
HALF_WORDS = D_MODEL // 2 // ROW_HALVES
SC_WINDOW = 128
GATHER_PIECES = 4
DT_PAD = LANES
N_IN_PAD = D_SSM + D_CONV + 4 * D_HGRN + DT_PAD

TM_PROJ = 256
TM_TOK = 256
TM_SHARED = 512
EXPERT_ROWS = 256
MAX_CHUNK = 2
X_RING = 8
Y_RING = 8
NEG = -1e30


def _sigmoid(x):
    return 1.0 / (1.0 + jnp.exp(-x))


def _silu(x):
    return x * _sigmoid(x)


def _split3(x):
    hi = x.astype(BF16)
    r = x - hi.astype(F32)
    mid = r.astype(BF16)
    lo = (r - mid.astype(F32)).astype(BF16)
    return hi, mid, lo


def _dot(a, b):
    return jnp.dot(a, b, preferred_element_type=F32)


def _dot_nt(a, b):
    return lax.dot_general(a, b, (((1,), (1,)), ((), ())), preferred_element_type=F32)


def _dot_tn(a, b):
    return lax.dot_general(a, b, (((0,), (0,)), ((), ())), preferred_element_type=F32)


def _sel_dot(sel, x):
    hi, mid, lo = _split3(x)
    return _dot(sel, hi) + _dot(sel, mid) + _dot(sel, lo)


def _dot_sel(x, sel):
    hi, mid, lo = _split3(x)
    return _dot(hi, sel) + _dot(mid, sel) + _dot(lo, sel)


def _pack_pairs(lo, hi):
    lo_bits = pltpu.bitcast(lo.astype(BF16).astype(F32), U32) >> 16
    hi_bits = pltpu.bitcast(hi.astype(BF16).astype(F32), U32) & jnp.uint32(0xFFFF0000)
    return hi_bits | lo_bits


def _unpack_pairs(p):
    return pltpu.bitcast(p << 16, F32), pltpu.bitcast(p & jnp.uint32(0xFFFF0000), F32)


def _layer_norm(x, g, b):
    mu = jnp.mean(x, axis=-1, keepdims=True)
    xc = x - mu
    var = jnp.mean(xc * xc, axis=-1, keepdims=True)
    return xc * lax.rsqrt(var + LN_EPS) * g + b


def _inproj_kernel(x_ref, w_ref, z_ref, xbc_ref, q_ref, f_ref, i_ref, g_ref, dt_ref):
    xb = x_ref[...].astype(BF16)
    col = 0
    for ref in (z_ref, xbc_ref, q_ref, f_ref, i_ref, g_ref, dt_ref):
        n = ref.shape[-1]
        ref[...] = _dot(xb, w_ref[:, col:col + n])
        col += n


def _inproj(x2, w_perm):
    t = x2.shape[0]
    widths = (D_SSM, D_CONV, D_HGRN, D_HGRN, D_HGRN, D_HGRN, DT_PAD)
    return pl.pallas_call(
        _inproj_kernel,
        grid=(t // TM_PROJ,),
        in_specs=[pl.BlockSpec((TM_PROJ, D_MODEL), lambda i: (i, 0)),
                  pl.BlockSpec((D_MODEL, N_IN_PAD), lambda i: (0, 0))],
        out_specs=[pl.BlockSpec((TM_PROJ, n), lambda i: (i, 0)) for n in widths],
        out_shape=[jax.ShapeDtypeStruct((t, n), F32) for n in widths],
        compiler_params=pltpu.CompilerParams(dimension_semantics=("arbitrary",),
                                             vmem_limit_bytes=48 * 1024 * 1024),
        name="inproj",
    )(x2, w_perm)


def _ssd_kernel(z_ref, xbc_ref, dt_ref, cw_ref, cb_ref, dtb_ref, alog_ref, dskip_ref, nw_ref,
                y_ref, ext_ref, st_ref):
    n = xbc_ref.shape[0]
    halo = SUBLANES

    @pl.when(pl.program_id(0) == 0)
    def _():
        ext_ref[0:halo, :] = jnp.zeros((halo, D_CONV), F32)
        st_ref[...] = jnp.zeros(st_ref.shape, F32)

    ext_ref[halo:halo + n, :] = xbc_ref[...]
    acc = jnp.broadcast_to(cb_ref[...], (n, D_CONV))
    for k in range(SSD_CONV):
        off = halo - (SSD_CONV - 1) + k
        acc = acc + cw_ref[k:k + 1, :] * ext_ref[off:off + n, :]
    ext_ref[0:halo, :] = xbc_ref[n - halo:n, :]
    u = _silu(acc)
    for c in range(n // SSD_CHUNK):
        rows = slice(c * SSD_CHUNK, (c + 1) * SSD_CHUNK)
        y_ref[rows, :] = _ssd_chunk(u[rows, :], z_ref[rows, :], dt_ref[rows, :] + dtb_ref[...],
                                    -jnp.exp(alog_ref[...]), dskip_ref[...], nw_ref[...], st_ref)


def _ssd_chunk(u, z, draw, a, dskip, nw, st_ref):
    L = SSD_CHUNK
    xs = u[:, :D_SSM]
    bm = u[:, D_SSM:D_SSM + SSD_GROUPS * SSD_STATE]
    cm = u[:, D_SSM + SSD_GROUPS * SSD_STATE:]

    dt = jnp.maximum(draw, 0.0) + jnp.log(1.0 + jnp.exp(-jnp.abs(draw)))
    ad = dt * a
    rows = lax.broadcasted_iota(I32, (L, L), 0)
    cols = lax.broadcasted_iota(I32, (L, L), 1)
    causal = rows >= cols
    a_cum = _sel_dot(causal.astype(BF16), ad)
    a_cum_t = a_cum.T

    hrow = lax.broadcasted_iota(I32, (LANES, D_SSM), 0)
    hcol = lax.broadcasted_iota(I32, (LANES, D_SSM), 1) // SSD_HEAD_DIM
    expand = (hrow == hcol).astype(BF16)
    dt_x = _dot_sel(dt, expand)
    acx = _dot_sel(a_cum, expand)
    last = acx[L - 1:L, :]
    ea_x = jnp.exp(acx)
    dec_x = jnp.exp(last - acx)
    elast_x = jnp.exp(last)

    xdt = xs * dt_x
    gw = SSD_HEADS // SSD_GROUPS * SSD_HEAD_DIM
    lane_head = lax.broadcasted_iota(I32, (L, gw), 1) // SSD_HEAD_DIM
    ys = []
    for g in range(SSD_GROUPS):
        bg = bm[:, g * SSD_STATE:(g + 1) * SSD_STATE]
        cg = cm[:, g * SSD_STATE:(g + 1) * SSD_STATE].astype(BF16)
        bg_t = bg.T.astype(BF16)
        gmat = _dot(cg, bg_t)
        xdt_g = xdt[:, g * gw:(g + 1) * gw]
        xdt_gb = xdt_g.astype(BF16)
        r_prev = st_ref[g]
        y_g = _dot(cg, r_prev.astype(BF16)) * ea_x[:, g * gw:(g + 1) * gw]
        new_s = _dot(bg_t, (xdt_g * dec_x[:, g * gw:(g + 1) * gw]).astype(BF16))
        st_ref[g] = r_prev * elast_x[:, g * gw:(g + 1) * gw] + new_s
        for j in range(SSD_HEADS // SSD_GROUPS):
            h = g * (SSD_HEADS // SSD_GROUPS) + j
            diff = a_cum[:, h:h + 1] - a_cum_t[h:h + 1, :]
            decay = jnp.exp(jnp.where(causal, diff, NEG))
            yd = _dot((gmat * decay).astype(BF16), xdt_gb)
            y_g = y_g + jnp.where(lane_head == j, yd, 0.0)
        ys.append(y_g)
    y = jnp.concatenate(ys, axis=-1) + xs * dskip
    y = y * _silu(z)
    outs = []
    ng = D_SSM // SSD_GROUPS
    for g in range(SSD_GROUPS):
        yg = y[:, g * ng:(g + 1) * ng]
        ms = jnp.mean(yg * yg, axis=-1, keepdims=True)
        outs.append(yg * lax.rsqrt(ms + RMS_EPS))
    return jnp.concatenate(outs, axis=-1) * nw


def _ssd(z, xbc, dt, conv_w, conv_b, dt_bias, a_log, d_skip, norm_w):
    t = z.shape[0]
    L = SSD_CHUNK * SSD_PER_STEP
    pad = lambda v: jnp.pad(v.astype(F32), (0, LANES - v.shape[0])).reshape(1, LANES)
    full = lambda shape: pl.BlockSpec(shape, lambda c: (0,) * len(shape))
    return pl.pallas_call(
        _ssd_kernel,
        grid=(t // L,),
        in_specs=[pl.BlockSpec((L, D_SSM), lambda c: (c, 0)),
                  pl.BlockSpec((L, D_CONV), lambda c: (c, 0)),
                  pl.BlockSpec((L, DT_PAD), lambda c: (c, 0)),
                  full((SSD_CONV, D_CONV)), full((1, D_CONV)), full((1, LANES)), full((1, LANES)),
                  full((1, D_SSM)), full((1, D_SSM))],
        out_specs=pl.BlockSpec((L, D_SSM), lambda c: (c, 0)),
        out_shape=jax.ShapeDtypeStruct((t, D_SSM), F32),
        scratch_shapes=[pltpu.VMEM((L + SUBLANES, D_CONV), F32),
                        pltpu.VMEM((SSD_GROUPS, SSD_STATE, D_SSM // SSD_GROUPS), F32)],
        compiler_params=pltpu.CompilerParams(dimension_semantics=("arbitrary",)),
        name="ssd",
    )(z, xbc, dt, conv_w.astype(F32), conv_b.reshape(1, D_CONV).astype(F32), pad(dt_bias), pad(a_log),
      jnp.repeat(d_skip.astype(F32), SSD_HEAD_DIM).reshape(1, D_SSM), norm_w.reshape(1, D_SSM).astype(F32))


def _tile_bcast(x, r):
    n, d = x.shape
    x3 = x.reshape(n // SUBLANES, SUBLANES, d)
    return jnp.broadcast_to(x3[:, r:r + 1, :], x3.shape).reshape(n, d)


def _hgrn_kernel(q_ref, f_ref, i_ref, g_ref, lb_ref, nw_ref, o_ref, st_ref):
    @pl.when(pl.program_id(0) == 0)
    def _():
        st_ref[...] = jnp.zeros(st_ref.shape, F32)

    for c in range(o_ref.shape[0] // HGRN_CHUNK):
        rows = slice(c * HGRN_CHUNK, (c + 1) * HGRN_CHUNK)
        o_ref[rows, :] = _hgrn_chunk(q_ref[rows, :], f_ref[rows, :], i_ref[rows, :], g_ref[rows, :],
                                     lb_ref[...], nw_ref[...], st_ref)


def _hgrn_chunk(q, f, v, g, lb, nw, st_ref):
    C = HGRN_CHUNK
    S = HGRN_SUB
    nsub = C // S
    fg = lb + (1.0 - lb) * _sigmoid(f)
    qs = _silu(q)
    rows = lax.broadcasted_iota(I32, (C, C), 0)
    cols = lax.broadcasted_iota(I32, (C, C), 1)
    cum = _sel_dot((rows >= cols).astype(BF16), jnp.log2(fg))
    ck_all = cum - jnp.log2(1.0 - fg)

    srow = lax.broadcasted_iota(I32, (S * HGRN_DK, C), 0) // HGRN_DK
    scol = lax.broadcasted_iota(I32, (S * HGRN_DK, C), 1) % S
    spread = (srow == scol).astype(BF16)
    diag_keep = jnp.logical_and(rows // S == cols // S, rows >= cols)
    outs = []
    for h in range(HGRN_HEADS):
        blk = slice(h * HGRN_DK, (h + 1) * HGRN_DK)
        qh, ch, ck = qs[:, blk], cum[:, blk], ck_all[:, blk]
        last = ch[C - 1:C, :]
        x_cat = jnp.concatenate(
            [(qh * jnp.exp2(jnp.minimum(ch - _tile_bcast(ck, r), 0.0))).astype(BF16) for r in range(S)], axis=-1)
        kbe = jnp.exp2(_tile_bcast(ch, S - 1) - ck).astype(BF16)
        a_pieces, b_pieces = [], []
        for j in range(nsub - 1):
            lo = (j + 1) * S
            a = (qh[lo:, :] * jnp.exp2(ch[lo:, :] - ch[lo - 1:lo, :])).astype(BF16)
            a_pieces.append(jnp.concatenate([jnp.zeros((lo, HGRN_DK), BF16), a], axis=0))
            b = [kbe[j * S:lo, :], jnp.zeros((C - lo, HGRN_DK), BF16)]
            if j > 0:
                b.insert(0, jnp.zeros((j * S, HGRN_DK), BF16))
            b_pieces.append(jnp.concatenate(b, axis=0))
        att = (_dot_nt(jnp.concatenate(a_pieces, axis=-1), jnp.concatenate(b_pieces, axis=-1))
               + jnp.where(diag_keep, _dot(x_cat, spread), 0.0))
        vb = v[:, blk].astype(BF16)
        s_prev = st_ref[h]
        o = _dot_nt((qh * jnp.exp2(ch)).astype(BF16), s_prev.astype(BF16)) + _dot(att.astype(BF16), vb)
        st_ref[h] = s_prev * jnp.exp2(last) + _dot_tn(vb, jnp.exp2(last - ck).astype(BF16))
        ms = jnp.mean(o * o, axis=-1, keepdims=True)
        outs.append(o * lax.rsqrt(ms + RMS_EPS))
    return jnp.concatenate(outs, axis=-1) * nw * _silu(g)


def _hgrn(q, f, i, g, lower_bound, norm_w):
    t = q.shape[0]
    C = HGRN_CHUNK * HGRN_PER_STEP
    tok = pl.BlockSpec((C, D_HGRN), lambda c: (c, 0))
    vec = pl.BlockSpec((1, D_HGRN), lambda c: (0, 0))
    return pl.pallas_call(
        _hgrn_kernel,
        grid=(t // C,),
        in_specs=[tok, tok, tok, tok, vec, vec],
        out_specs=tok,
        out_shape=jax.ShapeDtypeStruct((t, D_HGRN), F32),
        scratch_shapes=[pltpu.VMEM((HGRN_HEADS, D_HGRN // HGRN_HEADS, HGRN_DK), F32)],
        compiler_params=pltpu.CompilerParams(dimension_semantics=("arbitrary",)),
        name="hgrn",
    )(q, f, i, g, lower_bound.reshape(1, D_HGRN).astype(F32), norm_w.reshape(1, D_HGRN).astype(F32))


def _post_kernel(x_ref, ys_ref, yh_ref, wo_ref, g1_ref, b1_ref, wrh_ref, wrl_ref, rb_ref,
                 htt_ref, base_ref, idx_ref, gate_ref, cnt_ref):
    tm = x_ref.shape[0]
    mix = (_dot(ys_ref[...].astype(BF16), wo_ref[0:D_SSM, :])
           + _dot(yh_ref[...].astype(BF16), wo_ref[D_SSM:, :]))
    h1 = _layer_norm(ALPHA * x_ref[...] + mix, g1_ref[...], b1_ref[...])
    packed = _pack_pairs(h1[:, :D_MODEL // 2], h1[:, D_MODEL // 2:])
    for hh in range(ROW_HALVES):
        htt_ref[hh] = packed[:, hh * HALF_WORDS:(hh + 1) * HALF_WORDS]
    hb = h1.astype(BF16)
    base_ref[...] = ALPHA * h1

    hlo = (h1 - hb.astype(F32)).astype(BF16)
    logits = _dot_nt(wrh_ref[...], hb) + _dot_nt(wrh_ref[...], hlo) + _dot_nt(wrl_ref[...], hb)
    scores = _sigmoid(logits)
    biased = scores + rb_ref[...]
    per_group = N_EXPERTS // N_EXPERT_GROUPS
    eidx = lax.broadcasted_iota(I32, (N_EXPERTS, tm), 0)
    big = jnp.int32(1 << 20)
    gsc = []
    bidx = lax.broadcasted_iota(I32, (per_group, tm), 0)
    for gi in range(N_EXPERT_GROUPS):
        blk = biased[gi * per_group:(gi + 1) * per_group, :]
        m1 = jnp.max(blk, axis=0, keepdims=True)
        i1 = jnp.min(jnp.where(blk == m1, bidx, big), axis=0, keepdims=True)
        m2 = jnp.max(jnp.where(bidx == i1, NEG, blk), axis=0, keepdims=True)
        gsc.append(m1 + m2)
    cur = jnp.concatenate(gsc, axis=0)
    gidx = lax.broadcasted_iota(I32, (N_EXPERT_GROUPS, tm), 0)
    gsel = jnp.zeros((N_EXPERT_GROUPS, tm), F32)
    for _ in range(TOPK_GROUPS):
        m = jnp.max(cur, axis=0, keepdims=True)
        i = jnp.min(jnp.where(cur == m, gidx, big), axis=0, keepdims=True)
        hit = gidx == i
        gsel = jnp.where(hit, 1.0, gsel)
        cur = jnp.where(hit, NEG, cur)
    emask = jnp.concatenate(
        [jnp.broadcast_to(gsel[gi:gi + 1, :], (per_group, tm)) for gi in range(N_EXPERT_GROUPS)], axis=0)
    masked = jnp.where(emask > 0.0, biased, NEG)
    idx_rows, gate_rows = [], []
    for _ in range(TOP_K):
        m = jnp.max(masked, axis=0, keepdims=True)
        i = jnp.min(jnp.where(masked == m, eidx, big), axis=0, keepdims=True)
        hit = eidx == i
        idx_rows.append(i)
        gate_rows.append(jnp.sum(jnp.where(hit, scores, 0.0), axis=0, keepdims=True))
        masked = jnp.where(hit, NEG, masked)
    picked = jnp.where(jnp.logical_and(masked == NEG, emask > 0.0), 1.0, 0.0)
    gates = jnp.concatenate(gate_rows, axis=0)
    gates = gates / jnp.sum(gates, axis=0, keepdims=True) * ROUTED_SCALE
    idx_ref[0] = jnp.concatenate(idx_rows, axis=0)
    gate_ref[0] = gates

    @pl.when(pl.program_id(0) == 0)
    def _():
        cnt_ref[...] = jnp.zeros(cnt_ref.shape, F32)

    cnt_ref[...] += _dot(picked.astype(BF16), jnp.ones((tm, LANES), BF16))


def _post(x2, y_ssd, y_hgrn, wo, g1, b1, wr_hi, wr_lo, rbias):
    t = x2.shape[0]
    tm = TM_TOK
    nt = t // tm
    full = lambda shape: pl.BlockSpec(shape, lambda i: (0,) * len(shape))
    return pl.pallas_call(
        _post_kernel,
        grid=(nt,),
        in_specs=[pl.BlockSpec((tm, D_MODEL), lambda i: (i, 0)),
                  pl.BlockSpec((tm, D_SSM), lambda i: (i, 0)),
                  pl.BlockSpec((tm, D_HGRN), lambda i: (i, 0)),
                  full((D_MODEL, D_MODEL)), full((1, D_MODEL)), full((1, D_MODEL)),
                  full((N_EXPERTS, D_MODEL)), full((N_EXPERTS, D_MODEL)), full((N_EXPERTS, 1))],
        out_specs=[pl.BlockSpec((ROW_HALVES, tm, HALF_WORDS), lambda i: (0, i, 0)),
                   pl.BlockSpec((tm, D_MODEL), lambda i: (i, 0)),
                   pl.BlockSpec((1, TOP_K, tm), lambda i: (i, 0, 0)),
                   pl.BlockSpec((1, TOP_K, tm), lambda i: (i, 0, 0)),
                   pl.BlockSpec((N_EXPERTS, LANES), lambda i: (0, 0))],
        out_shape=[jax.ShapeDtypeStruct((ROW_HALVES, t, HALF_WORDS), U32),
                   jax.ShapeDtypeStruct((t, D_MODEL), F32),
                   jax.ShapeDtypeStruct((nt, TOP_K, tm), I32),
                   jax.ShapeDtypeStruct((nt, TOP_K, tm), F32),
                   jax.ShapeDtypeStruct((N_EXPERTS, LANES), F32)],
        compiler_params=pltpu.CompilerParams(dimension_semantics=("arbitrary",),
                                             vmem_limit_bytes=48 * 1024 * 1024),
        name="post",
    )(x2, y_ssd, y_hgrn, wo, g1, b1, wr_hi, wr_lo, rbias)


def _shared_kernel(htt_ref, base_ref, wg_ref, wu_ref, wd_ref, o_ref):
    halves = [_unpack_pairs(htt_ref[hh]) for hh in range(ROW_HALVES)]
    hb = jnp.concatenate([lo for lo, _ in halves] + [hi for _, hi in halves], axis=-1).astype(BF16)
    hid = _silu(_dot(hb, wg_ref[...])) * _dot(hb, wu_ref[...])
    o_ref[...] = base_ref[...] + _dot(hid.astype(BF16), wd_ref[...])


def _shared(htt, base, wgs, wus, wds):
    t = base.shape[0]
    tm = TM_SHARED
    full = lambda shape: pl.BlockSpec(shape, lambda i: (0,) * len(shape))
    return pl.pallas_call(
        _shared_kernel,
        grid=(t // tm,),
        in_specs=[pl.BlockSpec((ROW_HALVES, tm, HALF_WORDS), lambda i: (0, i, 0)),
                  pl.BlockSpec((tm, D_MODEL), lambda i: (i, 0)),
                  full((D_MODEL, D_EXPERT)), full((D_MODEL, D_EXPERT)), full((D_EXPERT, D_MODEL))],
        out_specs=pl.BlockSpec((tm, D_MODEL), lambda i: (i, 0)),
        out_shape=jax.ShapeDtypeStruct((t, D_MODEL), F32),
        input_output_aliases={1: 0},
        compiler_params=pltpu.CompilerParams(dimension_semantics=("arbitrary",)),
        name="shared",
    )(htt, base, wgs, wus, wds)


def _max_blocks(t):
    return (t * TOP_K + N_EXPERTS * (EXPERT_ROWS - 1)) // EXPERT_ROWS


def _pos_kernel(idx_ref, total_ref, dest_ref, bstart_ref, nblk_ref, cnt_ref, start_ref):
    i = pl.program_id(0)
    tm = idx_ref.shape[-1]
    idx = idx_ref[0]
    eidx = lax.broadcasted_iota(I32, (N_EXPERTS, tm), 0)
    sel = [eidx == idx[k:k + 1, :] for k in range(TOP_K)]
    onehot = sel[0]
    for k in range(1, TOP_K):
        onehot = jnp.logical_or(onehot, sel[k])
    mt = jnp.where(onehot, 1.0, 0.0).astype(BF16)

    @pl.when(i == 0)
    def _():
        nb = jnp.floor((total_ref[...] + (EXPERT_ROWS - 1)) * (1.0 / EXPERT_ROWS))
        r = lax.broadcasted_iota(I32, (N_EXPERTS, N_EXPERTS), 0)
        c = lax.broadcasted_iota(I32, (N_EXPERTS, N_EXPERTS), 1)
        end = _dot((r >= c).astype(BF16), nb.astype(BF16))
        start_ref[...] = (end - nb) * EXPERT_ROWS
        cnt_ref[...] = jnp.zeros(cnt_ref.shape, F32)
        bstart_ref[...] = (end - nb).astype(I32)
        nblk_ref[...] = nb.astype(I32)

    r = lax.broadcasted_iota(I32, (tm, tm), 0)
    c = lax.broadcasted_iota(I32, (tm, tm), 1)
    before = _dot(mt, (r < c).astype(BF16))
    slot = start_ref[:, 0:1] + cnt_ref[:, 0:1] + before
    rows = [jnp.sum(jnp.where(sel[k], slot, 0.0), axis=0, keepdims=True) for k in range(TOP_K)]
    dest_ref[0] = jnp.concatenate(rows, axis=0).astype(I32)
    cnt_ref[...] += _dot(mt, jnp.ones((tm, LANES), BF16))


def _positions(idx, totals):
    nt, _, tm = idx.shape
    return pl.pallas_call(
        _pos_kernel,
        grid=(nt,),
        in_specs=[pl.BlockSpec((1, TOP_K, tm), lambda i: (i, 0, 0)),
                  pl.BlockSpec((N_EXPERTS, LANES), lambda i: (0, 0))],
        out_specs=[pl.BlockSpec((1, TOP_K, tm), lambda i: (i, 0, 0)),
                   pl.BlockSpec((N_EXPERTS, LANES), lambda i: (0, 0)),
                   pl.BlockSpec((N_EXPERTS, LANES), lambda i: (0, 0))],
        out_shape=[jax.ShapeDtypeStruct((nt, TOP_K, tm), I32),
                   jax.ShapeDtypeStruct((N_EXPERTS, LANES), I32),
                   jax.ShapeDtypeStruct((N_EXPERTS, LANES), I32)],
        scratch_shapes=[pltpu.VMEM((N_EXPERTS, LANES), F32), pltpu.VMEM((N_EXPERTS, LANES), F32)],
        compiler_params=pltpu.CompilerParams(dimension_semantics=("arbitrary",)),
        name="positions",
    )(idx, totals)


def _sc_mesh():
    return plsc.VectorSubcoreMesh(core_axis_name="core", subcore_axis_name="subcore")


def _sc_dispatch(src, index, n_out):
    n = index.shape[0]
    src_blocks = src.shape[0] // SC_WINDOW
    index = index.reshape(1, n)

    @pl.kernel(out_type=jax.ShapeDtypeStruct((n_out, HALF_WORDS), src.dtype), mesh=_sc_mesh(), name="sc_dispatch")
    def scatter(src_hbm, idx_hbm, out_hbm):
        def body(rows_vmem, idx_vmem):
            pltpu.sync_copy(rows_vmem, out_hbm.at[idx_vmem.at[0]])

        pltpu.emit_pipeline(
            body, grid=(n // SC_WINDOW,),
            in_specs=[pl.BlockSpec((SC_WINDOW, HALF_WORDS), index_map=lambda i: (i % src_blocks, 0)),
                      pl.BlockSpec((1, SC_WINDOW), index_map=lambda i: (0, i))],
            out_specs=[],
            core_axis_name=("core", "subcore"), dimension_semantics=(pltpu.PARALLEL,))(src_hbm, idx_hbm)

    return scatter(src, index)


def _sc_gather(src, index):
    n = index.shape[0]
    index = index.reshape(1, n)

    @pl.kernel(out_type=jax.ShapeDtypeStruct((n, HALF_WORDS), src.dtype), mesh=_sc_mesh(), name="sc_gather")
    def gather(src_hbm, idx_hbm, out_hbm):
        def body(idx_vmem, rows_vmem):
            pltpu.sync_copy(src_hbm.at[idx_vmem.at[0]], rows_vmem)

        pltpu.emit_pipeline(
            body, grid=(n // SC_WINDOW,),
            in_specs=[pl.BlockSpec((1, SC_WINDOW), index_map=lambda i: (0, i))],
            out_specs=[pl.BlockSpec((SC_WINDOW, HALF_WORDS), index_map=lambda i: (i, 0))],
            core_axis_name=("core", "subcore"), dimension_semantics=(pltpu.PARALLEL,))(idx_hbm, out_hbm)

    return gather(src, index)


def _expert_kernel(bstart_ref, nblk_ref, xs_hbm, wg_ref, wu_ref, wd_ref, after_hbm, ys_hbm,
                   xbuf, ybuf, wgb_ref, wub_ref, wdb_ref, xsem, ysem, ypend):
    e = pl.program_id(0)
    last = pl.num_programs(0) - 1
    n = nblk_ref[e]
    b0 = bstart_ref[e]
    total = bstart_ref[last] + nblk_ref[last]
    blk_rows = EXPERT_ROWS * ROW_HALVES
    xdepth = xbuf.shape[0]
    ydepth = ybuf.shape[0]
    lookahead = xdepth - MAX_CHUNK

    def hbm_block(ref, b):
        return ref.at[pl.ds(pl.multiple_of(b * blk_rows, blk_rows), blk_rows)]

    def x_copy(b):
        slot = b % xdepth
        return pltpu.make_async_copy(hbm_block(xs_hbm, b), xbuf.at[slot], xsem.at[slot])

    def y_copy(b, slot):
        return pltpu.make_async_copy(ybuf.at[slot], hbm_block(ys_hbm, b), ysem.at[slot])

    @pl.when(e == 0)
    def _():
        for s in range(ydepth):
            ypend[s] = 0
        for b in range(lookahead):
            @pl.when(b < total)
            def _():
                x_copy(b).start()

    def chunk(j, m):
        b = b0 + j
        for i in range(m):
            @pl.when(b + lookahead + i < total)
            def _():
                x_copy(b + lookahead + i).start()

        for i in range(m):
            x_copy(b + i).wait()

            @pl.when(ypend[(b + i) % ydepth] == 1)
            def _():
                y_copy(b + i, (b + i) % ydepth).wait()

        for i in range(m):
            xslot = (b + i) % xdepth
            yslot = (b + i) % ydepth
            halves = [_unpack_pairs(xbuf[xslot, hh * EXPERT_ROWS:(hh + 1) * EXPERT_ROWS, :])
                      for hh in range(ROW_HALVES)]
            xb = jnp.concatenate([lo for lo, _ in halves] + [hi for _, hi in halves], axis=-1).astype(BF16)
            hid = _silu(_dot(xb, wgb_ref[...])) * _dot(xb, wub_ref[...])
            yb = _dot(hid.astype(BF16), wdb_ref[...])
            packed = _pack_pairs(yb[:, :D_MODEL // 2], yb[:, D_MODEL // 2:])
            for hh in range(ROW_HALVES):
                ybuf[yslot, hh * EXPERT_ROWS:(hh + 1) * EXPERT_ROWS, :] = (
                    packed[:, hh * HALF_WORDS:(hh + 1) * HALF_WORDS])

        for i in range(m):
            y_copy(b + i, (b + i) % ydepth).start()
            ypend[(b + i) % ydepth] = 1

    @pl.when(n > 0)
    def _():
        wgb_ref[...] = wg_ref[0].astype(BF16)
        wub_ref[...] = wu_ref[0].astype(BF16)
        wdb_ref[...] = wd_ref[0].astype(BF16)
        full = n // MAX_CHUNK

        def full_chunk(q, carry):
            chunk(q * MAX_CHUNK, MAX_CHUNK)
            return carry

        lax.fori_loop(0, full, full_chunk, 0)
        for m in range(1, MAX_CHUNK):
            @pl.when(n - full * MAX_CHUNK == m)
            def _():
                chunk(full * MAX_CHUNK, m)

    @pl.when(e == last)
    def _():
        for s in range(ydepth):
            @pl.when(ypend[s] == 1)
            def _():
                y_copy(b0, s).wait()


def _experts(bstart, nblk, xs, w_gate, w_up, w_down, after):
    blk_shape = (EXPERT_ROWS * ROW_HALVES, HALF_WORDS)
    wspec = lambda shape: pl.BlockSpec((1,) + shape, lambda e, bs, nb: (e, 0, 0))
    grid_spec = pltpu.PrefetchScalarGridSpec(
        num_scalar_prefetch=2,
        grid=(N_EXPERTS,),
        in_specs=[pl.BlockSpec(memory_space=pl.ANY),
                  wspec((D_MODEL, D_EXPERT)), wspec((D_MODEL, D_EXPERT)), wspec((D_EXPERT, D_MODEL)),
                  pl.BlockSpec(memory_space=pl.ANY)],
        out_specs=pl.BlockSpec(memory_space=pl.ANY),
        scratch_shapes=[pltpu.VMEM((X_RING,) + blk_shape, U32), pltpu.VMEM((Y_RING,) + blk_shape, U32),
                        pltpu.VMEM((D_MODEL, D_EXPERT), BF16), pltpu.VMEM((D_MODEL, D_EXPERT), BF16),
                        pltpu.VMEM((D_EXPERT, D_MODEL), BF16),
                        pltpu.SemaphoreType.DMA((X_RING,)), pltpu.SemaphoreType.DMA((Y_RING,)),
                        pltpu.SMEM((Y_RING,), I32)],
    )
    return pl.pallas_call(
        _expert_kernel,
        grid_spec=grid_spec,
        out_shape=jax.ShapeDtypeStruct(xs.shape, U32),
        compiler_params=pltpu.CompilerParams(dimension_semantics=("arbitrary",), has_side_effects=True),
        name="experts",
    )(bstart, nblk, xs, w_gate, w_up, w_down, after)


def _combine_kernel(gate_ref, base_ref, g2_ref, b2_ref, rows_ref, o_ref):
    tm = base_ref.shape[0]
    gates_t = jnp.concatenate([gate_ref[0], jnp.zeros((LANES - TOP_K, tm), F32)], axis=0).T
    half = D_MODEL // 2
    lo_acc = [base_ref[:, hh * HALF_WORDS:(hh + 1) * HALF_WORDS] for hh in range(ROW_HALVES)]
    hi_acc = [base_ref[:, half + hh * HALF_WORDS:half + (hh + 1) * HALF_WORDS] for hh in range(ROW_HALVES)]
    for k in range(TOP_K):
        gk = gates_t[:, k:k + 1]
        for hh in range(ROW_HALVES):
            lo, hi = _unpack_pairs(rows_ref[k, hh])
            lo_acc[hh] = lo_acc[hh] + gk * lo
            hi_acc[hh] = hi_acc[hh] + gk * hi
    o_ref[...] = _layer_norm(jnp.concatenate(lo_acc + hi_acc, axis=-1), g2_ref[...], b2_ref[...])


def _combine_piece_kernel(gate_ref, base_ref, g2_ref, b2_ref, rows_ref, prev_ref, o_ref):
    del prev_ref
    _combine_kernel(gate_ref, base_ref, g2_ref, b2_ref, rows_ref, o_ref)


def _combine(gates, base, g2, b2, rows, first_tile, prev):
    t = base.shape[0]
    tm = gates.shape[-1]
    tiles = rows.shape[2] // tm
    in_specs = [pl.BlockSpec((1, TOP_K, tm), lambda i: (i + first_tile, 0, 0)),
                pl.BlockSpec((tm, D_MODEL), lambda i: (i + first_tile, 0)),
                pl.BlockSpec((1, D_MODEL), lambda i: (0, 0)),
                pl.BlockSpec((1, D_MODEL), lambda i: (0, 0)),
                pl.BlockSpec((TOP_K, ROW_HALVES, tm, HALF_WORDS), lambda i: (0, 0, i, 0))]
    args = [gates, base, g2, b2, rows]
    if prev is not None:
        in_specs.append(pl.BlockSpec(memory_space=pl.ANY))
        args.append(prev)
    return pl.pallas_call(
        _combine_kernel if prev is None else _combine_piece_kernel,
        grid=(tiles,),
        in_specs=in_specs,
        out_specs=pl.BlockSpec((tm, D_MODEL), lambda i: (i + first_tile, 0)),
        out_shape=jax.ShapeDtypeStruct((t, D_MODEL), F32),
        input_output_aliases={} if prev is None else {len(args) - 1: 0},
        compiler_params=pltpu.CompilerParams(dimension_semantics=("arbitrary",),
                                             vmem_limit_bytes=48 * 1024 * 1024),
        name="combine",
    )(*args)


def _layer(h2, w_in, conv_w, conv_b, dt_bias, a_log, d_skip, ssd_norm_w, lower_bound, hgrn_norm_w, w_out,
           ln1_g, ln1_b, w_router, router_bias, w_gate_e, w_up_e, w_down_e, w_gate_s, w_up_s, w_down_s,
           ln2_g, ln2_b):
    t = h2.shape[0]
    dt0 = D_SSM + D_CONV
    q0 = dt0 + SSD_HEADS
    w_perm = jnp.concatenate(
        [w_in[:, :dt0], w_in[:, q0:], w_in[:, dt0:q0], jnp.zeros((D_MODEL, DT_PAD - SSD_HEADS), w_in.dtype)],
        axis=1).astype(BF16)
    z, xbc, q, f, i, g, dt = _inproj(h2, w_perm)
    y_ssd = _ssd(z, xbc, dt, conv_w, conv_b, dt_bias, a_log, d_skip, ssd_norm_w)
    y_hgrn = _hgrn(q, f, i, g, lower_bound, hgrn_norm_w)

    wr_t = w_router.astype(F32).T
    wr_hi = wr_t.astype(BF16)
    wr_lo = (wr_t - wr_hi.astype(F32)).astype(BF16)
    row = lambda v: v.reshape(1, -1).astype(F32)
    htt, base, idx, gates, totals = _post(
        h2, y_ssd, y_hgrn, w_out.astype(BF16), row(ln1_g), row(ln1_b), wr_hi, wr_lo,
        router_bias.reshape(N_EXPERTS, 1).astype(F32))

    dest, bstart, nblk = _positions(idx, totals)
    n_rows = _max_blocks(t) * EXPERT_ROWS
    d = jnp.transpose(dest, (1, 0, 2)).reshape(TOP_K, 1, t)
    slot = (d // EXPERT_ROWS) * (EXPERT_ROWS * ROW_HALVES) + d % EXPERT_ROWS
    slot = (slot + jnp.arange(ROW_HALVES, dtype=I32).reshape(1, ROW_HALVES, 1) * EXPERT_ROWS).reshape(-1)
    xs = _sc_dispatch(htt.reshape(ROW_HALVES * t, HALF_WORDS), slot, n_rows * ROW_HALVES)
    base = _shared(htt, base, w_gate_s.astype(BF16), w_up_s.astype(BF16), w_down_s.astype(BF16))
    ys = _experts(bstart[:, 0], nblk[:, 0], xs, w_gate_e, w_up_e, w_down_e, base)
    slot = slot.reshape(TOP_K, ROW_HALVES, t)
    tp = t // GATHER_PIECES
    out = None
    for p in range(GATHER_PIECES):
        rows = _sc_gather(ys, slot[:, :, p * tp:(p + 1) * tp].reshape(-1))
        out = _combine(gates, base, row(ln2_g), row(ln2_b), rows.reshape(TOP_K, ROW_HALVES, tp, HALF_WORDS),
                       p * (tp // TM_TOK), out)
    return out


def kernel(x, w_in, conv_w, conv_b, dt_bias, a_log, d_skip, ssd_norm_w, hgrn_lb_logits, hgrn_norm_w, w_out,
           ln1_g, ln1_b, w_router, router_bias, w_gate_e, w_up_e, w_down_e, w_gate_s, w_up_s, w_down_s,
           ln2_g, ln2_b):
    bsz, t, d = x.shape
    assert bsz == 1 and d == D_MODEL, "the recurrent mixers carry state across the flattened token axis"
    depth = w_in.shape[0]
    lower_bounds = jnp.cumsum(jax.nn.softmax(hgrn_lb_logits.astype(F32), axis=0), axis=0)
    h = x.reshape(bsz * t, d)
    for l in range(depth):
        h = _layer(h, w_in[l], conv_w[l], conv_b[l], dt_bias[l], a_log[l], d_skip[l], ssd_norm_w[l],
                   lower_bounds[l], hgrn_norm_w[l], w_out[l], ln1_g[l], ln1_b[l], w_router[l],
                   router_bias[l], w_gate_e[l], w_up_e[l], w_down_e[l], w_gate_s[l], w_up_s[l],
                   w_down_s[l], ln2_g[l], ln2_b[l])
    return h.reshape(bsz, t, d)
```

```python
import jax
import jax.numpy as jnp
from jax import lax
from jax.experimental import pallas as pl
from jax.experimental.pallas import tpu as pltpu
from jax.experimental.pallas import tpu_sc as plsc

F32 = jnp.float32
BF16 = jnp.bfloat16
I32 = jnp.int32
U32 = jnp.uint32

D_MODEL = 1024
D_SSM = 512
D_HGRN = 512
SSD_HEADS = 8
SSD_HEAD_DIM = 64
SSD_GROUPS = 2
SSD_STATE = 128
SSD_CONV = 4
SSD_CHUNK = 128
SSD_PER_STEP = 4
D_CONV = D_SSM + 2 * SSD_GROUPS * SSD_STATE
HGRN_HEADS = 4
HGRN_DK = 128
HGRN_CHUNK = 64
HGRN_SUB = 8
HGRN_PER_STEP = 4
N_EXPERTS = 256
TOP_K = 8
N_EXPERT_GROUPS = 8
TOPK_GROUPS = 4
D_EXPERT = 256
ROUTED_SCALE = 2.5
ALPHA = 2.0 ** 0.25
LN_EPS = 1e-5
RMS_EPS = 1e-6

LANES = 128
SUBLANES = 8
ROW_TILES = D_MODEL // LANES
ROW_HALVES = 2
HALF_WORDS = D_MODEL // 2 // ROW_HALVES
SC_WINDOW = 128
GATHER_PIECES = 4
DT_PAD = LANES
N_IN_PAD = D_SSM + D_CONV + 4 * D_HGRN + DT_PAD

TM_PROJ = 256
TM_TOK = 256
EXPERT_ROWS = 256
MAX_CHUNK = 2
X_RING = 8
Y_RING = 8
NEG = -1e30


def _sigmoid(x):
    return 1.0 / (1.0 + jnp.exp(-x))


def _silu(x):
    return x * _sigmoid(x)


def _split3(x):
    hi = x.astype(BF16)
    r = x - hi.astype(F32)
    mid = r.astype(BF16)
    lo = (r - mid.astype(F32)).astype(BF16)
    return hi, mid, lo


def _dot(a, b):
    return jnp.dot(a, b, preferred_element_type=F32)


def _dot_nt(a, b):
    return lax.dot_general(a, b, (((1,), (1,)), ((), ())), preferred_element_type=F32)


def _dot_tn(a, b):
    return lax.dot_general(a, b, (((0,), (0,)), ((), ())), preferred_element_type=F32)


def _sel_dot(sel, x):
    hi, mid, lo = _split3(x)
    return _dot(sel, hi) + _dot(sel, mid) + _dot(sel, lo)


def _dot_sel(x, sel):
    hi, mid, lo = _split3(x)
    return _dot(hi, sel) + _dot(mid, sel) + _dot(lo, sel)


def _pack_pairs(lo, hi):
    lo_bits = pltpu.bitcast(lo.astype(BF16).astype(F32), U32) >> 16
    hi_bits = pltpu.bitcast(hi.astype(BF16).astype(F32), U32) & jnp.uint32(0xFFFF0000)
    return hi_bits | lo_bits


def _unpack_pairs(p):
    return pltpu.bitcast(p << 16, F32), pltpu.bitcast(p & jnp.uint32(0xFFFF0000), F32)


def _layer_norm(x, g, b):
    mu = jnp.mean(x, axis=-1, keepdims=True)
    xc = x - mu
    var = jnp.mean(xc * xc, axis=-1, keepdims=True)
    return xc * lax.rsqrt(var + LN_EPS) * g + b


def _inproj_kernel(x_ref, w_ref, z_ref, xbc_ref, q_ref, f_ref, i_ref, g_ref, dt_ref):
    xb = x_ref[...].astype(BF16)
    col = 0
    for ref in (z_ref, xbc_ref, q_ref, f_ref, i_ref, g_ref, dt_ref):
        n = ref.shape[-1]
        ref[...] = _dot(xb, w_ref[:, col:col + n])
        col += n


def _inproj(x2, w_perm):
    t = x2.shape[0]
    widths = (D_SSM, D_CONV, D_HGRN, D_HGRN, D_HGRN, D_HGRN, DT_PAD)
    return pl.pallas_call(
        _inproj_kernel,
        grid=(t // TM_PROJ,),
        in_specs=[pl.BlockSpec((TM_PROJ, D_MODEL), lambda i: (i, 0)),
                  pl.BlockSpec((D_MODEL, N_IN_PAD), lambda i: (0, 0))],
        out_specs=[pl.BlockSpec((TM_PROJ, n), lambda i: (i, 0)) for n in widths],
        out_shape=[jax.ShapeDtypeStruct((t, n), F32) for n in widths],
        compiler_params=pltpu.CompilerParams(dimension_semantics=("arbitrary",),
                                             vmem_limit_bytes=48 * 1024 * 1024),
        name="inproj",
    )(x2, w_perm)


def _ssd_kernel(z_ref, xbc_ref, dt_ref, cw_ref, cb_ref, dtb_ref, alog_ref, dskip_ref, nw_ref,
                y_ref, ext_ref, st_ref):
    n = xbc_ref.shape[0]
    halo = SUBLANES

    @pl.when(pl.program_id(0) == 0)
    def _():
        ext_ref[0:halo, :] = jnp.zeros((halo, D_CONV), F32)
        st_ref[...] = jnp.zeros(st_ref.shape, F32)

    ext_ref[halo:halo + n, :] = xbc_ref[...]
    acc = jnp.broadcast_to(cb_ref[...], (n, D_CONV))
    for k in range(SSD_CONV):
        off = halo - (SSD_CONV - 1) + k
        acc = acc + cw_ref[k:k + 1, :] * ext_ref[off:off + n, :]
    ext_ref[0:halo, :] = xbc_ref[n - halo:n, :]
    u = _silu(acc)
    for c in range(n // SSD_CHUNK):
        rows = slice(c * SSD_CHUNK, (c + 1) * SSD_CHUNK)
        y_ref[rows, :] = _ssd_chunk(u[rows, :], z_ref[rows, :], dt_ref[rows, :] + dtb_ref[...],
                                    -jnp.exp(alog_ref[...]), dskip_ref[...], nw_ref[...], st_ref)


def _ssd_chunk(u, z, draw, a, dskip, nw, st_ref):
    L = SSD_CHUNK
    xs = u[:, :D_SSM]
    bm = u[:, D_SSM:D_SSM + SSD_GROUPS * SSD_STATE]
    cm = u[:, D_SSM + SSD_GROUPS * SSD_STATE:]

    dt = jnp.maximum(draw, 0.0) + jnp.log(1.0 + jnp.exp(-jnp.abs(draw)))
    ad = dt * a
    rows = lax.broadcasted_iota(I32, (L, L), 0)
    cols = lax.broadcasted_iota(I32, (L, L), 1)
    causal = rows >= cols
    a_cum = _sel_dot(causal.astype(BF16), ad)
    a_cum_t = a_cum.T

    hrow = lax.broadcasted_iota(I32, (LANES, D_SSM), 0)
    hcol = lax.broadcasted_iota(I32, (LANES, D_SSM), 1) // SSD_HEAD_DIM
    expand = (hrow == hcol).astype(BF16)
    dt_x = _dot_sel(dt, expand)
    acx = _dot_sel(a_cum, expand)
    last = acx[L - 1:L, :]
    ea_x = jnp.exp(acx)
    dec_x = jnp.exp(last - acx)
    elast_x = jnp.exp(last)

    xdt = xs * dt_x
    gw = SSD_HEADS // SSD_GROUPS * SSD_HEAD_DIM
    lane_head = lax.broadcasted_iota(I32, (L, gw), 1) // SSD_HEAD_DIM
    ys = []
    for g in range(SSD_GROUPS):
        bg = bm[:, g * SSD_STATE:(g + 1) * SSD_STATE]
        cg = cm[:, g * SSD_STATE:(g + 1) * SSD_STATE].astype(BF16)
        bg_t = bg.T.astype(BF16)
        gmat = _dot(cg, bg_t)
        xdt_g = xdt[:, g * gw:(g + 1) * gw]
        xdt_gb = xdt_g.astype(BF16)
        r_prev = st_ref[g]
        y_g = _dot(cg, r_prev.astype(BF16)) * ea_x[:, g * gw:(g + 1) * gw]
        new_s = _dot(bg_t, (xdt_g * dec_x[:, g * gw:(g + 1) * gw]).astype(BF16))
        st_ref[g] = r_prev * elast_x[:, g * gw:(g + 1) * gw] + new_s
        for j in range(SSD_HEADS // SSD_GROUPS):
            h = g * (SSD_HEADS // SSD_GROUPS) + j
            diff = a_cum[:, h:h + 1] - a_cum_t[h:h + 1, :]
            decay = jnp.exp(jnp.where(causal, diff, NEG))
            yd = _dot((gmat * decay).astype(BF16), xdt_gb)
            y_g = y_g + jnp.where(lane_head == j, yd, 0.0)
        ys.append(y_g)
    y = jnp.concatenate(ys, axis=-1) + xs * dskip
    y = y * _silu(z)
    outs = []
    ng = D_SSM // SSD_GROUPS
    for g in range(SSD_GROUPS):
        yg = y[:, g * ng:(g + 1) * ng]
        ms = jnp.mean(yg * yg, axis=-1, keepdims=True)
        outs.append(yg * lax.rsqrt(ms + RMS_EPS))
    return jnp.concatenate(outs, axis=-1) * nw


def _ssd(z, xbc, dt, conv_w, conv_b, dt_bias, a_log, d_skip, norm_w):
    t = z.shape[0]
    L = SSD_CHUNK * SSD_PER_STEP
    pad = lambda v: jnp.pad(v.astype(F32), (0, LANES - v.shape[0])).reshape(1, LANES)
    full = lambda shape: pl.BlockSpec(shape, lambda c: (0,) * len(shape))
    return pl.pallas_call(
        _ssd_kernel,
        grid=(t // L,),
        in_specs=[pl.BlockSpec((L, D_SSM), lambda c: (c, 0)),
                  pl.BlockSpec((L, D_CONV), lambda c: (c, 0)),
                  pl.BlockSpec((L, DT_PAD), lambda c: (c, 0)),
                  full((SSD_CONV, D_CONV)), full((1, D_CONV)), full((1, LANES)), full((1, LANES)),
                  full((1, D_SSM)), full((1, D_SSM))],
        out_specs=pl.BlockSpec((L, D_SSM), lambda c: (c, 0)),
        out_shape=jax.ShapeDtypeStruct((t, D_SSM), F32),
        scratch_shapes=[pltpu.VMEM((L + SUBLANES, D_CONV), F32),
                        pltpu.VMEM((SSD_GROUPS, SSD_STATE, D_SSM // SSD_GROUPS), F32)],
        compiler_params=pltpu.CompilerParams(dimension_semantics=("arbitrary",)),
        name="ssd",
    )(z, xbc, dt, conv_w.astype(F32), conv_b.reshape(1, D_CONV).astype(F32), pad(dt_bias), pad(a_log),
      jnp.repeat(d_skip.astype(F32), SSD_HEAD_DIM).reshape(1, D_SSM), norm_w.reshape(1, D_SSM).astype(F32))


def _tile_bcast(x, r):
    n, d = x.shape
    x3 = x.reshape(n // SUBLANES, SUBLANES, d)
    return jnp.broadcast_to(x3[:, r:r + 1, :], x3.shape).reshape(n, d)


def _hgrn_kernel(q_ref, f_ref, i_ref, g_ref, lb_ref, nw_ref, o_ref, st_ref):
    @pl.when(pl.program_id(0) == 0)
    def _():
        st_ref[...] = jnp.zeros(st_ref.shape, F32)

    for c in range(o_ref.shape[0] // HGRN_CHUNK):
        rows = slice(c * HGRN_CHUNK, (c + 1) * HGRN_CHUNK)
        o_ref[rows, :] = _hgrn_chunk(q_ref[rows, :], f_ref[rows, :], i_ref[rows, :], g_ref[rows, :],
                                     lb_ref[...], nw_ref[...], st_ref)


def _hgrn_chunk(q, f, v, g, lb, nw, st_ref):
    C = HGRN_CHUNK
    S = HGRN_SUB
    nsub = C // S
    fg = lb + (1.0 - lb) * _sigmoid(f)
    qs = _silu(q)
    rows = lax.broadcasted_iota(I32, (C, C), 0)
    cols = lax.broadcasted_iota(I32, (C, C), 1)
    cum = _sel_dot((rows >= cols).astype(BF16), jnp.log2(fg))
    ck_all = cum - jnp.log2(1.0 - fg)

    srow = lax.broadcasted_iota(I32, (S * HGRN_DK, C), 0) // HGRN_DK
    scol = lax.broadcasted_iota(I32, (S * HGRN_DK, C), 1) % S
    spread = (srow == scol).astype(BF16)
    diag_keep = jnp.logical_and(rows // S == cols // S, rows >= cols)
    outs = []
    for h in range(HGRN_HEADS):
        blk = slice(h * HGRN_DK, (h + 1) * HGRN_DK)
        qh, ch, ck = qs[:, blk], cum[:, blk], ck_all[:, blk]
        last = ch[C - 1:C, :]
        x_cat = jnp.concatenate(
            [(qh * jnp.exp2(jnp.minimum(ch - _tile_bcast(ck, r), 0.0))).astype(BF16) for r in range(S)], axis=-1)
        kbe = jnp.exp2(_tile_bcast(ch, S - 1) - ck).astype(BF16)
        a_pieces, b_pieces = [], []
        for j in range(nsub - 1):
            lo = (j + 1) * S
            a = (qh[lo:, :] * jnp.exp2(ch[lo:, :] - ch[lo - 1:lo, :])).astype(BF16)
            a_pieces.append(jnp.concatenate([jnp.zeros((lo, HGRN_DK), BF16), a], axis=0))
            b = [kbe[j * S:lo, :], jnp.zeros((C - lo, HGRN_DK), BF16)]
            if j > 0:
                b.insert(0, jnp.zeros((j * S, HGRN_DK), BF16))
            b_pieces.append(jnp.concatenate(b, axis=0))
        att = (_dot_nt(jnp.concatenate(a_pieces, axis=-1), jnp.concatenate(b_pieces, axis=-1))
               + jnp.where(diag_keep, _dot(x_cat, spread), 0.0))
        vb = v[:, blk].astype(BF16)
        s_prev = st_ref[h]
        o = _dot_nt((qh * jnp.exp2(ch)).astype(BF16), s_prev.astype(BF16)) + _dot(att.astype(BF16), vb)
        st_ref[h] = s_prev * jnp.exp2(last) + _dot_tn(vb, jnp.exp2(last - ck).astype(BF16))
        ms = jnp.mean(o * o, axis=-1, keepdims=True)
        outs.append(o * lax.rsqrt(ms + RMS_EPS))
    return jnp.concatenate(outs, axis=-1) * nw * _silu(g)


def _hgrn(q, f, i, g, lower_bound, norm_w):
    t = q.shape[0]
    C = HGRN_CHUNK * HGRN_PER_STEP
    tok = pl.BlockSpec((C, D_HGRN), lambda c: (c, 0))
    vec = pl.BlockSpec((1, D_HGRN), lambda c: (0, 0))
    return pl.pallas_call(
        _hgrn_kernel,
        grid=(t // C,),
        in_specs=[tok, tok, tok, tok, vec, vec],
        out_specs=tok,
        out_shape=jax.ShapeDtypeStruct((t, D_HGRN), F32),
        scratch_shapes=[pltpu.VMEM((HGRN_HEADS, D_HGRN // HGRN_HEADS, HGRN_DK), F32)],
        compiler_params=pltpu.CompilerParams(dimension_semantics=("arbitrary",)),
        name="hgrn",
    )(q, f, i, g, lower_bound.reshape(1, D_HGRN).astype(F32), norm_w.reshape(1, D_HGRN).astype(F32))


def _post_kernel(x_ref, ys_ref, yh_ref, wo_ref, g1_ref, b1_ref, wrh_ref, wrl_ref, rb_ref,
                 htt_ref, base_ref, idx_ref, gate_ref, cnt_ref):
    tm = x_ref.shape[0]
    mix = (_dot(ys_ref[...].astype(BF16), wo_ref[0:D_SSM, :])
           + _dot(yh_ref[...].astype(BF16), wo_ref[D_SSM:, :]))
    h1 = _layer_norm(ALPHA * x_ref[...] + mix, g1_ref[...], b1_ref[...])
    packed = _pack_pairs(h1[:, :D_MODEL // 2], h1[:, D_MODEL // 2:])
    for hh in range(ROW_HALVES):
        htt_ref[hh] = packed[:, hh * HALF_WORDS:(hh + 1) * HALF_WORDS]
    hb = h1.astype(BF16)
    base_ref[...] = ALPHA * h1

    hlo = (h1 - hb.astype(F32)).astype(BF16)
    logits = _dot_nt(wrh_ref[...], hb) + _dot_nt(wrh_ref[...], hlo) + _dot_nt(wrl_ref[...], hb)
    scores = _sigmoid(logits)
    biased = scores + rb_ref[...]
    per_group = N_EXPERTS // N_EXPERT_GROUPS
    eidx = lax.broadcasted_iota(I32, (N_EXPERTS, tm), 0)
    big = jnp.int32(1 << 20)
    gsc = []
    bidx = lax.broadcasted_iota(I32, (per_group, tm), 0)
    for gi in range(N_EXPERT_GROUPS):
        blk = biased[gi * per_group:(gi + 1) * per_group, :]
        m1 = jnp.max(blk, axis=0, keepdims=True)
        i1 = jnp.min(jnp.where(blk == m1, bidx, big), axis=0, keepdims=True)
        m2 = jnp.max(jnp.where(bidx == i1, NEG, blk), axis=0, keepdims=True)
        gsc.append(m1 + m2)
    cur = jnp.concatenate(gsc, axis=0)
    gidx = lax.broadcasted_iota(I32, (N_EXPERT_GROUPS, tm), 0)
    gsel = jnp.zeros((N_EXPERT_GROUPS, tm), F32)
    for _ in range(TOPK_GROUPS):
        m = jnp.max(cur, axis=0, keepdims=True)
        i = jnp.min(jnp.where(cur == m, gidx, big), axis=0, keepdims=True)
        hit = gidx == i
        gsel = jnp.where(hit, 1.0, gsel)
        cur = jnp.where(hit, NEG, cur)
    emask = jnp.concatenate(
        [jnp.broadcast_to(gsel[gi:gi + 1, :], (per_group, tm)) for gi in range(N_EXPERT_GROUPS)], axis=0)
    masked = jnp.where(emask > 0.0, biased, NEG)
    idx_rows, gate_rows = [], []
    for _ in range(TOP_K):
        m = jnp.max(masked, axis=0, keepdims=True)
        i = jnp.min(jnp.where(masked == m, eidx, big), axis=0, keepdims=True)
        hit = eidx == i
        idx_rows.append(i)
        gate_rows.append(jnp.sum(jnp.where(hit, scores, 0.0), axis=0, keepdims=True))
        masked = jnp.where(hit, NEG, masked)
    picked = jnp.where(jnp.logical_and(masked == NEG, emask > 0.0), 1.0, 0.0)
    gates = jnp.concatenate(gate_rows, axis=0)
    gates = gates / jnp.sum(gates, axis=0, keepdims=True) * ROUTED_SCALE
    idx_ref[0] = jnp.concatenate(idx_rows, axis=0)
    gate_ref[0] = gates

    @pl.when(pl.program_id(0) == 0)
    def _():
        cnt_ref[...] = jnp.zeros(cnt_ref.shape, F32)

    cnt_ref[...] += _dot(picked.astype(BF16), jnp.ones((tm, LANES), BF16))


def _post(x2, y_ssd, y_hgrn, wo, g1, b1, wr_hi, wr_lo, rbias):
    t = x2.shape[0]
    tm = TM_TOK
    nt = t // tm
    full = lambda shape: pl.BlockSpec(shape, lambda i: (0,) * len(shape))
    return pl.pallas_call(
        _post_kernel,
        grid=(nt,),
        in_specs=[pl.BlockSpec((tm, D_MODEL), lambda i: (i, 0)),
                  pl.BlockSpec((tm, D_SSM), lambda i: (i, 0)),
                  pl.BlockSpec((tm, D_HGRN), lambda i: (i, 0)),
                  full((D_MODEL, D_MODEL)), full((1, D_MODEL)), full((1, D_MODEL)),
                  full((N_EXPERTS, D_MODEL)), full((N_EXPERTS, D_MODEL)), full((N_EXPERTS, 1))],
        out_specs=[pl.BlockSpec((ROW_HALVES, tm, HALF_WORDS), lambda i: (0, i, 0)),
                   pl.BlockSpec((tm, D_MODEL), lambda i: (i, 0)),
                   pl.BlockSpec((1, TOP_K, tm), lambda i: (i, 0, 0)),
                   pl.BlockSpec((1, TOP_K, tm), lambda i: (i, 0, 0)),
                   pl.BlockSpec((N_EXPERTS, LANES), lambda i: (0, 0))],
        out_shape=[jax.ShapeDtypeStruct((ROW_HALVES, t, HALF_WORDS), U32),
                   jax.ShapeDtypeStruct((t, D_MODEL), F32),
                   jax.ShapeDtypeStruct((nt, TOP_K, tm), I32),
                   jax.ShapeDtypeStruct((nt, TOP_K, tm), F32),
                   jax.ShapeDtypeStruct((N_EXPERTS, LANES), F32)],
        compiler_params=pltpu.CompilerParams(dimension_semantics=("arbitrary",),
                                             vmem_limit_bytes=48 * 1024 * 1024),
        name="post",
    )(x2, y_ssd, y_hgrn, wo, g1, b1, wr_hi, wr_lo, rbias)


def _max_blocks(t):
    return (t * TOP_K + N_EXPERTS * (EXPERT_ROWS - 1)) // EXPERT_ROWS


def _pos_kernel(idx_ref, total_ref, dest_ref, bstart_ref, nblk_ref, cnt_ref, start_ref):
    i = pl.program_id(0)
    tm = idx_ref.shape[-1]
    idx = idx_ref[0]
    eidx = lax.broadcasted_iota(I32, (N_EXPERTS, tm), 0)
    sel = [eidx == idx[k:k + 1, :] for k in range(TOP_K)]
    onehot = sel[0]
    for k in range(1, TOP_K):
        onehot = jnp.logical_or(onehot, sel[k])
    mt = jnp.where(onehot, 1.0, 0.0).astype(BF16)

    @pl.when(i == 0)
    def _():
        nb = jnp.floor((total_ref[...] + (EXPERT_ROWS - 1)) * (1.0 / EXPERT_ROWS))
        r = lax.broadcasted_iota(I32, (N_EXPERTS, N_EXPERTS), 0)
        c = lax.broadcasted_iota(I32, (N_EXPERTS, N_EXPERTS), 1)
        end = _dot((r >= c).astype(BF16), nb.astype(BF16))
        start_ref[...] = (end - nb) * EXPERT_ROWS
        cnt_ref[...] = jnp.zeros(cnt_ref.shape, F32)
        bstart_ref[...] = (end - nb).astype(I32)
        nblk_ref[...] = nb.astype(I32)

    r = lax.broadcasted_iota(I32, (tm, tm), 0)
    c = lax.broadcasted_iota(I32, (tm, tm), 1)
    before = _dot(mt, (r < c).astype(BF16))
    slot = start_ref[:, 0:1] + cnt_ref[:, 0:1] + before
    rows = [jnp.sum(jnp.where(sel[k], slot, 0.0), axis=0, keepdims=True) for k in range(TOP_K)]
    dest_ref[0] = jnp.concatenate(rows, axis=0).astype(I32)
    cnt_ref[...] += _dot(mt, jnp.ones((tm, LANES), BF16))


def _positions(idx, totals):
    nt, _, tm = idx.shape
    return pl.pallas_call(
        _pos_kernel,
        grid=(nt,),
        in_specs=[pl.BlockSpec((1, TOP_K, tm), lambda i: (i, 0, 0)),
                  pl.BlockSpec((N_EXPERTS, LANES), lambda i: (0, 0))],
        out_specs=[pl.BlockSpec((1, TOP_K, tm), lambda i: (i, 0, 0)),
                   pl.BlockSpec((N_EXPERTS, LANES), lambda i: (0, 0)),
                   pl.BlockSpec((N_EXPERTS, LANES), lambda i: (0, 0))],
        out_shape=[jax.ShapeDtypeStruct((nt, TOP_K, tm), I32),
                   jax.ShapeDtypeStruct((N_EXPERTS, LANES), I32),
                   jax.ShapeDtypeStruct((N_EXPERTS, LANES), I32)],
        scratch_shapes=[pltpu.VMEM((N_EXPERTS, LANES), F32), pltpu.VMEM((N_EXPERTS, LANES), F32)],
        compiler_params=pltpu.CompilerParams(dimension_semantics=("arbitrary",)),
        name="positions",
    )(idx, totals)


def _sc_mesh():
    return plsc.VectorSubcoreMesh(core_axis_name="core", subcore_axis_name="subcore")


def _sc_dispatch(src, index, n_out):
    n = index.shape[0]
    src_blocks = src.shape[0] // SC_WINDOW
    index = index.reshape(1, n)

    @pl.kernel(out_type=jax.ShapeDtypeStruct((n_out, HALF_WORDS), src.dtype), mesh=_sc_mesh(), name="sc_dispatch")
    def scatter(src_hbm, idx_hbm, out_hbm):
        def body(rows_vmem, idx_vmem):
            pltpu.sync_copy(rows_vmem, out_hbm.at[idx_vmem.at[0]])

        pltpu.emit_pipeline(
            body, grid=(n // SC_WINDOW,),
            in_specs=[pl.BlockSpec((SC_WINDOW, HALF_WORDS), index_map=lambda i: (i % src_blocks, 0)),
                      pl.BlockSpec((1, SC_WINDOW), index_map=lambda i: (0, i))],
            out_specs=[],
            core_axis_name=("core", "subcore"), dimension_semantics=(pltpu.PARALLEL,))(src_hbm, idx_hbm)

    return scatter(src, index)


def _sc_gather(src, index):
    n = index.shape[0]
    index = index.reshape(1, n)

    @pl.kernel(out_type=jax.ShapeDtypeStruct((n, HALF_WORDS), src.dtype), mesh=_sc_mesh(), name="sc_gather")
    def gather(src_hbm, idx_hbm, out_hbm):
        def body(idx_vmem, rows_vmem):
            pltpu.sync_copy(src_hbm.at[idx_vmem.at[0]], rows_vmem)

        pltpu.emit_pipeline(
            body, grid=(n // SC_WINDOW,),
            in_specs=[pl.BlockSpec((1, SC_WINDOW), index_map=lambda i: (0, i))],
            out_specs=[pl.BlockSpec((SC_WINDOW, HALF_WORDS), index_map=lambda i: (i, 0))],
            core_axis_name=("core", "subcore"), dimension_semantics=(pltpu.PARALLEL,))(idx_hbm, out_hbm)

    return gather(src, index)


def _expert_kernel(bstart_ref, nblk_ref, xs_hbm, wg_ref, wu_ref, wd_ref, ys_hbm,
                   xbuf, ybuf, wgb_ref, wub_ref, wdb_ref, xsem, ysem, ypend):
    e = pl.program_id(0)
    last = pl.num_programs(0) - 1
    n = nblk_ref[e]
    b0 = bstart_ref[e]
    total = bstart_ref[last] + nblk_ref[last]
    blk_rows = EXPERT_ROWS * ROW_HALVES
    xdepth = xbuf.shape[0]
    ydepth = ybuf.shape[0]
    lookahead = xdepth - MAX_CHUNK

    def hbm_block(ref, b):
        return ref.at[pl.ds(pl.multiple_of(b * blk_rows, blk_rows), blk_rows)]

    def x_copy(b):
        slot = b % xdepth
        return pltpu.make_async_copy(hbm_block(xs_hbm, b), xbuf.at[slot], xsem.at[slot])

    def y_copy(b, slot):
        return pltpu.make_async_copy(ybuf.at[slot], hbm_block(ys_hbm, b), ysem.at[slot])

    @pl.when(e == 0)
    def _():
        for s in range(ydepth):
            ypend[s] = 0
        for b in range(lookahead):
            @pl.when(b < total)
            def _():
                x_copy(b).start()

    def chunk(j, m):
        b = b0 + j
        for i in range(m):
            @pl.when(b + lookahead + i < total)
            def _():
                x_copy(b + lookahead + i).start()

        for i in range(m):
            x_copy(b + i).wait()

            @pl.when(ypend[(b + i) % ydepth] == 1)
            def _():
                y_copy(b + i, (b + i) % ydepth).wait()

        for i in range(m):
            xslot = (b + i) % xdepth
            yslot = (b + i) % ydepth
            halves = [_unpack_pairs(xbuf[xslot, hh * EXPERT_ROWS:(hh + 1) * EXPERT_ROWS, :])
                      for hh in range(ROW_HALVES)]
            xb = jnp.concatenate([lo for lo, _ in halves] + [hi for _, hi in halves], axis=-1).astype(BF16)
            hid = _silu(_dot(xb, wgb_ref[...])) * _dot(xb, wub_ref[...])
            yb = _dot(hid.astype(BF16), wdb_ref[...])
            packed = _pack_pairs(yb[:, :D_MODEL // 2], yb[:, D_MODEL // 2:])
            for hh in range(ROW_HALVES):
                ybuf[yslot, hh * EXPERT_ROWS:(hh + 1) * EXPERT_ROWS, :] = (
                    packed[:, hh * HALF_WORDS:(hh + 1) * HALF_WORDS])

        for i in range(m):
            y_copy(b + i, (b + i) % ydepth).start()
            ypend[(b + i) % ydepth] = 1

    @pl.when(n > 0)
    def _():
        wgb_ref[...] = wg_ref[0].astype(BF16)
        wub_ref[...] = wu_ref[0].astype(BF16)
        wdb_ref[...] = wd_ref[0].astype(BF16)
        full = n // MAX_CHUNK

        def full_chunk(q, carry):
            chunk(q * MAX_CHUNK, MAX_CHUNK)
            return carry

        lax.fori_loop(0, full, full_chunk, 0)
        for m in range(1, MAX_CHUNK):
            @pl.when(n - full * MAX_CHUNK == m)
            def _():
                chunk(full * MAX_CHUNK, m)

    @pl.when(e == last)
    def _():
        for s in range(ydepth):
            @pl.when(ypend[s] == 1)
            def _():
                y_copy(b0, s).wait()


def _experts(bstart, nblk, xs, w_gate, w_up, w_down):
    blk_shape = (EXPERT_ROWS * ROW_HALVES, HALF_WORDS)
    wspec = lambda shape: pl.BlockSpec((1,) + shape, lambda e, bs, nb: (e, 0, 0))
    grid_spec = pltpu.PrefetchScalarGridSpec(
        num_scalar_prefetch=2,
        grid=(N_EXPERTS,),
        in_specs=[pl.BlockSpec(memory_space=pl.ANY),
                  wspec((D_MODEL, D_EXPERT)), wspec((D_MODEL, D_EXPERT)), wspec((D_EXPERT, D_MODEL))],
        out_specs=pl.BlockSpec(memory_space=pl.ANY),
        scratch_shapes=[pltpu.VMEM((X_RING,) + blk_shape, U32), pltpu.VMEM((Y_RING,) + blk_shape, U32),
                        pltpu.VMEM((D_MODEL, D_EXPERT), BF16), pltpu.VMEM((D_MODEL, D_EXPERT), BF16),
                        pltpu.VMEM((D_EXPERT, D_MODEL), BF16),
                        pltpu.SemaphoreType.DMA((X_RING,)), pltpu.SemaphoreType.DMA((Y_RING,)),
                        pltpu.SMEM((Y_RING,), I32)],
    )
    return pl.pallas_call(
        _expert_kernel,
        grid_spec=grid_spec,
        out_shape=jax.ShapeDtypeStruct(xs.shape, U32),
        compiler_params=pltpu.CompilerParams(dimension_semantics=("arbitrary",), has_side_effects=True),
        name="experts",
    )(bstart, nblk, xs, w_gate, w_up, w_down)


def _combine_kernel(gate_ref, base_ref, g2_ref, b2_ref, htt_ref, wg_ref, wu_ref, wd_ref, rows_ref, o_ref):
    tm = base_ref.shape[0]
    halves = [_unpack_pairs(htt_ref[hh]) for hh in range(ROW_HALVES)]
    hb = jnp.concatenate([lo for lo, _ in halves] + [hi for _, hi in halves], axis=-1).astype(BF16)
    hid = _silu(_dot(hb, wg_ref[...])) * _dot(hb, wu_ref[...])
    acc = base_ref[...] + _dot(hid.astype(BF16), wd_ref[...])
    gates_t = jnp.concatenate([gate_ref[0], jnp.zeros((LANES - TOP_K, tm), F32)], axis=0).T
    half = D_MODEL // 2
    lo_acc = [acc[:, hh * HALF_WORDS:(hh + 1) * HALF_WORDS] for hh in range(ROW_HALVES)]
    hi_acc = [acc[:, half + hh * HALF_WORDS:half + (hh + 1) * HALF_WORDS] for hh in range(ROW_HALVES)]
    for k in range(TOP_K):
        gk = gates_t[:, k:k + 1]
        for hh in range(ROW_HALVES):
            lo, hi = _unpack_pairs(rows_ref[k, hh])
            lo_acc[hh] = lo_acc[hh] + gk * lo
            hi_acc[hh] = hi_acc[hh] + gk * hi
    o_ref[...] = _layer_norm(jnp.concatenate(lo_acc + hi_acc, axis=-1), g2_ref[...], b2_ref[...])


def _combine_piece_kernel(*refs):
    _combine_kernel(*refs[:-2], refs[-1])


def _combine(gates, base, g2, b2, htt, wgs, wus, wds, rows, first_tile, prev):
    t = base.shape[0]
    tm = gates.shape[-1]
    tiles = rows.shape[2] // tm
    full = lambda shape: pl.BlockSpec(shape, lambda i: (0,) * len(shape))
    in_specs = [pl.BlockSpec((1, TOP_K, tm), lambda i: (i + first_tile, 0, 0)),
                pl.BlockSpec((tm, D_MODEL), lambda i: (i + first_tile, 0)),
                full((1, D_MODEL)), full((1, D_MODEL)),
                pl.BlockSpec((ROW_HALVES, tm, HALF_WORDS), lambda i: (0, i + first_tile, 0)),
                full((D_MODEL, D_EXPERT)), full((D_MODEL, D_EXPERT)), full((D_EXPERT, D_MODEL)),
                pl.BlockSpec((TOP_K, ROW_HALVES, tm, HALF_WORDS), lambda i: (0, 0, i, 0))]
    args = [gates, base, g2, b2, htt, wgs, wus, wds, rows]
    if prev is not None:
        in_specs.append(pl.BlockSpec(memory_space=pl.ANY))
        args.append(prev)
    return pl.pallas_call(
        _combine_kernel if prev is None else _combine_piece_kernel,
        grid=(tiles,),
        in_specs=in_specs,
        out_specs=pl.BlockSpec((tm, D_MODEL), lambda i: (i + first_tile, 0)),
        out_shape=jax.ShapeDtypeStruct((t, D_MODEL), F32),
        input_output_aliases={} if prev is None else {len(args) - 1: 0},
        compiler_params=pltpu.CompilerParams(dimension_semantics=("arbitrary",),
                                             vmem_limit_bytes=48 * 1024 * 1024),
        name="combine",
    )(*args)


def _layer(h2, w_in, conv_w, conv_b, dt_bias, a_log, d_skip, ssd_norm_w, lower_bound, hgrn_norm_w, w_out,
           ln1_g, ln1_b, w_router, router_bias, w_gate_e, w_up_e, w_down_e, w_gate_s, w_up_s, w_down_s,
           ln2_g, ln2_b):
    t = h2.shape[0]
    dt0 = D_SSM + D_CONV
    q0 = dt0 + SSD_HEADS
    w_perm = jnp.concatenate(
        [w_in[:, :dt0], w_in[:, q0:], w_in[:, dt0:q0], jnp.zeros((D_MODEL, DT_PAD - SSD_HEADS), w_in.dtype)],
        axis=1).astype(BF16)
    z, xbc, q, f, i, g, dt = _inproj(h2, w_perm)
    y_ssd = _ssd(z, xbc, dt, conv_w, conv_b, dt_bias, a_log, d_skip, ssd_norm_w)
    y_hgrn = _hgrn(q, f, i, g, lower_bound, hgrn_norm_w)

    wr_t = w_router.astype(F32).T
    wr_hi = wr_t.astype(BF16)
    wr_lo = (wr_t - wr_hi.astype(F32)).astype(BF16)
    row = lambda v: v.reshape(1, -1).astype(F32)
    htt, base, idx, gates, totals = _post(
        h2, y_ssd, y_hgrn, w_out.astype(BF16), row(ln1_g), row(ln1_b), wr_hi, wr_lo,
        router_bias.reshape(N_EXPERTS, 1).astype(F32))

    dest, bstart, nblk = _positions(idx, totals)
    n_rows = _max_blocks(t) * EXPERT_ROWS
    d = jnp.transpose(dest, (1, 0, 2)).reshape(TOP_K, 1, t)
    slot = (d // EXPERT_ROWS) * (EXPERT_ROWS * ROW_HALVES) + d % EXPERT_ROWS
    slot = (slot + jnp.arange(ROW_HALVES, dtype=I32).reshape(1, ROW_HALVES, 1) * EXPERT_ROWS).reshape(-1)
    xs = _sc_dispatch(htt.reshape(ROW_HALVES * t, HALF_WORDS), slot, n_rows * ROW_HALVES)
    ys = _experts(bstart[:, 0], nblk[:, 0], xs, w_gate_e, w_up_e, w_down_e)
    slot = slot.reshape(TOP_K, ROW_HALVES, t)
    tp = t // GATHER_PIECES
    shared_w = (w_gate_s.astype(BF16), w_up_s.astype(BF16), w_down_s.astype(BF16))
    out = None
    for p in range(GATHER_PIECES):
        rows = _sc_gather(ys, slot[:, :, p * tp:(p + 1) * tp].reshape(-1))
        out = _combine(gates, base, row(ln2_g), row(ln2_b), htt, *shared_w,
                       rows.reshape(TOP_K, ROW_HALVES, tp, HALF_WORDS), p * (tp // TM_TOK), out)
    return out


def kernel(x, w_in, conv_w, conv_b, dt_bias, a_log, d_skip, ssd_norm_w, hgrn_lb_logits, hgrn_norm_w, w_out,
           ln1_g, ln1_b, w_router, router_bias, w_gate_e, w_up_e, w_down_e, w_gate_s, w_up_s, w_down_s,
           ln2_g, ln2_b):
    bsz, t, d = x.shape
    assert bsz == 1 and d == D_MODEL, "the recurrent mixers carry state across the flattened token axis"
    depth = w_in.shape[0]
    lower_bounds = jnp.cumsum(jax.nn.softmax(hgrn_lb_logits.astype(F32), axis=0), axis=0)
    h = x.reshape(bsz * t, d)
    for l in range(depth):
        h = _layer(h, w_in[l], conv_w[l], conv_b[l], dt_bias[l], a_log[l], d_skip[l], ssd_norm_w[l],
                   lower_bounds[l], hgrn_norm_w[l], w_out[l], ln1_g[l], ln1_b[l], w_router[l],
                   router_bias[l], w_gate_e[l], w_up_e[l], w_down_e[l], w_gate_s[l], w_up_s[l],
                   w_down_s[l], ln2_g[l], ln2_b[l])
    return h.reshape(bsz, t, d)
```

```python
import jax
import jax.numpy as jnp
from jax import lax
from jax.experimental import pallas as pl
from jax.experimental.pallas import tpu as pltpu
from jax.experimental.pallas import tpu_sc as plsc

F32 = jnp.float32
BF16 = jnp.bfloat16
I32 = jnp.int32
U32 = jnp.uint32

D_MODEL = 1024
D_SSM = 512
D_HGRN = 512
SSD_HEADS = 8
SSD_HEAD_DIM = 64
SSD_GROUPS = 2
SSD_STATE = 128
SSD_CONV = 4
SSD_CHUNK = 128
SSD_PER_STEP = 8
D_CONV = D_SSM + 2 * SSD_GROUPS * SSD_STATE
HGRN_HEADS = 4
HGRN_DK = 128
HGRN_CHUNK = 64
HGRN_SUB = 8
HGRN_PER_STEP = 8
N_EXPERTS = 256
TOP_K = 8
N_EXPERT_GROUPS = 8
TOPK_GROUPS = 4
D_EXPERT = 256
ROUTED_SCALE = 2.5
ALPHA = 2.0 ** 0.25
LN_EPS = 1e-5
RMS_EPS = 1e-6

LANES = 128
SUBLANES = 8
ROW_TILES = D_MODEL // LANES
ROW_HALVES = 2
HALF_WORDS = D_MODEL // 2 // ROW_HALVES
SC_WINDOW = 128
GATHER_PIECES = 8
DT_PAD = LANES
N_IN_PAD = D_SSM + D_CONV + 4 * D_HGRN + DT_PAD

TM_PROJ = 512
TM_TOK = 256
EXPERT_ROWS = 256
MAX_CHUNK = 2
X_RING = 8
Y_RING = 8
NEG = -1e30


def _sigmoid(x):
    return 1.0 / (1.0 + jnp.exp(-x))


def _silu(x):
    return x * _sigmoid(x)


def _split3(x):
    hi = x.astype(BF16)
    r = x - hi.astype(F32)
    mid = r.astype(BF16)
    lo = (r - mid.astype(F32)).astype(BF16)
    return hi, mid, lo


def _dot(a, b):
    return jnp.dot(a, b, preferred_element_type=F32)


def _dot_nt(a, b):
    return lax.dot_general(a, b, (((1,), (1,)), ((), ())), preferred_element_type=F32)


def _dot_tn(a, b):
    return lax.dot_general(a, b, (((0,), (0,)), ((), ())), preferred_element_type=F32)


def _sel_dot(sel, x):
    hi, mid, lo = _split3(x)
    return _dot(sel, hi) + _dot(sel, mid) + _dot(sel, lo)


def _dot_sel(x, sel):
    hi, mid, lo = _split3(x)
    return _dot(hi, sel) + _dot(mid, sel) + _dot(lo, sel)


def _pack_pairs(lo, hi):
    lo_bits = pltpu.bitcast(lo.astype(BF16).astype(F32), U32) >> 16
    hi_bits = pltpu.bitcast(hi.astype(BF16).astype(F32), U32) & jnp.uint32(0xFFFF0000)
    return hi_bits | lo_bits


def _unpack_pairs(p):
    return pltpu.bitcast(p << 16, F32), pltpu.bitcast(p & jnp.uint32(0xFFFF0000), F32)


def _layer_norm(x, g, b):
    mu = jnp.mean(x, axis=-1, keepdims=True)
    xc = x - mu
    var = jnp.mean(xc * xc, axis=-1, keepdims=True)
    return xc * lax.rsqrt(var + LN_EPS) * g + b


def _inproj_kernel(x_ref, w_ref, z_ref, xbc_ref, q_ref, f_ref, i_ref, g_ref, dt_ref):
    xb = x_ref[...].astype(BF16)
    col = 0
    for ref in (z_ref, xbc_ref, q_ref, f_ref, i_ref, g_ref, dt_ref):
        n = ref.shape[-1]
        ref[...] = _dot(xb, w_ref[:, col:col + n])
        col += n


def _inproj(x2, w_perm):
    t = x2.shape[0]
    widths = (D_SSM, D_CONV, D_HGRN, D_HGRN, D_HGRN, D_HGRN, DT_PAD)
    return pl.pallas_call(
        _inproj_kernel,
        grid=(t // TM_PROJ,),
        in_specs=[pl.BlockSpec((TM_PROJ, D_MODEL), lambda i: (i, 0)),
                  pl.BlockSpec((D_MODEL, N_IN_PAD), lambda i: (0, 0))],
        out_specs=[pl.BlockSpec((TM_PROJ, n), lambda i: (i, 0)) for n in widths],
        out_shape=[jax.ShapeDtypeStruct((t, n), F32) for n in widths],
        compiler_params=pltpu.CompilerParams(dimension_semantics=("arbitrary",),
                                             vmem_limit_bytes=48 * 1024 * 1024),
        name="inproj",
    )(x2, w_perm)


def _ssd_kernel(z_ref, xbc_ref, dt_ref, cw_ref, cb_ref, dtb_ref, alog_ref, dskip_ref, nw_ref,
                y_ref, ext_ref, st_ref):
    n = xbc_ref.shape[0]
    halo = SUBLANES

    @pl.when(pl.program_id(0) == 0)
    def _():
        ext_ref[0:halo, :] = jnp.zeros((halo, D_CONV), F32)
        st_ref[...] = jnp.zeros(st_ref.shape, F32)

    ext_ref[halo:halo + n, :] = xbc_ref[...]
    acc = jnp.broadcast_to(cb_ref[...], (n, D_CONV))
    for k in range(SSD_CONV):
        off = halo - (SSD_CONV - 1) + k
        acc = acc + cw_ref[k:k + 1, :] * ext_ref[off:off + n, :]
    ext_ref[0:halo, :] = xbc_ref[n - halo:n, :]
    u = _silu(acc)
    for c in range(n // SSD_CHUNK):
        rows = slice(c * SSD_CHUNK, (c + 1) * SSD_CHUNK)
        y_ref[rows, :] = _ssd_chunk(u[rows, :], z_ref[rows, :], dt_ref[rows, :] + dtb_ref[...],
                                    -jnp.exp(alog_ref[...]), dskip_ref[...], nw_ref[...], st_ref)


def _ssd_chunk(u, z, draw, a, dskip, nw, st_ref):
    L = SSD_CHUNK
    xs = u[:, :D_SSM]
    bm = u[:, D_SSM:D_SSM + SSD_GROUPS * SSD_STATE]
    cm = u[:, D_SSM + SSD_GROUPS * SSD_STATE:]

    dt = jnp.maximum(draw, 0.0) + jnp.log(1.0 + jnp.exp(-jnp.abs(draw)))
    ad = dt * a
    rows = lax.broadcasted_iota(I32, (L, L), 0)
    cols = lax.broadcasted_iota(I32, (L, L), 1)
    causal = rows >= cols
    a_cum = _sel_dot(causal.astype(BF16), ad)
    a_cum_t = a_cum.T

    hrow = lax.broadcasted_iota(I32, (LANES, D_SSM), 0)
    hcol = lax.broadcasted_iota(I32, (LANES, D_SSM), 1) // SSD_HEAD_DIM
    expand = (hrow == hcol).astype(BF16)
    dt_x = _dot_sel(dt, expand)
    acx = _dot_sel(a_cum, expand)
    last = acx[L - 1:L, :]
    ea_x = jnp.exp(acx)
    dec_x = jnp.exp(last - acx)
    elast_x = jnp.exp(last)

    xdt = xs * dt_x
    gw = SSD_HEADS // SSD_GROUPS * SSD_HEAD_DIM
    lane_head = lax.broadcasted_iota(I32, (L, gw), 1) // SSD_HEAD_DIM
    ys = []
    for g in range(SSD_GROUPS):
        bg = bm[:, g * SSD_STATE:(g + 1) * SSD_STATE]
        cg = cm[:, g * SSD_STATE:(g + 1) * SSD_STATE].astype(BF16)
        bg_t = bg.T.astype(BF16)
        gmat = _dot(cg, bg_t)
        xdt_g = xdt[:, g * gw:(g + 1) * gw]
        xdt_gb = xdt_g.astype(BF16)
        r_prev = st_ref[g]
        y_g = _dot(cg, r_prev.astype(BF16)) * ea_x[:, g * gw:(g + 1) * gw]
        new_s = _dot(bg_t, (xdt_g * dec_x[:, g * gw:(g + 1) * gw]).astype(BF16))
        st_ref[g] = r_prev * elast_x[:, g * gw:(g + 1) * gw] + new_s
        for j in range(SSD_HEADS // SSD_GROUPS):
            h = g * (SSD_HEADS // SSD_GROUPS) + j
            diff = a_cum[:, h:h + 1] - a_cum_t[h:h + 1, :]
            decay = jnp.exp(jnp.where(causal, diff, NEG))
            yd = _dot((gmat * decay).astype(BF16), xdt_gb)
            y_g = y_g + jnp.where(lane_head == j, yd, 0.0)
        ys.append(y_g)
    y = jnp.concatenate(ys, axis=-1) + xs * dskip
    y = y * _silu(z)
    outs = []
    ng = D_SSM // SSD_GROUPS
    for g in range(SSD_GROUPS):
        yg = y[:, g * ng:(g + 1) * ng]
        ms = jnp.mean(yg * yg, axis=-1, keepdims=True)
        outs.append(yg * lax.rsqrt(ms + RMS_EPS))
    return jnp.concatenate(outs, axis=-1) * nw


def _ssd(z, xbc, dt, conv_w, conv_b, dt_bias, a_log, d_skip, norm_w):
    t = z.shape[0]
    L = SSD_CHUNK * SSD_PER_STEP
    pad = lambda v: jnp.pad(v.astype(F32), (0, LANES - v.shape[0])).reshape(1, LANES)
    full = lambda shape: pl.BlockSpec(shape, lambda c: (0,) * len(shape))
    return pl.pallas_call(
        _ssd_kernel,
        grid=(t // L,),
        in_specs=[pl.BlockSpec((L, D_SSM), lambda c: (c, 0)),
                  pl.BlockSpec((L, D_CONV), lambda c: (c, 0)),
                  pl.BlockSpec((L, DT_PAD), lambda c: (c, 0)),
                  full((SSD_CONV, D_CONV)), full((1, D_CONV)), full((1, LANES)), full((1, LANES)),
                  full((1, D_SSM)), full((1, D_SSM))],
        out_specs=pl.BlockSpec((L, D_SSM), lambda c: (c, 0)),
        out_shape=jax.ShapeDtypeStruct((t, D_SSM), F32),
        scratch_shapes=[pltpu.VMEM((L + SUBLANES, D_CONV), F32),
                        pltpu.VMEM((SSD_GROUPS, SSD_STATE, D_SSM // SSD_GROUPS), F32)],
        compiler_params=pltpu.CompilerParams(dimension_semantics=("arbitrary",)),
        name="ssd",
    )(z, xbc, dt, conv_w.astype(F32), conv_b.reshape(1, D_CONV).astype(F32), pad(dt_bias), pad(a_log),
      jnp.repeat(d_skip.astype(F32), SSD_HEAD_DIM).reshape(1, D_SSM), norm_w.reshape(1, D_SSM).astype(F32))


def _tile_bcast(x, r):
    n, d = x.shape
    x3 = x.reshape(n // SUBLANES, SUBLANES, d)
    return jnp.broadcast_to(x3[:, r:r + 1, :], x3.shape).reshape(n, d)


def _hgrn_kernel(q_ref, f_ref, i_ref, g_ref, lb_ref, nw_ref, o_ref, st_ref):
    @pl.when(pl.program_id(0) == 0)
    def _():
        st_ref[...] = jnp.zeros(st_ref.shape, F32)

    for c in range(o_ref.shape[0] // HGRN_CHUNK):
        rows = slice(c * HGRN_CHUNK, (c + 1) * HGRN_CHUNK)
        o_ref[rows, :] = _hgrn_chunk(q_ref[rows, :], f_ref[rows, :], i_ref[rows, :], g_ref[rows, :],
                                     lb_ref[...], nw_ref[...], st_ref)


def _hgrn_chunk(q, f, v, g, lb, nw, st_ref):
    C = HGRN_CHUNK
    S = HGRN_SUB
    nsub = C // S
    fg = lb + (1.0 - lb) * _sigmoid(f)
    qs = _silu(q)
    rows = lax.broadcasted_iota(I32, (C, C), 0)
    cols = lax.broadcasted_iota(I32, (C, C), 1)
    cum = _sel_dot((rows >= cols).astype(BF16), jnp.log2(fg))
    ck_all = cum - jnp.log2(1.0 - fg)

    srow = lax.broadcasted_iota(I32, (S * HGRN_DK, C), 0) // HGRN_DK
    scol = lax.broadcasted_iota(I32, (S * HGRN_DK, C), 1) % S
    spread = (srow == scol).astype(BF16)
    diag_keep = jnp.logical_and(rows // S == cols // S, rows >= cols)
    outs = []
    for h in range(HGRN_HEADS):
        blk = slice(h * HGRN_DK, (h + 1) * HGRN_DK)
        qh, ch, ck = qs[:, blk], cum[:, blk], ck_all[:, blk]
        last = ch[C - 1:C, :]
        x_cat = jnp.concatenate(
            [(qh * jnp.exp2(jnp.minimum(ch - _tile_bcast(ck, r), 0.0))).astype(BF16) for r in range(S)], axis=-1)
        kbe = jnp.exp2(_tile_bcast(ch, S - 1) - ck).astype(BF16)
        a_pieces, b_pieces = [], []
        for j in range(nsub - 1):
            lo = (j + 1) * S
            a = (qh[lo:, :] * jnp.exp2(ch[lo:, :] - ch[lo - 1:lo, :])).astype(BF16)
            a_pieces.append(jnp.concatenate([jnp.zeros((lo, HGRN_DK), BF16), a], axis=0))
            b = [kbe[j * S:lo, :], jnp.zeros((C - lo, HGRN_DK), BF16)]
            if j > 0:
                b.insert(0, jnp.zeros((j * S, HGRN_DK), BF16))
            b_pieces.append(jnp.concatenate(b, axis=0))
        att = (_dot_nt(jnp.concatenate(a_pieces, axis=-1), jnp.concatenate(b_pieces, axis=-1))
               + jnp.where(diag_keep, _dot(x_cat, spread), 0.0))
        vb = v[:, blk].astype(BF16)
        s_prev = st_ref[h]
        o = _dot_nt((qh * jnp.exp2(ch)).astype(BF16), s_prev.astype(BF16)) + _dot(att.astype(BF16), vb)
        st_ref[h] = s_prev * jnp.exp2(last) + _dot_tn(vb, jnp.exp2(last - ck).astype(BF16))
        ms = jnp.mean(o * o, axis=-1, keepdims=True)
        outs.append(o * lax.rsqrt(ms + RMS_EPS))
    return jnp.concatenate(outs, axis=-1) * nw * _silu(g)


def _hgrn(q, f, i, g, lower_bound, norm_w):
    t = q.shape[0]
    C = HGRN_CHUNK * HGRN_PER_STEP
    tok = pl.BlockSpec((C, D_HGRN), lambda c: (c, 0))
    vec = pl.BlockSpec((1, D_HGRN), lambda c: (0, 0))
    return pl.pallas_call(
        _hgrn_kernel,
        grid=(t // C,),
        in_specs=[tok, tok, tok, tok, vec, vec],
        out_specs=tok,
        out_shape=jax.ShapeDtypeStruct((t, D_HGRN), F32),
        scratch_shapes=[pltpu.VMEM((HGRN_HEADS, D_HGRN // HGRN_HEADS, HGRN_DK), F32)],
        compiler_params=pltpu.CompilerParams(dimension_semantics=("arbitrary",)),
        name="hgrn",
    )(q, f, i, g, lower_bound.reshape(1, D_HGRN).astype(F32), norm_w.reshape(1, D_HGRN).astype(F32))


def _post_kernel(x_ref, ys_ref, yh_ref, wo_ref, g1_ref, b1_ref, wrh_ref, wrl_ref, rb_ref,
                 htt_ref, base_ref, idx_ref, gate_ref, cnt_ref):
    tm = x_ref.shape[0]
    mix = (_dot(ys_ref[...].astype(BF16), wo_ref[0:D_SSM, :])
           + _dot(yh_ref[...].astype(BF16), wo_ref[D_SSM:, :]))
    h1 = _layer_norm(ALPHA * x_ref[...] + mix, g1_ref[...], b1_ref[...])
    packed = _pack_pairs(h1[:, :D_MODEL // 2], h1[:, D_MODEL // 2:])
    for hh in range(ROW_HALVES):
        htt_ref[hh] = packed[:, hh * HALF_WORDS:(hh + 1) * HALF_WORDS]
    hb = h1.astype(BF16)
    base_ref[...] = ALPHA * h1

    hlo = (h1 - hb.astype(F32)).astype(BF16)
    logits = _dot_nt(wrh_ref[...], hb) + _dot_nt(wrh_ref[...], hlo) + _dot_nt(wrl_ref[...], hb)
    scores = _sigmoid(logits)
    biased = scores + rb_ref[...]
    per_group = N_EXPERTS // N_EXPERT_GROUPS
    eidx = lax.broadcasted_iota(I32, (N_EXPERTS, tm), 0)
    big = jnp.int32(1 << 20)
    gsc = []
    bidx = lax.broadcasted_iota(I32, (per_group, tm), 0)
    for gi in range(N_EXPERT_GROUPS):
        blk = biased[gi * per_group:(gi + 1) * per_group, :]
        m1 = jnp.max(blk, axis=0, keepdims=True)
        i1 = jnp.min(jnp.where(blk == m1, bidx, big), axis=0, keepdims=True)
        m2 = jnp.max(jnp.where(bidx == i1, NEG, blk), axis=0, keepdims=True)
        gsc.append(m1 + m2)
    cur = jnp.concatenate(gsc, axis=0)
    gidx = lax.broadcasted_iota(I32, (N_EXPERT_GROUPS, tm), 0)
    gsel = jnp.zeros((N_EXPERT_GROUPS, tm), F32)
    for _ in range(TOPK_GROUPS):
        m = jnp.max(cur, axis=0, keepdims=True)
        i = jnp.min(jnp.where(cur == m, gidx, big), axis=0, keepdims=True)
        hit = gidx == i
        gsel = jnp.where(hit, 1.0, gsel)
        cur = jnp.where(hit, NEG, cur)
    emask = jnp.concatenate(
        [jnp.broadcast_to(gsel[gi:gi + 1, :], (per_group, tm)) for gi in range(N_EXPERT_GROUPS)], axis=0)
    masked = jnp.where(emask > 0.0, biased, NEG)
    idx_rows, gate_rows = [], []
    for _ in range(TOP_K):
        m = jnp.max(masked, axis=0, keepdims=True)
        i = jnp.min(jnp.where(masked == m, eidx, big), axis=0, keepdims=True)
        hit = eidx == i
        idx_rows.append(i)
        gate_rows.append(jnp.sum(jnp.where(hit, scores, 0.0), axis=0, keepdims=True))
        masked = jnp.where(hit, NEG, masked)
    picked = jnp.where(jnp.logical_and(masked == NEG, emask > 0.0), 1.0, 0.0)
    gates = jnp.concatenate(gate_rows, axis=0)
    gates = gates / jnp.sum(gates, axis=0, keepdims=True) * ROUTED_SCALE
    idx_ref[0] = jnp.concatenate(idx_rows, axis=0)
    gate_ref[0] = gates

    @pl.when(pl.program_id(0) == 0)
    def _():
        cnt_ref[...] = jnp.zeros(cnt_ref.shape, F32)

    cnt_ref[...] += _dot(picked.astype(BF16), jnp.ones((tm, LANES), BF16))


def _post(x2, y_ssd, y_hgrn, wo, g1, b1, wr_hi, wr_lo, rbias):
    t = x2.shape[0]
    tm = TM_TOK
    nt = t // tm
    full = lambda shape: pl.BlockSpec(shape, lambda i: (0,) * len(shape))
    return pl.pallas_call(
        _post_kernel,
        grid=(nt,),
        in_specs=[pl.BlockSpec((tm, D_MODEL), lambda i: (i, 0)),
                  pl.BlockSpec((tm, D_SSM), lambda i: (i, 0)),
                  pl.BlockSpec((tm, D_HGRN), lambda i: (i, 0)),
                  full((D_MODEL, D_MODEL)), full((1, D_MODEL)), full((1, D_MODEL)),
                  full((N_EXPERTS, D_MODEL)), full((N_EXPERTS, D_MODEL)), full((N_EXPERTS, 1))],
        out_specs=[pl.BlockSpec((ROW_HALVES, tm, HALF_WORDS), lambda i: (0, i, 0)),
                   pl.BlockSpec((tm, D_MODEL), lambda i: (i, 0)),
                   pl.BlockSpec((1, TOP_K, tm), lambda i: (i, 0, 0)),
                   pl.BlockSpec((1, TOP_K, tm), lambda i: (i, 0, 0)),
                   pl.BlockSpec((N_EXPERTS, LANES), lambda i: (0, 0))],
        out_shape=[jax.ShapeDtypeStruct((ROW_HALVES, t, HALF_WORDS), U32),
                   jax.ShapeDtypeStruct((t, D_MODEL), F32),
                   jax.ShapeDtypeStruct((nt, TOP_K, tm), I32),
                   jax.ShapeDtypeStruct((nt, TOP_K, tm), F32),
                   jax.ShapeDtypeStruct((N_EXPERTS, LANES), F32)],
        compiler_params=pltpu.CompilerParams(dimension_semantics=("arbitrary",),
                                             vmem_limit_bytes=48 * 1024 * 1024),
        name="post",
    )(x2, y_ssd, y_hgrn, wo, g1, b1, wr_hi, wr_lo, rbias)


def _max_blocks(t):
    return (t * TOP_K + N_EXPERTS * (EXPERT_ROWS - 1)) // EXPERT_ROWS


def _pos_kernel(idx_ref, total_ref, dest_ref, bstart_ref, nblk_ref, cnt_ref, start_ref):
    i = pl.program_id(0)
    tm = idx_ref.shape[-1]
    idx = idx_ref[0]
    eidx = lax.broadcasted_iota(I32, (N_EXPERTS, tm), 0)
    sel = [eidx == idx[k:k + 1, :] for k in range(TOP_K)]
    onehot = sel[0]
    for k in range(1, TOP_K):
        onehot = jnp.logical_or(onehot, sel[k])
    mt = jnp.where(onehot, 1.0, 0.0).astype(BF16)

    @pl.when(i == 0)
    def _():
        nb = jnp.floor((total_ref[...] + (EXPERT_ROWS - 1)) * (1.0 / EXPERT_ROWS))
        r = lax.broadcasted_iota(I32, (N_EXPERTS, N_EXPERTS), 0)
        c = lax.broadcasted_iota(I32, (N_EXPERTS, N_EXPERTS), 1)
        end = _dot((r >= c).astype(BF16), nb.astype(BF16))
        start_ref[...] = (end - nb) * EXPERT_ROWS
        cnt_ref[...] = jnp.zeros(cnt_ref.shape, F32)
        bstart_ref[...] = (end - nb).astype(I32)
        nblk_ref[...] = nb.astype(I32)

    r = lax.broadcasted_iota(I32, (tm, tm), 0)
    c = lax.broadcasted_iota(I32, (tm, tm), 1)
    before = _dot(mt, (r < c).astype(BF16))
    slot = start_ref[:, 0:1] + cnt_ref[:, 0:1] + before
    rows = [jnp.sum(jnp.where(sel[k], slot, 0.0), axis=0, keepdims=True) for k in range(TOP_K)]
    dest_ref[0] = jnp.concatenate(rows, axis=0).astype(I32)
    cnt_ref[...] += _dot(mt, jnp.ones((tm, LANES), BF16))


def _positions(idx, totals):
    nt, _, tm = idx.shape
    return pl.pallas_call(
        _pos_kernel,
        grid=(nt,),
        in_specs=[pl.BlockSpec((1, TOP_K, tm), lambda i: (i, 0, 0)),
                  pl.BlockSpec((N_EXPERTS, LANES), lambda i: (0, 0))],
        out_specs=[pl.BlockSpec((1, TOP_K, tm), lambda i: (i, 0, 0)),
                   pl.BlockSpec((N_EXPERTS, LANES), lambda i: (0, 0)),
                   pl.BlockSpec((N_EXPERTS, LANES), lambda i: (0, 0))],
        out_shape=[jax.ShapeDtypeStruct((nt, TOP_K, tm), I32),
                   jax.ShapeDtypeStruct((N_EXPERTS, LANES), I32),
                   jax.ShapeDtypeStruct((N_EXPERTS, LANES), I32)],
        scratch_shapes=[pltpu.VMEM((N_EXPERTS, LANES), F32), pltpu.VMEM((N_EXPERTS, LANES), F32)],
        compiler_params=pltpu.CompilerParams(dimension_semantics=("arbitrary",)),
        name="positions",
    )(idx, totals)


def _sc_mesh():
    return plsc.VectorSubcoreMesh(core_axis_name="core", subcore_axis_name="subcore")


def _sc_dispatch(src, index, n_out):
    n = index.shape[0]
    src_blocks = src.shape[0] // SC_WINDOW
    index = index.reshape(1, n)

    @pl.kernel(out_type=jax.ShapeDtypeStruct((n_out, HALF_WORDS), src.dtype), mesh=_sc_mesh(), name="sc_dispatch")
    def scatter(src_hbm, idx_hbm, out_hbm):
        def body(rows_vmem, idx_vmem):
            pltpu.sync_copy(rows_vmem, out_hbm.at[idx_vmem.at[0]])

        pltpu.emit_pipeline(
            body, grid=(n // SC_WINDOW,),
            in_specs=[pl.BlockSpec((SC_WINDOW, HALF_WORDS), index_map=lambda i: (i % src_blocks, 0)),
                      pl.BlockSpec((1, SC_WINDOW), index_map=lambda i: (0, i))],
            out_specs=[],
            core_axis_name=("core", "subcore"), dimension_semantics=(pltpu.PARALLEL,))(src_hbm, idx_hbm)

    return scatter(src, index)


def _sc_gather(src, index):
    n = index.shape[0]
    index = index.reshape(1, n)

    @pl.kernel(out_type=jax.ShapeDtypeStruct((n, HALF_WORDS), src.dtype), mesh=_sc_mesh(), name="sc_gather")
    def gather(src_hbm, idx_hbm, out_hbm):
        def body(idx_vmem, rows_vmem):
            pltpu.sync_copy(src_hbm.at[idx_vmem.at[0]], rows_vmem)

        pltpu.emit_pipeline(
            body, grid=(n // SC_WINDOW,),
            in_specs=[pl.BlockSpec((1, SC_WINDOW), index_map=lambda i: (0, i))],
            out_specs=[pl.BlockSpec((SC_WINDOW, HALF_WORDS), index_map=lambda i: (i, 0))],
            core_axis_name=("core", "subcore"), dimension_semantics=(pltpu.PARALLEL,))(idx_hbm, out_hbm)

    return gather(src, index)


def _expert_kernel(bstart_ref, nblk_ref, xs_hbm, wg_ref, wu_ref, wd_ref, ys_hbm,
                   xbuf, ybuf, wgb_ref, wub_ref, wdb_ref, xsem, ysem, ypend):
    e = pl.program_id(0)
    last = pl.num_programs(0) - 1
    n = nblk_ref[e]
    b0 = bstart_ref[e]
    total = bstart_ref[last] + nblk_ref[last]
    blk_rows = EXPERT_ROWS * ROW_HALVES
    xdepth = xbuf.shape[0]
    ydepth = ybuf.shape[0]
    lookahead = xdepth - MAX_CHUNK

    def hbm_block(ref, b):
        return ref.at[pl.ds(pl.multiple_of(b * blk_rows, blk_rows), blk_rows)]

    def x_copy(b):
        slot = b % xdepth
        return pltpu.make_async_copy(hbm_block(xs_hbm, b), xbuf.at[slot], xsem.at[slot])

    def y_copy(b, slot):
        return pltpu.make_async_copy(ybuf.at[slot], hbm_block(ys_hbm, b), ysem.at[slot])

    @pl.when(e == 0)
    def _():
        for s in range(ydepth):
            ypend[s] = 0
        for b in range(lookahead):
            @pl.when(b < total)
            def _():
                x_copy(b).start()

    def chunk(j, m):
        b = b0 + j
        for i in range(m):
            @pl.when(b + lookahead + i < total)
            def _():
                x_copy(b + lookahead + i).start()

        for i in range(m):
            x_copy(b + i).wait()

            @pl.when(ypend[(b + i) % ydepth] == 1)
            def _():
                y_copy(b + i, (b + i) % ydepth).wait()

        for i in range(m):
            xslot = (b + i) % xdepth
            yslot = (b + i) % ydepth
            halves = [_unpack_pairs(xbuf[xslot, hh * EXPERT_ROWS:(hh + 1) * EXPERT_ROWS, :])
                      for hh in range(ROW_HALVES)]
            xb = jnp.concatenate([lo for lo, _ in halves] + [hi for _, hi in halves], axis=-1).astype(BF16)
            hid = _silu(_dot(xb, wgb_ref[...])) * _dot(xb, wub_ref[...])
            yb = _dot(hid.astype(BF16), wdb_ref[...])
            packed = _pack_pairs(yb[:, :D_MODEL // 2], yb[:, D_MODEL // 2:])
            for hh in range(ROW_HALVES):
                ybuf[yslot, hh * EXPERT_ROWS:(hh + 1) * EXPERT_ROWS, :] = (
                    packed[:, hh * HALF_WORDS:(hh + 1) * HALF_WORDS])

        for i in range(m):
            y_copy(b + i, (b + i) % ydepth).start()
            ypend[(b + i) % ydepth] = 1

    @pl.when(n > 0)
    def _():
        wgb_ref[...] = wg_ref[0].astype(BF16)
        wub_ref[...] = wu_ref[0].astype(BF16)
        wdb_ref[...] = wd_ref[0].astype(BF16)
        full = n // MAX_CHUNK

        def full_chunk(q, carry):
            chunk(q * MAX_CHUNK, MAX_CHUNK)
            return carry

        lax.fori_loop(0, full, full_chunk, 0)
        for m in range(1, MAX_CHUNK):
            @pl.when(n - full * MAX_CHUNK == m)
            def _():
                chunk(full * MAX_CHUNK, m)

    @pl.when(e == last)
    def _():
        for s in range(ydepth):
            @pl.when(ypend[s] == 1)
            def _():
                y_copy(b0, s).wait()


def _experts(bstart, nblk, xs, w_gate, w_up, w_down):
    blk_shape = (EXPERT_ROWS * ROW_HALVES, HALF_WORDS)
    wspec = lambda shape: pl.BlockSpec((1,) + shape, lambda e, bs, nb: (e, 0, 0))
    grid_spec = pltpu.PrefetchScalarGridSpec(
        num_scalar_prefetch=2,
        grid=(N_EXPERTS,),
        in_specs=[pl.BlockSpec(memory_space=pl.ANY),
                  wspec((D_MODEL, D_EXPERT)), wspec((D_MODEL, D_EXPERT)), wspec((D_EXPERT, D_MODEL))],
        out_specs=pl.BlockSpec(memory_space=pl.ANY),
        scratch_shapes=[pltpu.VMEM((X_RING,) + blk_shape, U32), pltpu.VMEM((Y_RING,) + blk_shape, U32),
                        pltpu.VMEM((D_MODEL, D_EXPERT), BF16), pltpu.VMEM((D_MODEL, D_EXPERT), BF16),
                        pltpu.VMEM((D_EXPERT, D_MODEL), BF16),
                        pltpu.SemaphoreType.DMA((X_RING,)), pltpu.SemaphoreType.DMA((Y_RING,)),
                        pltpu.SMEM((Y_RING,), I32)],
    )
    return pl.pallas_call(
        _expert_kernel,
        grid_spec=grid_spec,
        out_shape=jax.ShapeDtypeStruct(xs.shape, U32),
        compiler_params=pltpu.CompilerParams(dimension_semantics=("arbitrary",), has_side_effects=True),
        name="experts",
    )(bstart, nblk, xs, w_gate, w_up, w_down)


def _combine_kernel(gate_ref, base_ref, g2_ref, b2_ref, htt_ref, wg_ref, wu_ref, wd_ref, rows_ref, o_ref):
    tm = base_ref.shape[0]
    halves = [_unpack_pairs(htt_ref[hh]) for hh in range(ROW_HALVES)]
    hb = jnp.concatenate([lo for lo, _ in halves] + [hi for _, hi in halves], axis=-1).astype(BF16)
    hid = _silu(_dot(hb, wg_ref[...])) * _dot(hb, wu_ref[...])
    acc = base_ref[...] + _dot(hid.astype(BF16), wd_ref[...])
    gates_t = jnp.concatenate([gate_ref[0], jnp.zeros((LANES - TOP_K, tm), F32)], axis=0).T
    half = D_MODEL // 2
    lo_acc = [acc[:, hh * HALF_WORDS:(hh + 1) * HALF_WORDS] for hh in range(ROW_HALVES)]
    hi_acc = [acc[:, half + hh * HALF_WORDS:half + (hh + 1) * HALF_WORDS] for hh in range(ROW_HALVES)]
    for k in range(TOP_K):
        gk = gates_t[:, k:k + 1]
        for hh in range(ROW_HALVES):
            lo, hi = _unpack_pairs(rows_ref[k, hh])
            lo_acc[hh] = lo_acc[hh] + gk * lo
            hi_acc[hh] = hi_acc[hh] + gk * hi
    o_ref[...] = _layer_norm(jnp.concatenate(lo_acc + hi_acc, axis=-1), g2_ref[...], b2_ref[...])


def _combine_piece_kernel(*refs):
    _combine_kernel(*refs[:-2], refs[-1])


def _combine(gates, base, g2, b2, htt, wgs, wus, wds, rows, first_tile, prev):
    t = base.shape[0]
    tm = gates.shape[-1]
    tiles = rows.shape[2] // tm
    full = lambda shape: pl.BlockSpec(shape, lambda i: (0,) * len(shape))
    in_specs = [pl.BlockSpec((1, TOP_K, tm), lambda i: (i + first_tile, 0, 0)),
                pl.BlockSpec((tm, D_MODEL), lambda i: (i + first_tile, 0)),
                full((1, D_MODEL)), full((1, D_MODEL)),
                pl.BlockSpec((ROW_HALVES, tm, HALF_WORDS), lambda i: (0, i + first_tile, 0)),
                full((D_MODEL, D_EXPERT)), full((D_MODEL, D_EXPERT)), full((D_EXPERT, D_MODEL)),
                pl.BlockSpec((TOP_K, ROW_HALVES, tm, HALF_WORDS), lambda i: (0, 0, i, 0))]
    args = [gates, base, g2, b2, htt, wgs, wus, wds, rows]
    if prev is not None:
        in_specs.append(pl.BlockSpec(memory_space=pl.ANY))
        args.append(prev)
    return pl.pallas_call(
        _combine_kernel if prev is None else _combine_piece_kernel,
        grid=(tiles,),
        in_specs=in_specs,
        out_specs=pl.BlockSpec((tm, D_MODEL), lambda i: (i + first_tile, 0)),
        out_shape=jax.ShapeDtypeStruct((t, D_MODEL), F32),
        input_output_aliases={} if prev is None else {len(args) - 1: 0},
        compiler_params=pltpu.CompilerParams(dimension_semantics=("arbitrary",),
                                             vmem_limit_bytes=48 * 1024 * 1024),
        name="combine",
    )(*args)


def _layer(h2, w_in, conv_w, conv_b, dt_bias, a_log, d_skip, ssd_norm_w, lower_bound, hgrn_norm_w, w_out,
           ln1_g, ln1_b, w_router, router_bias, w_gate_e, w_up_e, w_down_e, w_gate_s, w_up_s, w_down_s,
           ln2_g, ln2_b):
    t = h2.shape[0]
    dt0 = D_SSM + D_CONV
    q0 = dt0 + SSD_HEADS
    w_perm = jnp.concatenate(
        [w_in[:, :dt0], w_in[:, q0:], w_in[:, dt0:q0], jnp.zeros((D_MODEL, DT_PAD - SSD_HEADS), w_in.dtype)],
        axis=1).astype(BF16)
    z, xbc, q, f, i, g, dt = _inproj(h2, w_perm)
    y_ssd = _ssd(z, xbc, dt, conv_w, conv_b, dt_bias, a_log, d_skip, ssd_norm_w)
    y_hgrn = _hgrn(q, f, i, g, lower_bound, hgrn_norm_w)

    wr_t = w_router.astype(F32).T
    wr_hi = wr_t.astype(BF16)
    wr_lo = (wr_t - wr_hi.astype(F32)).astype(BF16)
    row = lambda v: v.reshape(1, -1).astype(F32)
    htt, base, idx, gates, totals = _post(
        h2, y_ssd, y_hgrn, w_out.astype(BF16), row(ln1_g), row(ln1_b), wr_hi, wr_lo,
        router_bias.reshape(N_EXPERTS, 1).astype(F32))

    dest, bstart, nblk = _positions(idx, totals)
    n_rows = _max_blocks(t) * EXPERT_ROWS
    d = jnp.transpose(dest, (1, 0, 2)).reshape(TOP_K, 1, t)
    slot = (d // EXPERT_ROWS) * (EXPERT_ROWS * ROW_HALVES) + d % EXPERT_ROWS
    slot = (slot + jnp.arange(ROW_HALVES, dtype=I32).reshape(1, ROW_HALVES, 1) * EXPERT_ROWS).reshape(-1)
    xs = _sc_dispatch(htt.reshape(ROW_HALVES * t, HALF_WORDS), slot, n_rows * ROW_HALVES)
    ys = _experts(bstart[:, 0], nblk[:, 0], xs, w_gate_e, w_up_e, w_down_e)
    slot = slot.reshape(TOP_K, ROW_HALVES, t)
    tp = t // GATHER_PIECES
    shared_w = (w_gate_s.astype(BF16), w_up_s.astype(BF16), w_down_s.astype(BF16))
    out = None
    for p in range(GATHER_PIECES):
        rows = _sc_gather(ys, slot[:, :, p * tp:(p + 1) * tp].reshape(-1))
        out = _combine(gates, base, row(ln2_g), row(ln2_b), htt, *shared_w,
                       rows.reshape(TOP_K, ROW_HALVES, tp, HALF_WORDS), p * (tp // TM_TOK), out)
    return out


def kernel(x, w_in, conv_w, conv_b, dt_bias, a_log, d_skip, ssd_norm_w, hgrn_lb_logits, hgrn_norm_w, w_out,
           ln1_g, ln1_b, w_router, router_bias, w_gate_e, w_up_e, w_down_e, w_gate_s, w_up_s, w_down_s,
           ln2_g, ln2_b):
    bsz, t, d = x.shape
    assert bsz == 1 and d == D_MODEL, "the recurrent mixers carry state across the flattened token axis"
    depth = w_in.shape[0]
    lower_bounds = jnp.cumsum(jax.nn.softmax(hgrn_lb_logits.astype(F32), axis=0), axis=0)
    h = x.reshape(bsz * t, d)
    for l in range(depth):
        h = _layer(h, w_in[l], conv_w[l], conv_b[l], dt_bias[l], a_log[l], d_skip[l], ssd_norm_w[l],
                   lower_bounds[l], hgrn_norm_w[l], w_out[l], ln1_g[l], ln1_b[l], w_router[l],
                   router_bias[l], w_gate_e[l], w_up_e[l], w_down_e[l], w_gate_s[l], w_up_s[l],
                   w_down_s[l], ln2_g[l], ln2_b[l])
    return h.reshape(bsz, t, d)
```

```python
import jax
import jax.numpy as jnp
from jax import lax
from jax.experimental import pallas as pl
from jax.experimental.pallas import tpu as pltpu
from jax.experimental.pallas import tpu_sc as plsc

F32 = jnp.float32
BF16 = jnp.bfloat16
I32 = jnp.int32
U32 = jnp.uint32

D_MODEL = 1024
D_SSM = 512
D_HGRN = 512
SSD_HEADS = 8
SSD_HEAD_DIM = 64
SSD_GROUPS = 2
SSD_STATE = 128
SSD_CONV = 4
SSD_CHUNK = 128
SSD_PER_STEP = 8
D_CONV = D_SSM + 2 * SSD_GROUPS * SSD_STATE
HGRN_HEADS = 4
HGRN_DK = 128
HGRN_CHUNK = 64
HGRN_SUB = 8
HGRN_PER_STEP = 16
N_EXPERTS = 256
TOP_K = 8
N_EXPERT_GROUPS = 8
TOPK_GROUPS = 4
D_EXPERT = 256
ROUTED_SCALE = 2.5
ALPHA = 2.0 ** 0.25
LN_EPS = 1e-5
RMS_EPS = 1e-6

LANES = 128
SUBLANES = 8
ROW_TILES = D_MODEL // LANES
ROW_HALVES = 2
HALF_WORDS = D_MODEL // 2 // ROW_HALVES
SC_WINDOW = 128
GATHER_PIECES = 8
DT_PAD = LANES
N_IN_PAD = D_SSM + D_CONV + 4 * D_HGRN + DT_PAD

TM_PROJ = 512
TM_TOK = 512
EXPERT_ROWS = 256
MAX_CHUNK = 2
X_RING = 8
Y_RING = 8
NEG = -1e30


def _sigmoid(x):
    return 1.0 / (1.0 + jnp.exp(-x))


def _silu(x):
    return x * _sigmoid(x)


def _split3(x):
    hi = x.astype(BF16)
    r = x - hi.astype(F32)
    mid = r.astype(BF16)
    lo = (r - mid.astype(F32)).astype(BF16)
    return hi, mid, lo


def _dot(a, b):
    return jnp.dot(a, b, preferred_element_type=F32)


def _dot_nt(a, b):
    return lax.dot_general(a, b, (((1,), (1,)), ((), ())), preferred_element_type=F32)


def _dot_tn(a, b):
    return lax.dot_general(a, b, (((0,), (0,)), ((), ())), preferred_element_type=F32)


def _sel_dot(sel, x):
    hi, mid, lo = _split3(x)
    return _dot(sel, hi) + _dot(sel, mid) + _dot(sel, lo)


def _dot_sel(x, sel):
    hi, mid, lo = _split3(x)
    return _dot(hi, sel) + _dot(mid, sel) + _dot(lo, sel)


def _pack_pairs(lo, hi):
    lo_bits = pltpu.bitcast(lo.astype(BF16).astype(F32), U32) >> 16
    hi_bits = pltpu.bitcast(hi.astype(BF16).astype(F32), U32) & jnp.uint32(0xFFFF0000)
    return hi_bits | lo_bits


def _unpack_pairs(p):
    return pltpu.bitcast(p << 16, F32), pltpu.bitcast(p & jnp.uint32(0xFFFF0000), F32)


def _layer_norm(x, g, b):
    mu = jnp.mean(x, axis=-1, keepdims=True)
    xc = x - mu
    var = jnp.mean(xc * xc, axis=-1, keepdims=True)
    return xc * lax.rsqrt(var + LN_EPS) * g + b


def _inproj_kernel(x_ref, w_ref, z_ref, xbc_ref, q_ref, f_ref, i_ref, g_ref, dt_ref):
    xb = x_ref[...].astype(BF16)
    col = 0
    for ref in (z_ref, xbc_ref, q_ref, f_ref, i_ref, g_ref, dt_ref):
        n = ref.shape[-1]
        ref[...] = _dot(xb, w_ref[:, col:col + n])
        col += n


def _inproj(x2, w_perm):
    t = x2.shape[0]
    widths = (D_SSM, D_CONV, D_HGRN, D_HGRN, D_HGRN, D_HGRN, DT_PAD)
    return pl.pallas_call(
        _inproj_kernel,
        grid=(t // TM_PROJ,),
        in_specs=[pl.BlockSpec((TM_PROJ, D_MODEL), lambda i: (i, 0)),
                  pl.BlockSpec((D_MODEL, N_IN_PAD), lambda i: (0, 0))],
        out_specs=[pl.BlockSpec((TM_PROJ, n), lambda i: (i, 0)) for n in widths],
        out_shape=[jax.ShapeDtypeStruct((t, n), F32) for n in widths],
        compiler_params=pltpu.CompilerParams(dimension_semantics=("arbitrary",),
                                             vmem_limit_bytes=48 * 1024 * 1024),
        name="inproj",
    )(x2, w_perm)


def _ssd_kernel(z_ref, xbc_ref, dt_ref, cw_ref, cb_ref, dtb_ref, alog_ref, dskip_ref, nw_ref,
                y_ref, ext_ref, st_ref):
    n = xbc_ref.shape[0]
    halo = SUBLANES

    @pl.when(pl.program_id(0) == 0)
    def _():
        ext_ref[0:halo, :] = jnp.zeros((halo, D_CONV), F32)
        st_ref[...] = jnp.zeros(st_ref.shape, F32)

    ext_ref[halo:halo + n, :] = xbc_ref[...]
    acc = jnp.broadcast_to(cb_ref[...], (n, D_CONV))
    for k in range(SSD_CONV):
        off = halo - (SSD_CONV - 1) + k
        acc = acc + cw_ref[k:k + 1, :] * ext_ref[off:off + n, :]
    ext_ref[0:halo, :] = xbc_ref[n - halo:n, :]
    u = _silu(acc)
    for c in range(n // SSD_CHUNK):
        rows = slice(c * SSD_CHUNK, (c + 1) * SSD_CHUNK)
        y_ref[rows, :] = _ssd_chunk(u[rows, :], z_ref[rows, :], dt_ref[rows, :] + dtb_ref[...],
                                    -jnp.exp(alog_ref[...]), dskip_ref[...], nw_ref[...], st_ref)


def _ssd_chunk(u, z, draw, a, dskip, nw, st_ref):
    L = SSD_CHUNK
    xs = u[:, :D_SSM]
    bm = u[:, D_SSM:D_SSM + SSD_GROUPS * SSD_STATE]
    cm = u[:, D_SSM + SSD_GROUPS * SSD_STATE:]

    dt = jnp.maximum(draw, 0.0) + jnp.log(1.0 + jnp.exp(-jnp.abs(draw)))
    ad = dt * a
    rows = lax.broadcasted_iota(I32, (L, L), 0)
    cols = lax.broadcasted_iota(I32, (L, L), 1)
    causal = rows >= cols
    a_cum = _sel_dot(causal.astype(BF16), ad)
    a_cum_t = a_cum.T

    hrow = lax.broadcasted_iota(I32, (LANES, D_SSM), 0)
    hcol = lax.broadcasted_iota(I32, (LANES, D_SSM), 1) // SSD_HEAD_DIM
    expand = (hrow == hcol).astype(BF16)
    dt_x = _dot_sel(dt, expand)
    acx = _dot_sel(a_cum, expand)
    last = acx[L - 1:L, :]
    ea_x = jnp.exp(acx)
    dec_x = jnp.exp(last - acx)
    elast_x = jnp.exp(last)

    xdt = xs * dt_x
    gw = SSD_HEADS // SSD_GROUPS * SSD_HEAD_DIM
    lane_head = lax.broadcasted_iota(I32, (L, gw), 1) // SSD_HEAD_DIM
    ys = []
    for g in range(SSD_GROUPS):
        bg = bm[:, g * SSD_STATE:(g + 1) * SSD_STATE]
        cg = cm[:, g * SSD_STATE:(g + 1) * SSD_STATE].astype(BF16)
        bg_t = bg.T.astype(BF16)
        gmat = _dot(cg, bg_t)
        xdt_g = xdt[:, g * gw:(g + 1) * gw]
        xdt_gb = xdt_g.astype(BF16)
        r_prev = st_ref[g]
        y_g = _dot(cg, r_prev.astype(BF16)) * ea_x[:, g * gw:(g + 1) * gw]
        new_s = _dot(bg_t, (xdt_g * dec_x[:, g * gw:(g + 1) * gw]).astype(BF16))
        st_ref[g] = r_prev * elast_x[:, g * gw:(g + 1) * gw] + new_s
        for j in range(SSD_HEADS // SSD_GROUPS):
            h = g * (SSD_HEADS // SSD_GROUPS) + j
            diff = a_cum[:, h:h + 1] - a_cum_t[h:h + 1, :]
            decay = jnp.exp(jnp.where(causal, diff, NEG))
            yd = _dot((gmat * decay).astype(BF16), xdt_gb)
            y_g = y_g + jnp.where(lane_head == j, yd, 0.0)
        ys.append(y_g)
    y = jnp.concatenate(ys, axis=-1) + xs * dskip
    y = y * _silu(z)
    outs = []
    ng = D_SSM // SSD_GROUPS
    for g in range(SSD_GROUPS):
        yg = y[:, g * ng:(g + 1) * ng]
        ms = jnp.mean(yg * yg, axis=-1, keepdims=True)
        outs.append(yg * lax.rsqrt(ms + RMS_EPS))
    return jnp.concatenate(outs, axis=-1) * nw


def _ssd(z, xbc, dt, conv_w, conv_b, dt_bias, a_log, d_skip, norm_w):
    t = z.shape[0]
    L = SSD_CHUNK * SSD_PER_STEP
    pad = lambda v: jnp.pad(v.astype(F32), (0, LANES - v.shape[0])).reshape(1, LANES)
    full = lambda shape: pl.BlockSpec(shape, lambda c: (0,) * len(shape))
    return pl.pallas_call(
        _ssd_kernel,
        grid=(t // L,),
        in_specs=[pl.BlockSpec((L, D_SSM), lambda c: (c, 0)),
                  pl.BlockSpec((L, D_CONV), lambda c: (c, 0)),
                  pl.BlockSpec((L, DT_PAD), lambda c: (c, 0)),
                  full((SSD_CONV, D_CONV)), full((1, D_CONV)), full((1, LANES)), full((1, LANES)),
                  full((1, D_SSM)), full((1, D_SSM))],
        out_specs=pl.BlockSpec((L, D_SSM), lambda c: (c, 0)),
        out_shape=jax.ShapeDtypeStruct((t, D_SSM), F32),
        scratch_shapes=[pltpu.VMEM((L + SUBLANES, D_CONV), F32),
                        pltpu.VMEM((SSD_GROUPS, SSD_STATE, D_SSM // SSD_GROUPS), F32)],
        compiler_params=pltpu.CompilerParams(dimension_semantics=("arbitrary",)),
        name="ssd",
    )(z, xbc, dt, conv_w.astype(F32), conv_b.reshape(1, D_CONV).astype(F32), pad(dt_bias), pad(a_log),
      jnp.repeat(d_skip.astype(F32), SSD_HEAD_DIM).reshape(1, D_SSM), norm_w.reshape(1, D_SSM).astype(F32))


def _tile_bcast(x, r):
    n, d = x.shape
    x3 = x.reshape(n // SUBLANES, SUBLANES, d)
    return jnp.broadcast_to(x3[:, r:r + 1, :], x3.shape).reshape(n, d)


def _hgrn_kernel(q_ref, f_ref, i_ref, g_ref, lb_ref, nw_ref, o_ref, st_ref):
    @pl.when(pl.program_id(0) == 0)
    def _():
        st_ref[...] = jnp.zeros(st_ref.shape, F32)

    for c in range(o_ref.shape[0] // HGRN_CHUNK):
        rows = slice(c * HGRN_CHUNK, (c + 1) * HGRN_CHUNK)
        o_ref[rows, :] = _hgrn_chunk(q_ref[rows, :], f_ref[rows, :], i_ref[rows, :], g_ref[rows, :],
                                     lb_ref[...], nw_ref[...], st_ref)


def _hgrn_chunk(q, f, v, g, lb, nw, st_ref):
    C = HGRN_CHUNK
    S = HGRN_SUB
    nsub = C // S
    fg = lb + (1.0 - lb) * _sigmoid(f)
    qs = _silu(q)
    rows = lax.broadcasted_iota(I32, (C, C), 0)
    cols = lax.broadcasted_iota(I32, (C, C), 1)
    cum = _sel_dot((rows >= cols).astype(BF16), jnp.log2(fg))
    ck_all = cum - jnp.log2(1.0 - fg)

    srow = lax.broadcasted_iota(I32, (S * HGRN_DK, C), 0) // HGRN_DK
    scol = lax.broadcasted_iota(I32, (S * HGRN_DK, C), 1) % S
    spread = (srow == scol).astype(BF16)
    diag_keep = jnp.logical_and(rows // S == cols // S, rows >= cols)
    outs = []
    for h in range(HGRN_HEADS):
        blk = slice(h * HGRN_DK, (h + 1) * HGRN_DK)
        qh, ch, ck = qs[:, blk], cum[:, blk], ck_all[:, blk]
        last = ch[C - 1:C, :]
        x_cat = jnp.concatenate(
            [(qh * jnp.exp2(jnp.minimum(ch - _tile_bcast(ck, r), 0.0))).astype(BF16) for r in range(S)], axis=-1)
        kbe = jnp.exp2(_tile_bcast(ch, S - 1) - ck).astype(BF16)
        a_pieces, b_pieces = [], []
        for j in range(nsub - 1):
            lo = (j + 1) * S
            a = (qh[lo:, :] * jnp.exp2(ch[lo:, :] - ch[lo - 1:lo, :])).astype(BF16)
            a_pieces.append(jnp.concatenate([jnp.zeros((lo, HGRN_DK), BF16), a], axis=0))
            b = [kbe[j * S:lo, :], jnp.zeros((C - lo, HGRN_DK), BF16)]
            if j > 0:
                b.insert(0, jnp.zeros((j * S, HGRN_DK), BF16))
            b_pieces.append(jnp.concatenate(b, axis=0))
        att = (_dot_nt(jnp.concatenate(a_pieces, axis=-1), jnp.concatenate(b_pieces, axis=-1))
               + jnp.where(diag_keep, _dot(x_cat, spread), 0.0))
        vb = v[:, blk].astype(BF16)
        s_prev = st_ref[h]
        o = _dot_nt((qh * jnp.exp2(ch)).astype(BF16), s_prev.astype(BF16)) + _dot(att.astype(BF16), vb)
        st_ref[h] = s_prev * jnp.exp2(last) + _dot_tn(vb, jnp.exp2(last - ck).astype(BF16))
        ms = jnp.mean(o * o, axis=-1, keepdims=True)
        outs.append(o * lax.rsqrt(ms + RMS_EPS))
    return jnp.concatenate(outs, axis=-1) * nw * _silu(g)


def _hgrn(q, f, i, g, lower_bound, norm_w):
    t = q.shape[0]
    C = HGRN_CHUNK * HGRN_PER_STEP
    tok = pl.BlockSpec((C, D_HGRN), lambda c: (c, 0))
    vec = pl.BlockSpec((1, D_HGRN), lambda c: (0, 0))
    return pl.pallas_call(
        _hgrn_kernel,
        grid=(t // C,),
        in_specs=[tok, tok, tok, tok, vec, vec],
        out_specs=tok,
        out_shape=jax.ShapeDtypeStruct((t, D_HGRN), F32),
        scratch_shapes=[pltpu.VMEM((HGRN_HEADS, D_HGRN // HGRN_HEADS, HGRN_DK), F32)],
        compiler_params=pltpu.CompilerParams(dimension_semantics=("arbitrary",)),
        name="hgrn",
    )(q, f, i, g, lower_bound.reshape(1, D_HGRN).astype(F32), norm_w.reshape(1, D_HGRN).astype(F32))


def _post_kernel(x_ref, ys_ref, yh_ref, wo_ref, g1_ref, b1_ref, wrh_ref, wrl_ref, rb_ref,
                 htt_ref, base_ref, idx_ref, gate_ref, cnt_ref):
    tm = x_ref.shape[0]
    mix = (_dot(ys_ref[...].astype(BF16), wo_ref[0:D_SSM, :])
           + _dot(yh_ref[...].astype(BF16), wo_ref[D_SSM:, :]))
    h1 = _layer_norm(ALPHA * x_ref[...] + mix, g1_ref[...], b1_ref[...])
    packed = _pack_pairs(h1[:, :D_MODEL // 2], h1[:, D_MODEL // 2:])
    for hh in range(ROW_HALVES):
        htt_ref[hh] = packed[:, hh * HALF_WORDS:(hh + 1) * HALF_WORDS]
    hb = h1.astype(BF16)
    base_ref[...] = ALPHA * h1

    hlo = (h1 - hb.astype(F32)).astype(BF16)
    logits = _dot_nt(wrh_ref[...], hb) + _dot_nt(wrh_ref[...], hlo) + _dot_nt(wrl_ref[...], hb)
    scores = _sigmoid(logits)
    biased = scores + rb_ref[...]
    per_group = N_EXPERTS // N_EXPERT_GROUPS
    eidx = lax.broadcasted_iota(I32, (N_EXPERTS, tm), 0)
    big = jnp.int32(1 << 20)
    gsc = []
    bidx = lax.broadcasted_iota(I32, (per_group, tm), 0)
    for gi in range(N_EXPERT_GROUPS):
        blk = biased[gi * per_group:(gi + 1) * per_group, :]
        m1 = jnp.max(blk, axis=0, keepdims=True)
        i1 = jnp.min(jnp.where(blk == m1, bidx, big), axis=0, keepdims=True)
        m2 = jnp.max(jnp.where(bidx == i1, NEG, blk), axis=0, keepdims=True)
        gsc.append(m1 + m2)
    cur = jnp.concatenate(gsc, axis=0)
    gidx = lax.broadcasted_iota(I32, (N_EXPERT_GROUPS, tm), 0)
    gsel = jnp.zeros((N_EXPERT_GROUPS, tm), F32)
    for _ in range(TOPK_GROUPS):
        m = jnp.max(cur, axis=0, keepdims=True)
        i = jnp.min(jnp.where(cur == m, gidx, big), axis=0, keepdims=True)
        hit = gidx == i
        gsel = jnp.where(hit, 1.0, gsel)
        cur = jnp.where(hit, NEG, cur)
    emask = jnp.concatenate(
        [jnp.broadcast_to(gsel[gi:gi + 1, :], (per_group, tm)) for gi in range(N_EXPERT_GROUPS)], axis=0)
    masked = jnp.where(emask > 0.0, biased, NEG)
    idx_rows, gate_rows = [], []
    for _ in range(TOP_K):
        m = jnp.max(masked, axis=0, keepdims=True)
        i = jnp.min(jnp.where(masked == m, eidx, big), axis=0, keepdims=True)
        hit = eidx == i
        idx_rows.append(i)
        gate_rows.append(jnp.sum(jnp.where(hit, scores, 0.0), axis=0, keepdims=True))
        masked = jnp.where(hit, NEG, masked)
    picked = jnp.where(jnp.logical_and(masked == NEG, emask > 0.0), 1.0, 0.0)
    gates = jnp.concatenate(gate_rows, axis=0)
    gates = gates / jnp.sum(gates, axis=0, keepdims=True) * ROUTED_SCALE
    idx_ref[0] = jnp.concatenate(idx_rows, axis=0)
    gate_ref[0] = gates

    @pl.when(pl.program_id(0) == 0)
    def _():
        cnt_ref[...] = jnp.zeros(cnt_ref.shape, F32)

    cnt_ref[...] += _dot(picked.astype(BF16), jnp.ones((tm, LANES), BF16))


def _post(x2, y_ssd, y_hgrn, wo, g1, b1, wr_hi, wr_lo, rbias):
    t = x2.shape[0]
    tm = TM_TOK
    nt = t // tm
    full = lambda shape: pl.BlockSpec(shape, lambda i: (0,) * len(shape))
    return pl.pallas_call(
        _post_kernel,
        grid=(nt,),
        in_specs=[pl.BlockSpec((tm, D_MODEL), lambda i: (i, 0)),
                  pl.BlockSpec((tm, D_SSM), lambda i: (i, 0)),
                  pl.BlockSpec((tm, D_HGRN), lambda i: (i, 0)),
                  full((D_MODEL, D_MODEL)), full((1, D_MODEL)), full((1, D_MODEL)),
                  full((N_EXPERTS, D_MODEL)), full((N_EXPERTS, D_MODEL)), full((N_EXPERTS, 1))],
        out_specs=[pl.BlockSpec((ROW_HALVES, tm, HALF_WORDS), lambda i: (0, i, 0)),
                   pl.BlockSpec((tm, D_MODEL), lambda i: (i, 0)),
                   pl.BlockSpec((1, TOP_K, tm), lambda i: (i, 0, 0)),
                   pl.BlockSpec((1, TOP_K, tm), lambda i: (i, 0, 0)),
                   pl.BlockSpec((N_EXPERTS, LANES), lambda i: (0, 0))],
        out_shape=[jax.ShapeDtypeStruct((ROW_HALVES, t, HALF_WORDS), U32),
                   jax.ShapeDtypeStruct((t, D_MODEL), F32),
                   jax.ShapeDtypeStruct((nt, TOP_K, tm), I32),
                   jax.ShapeDtypeStruct((nt, TOP_K, tm), F32),
                   jax.ShapeDtypeStruct((N_EXPERTS, LANES), F32)],
        compiler_params=pltpu.CompilerParams(dimension_semantics=("arbitrary",),
                                             vmem_limit_bytes=48 * 1024 * 1024),
        name="post",
    )(x2, y_ssd, y_hgrn, wo, g1, b1, wr_hi, wr_lo, rbias)


def _max_blocks(t):
    return (t * TOP_K + N_EXPERTS * (EXPERT_ROWS - 1)) // EXPERT_ROWS


def _pos_kernel(idx_ref, total_ref, dest_ref, bstart_ref, nblk_ref, cnt_ref, start_ref):
    i = pl.program_id(0)
    tm = idx_ref.shape[-1]
    idx = idx_ref[0]
    eidx = lax.broadcasted_iota(I32, (N_EXPERTS, tm), 0)
    sel = [eidx == idx[k:k + 1, :] for k in range(TOP_K)]
    onehot = sel[0]
    for k in range(1, TOP_K):
        onehot = jnp.logical_or(onehot, sel[k])
    mt = jnp.where(onehot, 1.0, 0.0).astype(BF16)

    @pl.when(i == 0)
    def _():
        nb = jnp.floor((total_ref[...] + (EXPERT_ROWS - 1)) * (1.0 / EXPERT_ROWS))
        r = lax.broadcasted_iota(I32, (N_EXPERTS, N_EXPERTS), 0)
        c = lax.broadcasted_iota(I32, (N_EXPERTS, N_EXPERTS), 1)
        end = _dot((r >= c).astype(BF16), nb.astype(BF16))
        start_ref[...] = (end - nb) * EXPERT_ROWS
        cnt_ref[...] = jnp.zeros(cnt_ref.shape, F32)
        bstart_ref[...] = (end - nb).astype(I32)
        nblk_ref[...] = nb.astype(I32)

    r = lax.broadcasted_iota(I32, (tm, tm), 0)
    c = lax.broadcasted_iota(I32, (tm, tm), 1)
    before = _dot(mt, (r < c).astype(BF16))
    slot = start_ref[:, 0:1] + cnt_ref[:, 0:1] + before
    rows = [jnp.sum(jnp.where(sel[k], slot, 0.0), axis=0, keepdims=True) for k in range(TOP_K)]
    dest_ref[0] = jnp.concatenate(rows, axis=0).astype(I32)
    cnt_ref[...] += _dot(mt, jnp.ones((tm, LANES), BF16))


def _positions(idx, totals):
    nt, _, tm = idx.shape
    return pl.pallas_call(
        _pos_kernel,
        grid=(nt,),
        in_specs=[pl.BlockSpec((1, TOP_K, tm), lambda i: (i, 0, 0)),
                  pl.BlockSpec((N_EXPERTS, LANES), lambda i: (0, 0))],
        out_specs=[pl.BlockSpec((1, TOP_K, tm), lambda i: (i, 0, 0)),
                   pl.BlockSpec((N_EXPERTS, LANES), lambda i: (0, 0)),
                   pl.BlockSpec((N_EXPERTS, LANES), lambda i: (0, 0))],
        out_shape=[jax.ShapeDtypeStruct((nt, TOP_K, tm), I32),
                   jax.ShapeDtypeStruct((N_EXPERTS, LANES), I32),
                   jax.ShapeDtypeStruct((N_EXPERTS, LANES), I32)],
        scratch_shapes=[pltpu.VMEM((N_EXPERTS, LANES), F32), pltpu.VMEM((N_EXPERTS, LANES), F32)],
        compiler_params=pltpu.CompilerParams(dimension_semantics=("arbitrary",)),
        name="positions",
    )(idx, totals)


def _sc_mesh():
    return plsc.VectorSubcoreMesh(core_axis_name="core", subcore_axis_name="subcore")


def _sc_dispatch(src, index, n_out):
    n = index.shape[0]
    src_blocks = src.shape[0] // SC_WINDOW
    index = index.reshape(1, n)

    @pl.kernel(out_type=jax.ShapeDtypeStruct((n_out, HALF_WORDS), src.dtype), mesh=_sc_mesh(), name="sc_dispatch")
    def scatter(src_hbm, idx_hbm, out_hbm):
        def body(rows_vmem, idx_vmem):
            pltpu.sync_copy(rows_vmem, out_hbm.at[idx_vmem.at[0]])

        pltpu.emit_pipeline(
            body, grid=(n // SC_WINDOW,),
            in_specs=[pl.BlockSpec((SC_WINDOW, HALF_WORDS), index_map=lambda i: (i % src_blocks, 0)),
                      pl.BlockSpec((1, SC_WINDOW), index_map=lambda i: (0, i))],
            out_specs=[],
            core_axis_name=("core", "subcore"), dimension_semantics=(pltpu.PARALLEL,))(src_hbm, idx_hbm)

    return scatter(src, index)


def _sc_gather(src, index):
    n = index.shape[0]
    index = index.reshape(1, n)

    @pl.kernel(out_type=jax.ShapeDtypeStruct((n, HALF_WORDS), src.dtype), mesh=_sc_mesh(), name="sc_gather")
    def gather(src_hbm, idx_hbm, out_hbm):
        def body(idx_vmem, rows_vmem):
            pltpu.sync_copy(src_hbm.at[idx_vmem.at[0]], rows_vmem)

        pltpu.emit_pipeline(
            body, grid=(n // SC_WINDOW,),
            in_specs=[pl.BlockSpec((1, SC_WINDOW), index_map=lambda i: (0, i))],
            out_specs=[pl.BlockSpec((SC_WINDOW, HALF_WORDS), index_map=lambda i: (i, 0))],
            core_axis_name=("core", "subcore"), dimension_semantics=(pltpu.PARALLEL,))(idx_hbm, out_hbm)

    return gather(src, index)


def _expert_kernel(bstart_ref, nblk_ref, xs_hbm, wg_ref, wu_ref, wd_ref, ys_hbm,
                   xbuf, ybuf, wgb_ref, wub_ref, wdb_ref, xsem, ysem, ypend):
    e = pl.program_id(0)
    last = pl.num_programs(0) - 1
    n = nblk_ref[e]
    b0 = bstart_ref[e]
    total = bstart_ref[last] + nblk_ref[last]
    blk_rows = EXPERT_ROWS * ROW_HALVES
    xdepth = xbuf.shape[0]
    ydepth = ybuf.shape[0]
    lookahead = xdepth - MAX_CHUNK

    def hbm_block(ref, b):
        return ref.at[pl.ds(pl.multiple_of(b * blk_rows, blk_rows), blk_rows)]

    def x_copy(b):
        slot = b % xdepth
        return pltpu.make_async_copy(hbm_block(xs_hbm, b), xbuf.at[slot], xsem.at[slot])

    def y_copy(b, slot):
        return pltpu.make_async_copy(ybuf.at[slot], hbm_block(ys_hbm, b), ysem.at[slot])

    @pl.when(e == 0)
    def _():
        for s in range(ydepth):
            ypend[s] = 0
        for b in range(lookahead):
            @pl.when(b < total)
            def _():
                x_copy(b).start()

    def chunk(j, m):
        b = b0 + j
        for i in range(m):
            @pl.when(b + lookahead + i < total)
            def _():
                x_copy(b + lookahead + i).start()

        for i in range(m):
            x_copy(b + i).wait()

            @pl.when(ypend[(b + i) % ydepth] == 1)
            def _():
                y_copy(b + i, (b + i) % ydepth).wait()

        for i in range(m):
            xslot = (b + i) % xdepth
            yslot = (b + i) % ydepth
            halves = [_unpack_pairs(xbuf[xslot, hh * EXPERT_ROWS:(hh + 1) * EXPERT_ROWS, :])
                      for hh in range(ROW_HALVES)]
            xb = jnp.concatenate([lo for lo, _ in halves] + [hi for _, hi in halves], axis=-1).astype(BF16)
            hid = _silu(_dot(xb, wgb_ref[...])) * _dot(xb, wub_ref[...])
            yb = _dot(hid.astype(BF16), wdb_ref[...])
            packed = _pack_pairs(yb[:, :D_MODEL // 2], yb[:, D_MODEL // 2:])
            for hh in range(ROW_HALVES):
                ybuf[yslot, hh * EXPERT_ROWS:(hh + 1) * EXPERT_ROWS, :] = (
                    packed[:, hh * HALF_WORDS:(hh + 1) * HALF_WORDS])

        for i in range(m):
            y_copy(b + i, (b + i) % ydepth).start()
            ypend[(b + i) % ydepth] = 1

    @pl.when(n > 0)
    def _():
        wgb_ref[...] = wg_ref[0].astype(BF16)
        wub_ref[...] = wu_ref[0].astype(BF16)
        wdb_ref[...] = wd_ref[0].astype(BF16)
        full = n // MAX_CHUNK

        def full_chunk(q, carry):
            chunk(q * MAX_CHUNK, MAX_CHUNK)
            return carry

        lax.fori_loop(0, full, full_chunk, 0)
        for m in range(1, MAX_CHUNK):
            @pl.when(n - full * MAX_CHUNK == m)
            def _():
                chunk(full * MAX_CHUNK, m)

    @pl.when(e == last)
    def _():
        for s in range(ydepth):
            @pl.when(ypend[s] == 1)
            def _():
                y_copy(b0, s).wait()


def _experts(bstart, nblk, xs, w_gate, w_up, w_down):
    blk_shape = (EXPERT_ROWS * ROW_HALVES, HALF_WORDS)
    wspec = lambda shape: pl.BlockSpec((1,) + shape, lambda e, bs, nb: (e, 0, 0))
    grid_spec = pltpu.PrefetchScalarGridSpec(
        num_scalar_prefetch=2,
        grid=(N_EXPERTS,),
        in_specs=[pl.BlockSpec(memory_space=pl.ANY),
                  wspec((D_MODEL, D_EXPERT)), wspec((D_MODEL, D_EXPERT)), wspec((D_EXPERT, D_MODEL))],
        out_specs=pl.BlockSpec(memory_space=pl.ANY),
        scratch_shapes=[pltpu.VMEM((X_RING,) + blk_shape, U32), pltpu.VMEM((Y_RING,) + blk_shape, U32),
                        pltpu.VMEM((D_MODEL, D_EXPERT), BF16), pltpu.VMEM((D_MODEL, D_EXPERT), BF16),
                        pltpu.VMEM((D_EXPERT, D_MODEL), BF16),
                        pltpu.SemaphoreType.DMA((X_RING,)), pltpu.SemaphoreType.DMA((Y_RING,)),
                        pltpu.SMEM((Y_RING,), I32)],
    )
    return pl.pallas_call(
        _expert_kernel,
        grid_spec=grid_spec,
        out_shape=jax.ShapeDtypeStruct(xs.shape, U32),
        compiler_params=pltpu.CompilerParams(dimension_semantics=("arbitrary",), has_side_effects=True),
        name="experts",
    )(bstart, nblk, xs, w_gate, w_up, w_down)


def _combine_kernel(gate_ref, base_ref, g2_ref, b2_ref, htt_ref, wg_ref, wu_ref, wd_ref, rows_ref, o_ref):
    tm = base_ref.shape[0]
    halves = [_unpack_pairs(htt_ref[hh]) for hh in range(ROW_HALVES)]
    hb = jnp.concatenate([lo for lo, _ in halves] + [hi for _, hi in halves], axis=-1).astype(BF16)
    hid = _silu(_dot(hb, wg_ref[...])) * _dot(hb, wu_ref[...])
    acc = base_ref[...] + _dot(hid.astype(BF16), wd_ref[...])
    gates_t = jnp.concatenate([gate_ref[0], jnp.zeros((LANES - TOP_K, tm), F32)], axis=0).T
    half = D_MODEL // 2
    lo_acc = [acc[:, hh * HALF_WORDS:(hh + 1) * HALF_WORDS] for hh in range(ROW_HALVES)]
    hi_acc = [acc[:, half + hh * HALF_WORDS:half + (hh + 1) * HALF_WORDS] for hh in range(ROW_HALVES)]
    for k in range(TOP_K):
        gk = gates_t[:, k:k + 1]
        for hh in range(ROW_HALVES):
            lo, hi = _unpack_pairs(rows_ref[k, hh])
            lo_acc[hh] = lo_acc[hh] + gk * lo
            hi_acc[hh] = hi_acc[hh] + gk * hi
    o_ref[...] = _layer_norm(jnp.concatenate(lo_acc + hi_acc, axis=-1), g2_ref[...], b2_ref[...])


def _combine_piece_kernel(*refs):
    _combine_kernel(*refs[:-2], refs[-1])


def _combine(gates, base, g2, b2, htt, wgs, wus, wds, rows, first_tile, prev):
    t = base.shape[0]
    tm = gates.shape[-1]
    tiles = rows.shape[2] // tm
    full = lambda shape: pl.BlockSpec(shape, lambda i: (0,) * len(shape))
    in_specs = [pl.BlockSpec((1, TOP_K, tm), lambda i: (i + first_tile, 0, 0)),
                pl.BlockSpec((tm, D_MODEL), lambda i: (i + first_tile, 0)),
                full((1, D_MODEL)), full((1, D_MODEL)),
                pl.BlockSpec((ROW_HALVES, tm, HALF_WORDS), lambda i: (0, i + first_tile, 0)),
                full((D_MODEL, D_EXPERT)), full((D_MODEL, D_EXPERT)), full((D_EXPERT, D_MODEL)),
                pl.BlockSpec((TOP_K, ROW_HALVES, tm, HALF_WORDS), lambda i: (0, 0, i, 0))]
    args = [gates, base, g2, b2, htt, wgs, wus, wds, rows]
    if prev is not None:
        in_specs.append(pl.BlockSpec(memory_space=pl.ANY))
        args.append(prev)
    return pl.pallas_call(
        _combine_kernel if prev is None else _combine_piece_kernel,
        grid=(tiles,),
        in_specs=in_specs,
        out_specs=pl.BlockSpec((tm, D_MODEL), lambda i: (i + first_tile, 0)),
        out_shape=jax.ShapeDtypeStruct((t, D_MODEL), F32),
        input_output_aliases={} if prev is None else {len(args) - 1: 0},
        compiler_params=pltpu.CompilerParams(dimension_semantics=("arbitrary",),
                                             vmem_limit_bytes=48 * 1024 * 1024),
        name="combine",
    )(*args)


def _layer(h2, w_in, conv_w, conv_b, dt_bias, a_log, d_skip, ssd_norm_w, lower_bound, hgrn_norm_w, w_out,
           ln1_g, ln1_b, w_router, router_bias, w_gate_e, w_up_e, w_down_e, w_gate_s, w_up_s, w_down_s,
           ln2_g, ln2_b):
    t = h2.shape[0]
    dt0 = D_SSM + D_CONV
    q0 = dt0 + SSD_HEADS
    w_perm = jnp.concatenate(
        [w_in[:, :dt0], w_in[:, q0:], w_in[:, dt0:q0], jnp.zeros((D_MODEL, DT_PAD - SSD_HEADS), w_in.dtype)],
        axis=1).astype(BF16)
    z, xbc, q, f, i, g, dt = _inproj(h2, w_perm)
    y_ssd = _ssd(z, xbc, dt, conv_w, conv_b, dt_bias, a_log, d_skip, ssd_norm_w)
    y_hgrn = _hgrn(q, f, i, g, lower_bound, hgrn_norm_w)

    wr_t = w_router.astype(F32).T
    wr_hi = wr_t.astype(BF16)
    wr_lo = (wr_t - wr_hi.astype(F32)).astype(BF16)
    row = lambda v: v.reshape(1, -1).astype(F32)
    htt, base, idx, gates, totals = _post(
        h2, y_ssd, y_hgrn, w_out.astype(BF16), row(ln1_g), row(ln1_b), wr_hi, wr_lo,
        router_bias.reshape(N_EXPERTS, 1).astype(F32))

    dest, bstart, nblk = _positions(idx, totals)
    n_rows = _max_blocks(t) * EXPERT_ROWS
    d = jnp.transpose(dest, (1, 0, 2)).reshape(TOP_K, 1, t)
    slot = (d // EXPERT_ROWS) * (EXPERT_ROWS * ROW_HALVES) + d % EXPERT_ROWS
    slot = (slot + jnp.arange(ROW_HALVES, dtype=I32).reshape(1, ROW_HALVES, 1) * EXPERT_ROWS).reshape(-1)
    xs = _sc_dispatch(htt.reshape(ROW_HALVES * t, HALF_WORDS), slot, n_rows * ROW_HALVES)
    ys = _experts(bstart[:, 0], nblk[:, 0], xs, w_gate_e, w_up_e, w_down_e)
    slot = slot.reshape(TOP_K, ROW_HALVES, t)
    tp = t // GATHER_PIECES
    shared_w = (w_gate_s.astype(BF16), w_up_s.astype(BF16), w_down_s.astype(BF16))
    out = None
    for p in range(GATHER_PIECES):
        rows = _sc_gather(ys, slot[:, :, p * tp:(p + 1) * tp].reshape(-1))
        out = _combine(gates, base, row(ln2_g), row(ln2_b), htt, *shared_w,
                       rows.reshape(TOP_K, ROW_HALVES, tp, HALF_WORDS), p * (tp // TM_TOK), out)
    return out


def kernel(x, w_in, conv_w, conv_b, dt_bias, a_log, d_skip, ssd_norm_w, hgrn_lb_logits, hgrn_norm_w, w_out,
           ln1_g, ln1_b, w_router, router_bias, w_gate_e, w_up_e, w_down_e, w_gate_s, w_up_s, w_down_s,
           ln2_g, ln2_b):
    bsz, t, d = x.shape
    assert bsz == 1 and d == D_MODEL, "the recurrent mixers carry state across the flattened token axis"
    depth = w_in.shape[0]
    lower_bounds = jnp.cumsum(jax.nn.softmax(hgrn_lb_logits.astype(F32), axis=0), axis=0)
    h = x.reshape(bsz * t, d)
    for l in range(depth):
        h = _layer(h, w_in[l], conv_w[l], conv_b[l], dt_bias[l], a_log[l], d_skip[l], ssd_norm_w[l],
                   lower_bounds[l], hgrn_norm_w[l], w_out[l], ln1_g[l], ln1_b[l], w_router[l],
                   router_bias[l], w_gate_e[l], w_up_e[l], w_down_e[l], w_gate_s[l], w_up_s[l],
                   w_down_s[l], ln2_g[l], ln2_b[l])
    return h.reshape(bsz, t, d)
```

```python
import jax
import jax.numpy as jnp
from jax import lax
from jax.experimental import pallas as pl
from jax.experimental.pallas import tpu as pltpu
from jax.experimental.pallas import tpu_sc as plsc

F32 = jnp.float32
BF16 = jnp.bfloat16
I32 = jnp.int32
U32 = jnp.uint32

D_MODEL = 1024
D_SSM = 512
D_HGRN = 512
SSD_HEADS = 8
SSD_HEAD_DIM = 64
SSD_GROUPS = 2
SSD_STATE = 128
SSD_CONV = 4
SSD_CHUNK = 128
SSD_PER_STEP = 8
D_CONV = D_SSM + 2 * SSD_GROUPS * SSD_STATE
HGRN_HEADS = 4
HGRN_DK = 128
HGRN_CHUNK = 64
HGRN_SUB = 8
HGRN_PER_STEP = 16
N_EXPERTS = 256
TOP_K = 8
N_EXPERT_GROUPS = 8
TOPK_GROUPS = 4
D_EXPERT = 256
ROUTED_SCALE = 2.5
ALPHA = 2.0 ** 0.25
LN_EPS = 1e-5
RMS_EPS = 1e-6

LANES = 128
SUBLANES = 8
ROW_TILES = D_MODEL // LANES
ROW_HALVES = 2
HALF_WORDS = D_MODEL // 2 // ROW_HALVES
SC_WINDOW = 128
GATHER_PIECES = 8
DT_PAD = LANES
N_IN_PAD = D_SSM + D_CONV + 4 * D_HGRN + DT_PAD

TM_PROJ = 512
TM_POST = 1024
TM_COMBINE = 512
EXPERT_ROWS = 256
MAX_CHUNK = 2
X_RING = 8
Y_RING = 8
NEG = -1e30


def _sigmoid(x):
    return 1.0 / (1.0 + jnp.exp(-x))


def _silu(x):
    return x * _sigmoid(x)


def _split3(x):
    hi = x.astype(BF16)
    r = x - hi.astype(F32)
    mid = r.astype(BF16)
    lo = (r - mid.astype(F32)).astype(BF16)
    return hi, mid, lo


def _dot(a, b):
    return jnp.dot(a, b, preferred_element_type=F32)


def _dot_nt(a, b):
    return lax.dot_general(a, b, (((1,), (1,)), ((), ())), preferred_element_type=F32)


def _dot_tn(a, b):
    return lax.dot_general(a, b, (((0,), (0,)), ((), ())), preferred_element_type=F32)


def _sel_dot(sel, x):
    hi, mid, lo = _split3(x)
    return _dot(sel, hi) + _dot(sel, mid) + _dot(sel, lo)


def _dot_sel(x, sel):
    hi, mid, lo = _split3(x)
    return _dot(hi, sel) + _dot(mid, sel) + _dot(lo, sel)


def _pack_pairs(lo, hi):
    lo_bits = pltpu.bitcast(lo.astype(BF16).astype(F32), U32) >> 16
    hi_bits = pltpu.bitcast(hi.astype(BF16).astype(F32), U32) & jnp.uint32(0xFFFF0000)
    return hi_bits | lo_bits


def _unpack_pairs(p):
    return pltpu.bitcast(p << 16, F32), pltpu.bitcast(p & jnp.uint32(0xFFFF0000), F32)


def _layer_norm(x, g, b):
    mu = jnp.mean(x, axis=-1, keepdims=True)
    xc = x - mu
    var = jnp.mean(xc * xc, axis=-1, keepdims=True)
    return xc * lax.rsqrt(var + LN_EPS) * g + b


def _inproj_kernel(x_ref, w_ref, z_ref, xbc_ref, q_ref, f_ref, i_ref, g_ref, dt_ref):
    xb = x_ref[...].astype(BF16)
    col = 0
    for ref in (z_ref, xbc_ref, q_ref, f_ref, i_ref, g_ref, dt_ref):
        n = ref.shape[-1]
        ref[...] = _dot(xb, w_ref[:, col:col + n])
        col += n


def _inproj(x2, w_perm):
    t = x2.shape[0]
    widths = (D_SSM, D_CONV, D_HGRN, D_HGRN, D_HGRN, D_HGRN, DT_PAD)
    return pl.pallas_call(
        _inproj_kernel,
        grid=(t // TM_PROJ,),
        in_specs=[pl.BlockSpec((TM_PROJ, D_MODEL), lambda i: (i, 0)),
                  pl.BlockSpec((D_MODEL, N_IN_PAD), lambda i: (0, 0))],
        out_specs=[pl.BlockSpec((TM_PROJ, n), lambda i: (i, 0)) for n in widths],
        out_shape=[jax.ShapeDtypeStruct((t, n), F32) for n in widths],
        compiler_params=pltpu.CompilerParams(dimension_semantics=("arbitrary",),
                                             vmem_limit_bytes=48 * 1024 * 1024),
        name="inproj",
    )(x2, w_perm)


def _ssd_kernel(z_ref, xbc_ref, dt_ref, cw_ref, cb_ref, dtb_ref, alog_ref, dskip_ref, nw_ref,
                y_ref, ext_ref, st_ref):
    n = xbc_ref.shape[0]
    halo = SUBLANES

    @pl.when(pl.program_id(0) == 0)
    def _():
        ext_ref[0:halo, :] = jnp.zeros((halo, D_CONV), F32)
        st_ref[...] = jnp.zeros(st_ref.shape, F32)

    ext_ref[halo:halo + n, :] = xbc_ref[...]
    acc = jnp.broadcast_to(cb_ref[...], (n, D_CONV))
    for k in range(SSD_CONV):
        off = halo - (SSD_CONV - 1) + k
        acc = acc + cw_ref[k:k + 1, :] * ext_ref[off:off + n, :]
    ext_ref[0:halo, :] = xbc_ref[n - halo:n, :]
    u = _silu(acc)
    for c in range(n // SSD_CHUNK):
        rows = slice(c * SSD_CHUNK, (c + 1) * SSD_CHUNK)
        y_ref[rows, :] = _ssd_chunk(u[rows, :], z_ref[rows, :], dt_ref[rows, :] + dtb_ref[...],
                                    -jnp.exp(alog_ref[...]), dskip_ref[...], nw_ref[...], st_ref)


def _ssd_chunk(u, z, draw, a, dskip, nw, st_ref):
    L = SSD_CHUNK
    xs = u[:, :D_SSM]
    bm = u[:, D_SSM:D_SSM + SSD_GROUPS * SSD_STATE]
    cm = u[:, D_SSM + SSD_GROUPS * SSD_STATE:]

    dt = jnp.maximum(draw, 0.0) + jnp.log(1.0 + jnp.exp(-jnp.abs(draw)))
    ad = dt * a
    rows = lax.broadcasted_iota(I32, (L, L), 0)
    cols = lax.broadcasted_iota(I32, (L, L), 1)
    causal = rows >= cols
    a_cum = _sel_dot(causal.astype(BF16), ad)
    a_cum_t = a_cum.T

    hrow = lax.broadcasted_iota(I32, (LANES, D_SSM), 0)
    hcol = lax.broadcasted_iota(I32, (LANES, D_SSM), 1) // SSD_HEAD_DIM
    expand = (hrow == hcol).astype(BF16)
    dt_x = _dot_sel(dt, expand)
    acx = _dot_sel(a_cum, expand)
    last = acx[L - 1:L, :]
    ea_x = jnp.exp(acx)
    dec_x = jnp.exp(last - acx)
    elast_x = jnp.exp(last)

    xdt = xs * dt_x
    gw = SSD_HEADS // SSD_GROUPS * SSD_HEAD_DIM
    lane_head = lax.broadcasted_iota(I32, (L, gw), 1) // SSD_HEAD_DIM
    ys = []
    for g in range(SSD_GROUPS):
        bg = bm[:, g * SSD_STATE:(g + 1) * SSD_STATE]
        cg = cm[:, g * SSD_STATE:(g + 1) * SSD_STATE].astype(BF16)
        bg_t = bg.T.astype(BF16)
        gmat = _dot(cg, bg_t)
        xdt_g = xdt[:, g * gw:(g + 1) * gw]
        xdt_gb = xdt_g.astype(BF16)
        r_prev = st_ref[g]
        y_g = _dot(cg, r_prev.astype(BF16)) * ea_x[:, g * gw:(g + 1) * gw]
        new_s = _dot(bg_t, (xdt_g * dec_x[:, g * gw:(g + 1) * gw]).astype(BF16))
        st_ref[g] = r_prev * elast_x[:, g * gw:(g + 1) * gw] + new_s
        for j in range(SSD_HEADS // SSD_GROUPS):
            h = g * (SSD_HEADS // SSD_GROUPS) + j
            diff = a_cum[:, h:h + 1] - a_cum_t[h:h + 1, :]
            decay = jnp.exp(jnp.where(causal, diff, NEG))
            yd = _dot((gmat * decay).astype(BF16), xdt_gb)
            y_g = y_g + jnp.where(lane_head == j, yd, 0.0)
        ys.append(y_g)
    y = jnp.concatenate(ys, axis=-1) + xs * dskip
    y = y * _silu(z)
    outs = []
    ng = D_SSM // SSD_GROUPS
    for g in range(SSD_GROUPS):
        yg = y[:, g * ng:(g + 1) * ng]
        ms = jnp.mean(yg * yg, axis=-1, keepdims=True)
        outs.append(yg * lax.rsqrt(ms + RMS_EPS))
    return jnp.concatenate(outs, axis=-1) * nw


def _ssd(z, xbc, dt, conv_w, conv_b, dt_bias, a_log, d_skip, norm_w):
    t = z.shape[0]
    L = SSD_CHUNK * SSD_PER_STEP
    pad = lambda v: jnp.pad(v.astype(F32), (0, LANES - v.shape[0])).reshape(1, LANES)
    full = lambda shape: pl.BlockSpec(shape, lambda c: (0,) * len(shape))
    return pl.pallas_call(
        _ssd_kernel,
        grid=(t // L,),
        in_specs=[pl.BlockSpec((L, D_SSM), lambda c: (c, 0)),
                  pl.BlockSpec((L, D_CONV), lambda c: (c, 0)),
                  pl.BlockSpec((L, DT_PAD), lambda c: (c, 0)),
                  full((SSD_CONV, D_CONV)), full((1, D_CONV)), full((1, LANES)), full((1, LANES)),
                  full((1, D_SSM)), full((1, D_SSM))],
        out_specs=pl.BlockSpec((L, D_SSM), lambda c: (c, 0)),
        out_shape=jax.ShapeDtypeStruct((t, D_SSM), F32),
        scratch_shapes=[pltpu.VMEM((L + SUBLANES, D_CONV), F32),
                        pltpu.VMEM((SSD_GROUPS, SSD_STATE, D_SSM // SSD_GROUPS), F32)],
        compiler_params=pltpu.CompilerParams(dimension_semantics=("arbitrary",)),
        name="ssd",
    )(z, xbc, dt, conv_w.astype(F32), conv_b.reshape(1, D_CONV).astype(F32), pad(dt_bias), pad(a_log),
      jnp.repeat(d_skip.astype(F32), SSD_HEAD_DIM).reshape(1, D_SSM), norm_w.reshape(1, D_SSM).astype(F32))


def _tile_bcast(x, r):
    n, d = x.shape
    x3 = x.reshape(n // SUBLANES, SUBLANES, d)
    return jnp.broadcast_to(x3[:, r:r + 1, :], x3.shape).reshape(n, d)


def _hgrn_kernel(q_ref, f_ref, i_ref, g_ref, lb_ref, nw_ref, o_ref, st_ref):
    @pl.when(pl.program_id(0) == 0)
    def _():
        st_ref[...] = jnp.zeros(st_ref.shape, F32)

    for c in range(o_ref.shape[0] // HGRN_CHUNK):
        rows = slice(c * HGRN_CHUNK, (c + 1) * HGRN_CHUNK)
        o_ref[rows, :] = _hgrn_chunk(q_ref[rows, :], f_ref[rows, :], i_ref[rows, :], g_ref[rows, :],
                                     lb_ref[...], nw_ref[...], st_ref)


def _hgrn_chunk(q, f, v, g, lb, nw, st_ref):
    C = HGRN_CHUNK
    S = HGRN_SUB
    nsub = C // S
    fg = lb + (1.0 - lb) * _sigmoid(f)
    qs = _silu(q)
    rows = lax.broadcasted_iota(I32, (C, C), 0)
    cols = lax.broadcasted_iota(I32, (C, C), 1)
    cum = _sel_dot((rows >= cols).astype(BF16), jnp.log2(fg))
    ck_all = cum - jnp.log2(1.0 - fg)

    srow = lax.broadcasted_iota(I32, (S * HGRN_DK, C), 0) // HGRN_DK
    scol = lax.broadcasted_iota(I32, (S * HGRN_DK, C), 1) % S
    spread = (srow == scol).astype(BF16)
    diag_keep = jnp.logical_and(rows // S == cols // S, rows >= cols)
    outs = []
    for h in range(HGRN_HEADS):
        blk = slice(h * HGRN_DK, (h + 1) * HGRN_DK)
        qh, ch, ck = qs[:, blk], cum[:, blk], ck_all[:, blk]
        last = ch[C - 1:C, :]
        x_cat = jnp.concatenate(
            [(qh * jnp.exp2(jnp.minimum(ch - _tile_bcast(ck, r), 0.0))).astype(BF16) for r in range(S)], axis=-1)
        kbe = jnp.exp2(_tile_bcast(ch, S - 1) - ck).astype(BF16)
        a_pieces, b_pieces = [], []
        for j in range(nsub - 1):
            lo = (j + 1) * S
            a = (qh[lo:, :] * jnp.exp2(ch[lo:, :] - ch[lo - 1:lo, :])).astype(BF16)
            a_pieces.append(jnp.concatenate([jnp.zeros((lo, HGRN_DK), BF16), a], axis=0))
            b = [kbe[j * S:lo, :], jnp.zeros((C - lo, HGRN_DK), BF16)]
            if j > 0:
                b.insert(0, jnp.zeros((j * S, HGRN_DK), BF16))
            b_pieces.append(jnp.concatenate(b, axis=0))
        att = (_dot_nt(jnp.concatenate(a_pieces, axis=-1), jnp.concatenate(b_pieces, axis=-1))
               + jnp.where(diag_keep, _dot(x_cat, spread), 0.0))
        vb = v[:, blk].astype(BF16)
        s_prev = st_ref[h]
        o = _dot_nt((qh * jnp.exp2(ch)).astype(BF16), s_prev.astype(BF16)) + _dot(att.astype(BF16), vb)
        st_ref[h] = s_prev * jnp.exp2(last) + _dot_tn(vb, jnp.exp2(last - ck).astype(BF16))
        ms = jnp.mean(o * o, axis=-1, keepdims=True)
        outs.append(o * lax.rsqrt(ms + RMS_EPS))
    return jnp.concatenate(outs, axis=-1) * nw * _silu(g)


def _hgrn(q, f, i, g, lower_bound, norm_w):
    t = q.shape[0]
    C = HGRN_CHUNK * HGRN_PER_STEP
    tok = pl.BlockSpec((C, D_HGRN), lambda c: (c, 0))
    vec = pl.BlockSpec((1, D_HGRN), lambda c: (0, 0))
    return pl.pallas_call(
        _hgrn_kernel,
        grid=(t // C,),
        in_specs=[tok, tok, tok, tok, vec, vec],
        out_specs=tok,
        out_shape=jax.ShapeDtypeStruct((t, D_HGRN), F32),
        scratch_shapes=[pltpu.VMEM((HGRN_HEADS, D_HGRN // HGRN_HEADS, HGRN_DK), F32)],
        compiler_params=pltpu.CompilerParams(dimension_semantics=("arbitrary",)),
        name="hgrn",
    )(q, f, i, g, lower_bound.reshape(1, D_HGRN).astype(F32), norm_w.reshape(1, D_HGRN).astype(F32))


def _post_kernel(x_ref, ys_ref, yh_ref, wo_ref, g1_ref, b1_ref, wr_ref, rb_ref,
                 htt_ref, base_ref, idx_ref, gate_ref, cnt_ref, wrh_ref, wrl_ref):
    @pl.when(pl.program_id(0) == 0)
    def _():
        wr_t = wr_ref[...].T
        hi = wr_t.astype(BF16)
        wrh_ref[...] = hi
        wrl_ref[...] = (wr_t - hi.astype(F32)).astype(BF16)

    tm = x_ref.shape[0]
    mix = (_dot(ys_ref[...].astype(BF16), wo_ref[0:D_SSM, :])
           + _dot(yh_ref[...].astype(BF16), wo_ref[D_SSM:, :]))
    h1 = _layer_norm(ALPHA * x_ref[...] + mix, g1_ref[...], b1_ref[...])
    packed = _pack_pairs(h1[:, :D_MODEL // 2], h1[:, D_MODEL // 2:])
    for hh in range(ROW_HALVES):
        htt_ref[hh] = packed[:, hh * HALF_WORDS:(hh + 1) * HALF_WORDS]
    hb = h1.astype(BF16)
    base_ref[...] = ALPHA * h1

    hlo = (h1 - hb.astype(F32)).astype(BF16)
    logits = _dot_nt(wrh_ref[...], hb) + _dot_nt(wrh_ref[...], hlo) + _dot_nt(wrl_ref[...], hb)
    scores = _sigmoid(logits)
    biased = scores + rb_ref[...]
    per_group = N_EXPERTS // N_EXPERT_GROUPS
    eidx = lax.broadcasted_iota(I32, (N_EXPERTS, tm), 0)
    big = jnp.int32(1 << 20)
    gsc = []
    bidx = lax.broadcasted_iota(I32, (per_group, tm), 0)
    for gi in range(N_EXPERT_GROUPS):
        blk = biased[gi * per_group:(gi + 1) * per_group, :]
        m1 = jnp.max(blk, axis=0, keepdims=True)
        i1 = jnp.min(jnp.where(blk == m1, bidx, big), axis=0, keepdims=True)
        m2 = jnp.max(jnp.where(bidx == i1, NEG, blk), axis=0, keepdims=True)
        gsc.append(m1 + m2)
    cur = jnp.concatenate(gsc, axis=0)
    gidx = lax.broadcasted_iota(I32, (N_EXPERT_GROUPS, tm), 0)
    gsel = jnp.zeros((N_EXPERT_GROUPS, tm), F32)
    for _ in range(TOPK_GROUPS):
        m = jnp.max(cur, axis=0, keepdims=True)
        i = jnp.min(jnp.where(cur == m, gidx, big), axis=0, keepdims=True)
        hit = gidx == i
        gsel = jnp.where(hit, 1.0, gsel)
        cur = jnp.where(hit, NEG, cur)
    emask = jnp.concatenate(
        [jnp.broadcast_to(gsel[gi:gi + 1, :], (per_group, tm)) for gi in range(N_EXPERT_GROUPS)], axis=0)
    masked = jnp.where(emask > 0.0, biased, NEG)
    idx_rows, gate_rows = [], []
    for _ in range(TOP_K):
        m = jnp.max(masked, axis=0, keepdims=True)
        i = jnp.min(jnp.where(masked == m, eidx, big), axis=0, keepdims=True)
        hit = eidx == i
        idx_rows.append(i)
        gate_rows.append(jnp.sum(jnp.where(hit, scores, 0.0), axis=0, keepdims=True))
        masked = jnp.where(hit, NEG, masked)
    picked = jnp.where(jnp.logical_and(masked == NEG, emask > 0.0), 1.0, 0.0)
    gates = jnp.concatenate(gate_rows, axis=0)
    gates = gates / jnp.sum(gates, axis=0, keepdims=True) * ROUTED_SCALE
    idx_ref[...] = jnp.concatenate(idx_rows, axis=0)
    gate_ref[...] = gates

    @pl.when(pl.program_id(0) == 0)
    def _():
        cnt_ref[...] = jnp.zeros(cnt_ref.shape, F32)

    cnt_ref[...] += _dot(picked.astype(BF16), jnp.ones((tm, LANES), BF16))


def _post(x2, y_ssd, y_hgrn, wo, g1, b1, w_router, rbias):
    t = x2.shape[0]
    tm = TM_POST
    full = lambda shape: pl.BlockSpec(shape, lambda i: (0,) * len(shape))
    return pl.pallas_call(
        _post_kernel,
        grid=(t // tm,),
        in_specs=[pl.BlockSpec((tm, D_MODEL), lambda i: (i, 0)),
                  pl.BlockSpec((tm, D_SSM), lambda i: (i, 0)),
                  pl.BlockSpec((tm, D_HGRN), lambda i: (i, 0)),
                  full((D_MODEL, D_MODEL)), full((1, D_MODEL)), full((1, D_MODEL)),
                  full((D_MODEL, N_EXPERTS)), full((N_EXPERTS, 1))],
        out_specs=[pl.BlockSpec((ROW_HALVES, tm, HALF_WORDS), lambda i: (0, i, 0)),
                   pl.BlockSpec((tm, D_MODEL), lambda i: (i, 0)),
                   pl.BlockSpec((TOP_K, tm), lambda i: (0, i)),
                   pl.BlockSpec((TOP_K, tm), lambda i: (0, i)),
                   pl.BlockSpec((N_EXPERTS, LANES), lambda i: (0, 0))],
        out_shape=[jax.ShapeDtypeStruct((ROW_HALVES, t, HALF_WORDS), U32),
                   jax.ShapeDtypeStruct((t, D_MODEL), F32),
                   jax.ShapeDtypeStruct((TOP_K, t), I32),
                   jax.ShapeDtypeStruct((TOP_K, t), F32),
                   jax.ShapeDtypeStruct((N_EXPERTS, LANES), F32)],
        scratch_shapes=[pltpu.VMEM((N_EXPERTS, D_MODEL), BF16), pltpu.VMEM((N_EXPERTS, D_MODEL), BF16)],
        compiler_params=pltpu.CompilerParams(dimension_semantics=("arbitrary",),
                                             vmem_limit_bytes=60 * 1024 * 1024),
        name="post",
    )(x2, y_ssd, y_hgrn, wo, g1, b1, w_router, rbias)


def _max_blocks(t):
    return (t * TOP_K + N_EXPERTS * (EXPERT_ROWS - 1)) // EXPERT_ROWS


def _pos_kernel(idx_ref, total_ref, slot_ref, bstart_ref, nblk_ref, cnt_ref, start_ref):
    i = pl.program_id(0)
    tm = idx_ref.shape[-1]
    idx = idx_ref[...]
    eidx = lax.broadcasted_iota(I32, (N_EXPERTS, tm), 0)
    sel = [eidx == idx[k:k + 1, :] for k in range(TOP_K)]
    onehot = sel[0]
    for k in range(1, TOP_K):
        onehot = jnp.logical_or(onehot, sel[k])
    mt = jnp.where(onehot, 1.0, 0.0).astype(BF16)

    @pl.when(i == 0)
    def _():
        nb = jnp.floor((total_ref[...] + (EXPERT_ROWS - 1)) * (1.0 / EXPERT_ROWS))
        r = lax.broadcasted_iota(I32, (N_EXPERTS, N_EXPERTS), 0)
        c = lax.broadcasted_iota(I32, (N_EXPERTS, N_EXPERTS), 1)
        end = _dot((r >= c).astype(BF16), nb.astype(BF16))
        start_ref[...] = (end - nb) * EXPERT_ROWS
        cnt_ref[...] = jnp.zeros(cnt_ref.shape, F32)
        bstart_ref[...] = (end - nb).astype(I32)
        nblk_ref[...] = nb.astype(I32)

    r = lax.broadcasted_iota(I32, (tm, tm), 0)
    c = lax.broadcasted_iota(I32, (tm, tm), 1)
    before = _dot(mt, (r < c).astype(BF16))
    sorted_row = start_ref[:, 0:1] + cnt_ref[:, 0:1] + before
    rows = []
    for k in range(TOP_K):
        dest = jnp.sum(jnp.where(sel[k], sorted_row, 0.0), axis=0, keepdims=True).astype(I32)
        first = (dest // EXPERT_ROWS) * (EXPERT_ROWS * ROW_HALVES) + dest % EXPERT_ROWS
        rows += [first + hh * EXPERT_ROWS for hh in range(ROW_HALVES)]
    slot_ref[...] = jnp.concatenate(rows, axis=0)
    cnt_ref[...] += _dot(mt, jnp.ones((tm, LANES), BF16))


def _positions(idx, totals):
    t = idx.shape[1]
    tm = TM_POST
    return pl.pallas_call(
        _pos_kernel,
        grid=(t // tm,),
        in_specs=[pl.BlockSpec((TOP_K, tm), lambda i: (0, i)),
                  pl.BlockSpec((N_EXPERTS, LANES), lambda i: (0, 0))],
        out_specs=[pl.BlockSpec((TOP_K * ROW_HALVES, tm), lambda i: (0, i)),
                   pl.BlockSpec((N_EXPERTS, LANES), lambda i: (0, 0)),
                   pl.BlockSpec((N_EXPERTS, LANES), lambda i: (0, 0))],
        out_shape=[jax.ShapeDtypeStruct((TOP_K * ROW_HALVES, t), I32),
                   jax.ShapeDtypeStruct((N_EXPERTS, LANES), I32),
                   jax.ShapeDtypeStruct((N_EXPERTS, LANES), I32)],
        scratch_shapes=[pltpu.VMEM((N_EXPERTS, LANES), F32), pltpu.VMEM((N_EXPERTS, LANES), F32)],
        compiler_params=pltpu.CompilerParams(dimension_semantics=("arbitrary",)),
        name="positions",
    )(idx, totals)


def _sc_mesh():
    return plsc.VectorSubcoreMesh(core_axis_name="core", subcore_axis_name="subcore")


def _sc_dispatch(src, slot, n_out):
    lists, t = slot.shape
    wins = t // SC_WINDOW

    @pl.kernel(out_type=jax.ShapeDtypeStruct((n_out, HALF_WORDS), src.dtype), mesh=_sc_mesh(), name="sc_dispatch")
    def scatter(src_hbm, idx_hbm, out_hbm):
        def body(rows_vmem, idx_vmem):
            pltpu.sync_copy(rows_vmem, out_hbm.at[idx_vmem.at[0]])

        pltpu.emit_pipeline(
            body, grid=(lists * wins,),
            in_specs=[pl.BlockSpec((SC_WINDOW, HALF_WORDS),
                                   index_map=lambda i: ((i // wins) % ROW_HALVES * wins + i % wins, 0)),
                      pl.BlockSpec((1, SC_WINDOW), index_map=lambda i: (i // wins, i % wins))],
            out_specs=[],
            core_axis_name=("core", "subcore"), dimension_semantics=(pltpu.PARALLEL,))(src_hbm, idx_hbm)

    return scatter(src, slot)


def _sc_gather(src, slot, first_token, tokens):
    lists = slot.shape[0]
    wins = tokens // SC_WINDOW
    first_win = first_token // SC_WINDOW

    @pl.kernel(out_type=jax.ShapeDtypeStruct((lists * tokens, HALF_WORDS), src.dtype), mesh=_sc_mesh(),
               name="sc_gather")
    def gather(src_hbm, idx_hbm, out_hbm):
        def body(idx_vmem, rows_vmem):
            pltpu.sync_copy(src_hbm.at[idx_vmem.at[0]], rows_vmem)

        pltpu.emit_pipeline(
            body, grid=(lists * wins,),
            in_specs=[pl.BlockSpec((1, SC_WINDOW), index_map=lambda i: (i // wins, first_win + i % wins))],
            out_specs=[pl.BlockSpec((SC_WINDOW, HALF_WORDS), index_map=lambda i: (i, 0))],
            core_axis_name=("core", "subcore"), dimension_semantics=(pltpu.PARALLEL,))(idx_hbm, out_hbm)

    return gather(src, slot)


def _expert_kernel(bstart_ref, nblk_ref, xs_hbm, wg_ref, wu_ref, wd_ref, ys_hbm,
                   xbuf, ybuf, wgb_ref, wub_ref, wdb_ref, xsem, ysem, ypend):
    e = pl.program_id(0)
    last = pl.num_programs(0) - 1
    n = nblk_ref[e]
    b0 = bstart_ref[e]
    total = bstart_ref[last] + nblk_ref[last]
    blk_rows = EXPERT_ROWS * ROW_HALVES
    xdepth = xbuf.shape[0]
    ydepth = ybuf.shape[0]
    lookahead = xdepth - MAX_CHUNK

    def hbm_block(ref, b):
        return ref.at[pl.ds(pl.multiple_of(b * blk_rows, blk_rows), blk_rows)]

    def x_copy(b):
        slot = b % xdepth
        return pltpu.make_async_copy(hbm_block(xs_hbm, b), xbuf.at[slot], xsem.at[slot])

    def y_copy(b, slot):
        return pltpu.make_async_copy(ybuf.at[slot], hbm_block(ys_hbm, b), ysem.at[slot])

    @pl.when(e == 0)
    def _():
        for s in range(ydepth):
            ypend[s] = 0
        for b in range(lookahead):
            @pl.when(b < total)
            def _():
                x_copy(b).start()

    def chunk(j, m):
        b = b0 + j
        for i in range(m):
            @pl.when(b + lookahead + i < total)
            def _():
                x_copy(b + lookahead + i).start()

        for i in range(m):
            x_copy(b + i).wait()

            @pl.when(ypend[(b + i) % ydepth] == 1)
            def _():
                y_copy(b + i, (b + i) % ydepth).wait()

        for i in range(m):
            xslot = (b + i) % xdepth
            yslot = (b + i) % ydepth
            halves = [_unpack_pairs(xbuf[xslot, hh * EXPERT_ROWS:(hh + 1) * EXPERT_ROWS, :])
                      for hh in range(ROW_HALVES)]
            xb = jnp.concatenate([lo for lo, _ in halves] + [hi for _, hi in halves], axis=-1).astype(BF16)
            hid = _silu(_dot(xb, wgb_ref[...])) * _dot(xb, wub_ref[...])
            yb = _dot(hid.astype(BF16), wdb_ref[...])
            packed = _pack_pairs(yb[:, :D_MODEL // 2], yb[:, D_MODEL // 2:])
            for hh in range(ROW_HALVES):
                ybuf[yslot, hh * EXPERT_ROWS:(hh + 1) * EXPERT_ROWS, :] = (
                    packed[:, hh * HALF_WORDS:(hh + 1) * HALF_WORDS])

        for i in range(m):
            y_copy(b + i, (b + i) % ydepth).start()
            ypend[(b + i) % ydepth] = 1

    @pl.when(n > 0)
    def _():
        wgb_ref[...] = wg_ref[0].astype(BF16)
        wub_ref[...] = wu_ref[0].astype(BF16)
        wdb_ref[...] = wd_ref[0].astype(BF16)
        full = n // MAX_CHUNK

        def full_chunk(q, carry):
            chunk(q * MAX_CHUNK, MAX_CHUNK)
            return carry

        lax.fori_loop(0, full, full_chunk, 0)
        for m in range(1, MAX_CHUNK):
            @pl.when(n - full * MAX_CHUNK == m)
            def _():
                chunk(full * MAX_CHUNK, m)

    @pl.when(e == last)
    def _():
        for s in range(ydepth):
            @pl.when(ypend[s] == 1)
            def _():
                y_copy(b0, s).wait()


def _experts(bstart, nblk, xs, w_gate, w_up, w_down):
    blk_shape = (EXPERT_ROWS * ROW_HALVES, HALF_WORDS)
    wspec = lambda shape: pl.BlockSpec((1,) + shape, lambda e, bs, nb: (e, 0, 0))
    grid_spec = pltpu.PrefetchScalarGridSpec(
        num_scalar_prefetch=2,
        grid=(N_EXPERTS,),
        in_specs=[pl.BlockSpec(memory_space=pl.ANY),
                  wspec((D_MODEL, D_EXPERT)), wspec((D_MODEL, D_EXPERT)), wspec((D_EXPERT, D_MODEL))],
        out_specs=pl.BlockSpec(memory_space=pl.ANY),
        scratch_shapes=[pltpu.VMEM((X_RING,) + blk_shape, U32), pltpu.VMEM((Y_RING,) + blk_shape, U32),
                        pltpu.VMEM((D_MODEL, D_EXPERT), BF16), pltpu.VMEM((D_MODEL, D_EXPERT), BF16),
                        pltpu.VMEM((D_EXPERT, D_MODEL), BF16),
                        pltpu.SemaphoreType.DMA((X_RING,)), pltpu.SemaphoreType.DMA((Y_RING,)),
                        pltpu.SMEM((Y_RING,), I32)],
    )
    return pl.pallas_call(
        _expert_kernel,
        grid_spec=grid_spec,
        out_shape=jax.ShapeDtypeStruct(xs.shape, U32),
        compiler_params=pltpu.CompilerParams(dimension_semantics=("arbitrary",), has_side_effects=True),
        name="experts",
    )(bstart, nblk, xs, w_gate, w_up, w_down)


def _combine_kernel(gate_ref, base_ref, g2_ref, b2_ref, htt_ref, wg_ref, wu_ref, wd_ref, rows_ref, o_ref):
    tm = base_ref.shape[0]
    halves = [_unpack_pairs(htt_ref[hh]) for hh in range(ROW_HALVES)]
    hb = jnp.concatenate([lo for lo, _ in halves] + [hi for _, hi in halves], axis=-1).astype(BF16)
    hid = _silu(_dot(hb, wg_ref[...])) * _dot(hb, wu_ref[...])
    acc = base_ref[...] + _dot(hid.astype(BF16), wd_ref[...])
    gates_t = jnp.concatenate([gate_ref[...], jnp.zeros((LANES - TOP_K, tm), F32)], axis=0).T
    half = D_MODEL // 2
    lo_acc = [acc[:, hh * HALF_WORDS:(hh + 1) * HALF_WORDS] for hh in range(ROW_HALVES)]
    hi_acc = [acc[:, half + hh * HALF_WORDS:half + (hh + 1) * HALF_WORDS] for hh in range(ROW_HALVES)]
    for k in range(TOP_K):
        gk = gates_t[:, k:k + 1]
        for hh in range(ROW_HALVES):
            lo, hi = _unpack_pairs(rows_ref[k, hh])
            lo_acc[hh] = lo_acc[hh] + gk * lo
            hi_acc[hh] = hi_acc[hh] + gk * hi
    o_ref[...] = _layer_norm(jnp.concatenate(lo_acc + hi_acc, axis=-1), g2_ref[...], b2_ref[...])


def _combine_piece_kernel(*refs):
    _combine_kernel(*refs[:-2], refs[-1])


def _combine(gates, base, g2, b2, htt, wgs, wus, wds, rows, first_tile, prev):
    t = base.shape[0]
    tm = TM_COMBINE
    tiles = rows.shape[2] // tm
    full = lambda shape: pl.BlockSpec(shape, lambda i: (0,) * len(shape))
    in_specs = [pl.BlockSpec((TOP_K, tm), lambda i: (0, i + first_tile)),
                pl.BlockSpec((tm, D_MODEL), lambda i: (i + first_tile, 0)),
                full((1, D_MODEL)), full((1, D_MODEL)),
                pl.BlockSpec((ROW_HALVES, tm, HALF_WORDS), lambda i: (0, i + first_tile, 0)),
                full((D_MODEL, D_EXPERT)), full((D_MODEL, D_EXPERT)), full((D_EXPERT, D_MODEL)),
                pl.BlockSpec((TOP_K, ROW_HALVES, tm, HALF_WORDS), lambda i: (0, 0, i, 0))]
    args = [gates, base, g2, b2, htt, wgs, wus, wds, rows]
    if prev is not None:
        in_specs.append(pl.BlockSpec(memory_space=pl.ANY))
        args.append(prev)
    return pl.pallas_call(
        _combine_kernel if prev is None else _combine_piece_kernel,
        grid=(tiles,),
        in_specs=in_specs,
        out_specs=pl.BlockSpec((tm, D_MODEL), lambda i: (i + first_tile, 0)),
        out_shape=jax.ShapeDtypeStruct((t, D_MODEL), F32),
        input_output_aliases={} if prev is None else {len(args) - 1: 0},
        compiler_params=pltpu.CompilerParams(dimension_semantics=("arbitrary",),
                                             vmem_limit_bytes=48 * 1024 * 1024),
        name="combine",
    )(*args)


def _layer(h2, w_in, conv_w, conv_b, dt_bias, a_log, d_skip, ssd_norm_w, lower_bound, hgrn_norm_w, w_out,
           ln1_g, ln1_b, w_router, router_bias, w_gate_e, w_up_e, w_down_e, w_gate_s, w_up_s, w_down_s,
           ln2_g, ln2_b):
    t = h2.shape[0]
    dt0 = D_SSM + D_CONV
    q0 = dt0 + SSD_HEADS
    w_perm = jnp.concatenate(
        [w_in[:, :dt0], w_in[:, q0:], w_in[:, dt0:q0], jnp.zeros((D_MODEL, DT_PAD - SSD_HEADS), w_in.dtype)],
        axis=1).astype(BF16)
    z, xbc, q, f, i, g, dt = _inproj(h2, w_perm)
    y_ssd = _ssd(z, xbc, dt, conv_w, conv_b, dt_bias, a_log, d_skip, ssd_norm_w)
    y_hgrn = _hgrn(q, f, i, g, lower_bound, hgrn_norm_w)

    row = lambda v: v.reshape(1, -1).astype(F32)
    htt, base, idx, gates, totals = _post(
        h2, y_ssd, y_hgrn, w_out.astype(BF16), row(ln1_g), row(ln1_b), w_router.astype(F32),
        router_bias.reshape(N_EXPERTS, 1).astype(F32))

    slot, bstart, nblk = _positions(idx, totals)
    n_rows = _max_blocks(t) * EXPERT_ROWS
    xs = _sc_dispatch(htt.reshape(ROW_HALVES * t, HALF_WORDS), slot, n_rows * ROW_HALVES)
    ys = _experts(bstart[:, 0], nblk[:, 0], xs, w_gate_e, w_up_e, w_down_e)
    tp = t // GATHER_PIECES
    shared_w = (w_gate_s.astype(BF16), w_up_s.astype(BF16), w_down_s.astype(BF16))
    out = None
    for p in range(GATHER_PIECES):
        rows = _sc_gather(ys, slot, p * tp, tp).reshape(TOP_K, ROW_HALVES, tp, HALF_WORDS)
        out = _combine(gates, base, row(ln2_g), row(ln2_b), htt, *shared_w, rows, p * (tp // TM_COMBINE), out)
    return out


def kernel(x, w_in, conv_w, conv_b, dt_bias, a_log, d_skip, ssd_norm_w, hgrn_lb_logits, hgrn_norm_w, w_out,
           ln1_g, ln1_b, w_router, router_bias, w_gate_e, w_up_e, w_down_e, w_gate_s, w_up_s, w_down_s,
           ln2_g, ln2_b):
    bsz, t, d = x.shape
    assert bsz == 1 and d == D_MODEL, "the recurrent mixers carry state across the flattened token axis"
    depth = w_in.shape[0]
    lower_bounds = jnp.cumsum(jax.nn.softmax(hgrn_lb_logits.astype(F32), axis=0), axis=0)
    h = x.reshape(bsz * t, d)
    for l in range(depth):
        h = _layer(h, w_in[l], conv_w[l], conv_b[l], dt_bias[l], a_log[l], d_skip[l], ssd_norm_w[l],
                   lower_bounds[l], hgrn_norm_w[l], w_out[l], ln1_g[l], ln1_b[l], w_router[l],
                   router_bias[l], w_gate_e[l], w_up_e[l], w_down_e[l], w_gate_s[l], w_up_s[l],
                   w_down_s[l], ln2_g[l], ln2_b[l])
    return h.reshape(bsz, t, d)
```

```python
import jax
import jax.numpy as jnp
from jax import lax
from jax.experimental import pallas as pl
from jax.experimental.pallas import tpu as pltpu
from jax.experimental.pallas import tpu_sc as plsc

F32 = jnp.float32
BF16 = jnp.bfloat16
I32 = jnp.int32
U32 = jnp.uint32

D_MODEL = 1024
D_SSM = 512
D_HGRN = 512
SSD_HEADS = 8
SSD_HEAD_DIM = 64
SSD_GROUPS = 2
SSD_STATE = 128
SSD_CONV = 4
SSD_CHUNK = 128
SSD_PER_STEP = 8
D_CONV = D_SSM + 2 * SSD_GROUPS * SSD_STATE
HGRN_HEADS = 4
HGRN_DK = 128
HGRN_CHUNK = 64
HGRN_SUB = 8
HGRN_PER_STEP = 16
N_EXPERTS = 256
TOP_K = 8
N_EXPERT_GROUPS = 8
TOPK_GROUPS = 4
D_EXPERT = 256
ROUTED_SCALE = 2.5
ALPHA = 2.0 ** 0.25
LN_EPS = 1e-5
RMS_EPS = 1e-6

LANES = 128
SUBLANES = 8
ROW_TILES = D_MODEL // LANES
ROW_HALVES = 2
HALF_WORDS = D_MODEL // 2 // ROW_HALVES
SC_WINDOW = 128
GATHER_PIECES = 8
DT_PAD = LANES
N_IN_PAD = D_SSM + D_CONV + 4 * D_HGRN + DT_PAD

TM_PROJ = 512
TM_POST = 1024
TM_POSITIONS = 512
TM_COMBINE = 512
EXPERT_ROWS = 256
MAX_CHUNK = 2
X_RING = 8
Y_RING = 8
NEG = -1e30


def _sigmoid(x):
    return 1.0 / (1.0 + jnp.exp(-x))


def _silu(x):
    return x * _sigmoid(x)


def _split3(x):
    hi = x.astype(BF16)
    r = x - hi.astype(F32)
    mid = r.astype(BF16)
    lo = (r - mid.astype(F32)).astype(BF16)
    return hi, mid, lo


def _dot(a, b):
    return jnp.dot(a, b, preferred_element_type=F32)


def _dot_nt(a, b):
    return lax.dot_general(a, b, (((1,), (1,)), ((), ())), preferred_element_type=F32)


def _dot_tn(a, b):
    return lax.dot_general(a, b, (((0,), (0,)), ((), ())), preferred_element_type=F32)


def _sel_dot(sel, x):
    hi, mid, lo = _split3(x)
    return _dot(sel, hi) + _dot(sel, mid) + _dot(sel, lo)


def _dot_sel(x, sel):
    hi, mid, lo = _split3(x)
    return _dot(hi, sel) + _dot(mid, sel) + _dot(lo, sel)


def _pack_pairs(lo, hi):
    lo_bits = pltpu.bitcast(lo.astype(BF16).astype(F32), U32) >> 16
    hi_bits = pltpu.bitcast(hi.astype(BF16).astype(F32), U32) & jnp.uint32(0xFFFF0000)
    return hi_bits | lo_bits


def _unpack_pairs(p):
    return pltpu.bitcast(p << 16, F32), pltpu.bitcast(p & jnp.uint32(0xFFFF0000), F32)


def _layer_norm(x, g, b):
    mu = jnp.mean(x, axis=-1, keepdims=True)
    xc = x - mu
    var = jnp.mean(xc * xc, axis=-1, keepdims=True)
    return xc * lax.rsqrt(var + LN_EPS) * g + b


def _inproj_kernel(x_ref, w_ref, z_ref, xbc_ref, q_ref, f_ref, i_ref, g_ref, dt_ref):
    xb = x_ref[...].astype(BF16)
    col = 0
    for ref in (z_ref, xbc_ref, q_ref, f_ref, i_ref, g_ref, dt_ref):
        n = ref.shape[-1]
        ref[...] = _dot(xb, w_ref[:, col:col + n])
        col += n


def _inproj(x2, w_perm):
    t = x2.shape[0]
    widths = (D_SSM, D_CONV, D_HGRN, D_HGRN, D_HGRN, D_HGRN, DT_PAD)
    return pl.pallas_call(
        _inproj_kernel,
        grid=(t // TM_PROJ,),
        in_specs=[pl.BlockSpec((TM_PROJ, D_MODEL), lambda i: (i, 0)),
                  pl.BlockSpec((D_MODEL, N_IN_PAD), lambda i: (0, 0))],
        out_specs=[pl.BlockSpec((TM_PROJ, n), lambda i: (i, 0)) for n in widths],
        out_shape=[jax.ShapeDtypeStruct((t, n), F32) for n in widths],
        compiler_params=pltpu.CompilerParams(dimension_semantics=("arbitrary",),
                                             vmem_limit_bytes=48 * 1024 * 1024),
        name="inproj",
    )(x2, w_perm)


def _ssd_kernel(z_ref, xbc_ref, dt_ref, cw_ref, cb_ref, dtb_ref, alog_ref, dskip_ref, nw_ref,
                y_ref, ext_ref, st_ref):
    n = xbc_ref.shape[0]
    halo = SUBLANES

    @pl.when(pl.program_id(0) == 0)
    def _():
        ext_ref[0:halo, :] = jnp.zeros((halo, D_CONV), F32)
        st_ref[...] = jnp.zeros(st_ref.shape, F32)

    ext_ref[halo:halo + n, :] = xbc_ref[...]
    acc = jnp.broadcast_to(cb_ref[...], (n, D_CONV))
    for k in range(SSD_CONV):
        off = halo - (SSD_CONV - 1) + k
        acc = acc + cw_ref[k:k + 1, :] * ext_ref[off:off + n, :]
    ext_ref[0:halo, :] = xbc_ref[n - halo:n, :]
    u = _silu(acc)
    for c in range(n // SSD_CHUNK):
        rows = slice(c * SSD_CHUNK, (c + 1) * SSD_CHUNK)
        y_ref[rows, :] = _ssd_chunk(u[rows, :], z_ref[rows, :], dt_ref[rows, :] + dtb_ref[...],
                                    -jnp.exp(alog_ref[...]), dskip_ref[...], nw_ref[...], st_ref)


def _ssd_chunk(u, z, draw, a, dskip, nw, st_ref):
    L = SSD_CHUNK
    xs = u[:, :D_SSM]
    bm = u[:, D_SSM:D_SSM + SSD_GROUPS * SSD_STATE]
    cm = u[:, D_SSM + SSD_GROUPS * SSD_STATE:]

    dt = jnp.maximum(draw, 0.0) + jnp.log(1.0 + jnp.exp(-jnp.abs(draw)))
    ad = dt * a
    rows = lax.broadcasted_iota(I32, (L, L), 0)
    cols = lax.broadcasted_iota(I32, (L, L), 1)
    causal = rows >= cols
    a_cum = _sel_dot(causal.astype(BF16), ad)
    a_cum_t = a_cum.T

    hrow = lax.broadcasted_iota(I32, (LANES, D_SSM), 0)
    hcol = lax.broadcasted_iota(I32, (LANES, D_SSM), 1) // SSD_HEAD_DIM
    expand = (hrow == hcol).astype(BF16)
    dt_x = _dot_sel(dt, expand)
    acx = _dot_sel(a_cum, expand)
    last = acx[L - 1:L, :]
    ea_x = jnp.exp(acx)
    dec_x = jnp.exp(last - acx)
    elast_x = jnp.exp(last)

    xdt = xs * dt_x
    gw = SSD_HEADS // SSD_GROUPS * SSD_HEAD_DIM
    lane_head = lax.broadcasted_iota(I32, (L, gw), 1) // SSD_HEAD_DIM
    ys = []
    for g in range(SSD_GROUPS):
        bg = bm[:, g * SSD_STATE:(g + 1) * SSD_STATE]
        cg = cm[:, g * SSD_STATE:(g + 1) * SSD_STATE].astype(BF16)
        bg_t = bg.T.astype(BF16)
        gmat = _dot(cg, bg_t)
        xdt_g = xdt[:, g * gw:(g + 1) * gw]
        xdt_gb = xdt_g.astype(BF16)
        r_prev = st_ref[g]
        y_g = _dot(cg, r_prev.astype(BF16)) * ea_x[:, g * gw:(g + 1) * gw]
        new_s = _dot(bg_t, (xdt_g * dec_x[:, g * gw:(g + 1) * gw]).astype(BF16))
        st_ref[g] = r_prev * elast_x[:, g * gw:(g + 1) * gw] + new_s
        for j in range(SSD_HEADS // SSD_GROUPS):
            h = g * (SSD_HEADS // SSD_GROUPS) + j
            diff = a_cum[:, h:h + 1] - a_cum_t[h:h + 1, :]
            decay = jnp.exp(jnp.where(causal, diff, NEG))
            yd = _dot((gmat * decay).astype(BF16), xdt_gb)
            y_g = y_g + jnp.where(lane_head == j, yd, 0.0)
        ys.append(y_g)
    y = jnp.concatenate(ys, axis=-1) + xs * dskip
    y = y * _silu(z)
    outs = []
    ng = D_SSM // SSD_GROUPS
    for g in range(SSD_GROUPS):
        yg = y[:, g * ng:(g + 1) * ng]
        ms = jnp.mean(yg * yg, axis=-1, keepdims=True)
        outs.append(yg * lax.rsqrt(ms + RMS_EPS))
    return jnp.concatenate(outs, axis=-1) * nw


def _ssd(z, xbc, dt, conv_w, conv_b, dt_bias, a_log, d_skip, norm_w):
    t = z.shape[0]
    L = SSD_CHUNK * SSD_PER_STEP
    pad = lambda v: jnp.pad(v.astype(F32), (0, LANES - v.shape[0])).reshape(1, LANES)
    full = lambda shape: pl.BlockSpec(shape, lambda c: (0,) * len(shape))
    return pl.pallas_call(
        _ssd_kernel,
        grid=(t // L,),
        in_specs=[pl.BlockSpec((L, D_SSM), lambda c: (c, 0)),
                  pl.BlockSpec((L, D_CONV), lambda c: (c, 0)),
                  pl.BlockSpec((L, DT_PAD), lambda c: (c, 0)),
                  full((SSD_CONV, D_CONV)), full((1, D_CONV)), full((1, LANES)), full((1, LANES)),
                  full((1, D_SSM)), full((1, D_SSM))],
        out_specs=pl.BlockSpec((L, D_SSM), lambda c: (c, 0)),
        out_shape=jax.ShapeDtypeStruct((t, D_SSM), F32),
        scratch_shapes=[pltpu.VMEM((L + SUBLANES, D_CONV), F32),
                        pltpu.VMEM((SSD_GROUPS, SSD_STATE, D_SSM // SSD_GROUPS), F32)],
        compiler_params=pltpu.CompilerParams(dimension_semantics=("arbitrary",)),
        name="ssd",
    )(z, xbc, dt, conv_w.astype(F32), conv_b.reshape(1, D_CONV).astype(F32), pad(dt_bias), pad(a_log),
      jnp.repeat(d_skip.astype(F32), SSD_HEAD_DIM).reshape(1, D_SSM), norm_w.reshape(1, D_SSM).astype(F32))


def _tile_bcast(x, r):
    n, d = x.shape
    x3 = x.reshape(n // SUBLANES, SUBLANES, d)
    return jnp.broadcast_to(x3[:, r:r + 1, :], x3.shape).reshape(n, d)


def _hgrn_kernel(q_ref, f_ref, i_ref, g_ref, lb_ref, nw_ref, o_ref, st_ref):
    @pl.when(pl.program_id(0) == 0)
    def _():
        st_ref[...] = jnp.zeros(st_ref.shape, F32)

    for c in range(o_ref.shape[0] // HGRN_CHUNK):
        rows = slice(c * HGRN_CHUNK, (c + 1) * HGRN_CHUNK)
        o_ref[rows, :] = _hgrn_chunk(q_ref[rows, :], f_ref[rows, :], i_ref[rows, :], g_ref[rows, :],
                                     lb_ref[...], nw_ref[...], st_ref)


def _hgrn_chunk(q, f, v, g, lb, nw, st_ref):
    C = HGRN_CHUNK
    S = HGRN_SUB
    nsub = C // S
    fg = lb + (1.0 - lb) * _sigmoid(f)
    qs = _silu(q)
    rows = lax.broadcasted_iota(I32, (C, C), 0)
    cols = lax.broadcasted_iota(I32, (C, C), 1)
    cum = _sel_dot((rows >= cols).astype(BF16), jnp.log2(fg))
    ck_all = cum - jnp.log2(1.0 - fg)

    srow = lax.broadcasted_iota(I32, (S * HGRN_DK, C), 0) // HGRN_DK
    scol = lax.broadcasted_iota(I32, (S * HGRN_DK, C), 1) % S
    spread = (srow == scol).astype(BF16)
    diag_keep = jnp.logical_and(rows // S == cols // S, rows >= cols)
    outs = []
    for h in range(HGRN_HEADS):
        blk = slice(h * HGRN_DK, (h + 1) * HGRN_DK)
        qh, ch, ck = qs[:, blk], cum[:, blk], ck_all[:, blk]
        last = ch[C - 1:C, :]
        x_cat = jnp.concatenate(
            [(qh * jnp.exp2(jnp.minimum(ch - _tile_bcast(ck, r), 0.0))).astype(BF16) for r in range(S)], axis=-1)
        kbe = jnp.exp2(_tile_bcast(ch, S - 1) - ck).astype(BF16)
        a_pieces, b_pieces = [], []
        for j in range(nsub - 1):
            lo = (j + 1) * S
            a = (qh[lo:, :] * jnp.exp2(ch[lo:, :] - ch[lo - 1:lo, :])).astype(BF16)
            a_pieces.append(jnp.concatenate([jnp.zeros((lo, HGRN_DK), BF16), a], axis=0))
            b = [kbe[j * S:lo, :], jnp.zeros((C - lo, HGRN_DK), BF16)]
            if j > 0:
                b.insert(0, jnp.zeros((j * S, HGRN_DK), BF16))
            b_pieces.append(jnp.concatenate(b, axis=0))
        att = (_dot_nt(jnp.concatenate(a_pieces, axis=-1), jnp.concatenate(b_pieces, axis=-1))
               + jnp.where(diag_keep, _dot(x_cat, spread), 0.0))
        vb = v[:, blk].astype(BF16)
        s_prev = st_ref[h]
        o = _dot_nt((qh * jnp.exp2(ch)).astype(BF16), s_prev.astype(BF16)) + _dot(att.astype(BF16), vb)
        st_ref[h] = s_prev * jnp.exp2(last) + _dot_tn(vb, jnp.exp2(last - ck).astype(BF16))
        ms = jnp.mean(o * o, axis=-1, keepdims=True)
        outs.append(o * lax.rsqrt(ms + RMS_EPS))
    return jnp.concatenate(outs, axis=-1) * nw * _silu(g)


def _hgrn(q, f, i, g, lower_bound, norm_w):
    t = q.shape[0]
    C = HGRN_CHUNK * HGRN_PER_STEP
    tok = pl.BlockSpec((C, D_HGRN), lambda c: (c, 0))
    vec = pl.BlockSpec((1, D_HGRN), lambda c: (0, 0))
    return pl.pallas_call(
        _hgrn_kernel,
        grid=(t // C,),
        in_specs=[tok, tok, tok, tok, vec, vec],
        out_specs=tok,
        out_shape=jax.ShapeDtypeStruct((t, D_HGRN), F32),
        scratch_shapes=[pltpu.VMEM((HGRN_HEADS, D_HGRN // HGRN_HEADS, HGRN_DK), F32)],
        compiler_params=pltpu.CompilerParams(dimension_semantics=("arbitrary",)),
        name="hgrn",
    )(q, f, i, g, lower_bound.reshape(1, D_HGRN).astype(F32), norm_w.reshape(1, D_HGRN).astype(F32))


def _post_kernel(x_ref, ys_ref, yh_ref, wo_ref, g1_ref, b1_ref, wr_ref, rb_ref,
                 htt_ref, base_ref, idx_ref, gate_ref, cnt_ref, wrh_ref, wrl_ref):
    @pl.when(pl.program_id(0) == 0)
    def _():
        wr_t = wr_ref[...].T
        hi = wr_t.astype(BF16)
        wrh_ref[...] = hi
        wrl_ref[...] = (wr_t - hi.astype(F32)).astype(BF16)

    tm = x_ref.shape[0]
    mix = (_dot(ys_ref[...].astype(BF16), wo_ref[0:D_SSM, :])
           + _dot(yh_ref[...].astype(BF16), wo_ref[D_SSM:, :]))
    h1 = _layer_norm(ALPHA * x_ref[...] + mix, g1_ref[...], b1_ref[...])
    packed = _pack_pairs(h1[:, :D_MODEL // 2], h1[:, D_MODEL // 2:])
    for hh in range(ROW_HALVES):
        htt_ref[hh] = packed[:, hh * HALF_WORDS:(hh + 1) * HALF_WORDS]
    hb = h1.astype(BF16)
    base_ref[...] = ALPHA * h1

    hlo = (h1 - hb.astype(F32)).astype(BF16)
    logits = _dot_nt(wrh_ref[...], hb) + _dot_nt(wrh_ref[...], hlo) + _dot_nt(wrl_ref[...], hb)
    scores = _sigmoid(logits)
    biased = scores + rb_ref[...]
    per_group = N_EXPERTS // N_EXPERT_GROUPS
    eidx = lax.broadcasted_iota(I32, (N_EXPERTS, tm), 0)
    big = jnp.int32(1 << 20)
    gsc = []
    bidx = lax.broadcasted_iota(I32, (per_group, tm), 0)
    for gi in range(N_EXPERT_GROUPS):
        blk = biased[gi * per_group:(gi + 1) * per_group, :]
        m1 = jnp.max(blk, axis=0, keepdims=True)
        i1 = jnp.min(jnp.where(blk == m1, bidx, big), axis=0, keepdims=True)
        m2 = jnp.max(jnp.where(bidx == i1, NEG, blk), axis=0, keepdims=True)
        gsc.append(m1 + m2)
    cur = jnp.concatenate(gsc, axis=0)
    gidx = lax.broadcasted_iota(I32, (N_EXPERT_GROUPS, tm), 0)
    gsel = jnp.zeros((N_EXPERT_GROUPS, tm), F32)
    for _ in range(TOPK_GROUPS):
        m = jnp.max(cur, axis=0, keepdims=True)
        i = jnp.min(jnp.where(cur == m, gidx, big), axis=0, keepdims=True)
        hit = gidx == i
        gsel = jnp.where(hit, 1.0, gsel)
        cur = jnp.where(hit, NEG, cur)
    emask = jnp.concatenate(
        [jnp.broadcast_to(gsel[gi:gi + 1, :], (per_group, tm)) for gi in range(N_EXPERT_GROUPS)], axis=0)
    masked = jnp.where(emask > 0.0, biased, NEG)
    idx_rows, gate_rows = [], []
    for _ in range(TOP_K):
        m = jnp.max(masked, axis=0, keepdims=True)
        i = jnp.min(jnp.where(masked == m, eidx, big), axis=0, keepdims=True)
        hit = eidx == i
        idx_rows.append(i)
        gate_rows.append(jnp.sum(jnp.where(hit, scores, 0.0), axis=0, keepdims=True))
        masked = jnp.where(hit, NEG, masked)
    picked = jnp.where(jnp.logical_and(masked == NEG, emask > 0.0), 1.0, 0.0)
    gates = jnp.concatenate(gate_rows, axis=0)
    gates = gates / jnp.sum(gates, axis=0, keepdims=True) * ROUTED_SCALE
    idx_ref[...] = jnp.concatenate(idx_rows, axis=0)
    gate_ref[...] = gates

    @pl.when(pl.program_id(0) == 0)
    def _():
        cnt_ref[...] = jnp.zeros(cnt_ref.shape, F32)

    cnt_ref[...] += _dot(picked.astype(BF16), jnp.ones((tm, LANES), BF16))


def _post(x2, y_ssd, y_hgrn, wo, g1, b1, w_router, rbias):
    t = x2.shape[0]
    tm = TM_POST
    full = lambda shape: pl.BlockSpec(shape, lambda i: (0,) * len(shape))
    return pl.pallas_call(
        _post_kernel,
        grid=(t // tm,),
        in_specs=[pl.BlockSpec((tm, D_MODEL), lambda i: (i, 0)),
                  pl.BlockSpec((tm, D_SSM), lambda i: (i, 0)),
                  pl.BlockSpec((tm, D_HGRN), lambda i: (i, 0)),
                  full((D_MODEL, D_MODEL)), full((1, D_MODEL)), full((1, D_MODEL)),
                  full((D_MODEL, N_EXPERTS)), full((N_EXPERTS, 1))],
        out_specs=[pl.BlockSpec((ROW_HALVES, tm, HALF_WORDS), lambda i: (0, i, 0)),
                   pl.BlockSpec((tm, D_MODEL), lambda i: (i, 0)),
                   pl.BlockSpec((TOP_K, tm), lambda i: (0, i)),
                   pl.BlockSpec((TOP_K, tm), lambda i: (0, i)),
                   pl.BlockSpec((N_EXPERTS, LANES), lambda i: (0, 0))],
        out_shape=[jax.ShapeDtypeStruct((ROW_HALVES, t, HALF_WORDS), U32),
                   jax.ShapeDtypeStruct((t, D_MODEL), F32),
                   jax.ShapeDtypeStruct((TOP_K, t), I32),
                   jax.ShapeDtypeStruct((TOP_K, t), F32),
                   jax.ShapeDtypeStruct((N_EXPERTS, LANES), F32)],
        scratch_shapes=[pltpu.VMEM((N_EXPERTS, D_MODEL), BF16), pltpu.VMEM((N_EXPERTS, D_MODEL), BF16)],
        compiler_params=pltpu.CompilerParams(dimension_semantics=("arbitrary",),
                                             vmem_limit_bytes=60 * 1024 * 1024),
        name="post",
    )(x2, y_ssd, y_hgrn, wo, g1, b1, w_router, rbias)


def _max_blocks(t):
    return (t * TOP_K + N_EXPERTS * (EXPERT_ROWS - 1)) // EXPERT_ROWS


def _pos_kernel(idx_ref, total_ref, slot_ref, bstart_ref, nblk_ref, cnt_ref, start_ref):
    i = pl.program_id(0)
    tm = idx_ref.shape[-1]
    idx = idx_ref[...]
    eidx = lax.broadcasted_iota(I32, (N_EXPERTS, tm), 0)
    sel = [eidx == idx[k:k + 1, :] for k in range(TOP_K)]
    onehot = sel[0]
    for k in range(1, TOP_K):
        onehot = jnp.logical_or(onehot, sel[k])
    mt = jnp.where(onehot, 1.0, 0.0).astype(BF16)

    @pl.when(i == 0)
    def _():
        nb = jnp.floor((total_ref[...] + (EXPERT_ROWS - 1)) * (1.0 / EXPERT_ROWS))
        r = lax.broadcasted_iota(I32, (N_EXPERTS, N_EXPERTS), 0)
        c = lax.broadcasted_iota(I32, (N_EXPERTS, N_EXPERTS), 1)
        end = _dot((r >= c).astype(BF16), nb.astype(BF16))
        start_ref[...] = (end - nb) * EXPERT_ROWS
        cnt_ref[...] = jnp.zeros(cnt_ref.shape, F32)
        bstart_ref[...] = (end - nb).astype(I32)
        nblk_ref[...] = nb.astype(I32)

    r = lax.broadcasted_iota(I32, (tm, tm), 0)
    c = lax.broadcasted_iota(I32, (tm, tm), 1)
    before = _dot(mt, (r < c).astype(BF16))
    sorted_row = start_ref[:, 0:1] + cnt_ref[:, 0:1] + before
    rows = []
    for k in range(TOP_K):
        dest = jnp.sum(jnp.where(sel[k], sorted_row, 0.0), axis=0, keepdims=True).astype(I32)
        first = (dest // EXPERT_ROWS) * (EXPERT_ROWS * ROW_HALVES) + dest % EXPERT_ROWS
        rows += [first + hh * EXPERT_ROWS for hh in range(ROW_HALVES)]
    slot_ref[...] = jnp.concatenate(rows, axis=0)
    cnt_ref[...] += _dot(mt, jnp.ones((tm, LANES), BF16))


def _positions(idx, totals):
    t = idx.shape[1]
    tm = TM_POSITIONS
    return pl.pallas_call(
        _pos_kernel,
        grid=(t // tm,),
        in_specs=[pl.BlockSpec((TOP_K, tm), lambda i: (0, i)),
                  pl.BlockSpec((N_EXPERTS, LANES), lambda i: (0, 0))],
        out_specs=[pl.BlockSpec((TOP_K * ROW_HALVES, tm), lambda i: (0, i)),
                   pl.BlockSpec((N_EXPERTS, LANES), lambda i: (0, 0)),
                   pl.BlockSpec((N_EXPERTS, LANES), lambda i: (0, 0))],
        out_shape=[jax.ShapeDtypeStruct((TOP_K * ROW_HALVES, t), I32),
                   jax.ShapeDtypeStruct((N_EXPERTS, LANES), I32),
                   jax.ShapeDtypeStruct((N_EXPERTS, LANES), I32)],
        scratch_shapes=[pltpu.VMEM((N_EXPERTS, LANES), F32), pltpu.VMEM((N_EXPERTS, LANES), F32)],
        compiler_params=pltpu.CompilerParams(dimension_semantics=("arbitrary",)),
        name="positions",
    )(idx, totals)


def _sc_mesh():
    return plsc.VectorSubcoreMesh(core_axis_name="core", subcore_axis_name="subcore")


def _sc_dispatch(src, slot, n_out):
    lists, t = slot.shape
    wins = t // SC_WINDOW

    @pl.kernel(out_type=jax.ShapeDtypeStruct((n_out, HALF_WORDS), src.dtype), mesh=_sc_mesh(), name="sc_dispatch")
    def scatter(src_hbm, idx_hbm, out_hbm):
        def body(rows_vmem, idx_vmem):
            pltpu.sync_copy(rows_vmem, out_hbm.at[idx_vmem.at[0]])

        pltpu.emit_pipeline(
            body, grid=(lists * wins,),
            in_specs=[pl.BlockSpec((SC_WINDOW, HALF_WORDS),
                                   index_map=lambda i: ((i // wins) % ROW_HALVES * wins + i % wins, 0)),
                      pl.BlockSpec((1, SC_WINDOW), index_map=lambda i: (i // wins, i % wins))],
            out_specs=[],
            core_axis_name=("core", "subcore"), dimension_semantics=(pltpu.PARALLEL,))(src_hbm, idx_hbm)

    return scatter(src, slot)


def _sc_gather(src, slot, first_token, tokens):
    lists = slot.shape[0]
    wins = tokens // SC_WINDOW
    first_win = first_token // SC_WINDOW

    @pl.kernel(out_type=jax.ShapeDtypeStruct((lists * tokens, HALF_WORDS), src.dtype), mesh=_sc_mesh(),
               name="sc_gather")
    def gather(src_hbm, idx_hbm, out_hbm):
        def body(idx_vmem, rows_vmem):
            pltpu.sync_copy(src_hbm.at[idx_vmem.at[0]], rows_vmem)

        pltpu.emit_pipeline(
            body, grid=(lists * wins,),
            in_specs=[pl.BlockSpec((1, SC_WINDOW), index_map=lambda i: (i // wins, first_win + i % wins))],
            out_specs=[pl.BlockSpec((SC_WINDOW, HALF_WORDS), index_map=lambda i: (i, 0))],
            core_axis_name=("core", "subcore"), dimension_semantics=(pltpu.PARALLEL,))(idx_hbm, out_hbm)

    return gather(src, slot)


def _expert_kernel(bstart_ref, nblk_ref, xs_hbm, wg_ref, wu_ref, wd_ref, ys_hbm,
                   xbuf, ybuf, wgb_ref, wub_ref, wdb_ref, xsem, ysem, ypend):
    e = pl.program_id(0)
    last = pl.num_programs(0) - 1
    n = nblk_ref[e]
    b0 = bstart_ref[e]
    total = bstart_ref[last] + nblk_ref[last]
    blk_rows = EXPERT_ROWS * ROW_HALVES
    xdepth = xbuf.shape[0]
    ydepth = ybuf.shape[0]
    lookahead = xdepth - MAX_CHUNK

    def hbm_block(ref, b):
        return ref.at[pl.ds(pl.multiple_of(b * blk_rows, blk_rows), blk_rows)]

    def x_copy(b):
        slot = b % xdepth
        return pltpu.make_async_copy(hbm_block(xs_hbm, b), xbuf.at[slot], xsem.at[slot])

    def y_copy(b, slot):
        return pltpu.make_async_copy(ybuf.at[slot], hbm_block(ys_hbm, b), ysem.at[slot])

    @pl.when(e == 0)
    def _():
        for s in range(ydepth):
            ypend[s] = 0
        for b in range(lookahead):
            @pl.when(b < total)
            def _():
                x_copy(b).start()

    def chunk(j, m):
        b = b0 + j
        for i in range(m):
            @pl.when(b + lookahead + i < total)
            def _():
                x_copy(b + lookahead + i).start()

        for i in range(m):
            x_copy(b + i).wait()

            @pl.when(ypend[(b + i) % ydepth] == 1)
            def _():
                y_copy(b + i, (b + i) % ydepth).wait()

        for i in range(m):
            xslot = (b + i) % xdepth
            yslot = (b + i) % ydepth
            halves = [_unpack_pairs(xbuf[xslot, hh * EXPERT_ROWS:(hh + 1) * EXPERT_ROWS, :])
                      for hh in range(ROW_HALVES)]
            xb = jnp.concatenate([lo for lo, _ in halves] + [hi for _, hi in halves], axis=-1).astype(BF16)
            hid = _silu(_dot(xb, wgb_ref[...])) * _dot(xb, wub_ref[...])
            yb = _dot(hid.astype(BF16), wdb_ref[...])
            packed = _pack_pairs(yb[:, :D_MODEL // 2], yb[:, D_MODEL // 2:])
            for hh in range(ROW_HALVES):
                ybuf[yslot, hh * EXPERT_ROWS:(hh + 1) * EXPERT_ROWS, :] = (
                    packed[:, hh * HALF_WORDS:(hh + 1) * HALF_WORDS])

        for i in range(m):
            y_copy(b + i, (b + i) % ydepth).start()
            ypend[(b + i) % ydepth] = 1

    @pl.when(n > 0)
    def _():
        wgb_ref[...] = wg_ref[0].astype(BF16)
        wub_ref[...] = wu_ref[0].astype(BF16)
        wdb_ref[...] = wd_ref[0].astype(BF16)
        full = n // MAX_CHUNK

        def full_chunk(q, carry):
            chunk(q * MAX_CHUNK, MAX_CHUNK)
            return carry

        lax.fori_loop(0, full, full_chunk, 0)
        for m in range(1, MAX_CHUNK):
            @pl.when(n - full * MAX_CHUNK == m)
            def _():
                chunk(full * MAX_CHUNK, m)

    @pl.when(e == last)
    def _():
        for s in range(ydepth):
            @pl.when(ypend[s] == 1)
            def _():
                y_copy(b0, s).wait()


def _experts(bstart, nblk, xs, w_gate, w_up, w_down):
    blk_shape = (EXPERT_ROWS * ROW_HALVES, HALF_WORDS)
    wspec = lambda shape: pl.BlockSpec((1,) + shape, lambda e, bs, nb: (e, 0, 0))
    grid_spec = pltpu.PrefetchScalarGridSpec(
        num_scalar_prefetch=2,
        grid=(N_EXPERTS,),
        in_specs=[pl.BlockSpec(memory_space=pl.ANY),
                  wspec((D_MODEL, D_EXPERT)), wspec((D_MODEL, D_EXPERT)), wspec((D_EXPERT, D_MODEL))],
        out_specs=pl.BlockSpec(memory_space=pl.ANY),
        scratch_shapes=[pltpu.VMEM((X_RING,) + blk_shape, U32), pltpu.VMEM((Y_RING,) + blk_shape, U32),
                        pltpu.VMEM((D_MODEL, D_EXPERT), BF16), pltpu.VMEM((D_MODEL, D_EXPERT), BF16),
                        pltpu.VMEM((D_EXPERT, D_MODEL), BF16),
                        pltpu.SemaphoreType.DMA((X_RING,)), pltpu.SemaphoreType.DMA((Y_RING,)),
                        pltpu.SMEM((Y_RING,), I32)],
    )
    return pl.pallas_call(
        _expert_kernel,
        grid_spec=grid_spec,
        out_shape=jax.ShapeDtypeStruct(xs.shape, U32),
        compiler_params=pltpu.CompilerParams(dimension_semantics=("arbitrary",), has_side_effects=True),
        name="experts",
    )(bstart, nblk, xs, w_gate, w_up, w_down)


def _combine_kernel(gate_ref, base_ref, g2_ref, b2_ref, htt_ref, wg_ref, wu_ref, wd_ref, rows_ref, o_ref):
    tm = base_ref.shape[0]
    halves = [_unpack_pairs(htt_ref[hh]) for hh in range(ROW_HALVES)]
    hb = jnp.concatenate([lo for lo, _ in halves] + [hi for _, hi in halves], axis=-1).astype(BF16)
    hid = _silu(_dot(hb, wg_ref[...])) * _dot(hb, wu_ref[...])
    acc = base_ref[...] + _dot(hid.astype(BF16), wd_ref[...])
    gates_t = jnp.concatenate([gate_ref[...], jnp.zeros((LANES - TOP_K, tm), F32)], axis=0).T
    half = D_MODEL // 2
    lo_acc = [acc[:, hh * HALF_WORDS:(hh + 1) * HALF_WORDS] for hh in range(ROW_HALVES)]
    hi_acc = [acc[:, half + hh * HALF_WORDS:half + (hh + 1) * HALF_WORDS] for hh in range(ROW_HALVES)]
    for k in range(TOP_K):
        gk = gates_t[:, k:k + 1]
        for hh in range(ROW_HALVES):
            lo, hi = _unpack_pairs(rows_ref[k, hh])
            lo_acc[hh] = lo_acc[hh] + gk * lo
            hi_acc[hh] = hi_acc[hh] + gk * hi
    o_ref[...] = _layer_norm(jnp.concatenate(lo_acc + hi_acc, axis=-1), g2_ref[...], b2_ref[...])


def _combine_piece_kernel(*refs):
    _combine_kernel(*refs[:-2], refs[-1])


def _combine(gates, base, g2, b2, htt, wgs, wus, wds, rows, first_tile, prev):
    t = base.shape[0]
    tm = TM_COMBINE
    tiles = rows.shape[2] // tm
    full = lambda shape: pl.BlockSpec(shape, lambda i: (0,) * len(shape))
    in_specs = [pl.BlockSpec((TOP_K, tm), lambda i: (0, i + first_tile)),
                pl.BlockSpec((tm, D_MODEL), lambda i: (i + first_tile, 0)),
                full((1, D_MODEL)), full((1, D_MODEL)),
                pl.BlockSpec((ROW_HALVES, tm, HALF_WORDS), lambda i: (0, i + first_tile, 0)),
                full((D_MODEL, D_EXPERT)), full((D_MODEL, D_EXPERT)), full((D_EXPERT, D_MODEL)),
                pl.BlockSpec((TOP_K, ROW_HALVES, tm, HALF_WORDS), lambda i: (0, 0, i, 0))]
    args = [gates, base, g2, b2, htt, wgs, wus, wds, rows]
    if prev is not None:
        in_specs.append(pl.BlockSpec(memory_space=pl.ANY))
        args.append(prev)
    return pl.pallas_call(
        _combine_kernel if prev is None else _combine_piece_kernel,
        grid=(tiles,),
        in_specs=in_specs,
        out_specs=pl.BlockSpec((tm, D_MODEL), lambda i: (i + first_tile, 0)),
        out_shape=jax.ShapeDtypeStruct((t, D_MODEL), F32),
        input_output_aliases={} if prev is None else {len(args) - 1: 0},
        compiler_params=pltpu.CompilerParams(dimension_semantics=("arbitrary",),
                                             vmem_limit_bytes=48 * 1024 * 1024),
        name="combine",
    )(*args)


def _layer(h2, w_in, conv_w, conv_b, dt_bias, a_log, d_skip, ssd_norm_w, lower_bound, hgrn_norm_w, w_out,
           ln1_g, ln1_b, w_router, router_bias, w_gate_e, w_up_e, w_down_e, w_gate_s, w_up_s, w_down_s,
           ln2_g, ln2_b):
    t = h2.shape[0]
    dt0 = D_SSM + D_CONV
    q0 = dt0 + SSD_HEADS
    w_perm = jnp.concatenate(
        [w_in[:, :dt0], w_in[:, q0:], w_in[:, dt0:q0], jnp.zeros((D_MODEL, DT_PAD - SSD_HEADS), w_in.dtype)],
        axis=1).astype(BF16)
    z, xbc, q, f, i, g, dt = _inproj(h2, w_perm)
    y_ssd = _ssd(z, xbc, dt, conv_w, conv_b, dt_bias, a_log, d_skip, ssd_norm_w)
    y_hgrn = _hgrn(q, f, i, g, lower_bound, hgrn_norm_w)

    row = lambda v: v.reshape(1, -1).astype(F32)
    htt, base, idx, gates, totals = _post(
        h2, y_ssd, y_hgrn, w_out.astype(BF16), row(ln1_g), row(ln1_b), w_router.astype(F32),
        router_bias.reshape(N_EXPERTS, 1).astype(F32))

    slot, bstart, nblk = _positions(idx, totals)
    n_rows = _max_blocks(t) * EXPERT_ROWS
    xs = _sc_dispatch(htt.reshape(ROW_HALVES * t, HALF_WORDS), slot, n_rows * ROW_HALVES)
    ys = _experts(bstart[:, 0], nblk[:, 0], xs, w_gate_e, w_up_e, w_down_e)
    tp = t // GATHER_PIECES
    shared_w = (w_gate_s.astype(BF16), w_up_s.astype(BF16), w_down_s.astype(BF16))
    out = None
    for p in range(GATHER_PIECES):
        rows = _sc_gather(ys, slot, p * tp, tp).reshape(TOP_K, ROW_HALVES, tp, HALF_WORDS)
        out = _combine(gates, base, row(ln2_g), row(ln2_b), htt, *shared_w, rows, p * (tp // TM_COMBINE), out)
    return out


def kernel(x, w_in, conv_w, conv_b, dt_bias, a_log, d_skip, ssd_norm_w, hgrn_lb_logits, hgrn_norm_w, w_out,
           ln1_g, ln1_b, w_router, router_bias, w_gate_e, w_up_e, w_down_e, w_gate_s, w_up_s, w_down_s,
           ln2_g, ln2_b):
    bsz, t, d = x.shape
    assert bsz == 1 and d == D_MODEL, "the recurrent mixers carry state across the flattened token axis"
    depth = w_in.shape[0]
    lower_bounds = jnp.cumsum(jax.nn.softmax(hgrn_lb_logits.astype(F32), axis=0), axis=0)
    h = x.reshape(bsz * t, d)
    for l in range(depth):
        h = _layer(h, w_in[l], conv_w[l], conv_b[l], dt_bias[l], a_log[l], d_skip[l], ssd_norm_w[l],
                   lower_bounds[l], hgrn_norm_w[l], w_out[l], ln1_g[l], ln1_b[l], w_router[l],
                   router_bias[l], w_gate_e[l], w_up_e[l], w_down_e[l], w_gate_s[l], w_up_s[l],
                   w_down_s[l], ln2_g[l], ln2_b[l])
    return h.reshape(bsz, t, d)
```

```python
import jax
import jax.numpy as jnp
from jax import lax
from jax.experimental import pallas as pl
from jax.experimental.pallas import tpu as pltpu
from jax.experimental.pallas import tpu_sc as plsc

F32 = jnp.float32
BF16 = jnp.bfloat16
I32 = jnp.int32
U32 = jnp.uint32

D_MODEL = 1024
D_SSM = 512
D_HGRN = 512
SSD_HEADS = 8
SSD_HEAD_DIM = 64
SSD_GROUPS = 2
SSD_STATE = 128
SSD_CONV = 4
SSD_CHUNK = 128
SSD_PER_STEP = 8
D_CONV = D_SSM + 2 * SSD_GROUPS * SSD_STATE
HGRN_HEADS = 4
HGRN_DK = 128
HGRN_CHUNK = 64
HGRN_SUB = 8
HGRN_PER_STEP = 16
N_EXPERTS = 256
TOP_K = 8
N_EXPERT_GROUPS = 8
TOPK_GROUPS = 4
D_EXPERT = 256
ROUTED_SCALE = 2.5
ALPHA = 2.0 ** 0.25
LN_EPS = 1e-5
RMS_EPS = 1e-6

LANES = 128
SUBLANES = 8
ROW_HALVES = 2
HALF_WORDS = D_MODEL // 2 // ROW_HALVES
SC_WINDOW = 128
GATHER_PIECES = 8
DT_PAD = LANES
N_IN_PAD = D_SSM + D_CONV + 4 * D_HGRN + DT_PAD

TM_PROJ = 512
TM_POST = 1024
TM_POSITIONS = 512
TM_COMBINE = 512
EXPERT_ROWS = 256
MAX_CHUNK = 2
X_RING = 8
Y_RING = 8
NEG = -1e30


def _sigmoid(x):
    return 1.0 / (1.0 + jnp.exp(-x))


def _silu(x):
    return x * _sigmoid(x)


def _split3(x):
    hi = x.astype(BF16)
    r = x - hi.astype(F32)
    mid = r.astype(BF16)
    lo = (r - mid.astype(F32)).astype(BF16)
    return hi, mid, lo


def _dot(a, b):
    return jnp.dot(a, b, preferred_element_type=F32)


def _dot_nt(a, b):
    return lax.dot_general(a, b, (((1,), (1,)), ((), ())), preferred_element_type=F32)


def _dot_tn(a, b):
    return lax.dot_general(a, b, (((0,), (0,)), ((), ())), preferred_element_type=F32)


def _sel_dot(sel, x):
    hi, mid, lo = _split3(x)
    return _dot(sel, hi) + _dot(sel, mid) + _dot(sel, lo)


def _dot_sel(x, sel):
    hi, mid, lo = _split3(x)
    return _dot(hi, sel) + _dot(mid, sel) + _dot(lo, sel)


def _pack_pairs(lo, hi):
    lo_bits = pltpu.bitcast(lo.astype(BF16).astype(F32), U32) >> 16
    hi_bits = pltpu.bitcast(hi.astype(BF16).astype(F32), U32) & jnp.uint32(0xFFFF0000)
    return hi_bits | lo_bits


def _unpack_pairs(p):
    return pltpu.bitcast(p << 16, F32), pltpu.bitcast(p & jnp.uint32(0xFFFF0000), F32)


def _layer_norm(x, g, b):
    mu = jnp.mean(x, axis=-1, keepdims=True)
    xc = x - mu
    var = jnp.mean(xc * xc, axis=-1, keepdims=True)
    return xc * lax.rsqrt(var + LN_EPS) * g + b


def _inproj_kernel(x_ref, w_ref, z_ref, xbc_ref, q_ref, f_ref, i_ref, g_ref, dt_ref):
    xb = x_ref[...].astype(BF16)
    col = 0
    for ref in (z_ref, xbc_ref, q_ref, f_ref, i_ref, g_ref, dt_ref):
        n = ref.shape[-1]
        ref[...] = _dot(xb, w_ref[:, col:col + n])
        col += n


def _inproj(x2, w_perm):
    t = x2.shape[0]
    widths = (D_SSM, D_CONV, D_HGRN, D_HGRN, D_HGRN, D_HGRN, DT_PAD)
    return pl.pallas_call(
        _inproj_kernel,
        grid=(t // TM_PROJ,),
        in_specs=[pl.BlockSpec((TM_PROJ, D_MODEL), lambda i: (i, 0)),
                  pl.BlockSpec((D_MODEL, N_IN_PAD), lambda i: (0, 0))],
        out_specs=[pl.BlockSpec((TM_PROJ, n), lambda i: (i, 0)) for n in widths],
        out_shape=[jax.ShapeDtypeStruct((t, n), F32) for n in widths],
        compiler_params=pltpu.CompilerParams(dimension_semantics=("arbitrary",),
                                             vmem_limit_bytes=48 * 1024 * 1024),
        name="inproj",
    )(x2, w_perm)


def _ssd_kernel(z_ref, xbc_ref, dt_ref, cw_ref, cb_ref, dtb_ref, alog_ref, dskip_ref, nw_ref,
                y_ref, ext_ref, st_ref):
    n = xbc_ref.shape[0]
    halo = SUBLANES

    @pl.when(pl.program_id(0) == 0)
    def _():
        ext_ref[0:halo, :] = jnp.zeros((halo, D_CONV), F32)
        st_ref[...] = jnp.zeros(st_ref.shape, F32)

    ext_ref[halo:halo + n, :] = xbc_ref[...]
    acc = jnp.broadcast_to(cb_ref[...], (n, D_CONV))
    for k in range(SSD_CONV):
        off = halo - (SSD_CONV - 1) + k
        acc = acc + cw_ref[k:k + 1, :] * ext_ref[off:off + n, :]
    ext_ref[0:halo, :] = xbc_ref[n - halo:n, :]
    u = _silu(acc)
    for c in range(n // SSD_CHUNK):
        rows = slice(c * SSD_CHUNK, (c + 1) * SSD_CHUNK)
        y_ref[rows, :] = _ssd_chunk(u[rows, :], z_ref[rows, :], dt_ref[rows, :] + dtb_ref[...],
                                    -jnp.exp(alog_ref[...]), dskip_ref[...], nw_ref[...], st_ref)


def _ssd_chunk(u, z, draw, a, dskip, nw, st_ref):
    L = SSD_CHUNK
    xs = u[:, :D_SSM]
    bm = u[:, D_SSM:D_SSM + SSD_GROUPS * SSD_STATE]
    cm = u[:, D_SSM + SSD_GROUPS * SSD_STATE:]

    dt = jnp.maximum(draw, 0.0) + jnp.log(1.0 + jnp.exp(-jnp.abs(draw)))
    ad = dt * a
    rows = lax.broadcasted_iota(I32, (L, L), 0)
    cols = lax.broadcasted_iota(I32, (L, L), 1)
    causal = rows >= cols
    a_cum = _sel_dot(causal.astype(BF16), ad)
    a_cum_t = a_cum.T

    hrow = lax.broadcasted_iota(I32, (LANES, D_SSM), 0)
    hcol = lax.broadcasted_iota(I32, (LANES, D_SSM), 1) // SSD_HEAD_DIM
    expand = (hrow == hcol).astype(BF16)
    dt_x = _dot_sel(dt, expand)
    acx = _dot_sel(a_cum, expand)
    last = acx[L - 1:L, :]
    ea_x = jnp.exp(acx)
    dec_x = jnp.exp(last - acx)
    elast_x = jnp.exp(last)

    xdt = xs * dt_x
    gw = SSD_HEADS // SSD_GROUPS * SSD_HEAD_DIM
    lane_head = lax.broadcasted_iota(I32, (L, gw), 1) // SSD_HEAD_DIM
    ys = []
    for g in range(SSD_GROUPS):
        bg = bm[:, g * SSD_STATE:(g + 1) * SSD_STATE]
        cg = cm[:, g * SSD_STATE:(g + 1) * SSD_STATE].astype(BF16)
        bg_t = bg.T.astype(BF16)
        gmat = _dot(cg, bg_t)
        xdt_g = xdt[:, g * gw:(g + 1) * gw]
        xdt_gb = xdt_g.astype(BF16)
        r_prev = st_ref[g]
        y_g = _dot(cg, r_prev.astype(BF16)) * ea_x[:, g * gw:(g + 1) * gw]
        new_s = _dot(bg_t, (xdt_g * dec_x[:, g * gw:(g + 1) * gw]).astype(BF16))
        st_ref[g] = r_prev * elast_x[:, g * gw:(g + 1) * gw] + new_s
        for j in range(SSD_HEADS // SSD_GROUPS):
            h = g * (SSD_HEADS // SSD_GROUPS) + j
            diff = a_cum[:, h:h + 1] - a_cum_t[h:h + 1, :]
            decay = jnp.exp(jnp.where(causal, diff, NEG))
            yd = _dot((gmat * decay).astype(BF16), xdt_gb)
            y_g = y_g + jnp.where(lane_head == j, yd, 0.0)
        ys.append(y_g)
    y = jnp.concatenate(ys, axis=-1) + xs * dskip
    y = y * _silu(z)
    outs = []
    ng = D_SSM // SSD_GROUPS
    for g in range(SSD_GROUPS):
        yg = y[:, g * ng:(g + 1) * ng]
        ms = jnp.mean(yg * yg, axis=-1, keepdims=True)
        outs.append(yg * lax.rsqrt(ms + RMS_EPS))
    return jnp.concatenate(outs, axis=-1) * nw


def _ssd(z, xbc, dt, conv_w, conv_b, dt_bias, a_log, d_skip, norm_w):
    t = z.shape[0]
    L = SSD_CHUNK * SSD_PER_STEP
    pad = lambda v: jnp.pad(v.astype(F32), (0, LANES - v.shape[0])).reshape(1, LANES)
    full = lambda shape: pl.BlockSpec(shape, lambda c: (0,) * len(shape))
    return pl.pallas_call(
        _ssd_kernel,
        grid=(t // L,),
        in_specs=[pl.BlockSpec((L, D_SSM), lambda c: (c, 0)),
                  pl.BlockSpec((L, D_CONV), lambda c: (c, 0)),
                  pl.BlockSpec((L, DT_PAD), lambda c: (c, 0)),
                  full((SSD_CONV, D_CONV)), full((1, D_CONV)), full((1, LANES)), full((1, LANES)),
                  full((1, D_SSM)), full((1, D_SSM))],
        out_specs=pl.BlockSpec((L, D_SSM), lambda c: (c, 0)),
        out_shape=jax.ShapeDtypeStruct((t, D_SSM), F32),
        scratch_shapes=[pltpu.VMEM((L + SUBLANES, D_CONV), F32),
                        pltpu.VMEM((SSD_GROUPS, SSD_STATE, D_SSM // SSD_GROUPS), F32)],
        compiler_params=pltpu.CompilerParams(dimension_semantics=("arbitrary",)),
        name="ssd",
    )(z, xbc, dt, conv_w.astype(F32), conv_b.reshape(1, D_CONV).astype(F32), pad(dt_bias), pad(a_log),
      jnp.repeat(d_skip.astype(F32), SSD_HEAD_DIM).reshape(1, D_SSM), norm_w.reshape(1, D_SSM).astype(F32))


def _tile_bcast(x, r):
    n, d = x.shape
    x3 = x.reshape(n // SUBLANES, SUBLANES, d)
    return jnp.broadcast_to(x3[:, r:r + 1, :], x3.shape).reshape(n, d)


def _hgrn_kernel(q_ref, f_ref, i_ref, g_ref, lb_ref, nw_ref, o_ref, st_ref):
    @pl.when(pl.program_id(0) == 0)
    def _():
        st_ref[...] = jnp.zeros(st_ref.shape, F32)

    for c in range(o_ref.shape[0] // HGRN_CHUNK):
        rows = slice(c * HGRN_CHUNK, (c + 1) * HGRN_CHUNK)
        o_ref[rows, :] = _hgrn_chunk(q_ref[rows, :], f_ref[rows, :], i_ref[rows, :], g_ref[rows, :],
                                     lb_ref[...], nw_ref[...], st_ref)


def _hgrn_chunk(q, f, v, g, lb, nw, st_ref):
    C = HGRN_CHUNK
    S = HGRN_SUB
    nsub = C // S
    fg = lb + (1.0 - lb) * _sigmoid(f)
    qs = _silu(q)
    rows = lax.broadcasted_iota(I32, (C, C), 0)
    cols = lax.broadcasted_iota(I32, (C, C), 1)
    cum = _sel_dot((rows >= cols).astype(BF16), jnp.log2(fg))
    ck_all = cum - jnp.log2(1.0 - fg)

    srow = lax.broadcasted_iota(I32, (S * HGRN_DK, C), 0) // HGRN_DK
    scol = lax.broadcasted_iota(I32, (S * HGRN_DK, C), 1) % S
    spread = (srow == scol).astype(BF16)
    diag_keep = jnp.logical_and(rows // S == cols // S, rows >= cols)
    outs = []
    for h in range(HGRN_HEADS):
        blk = slice(h * HGRN_DK, (h + 1) * HGRN_DK)
        qh, ch, ck = qs[:, blk], cum[:, blk], ck_all[:, blk]
        last = ch[C - 1:C, :]
        x_cat = jnp.concatenate(
            [(qh * jnp.exp2(jnp.minimum(ch - _tile_bcast(ck, r), 0.0))).astype(BF16) for r in range(S)], axis=-1)
        kbe = jnp.exp2(_tile_bcast(ch, S - 1) - ck).astype(BF16)
        a_pieces, b_pieces = [], []
        for j in range(nsub - 1):
            lo = (j + 1) * S
            a = (qh[lo:, :] * jnp.exp2(ch[lo:, :] - ch[lo - 1:lo, :])).astype(BF16)
            a_pieces.append(jnp.concatenate([jnp.zeros((lo, HGRN_DK), BF16), a], axis=0))
            b = [kbe[j * S:lo, :], jnp.zeros((C - lo, HGRN_DK), BF16)]
            if j > 0:
                b.insert(0, jnp.zeros((j * S, HGRN_DK), BF16))
            b_pieces.append(jnp.concatenate(b, axis=0))
        att = (_dot_nt(jnp.concatenate(a_pieces, axis=-1), jnp.concatenate(b_pieces, axis=-1))
               + jnp.where(diag_keep, _dot(x_cat, spread), 0.0))
        vb = v[:, blk].astype(BF16)
        s_prev = st_ref[h]
        o = _dot_nt((qh * jnp.exp2(ch)).astype(BF16), s_prev.astype(BF16)) + _dot(att.astype(BF16), vb)
        st_ref[h] = s_prev * jnp.exp2(last) + _dot_tn(vb, jnp.exp2(last - ck).astype(BF16))
        ms = jnp.mean(o * o, axis=-1, keepdims=True)
        outs.append(o * lax.rsqrt(ms + RMS_EPS))
    return jnp.concatenate(outs, axis=-1) * nw * _silu(g)


def _hgrn(q, f, i, g, lower_bound, norm_w):
    t = q.shape[0]
    C = HGRN_CHUNK * HGRN_PER_STEP
    tok = pl.BlockSpec((C, D_HGRN), lambda c: (c, 0))
    vec = pl.BlockSpec((1, D_HGRN), lambda c: (0, 0))
    return pl.pallas_call(
        _hgrn_kernel,
        grid=(t // C,),
        in_specs=[tok, tok, tok, tok, vec, vec],
        out_specs=tok,
        out_shape=jax.ShapeDtypeStruct((t, D_HGRN), F32),
        scratch_shapes=[pltpu.VMEM((HGRN_HEADS, D_HGRN // HGRN_HEADS, HGRN_DK), F32)],
        compiler_params=pltpu.CompilerParams(dimension_semantics=("arbitrary",)),
        name="hgrn",
    )(q, f, i, g, lower_bound.reshape(1, D_HGRN).astype(F32), norm_w.reshape(1, D_HGRN).astype(F32))


def _post_kernel(x_ref, ys_ref, yh_ref, wo_ref, g1_ref, b1_ref, wrh_ref, wrl_ref, rb_ref,
                 htt_ref, base_ref, idx_ref, gate_ref, cnt_ref):
    tm = x_ref.shape[0]
    mix = (_dot(ys_ref[...].astype(BF16), wo_ref[0:D_SSM, :])
           + _dot(yh_ref[...].astype(BF16), wo_ref[D_SSM:, :]))
    h1 = _layer_norm(ALPHA * x_ref[...] + mix, g1_ref[...], b1_ref[...])
    packed = _pack_pairs(h1[:, :D_MODEL // 2], h1[:, D_MODEL // 2:])
    for hh in range(ROW_HALVES):
        htt_ref[hh] = packed[:, hh * HALF_WORDS:(hh + 1) * HALF_WORDS]
    hb = h1.astype(BF16)
    base_ref[...] = ALPHA * h1

    hlo = (h1 - hb.astype(F32)).astype(BF16)
    logits = _dot_nt(wrh_ref[...], hb) + _dot_nt(wrh_ref[...], hlo) + _dot_nt(wrl_ref[...], hb)
    scores = _sigmoid(logits)
    biased = scores + rb_ref[...]
    per_group = N_EXPERTS // N_EXPERT_GROUPS
    eidx = lax.broadcasted_iota(I32, (N_EXPERTS, tm), 0)
    big = jnp.int32(1 << 20)
    gsc = []
    bidx = lax.broadcasted_iota(I32, (per_group, tm), 0)
    for gi in range(N_EXPERT_GROUPS):
        blk = biased[gi * per_group:(gi + 1) * per_group, :]
        m1 = jnp.max(blk, axis=0, keepdims=True)
        i1 = jnp.min(jnp.where(blk == m1, bidx, big), axis=0, keepdims=True)
        m2 = jnp.max(jnp.where(bidx == i1, NEG, blk), axis=0, keepdims=True)
        gsc.append(m1 + m2)
    cur = jnp.concatenate(gsc, axis=0)
    gidx = lax.broadcasted_iota(I32, (N_EXPERT_GROUPS, tm), 0)
    gsel = jnp.zeros((N_EXPERT_GROUPS, tm), F32)
    for _ in range(TOPK_GROUPS):
        m = jnp.max(cur, axis=0, keepdims=True)
        i = jnp.min(jnp.where(cur == m, gidx, big), axis=0, keepdims=True)
        hit = gidx == i
        gsel = jnp.where(hit, 1.0, gsel)
        cur = jnp.where(hit, NEG, cur)
    emask = jnp.concatenate(
        [jnp.broadcast_to(gsel[gi:gi + 1, :], (per_group, tm)) for gi in range(N_EXPERT_GROUPS)], axis=0)
    masked = jnp.where(emask > 0.0, biased, NEG)
    idx_rows, gate_rows = [], []
    for _ in range(TOP_K):
        m = jnp.max(masked, axis=0, keepdims=True)
        i = jnp.min(jnp.where(masked == m, eidx, big), axis=0, keepdims=True)
        hit = eidx == i
        idx_rows.append(i)
        gate_rows.append(jnp.sum(jnp.where(hit, scores, 0.0), axis=0, keepdims=True))
        masked = jnp.where(hit, NEG, masked)
    picked = jnp.where(jnp.logical_and(masked == NEG, emask > 0.0), 1.0, 0.0)
    gates = jnp.concatenate(gate_rows, axis=0)
    gates = gates / jnp.sum(gates, axis=0, keepdims=True) * ROUTED_SCALE
    idx_ref[...] = jnp.concatenate(idx_rows, axis=0)
    gate_ref[...] = gates

    @pl.when(pl.program_id(0) == 0)
    def _():
        cnt_ref[...] = jnp.zeros(cnt_ref.shape, F32)

    cnt_ref[...] += _dot(picked.astype(BF16), jnp.ones((tm, LANES), BF16))


def _post(x2, y_ssd, y_hgrn, wo, g1, b1, wr_hi, wr_lo, rbias):
    t = x2.shape[0]
    tm = TM_POST
    full = lambda shape: pl.BlockSpec(shape, lambda i: (0,) * len(shape))
    return pl.pallas_call(
        _post_kernel,
        grid=(t // tm,),
        in_specs=[pl.BlockSpec((tm, D_MODEL), lambda i: (i, 0)),
                  pl.BlockSpec((tm, D_SSM), lambda i: (i, 0)),
                  pl.BlockSpec((tm, D_HGRN), lambda i: (i, 0)),
                  full((D_MODEL, D_MODEL)), full((1, D_MODEL)), full((1, D_MODEL)),
                  full((N_EXPERTS, D_MODEL)), full((N_EXPERTS, D_MODEL)), full((N_EXPERTS, 1))],
        out_specs=[pl.BlockSpec((ROW_HALVES, tm, HALF_WORDS), lambda i: (0, i, 0)),
                   pl.BlockSpec((tm, D_MODEL), lambda i: (i, 0)),
                   pl.BlockSpec((TOP_K, tm), lambda i: (0, i)),
                   pl.BlockSpec((TOP_K, tm), lambda i: (0, i)),
                   pl.BlockSpec((N_EXPERTS, LANES), lambda i: (0, 0))],
        out_shape=[jax.ShapeDtypeStruct((ROW_HALVES, t, HALF_WORDS), U32),
                   jax.ShapeDtypeStruct((t, D_MODEL), F32),
                   jax.ShapeDtypeStruct((TOP_K, t), I32),
                   jax.ShapeDtypeStruct((TOP_K, t), F32),
                   jax.ShapeDtypeStruct((N_EXPERTS, LANES), F32)],
        compiler_params=pltpu.CompilerParams(dimension_semantics=("arbitrary",),
                                             vmem_limit_bytes=60 * 1024 * 1024),
        name="post",
    )(x2, y_ssd, y_hgrn, wo, g1, b1, wr_hi, wr_lo, rbias)


def _max_blocks(t):
    return (t * TOP_K + N_EXPERTS * (EXPERT_ROWS - 1)) // EXPERT_ROWS


def _pos_kernel(idx_ref, total_ref, slot_ref, bstart_ref, nblk_ref, cnt_ref, start_ref):
    i = pl.program_id(0)
    tm = idx_ref.shape[-1]
    idx = idx_ref[...]
    eidx = lax.broadcasted_iota(I32, (N_EXPERTS, tm), 0)
    sel = [eidx == idx[k:k + 1, :] for k in range(TOP_K)]
    onehot = sel[0]
    for k in range(1, TOP_K):
        onehot = jnp.logical_or(onehot, sel[k])
    mt = jnp.where(onehot, 1.0, 0.0).astype(BF16)

    @pl.when(i == 0)
    def _():
        nb = jnp.floor((total_ref[...] + (EXPERT_ROWS - 1)) * (1.0 / EXPERT_ROWS))
        r = lax.broadcasted_iota(I32, (N_EXPERTS, N_EXPERTS), 0)
        c = lax.broadcasted_iota(I32, (N_EXPERTS, N_EXPERTS), 1)
        end = _dot((r >= c).astype(BF16), nb.astype(BF16))
        start_ref[...] = (end - nb) * EXPERT_ROWS
        cnt_ref[...] = jnp.zeros(cnt_ref.shape, F32)
        bstart_ref[...] = (end - nb).astype(I32)
        nblk_ref[...] = nb.astype(I32)

    r = lax.broadcasted_iota(I32, (tm, tm), 0)
    c = lax.broadcasted_iota(I32, (tm, tm), 1)
    before = _dot(mt, (r < c).astype(BF16))
    sorted_row = start_ref[:, 0:1] + cnt_ref[:, 0:1] + before
    rows = []
    for k in range(TOP_K):
        dest = jnp.sum(jnp.where(sel[k], sorted_row, 0.0), axis=0, keepdims=True).astype(I32)
        first = (dest // EXPERT_ROWS) * (EXPERT_ROWS * ROW_HALVES) + dest % EXPERT_ROWS
        rows += [first + hh * EXPERT_ROWS for hh in range(ROW_HALVES)]
    slot_ref[...] = jnp.concatenate(rows, axis=0)
    cnt_ref[...] += _dot(mt, jnp.ones((tm, LANES), BF16))


def _positions(idx, totals):
    t = idx.shape[1]
    tm = TM_POSITIONS
    return pl.pallas_call(
        _pos_kernel,
        grid=(t // tm,),
        in_specs=[pl.BlockSpec((TOP_K, tm), lambda i: (0, i)),
                  pl.BlockSpec((N_EXPERTS, LANES), lambda i: (0, 0))],
        out_specs=[pl.BlockSpec((TOP_K * ROW_HALVES, tm), lambda i: (0, i)),
                   pl.BlockSpec((N_EXPERTS, LANES), lambda i: (0, 0)),
                   pl.BlockSpec((N_EXPERTS, LANES), lambda i: (0, 0))],
        out_shape=[jax.ShapeDtypeStruct((TOP_K * ROW_HALVES, t), I32),
                   jax.ShapeDtypeStruct((N_EXPERTS, LANES), I32),
                   jax.ShapeDtypeStruct((N_EXPERTS, LANES), I32)],
        scratch_shapes=[pltpu.VMEM((N_EXPERTS, LANES), F32), pltpu.VMEM((N_EXPERTS, LANES), F32)],
        compiler_params=pltpu.CompilerParams(dimension_semantics=("arbitrary",)),
        name="positions",
    )(idx, totals)


def _sc_mesh():
    return plsc.VectorSubcoreMesh(core_axis_name="core", subcore_axis_name="subcore")


def _sc_dispatch(src, slot, n_out):
    lists, t = slot.shape
    wins = t // SC_WINDOW

    @pl.kernel(out_type=jax.ShapeDtypeStruct((n_out, HALF_WORDS), src.dtype), mesh=_sc_mesh(), name="sc_dispatch")
    def scatter(src_hbm, idx_hbm, out_hbm):
        def body(rows_vmem, idx_vmem):
            pltpu.sync_copy(rows_vmem, out_hbm.at[idx_vmem.at[0]])

        pltpu.emit_pipeline(
            body, grid=(lists * wins,),
            in_specs=[pl.BlockSpec((SC_WINDOW, HALF_WORDS),
                                   index_map=lambda i: ((i // wins) % ROW_HALVES * wins + i % wins, 0)),
                      pl.BlockSpec((1, SC_WINDOW), index_map=lambda i: (i // wins, i % wins))],
            out_specs=[],
            core_axis_name=("core", "subcore"), dimension_semantics=(pltpu.PARALLEL,))(src_hbm, idx_hbm)

    return scatter(src, slot)


def _sc_gather(src, slot, first_token, tokens):
    lists = slot.shape[0]
    wins = tokens // SC_WINDOW
    first_win = first_token // SC_WINDOW

    @pl.kernel(out_type=jax.ShapeDtypeStruct((lists * tokens, HALF_WORDS), src.dtype), mesh=_sc_mesh(),
               name="sc_gather")
    def gather(src_hbm, idx_hbm, out_hbm):
        def body(idx_vmem, rows_vmem):
            pltpu.sync_copy(src_hbm.at[idx_vmem.at[0]], rows_vmem)

        pltpu.emit_pipeline(
            body, grid=(lists * wins,),
            in_specs=[pl.BlockSpec((1, SC_WINDOW), index_map=lambda i: (i // wins, first_win + i % wins))],
            out_specs=[pl.BlockSpec((SC_WINDOW, HALF_WORDS), index_map=lambda i: (i, 0))],
            core_axis_name=("core", "subcore"), dimension_semantics=(pltpu.PARALLEL,))(idx_hbm, out_hbm)

    return gather(src, slot)


def _expert_kernel(bstart_ref, nblk_ref, xs_hbm, wg_ref, wu_ref, wd_ref, ys_hbm,
                   xbuf, ybuf, wgb_ref, wub_ref, wdb_ref, xsem, ysem, ypend):
    e = pl.program_id(0)
    last = pl.num_programs(0) - 1
    n = nblk_ref[e]
    b0 = bstart_ref[e]
    total = bstart_ref[last] + nblk_ref[last]
    blk_rows = EXPERT_ROWS * ROW_HALVES
    xdepth = xbuf.shape[0]
    ydepth = ybuf.shape[0]
    lookahead = xdepth - MAX_CHUNK

    def hbm_block(ref, b):
        return ref.at[pl.ds(pl.multiple_of(b * blk_rows, blk_rows), blk_rows)]

    def x_copy(b):
        slot = b % xdepth
        return pltpu.make_async_copy(hbm_block(xs_hbm, b), xbuf.at[slot], xsem.at[slot])

    def y_copy(b, slot):
        return pltpu.make_async_copy(ybuf.at[slot], hbm_block(ys_hbm, b), ysem.at[slot])

    @pl.when(e == 0)
    def _():
        for s in range(ydepth):
            ypend[s] = 0
        for b in range(lookahead):
            @pl.when(b < total)
            def _():
                x_copy(b).start()

    def chunk(j, m):
        b = b0 + j
        for i in range(m):
            @pl.when(b + lookahead + i < total)
            def _():
                x_copy(b + lookahead + i).start()

        for i in range(m):
            x_copy(b + i).wait()

            @pl.when(ypend[(b + i) % ydepth] == 1)
            def _():
                y_copy(b + i, (b + i) % ydepth).wait()

        for i in range(m):
            xslot = (b + i) % xdepth
            yslot = (b + i) % ydepth
            halves = [_unpack_pairs(xbuf[xslot, hh * EXPERT_ROWS:(hh + 1) * EXPERT_ROWS, :])
                      for hh in range(ROW_HALVES)]
            xb = jnp.concatenate([lo for lo, _ in halves] + [hi for _, hi in halves], axis=-1).astype(BF16)
            hid = _silu(_dot(xb, wgb_ref[...])) * _dot(xb, wub_ref[...])
            yb = _dot(hid.astype(BF16), wdb_ref[...])
            packed = _pack_pairs(yb[:, :D_MODEL // 2], yb[:, D_MODEL // 2:])
            for hh in range(ROW_HALVES):
                ybuf[yslot, hh * EXPERT_ROWS:(hh + 1) * EXPERT_ROWS, :] = (
                    packed[:, hh * HALF_WORDS:(hh + 1) * HALF_WORDS])

        for i in range(m):
            y_copy(b + i, (b + i) % ydepth).start()
            ypend[(b + i) % ydepth] = 1

    @pl.when(n > 0)
    def _():
        wgb_ref[...] = wg_ref[0].astype(BF16)
        wub_ref[...] = wu_ref[0].astype(BF16)
        wdb_ref[...] = wd_ref[0].astype(BF16)
        full = n // MAX_CHUNK

        def full_chunk(q, carry):
            chunk(q * MAX_CHUNK, MAX_CHUNK)
            return carry

        lax.fori_loop(0, full, full_chunk, 0)
        for m in range(1, MAX_CHUNK):
            @pl.when(n - full * MAX_CHUNK == m)
            def _():
                chunk(full * MAX_CHUNK, m)

    @pl.when(e == last)
    def _():
        for s in range(ydepth):
            @pl.when(ypend[s] == 1)
            def _():
                y_copy(b0, s).wait()


def _experts(bstart, nblk, xs, w_gate, w_up, w_down):
    blk_shape = (EXPERT_ROWS * ROW_HALVES, HALF_WORDS)
    wspec = lambda shape: pl.BlockSpec((1,) + shape, lambda e, bs, nb: (e, 0, 0))
    grid_spec = pltpu.PrefetchScalarGridSpec(
        num_scalar_prefetch=2,
        grid=(N_EXPERTS,),
        in_specs=[pl.BlockSpec(memory_space=pl.ANY),
                  wspec((D_MODEL, D_EXPERT)), wspec((D_MODEL, D_EXPERT)), wspec((D_EXPERT, D_MODEL))],
        out_specs=pl.BlockSpec(memory_space=pl.ANY),
        scratch_shapes=[pltpu.VMEM((X_RING,) + blk_shape, U32), pltpu.VMEM((Y_RING,) + blk_shape, U32),
                        pltpu.VMEM((D_MODEL, D_EXPERT), BF16), pltpu.VMEM((D_MODEL, D_EXPERT), BF16),
                        pltpu.VMEM((D_EXPERT, D_MODEL), BF16),
                        pltpu.SemaphoreType.DMA((X_RING,)), pltpu.SemaphoreType.DMA((Y_RING,)),
                        pltpu.SMEM((Y_RING,), I32)],
    )
    return pl.pallas_call(
        _expert_kernel,
        grid_spec=grid_spec,
        out_shape=jax.ShapeDtypeStruct(xs.shape, U32),
        compiler_params=pltpu.CompilerParams(dimension_semantics=("arbitrary",), has_side_effects=True),
        name="experts",
    )(bstart, nblk, xs, w_gate, w_up, w_down)


def _combine_kernel(gate_ref, base_ref, g2_ref, b2_ref, htt_ref, wg_ref, wu_ref, wd_ref, rows_ref, o_ref):
    tm = base_ref.shape[0]
    halves = [_unpack_pairs(htt_ref[hh]) for hh in range(ROW_HALVES)]
    hb = jnp.concatenate([lo for lo, _ in halves] + [hi for _, hi in halves], axis=-1).astype(BF16)
    hid = _silu(_dot(hb, wg_ref[...])) * _dot(hb, wu_ref[...])
    acc = base_ref[...] + _dot(hid.astype(BF16), wd_ref[...])
    gates_t = jnp.concatenate([gate_ref[...], jnp.zeros((LANES - TOP_K, tm), F32)], axis=0).T
    half = D_MODEL // 2
    lo_acc = [acc[:, hh * HALF_WORDS:(hh + 1) * HALF_WORDS] for hh in range(ROW_HALVES)]
    hi_acc = [acc[:, half + hh * HALF_WORDS:half + (hh + 1) * HALF_WORDS] for hh in range(ROW_HALVES)]
    for k in range(TOP_K):
        gk = gates_t[:, k:k + 1]
        for hh in range(ROW_HALVES):
            lo, hi = _unpack_pairs(rows_ref[k, hh])
            lo_acc[hh] = lo_acc[hh] + gk * lo
            hi_acc[hh] = hi_acc[hh] + gk * hi
    o_ref[...] = _layer_norm(jnp.concatenate(lo_acc + hi_acc, axis=-1), g2_ref[...], b2_ref[...])


def _combine_piece_kernel(*refs):
    _combine_kernel(*refs[:-2], refs[-1])


def _combine(gates, base, g2, b2, htt, wgs, wus, wds, rows, first_tile, prev):
    t = base.shape[0]
    tm = TM_COMBINE
    tiles = rows.shape[2] // tm
    full = lambda shape: pl.BlockSpec(shape, lambda i: (0,) * len(shape))
    in_specs = [pl.BlockSpec((TOP_K, tm), lambda i: (0, i + first_tile)),
                pl.BlockSpec((tm, D_MODEL), lambda i: (i + first_tile, 0)),
                full((1, D_MODEL)), full((1, D_MODEL)),
                pl.BlockSpec((ROW_HALVES, tm, HALF_WORDS), lambda i: (0, i + first_tile, 0)),
                full((D_MODEL, D_EXPERT)), full((D_MODEL, D_EXPERT)), full((D_EXPERT, D_MODEL)),
                pl.BlockSpec((TOP_K, ROW_HALVES, tm, HALF_WORDS), lambda i: (0, 0, i, 0))]
    args = [gates, base, g2, b2, htt, wgs, wus, wds, rows]
    if prev is not None:
        in_specs.append(pl.BlockSpec(memory_space=pl.ANY))
        args.append(prev)
    return pl.pallas_call(
        _combine_kernel if prev is None else _combine_piece_kernel,
        grid=(tiles,),
        in_specs=in_specs,
        out_specs=pl.BlockSpec((tm, D_MODEL), lambda i: (i + first_tile, 0)),
        out_shape=jax.ShapeDtypeStruct((t, D_MODEL), F32),
        input_output_aliases={} if prev is None else {len(args) - 1: 0},
        compiler_params=pltpu.CompilerParams(dimension_semantics=("arbitrary",),
                                             vmem_limit_bytes=48 * 1024 * 1024),
        name="combine",
    )(*args)


def _layer(h2, w_in, conv_w, conv_b, dt_bias, a_log, d_skip, ssd_norm_w, lower_bound, hgrn_norm_w, w_out,
           ln1_g, ln1_b, w_router, router_bias, w_gate_e, w_up_e, w_down_e, w_gate_s, w_up_s, w_down_s,
           ln2_g, ln2_b):
    t = h2.shape[0]
    dt0 = D_SSM + D_CONV
    q0 = dt0 + SSD_HEADS
    w_perm = jnp.concatenate(
        [w_in[:, :dt0], w_in[:, q0:], w_in[:, dt0:q0], jnp.zeros((D_MODEL, DT_PAD - SSD_HEADS), w_in.dtype)],
        axis=1).astype(BF16)
    z, xbc, q, f, i, g, dt = _inproj(h2, w_perm)
    y_ssd = _ssd(z, xbc, dt, conv_w, conv_b, dt_bias, a_log, d_skip, ssd_norm_w)
    y_hgrn = _hgrn(q, f, i, g, lower_bound, hgrn_norm_w)

    wr_t = w_router.astype(F32).T
    wr_hi = wr_t.astype(BF16)
    wr_lo = (wr_t - wr_hi.astype(F32)).astype(BF16)
    row = lambda v: v.reshape(1, -1).astype(F32)
    htt, base, idx, gates, totals = _post(
        h2, y_ssd, y_hgrn, w_out.astype(BF16), row(ln1_g), row(ln1_b), wr_hi, wr_lo,
        router_bias.reshape(N_EXPERTS, 1).astype(F32))

    slot, bstart, nblk = _positions(idx, totals)
    n_rows = _max_blocks(t) * EXPERT_ROWS
    xs = _sc_dispatch(htt.reshape(ROW_HALVES * t, HALF_WORDS), slot, n_rows * ROW_HALVES)
    ys = _experts(bstart[:, 0], nblk[:, 0], xs, w_gate_e, w_up_e, w_down_e)
    tp = t // GATHER_PIECES
    shared_w = (w_gate_s.astype(BF16), w_up_s.astype(BF16), w_down_s.astype(BF16))
    out = None
    for p in range(GATHER_PIECES):
        rows = _sc_gather(ys, slot, p * tp, tp).reshape(TOP_K, ROW_HALVES, tp, HALF_WORDS)
        out = _combine(gates, base, row(ln2_g), row(ln2_b), htt, *shared_w, rows, p * (tp // TM_COMBINE), out)
    return out


def kernel(x, w_in, conv_w, conv_b, dt_bias, a_log, d_skip, ssd_norm_w, hgrn_lb_logits, hgrn_norm_w, w_out,
           ln1_g, ln1_b, w_router, router_bias, w_gate_e, w_up_e, w_down_e, w_gate_s, w_up_s, w_down_s,
           ln2_g, ln2_b):
    bsz, t, d = x.shape
    assert bsz == 1 and d == D_MODEL, "the recurrent mixers carry state across the flattened token axis"
    depth = w_in.shape[0]
    lower_bounds = jnp.cumsum(jax.nn.softmax(hgrn_lb_logits.astype(F32), axis=0), axis=0)
    h = x.reshape(bsz * t, d)
    for l in range(depth):
        h = _layer(h, w_in[l], conv_w[l], conv_b[l], dt_bias[l], a_log[l], d_skip[l], ssd_norm_w[l],
                   lower_bounds[l], hgrn_norm_w[l], w_out[l], ln1_g[l], ln1_b[l], w_router[l],
                   router_bias[l], w_gate_e[l], w_up_e[l], w_down_e[l], w_gate_s[l], w_up_s[l],
                   w_down_s[l], ln2_g[l], ln2_b[l])
    return h.reshape(bsz, t, d)
```
